```python
import jax, jax.numpy as jnp
from jax import lax
import numpy as np

D_MODEL = 2048
BATCH = 8
SEQ = 4096
DEPTH = 4

N_MIXERS = 3
MEM_LEN = 256
EPS = 1e-6
BLOCK = 128

SSD_EXPAND = 2
SSD_D_INNER = SSD_EXPAND * D_MODEL
SSD_HEAD_DIM = 64
SSD_HEADS = SSD_D_INNER // SSD_HEAD_DIM
SSD_GROUPS = 8
SSD_HEADS_PER_GROUP = SSD_HEADS // SSD_GROUPS
SSD_STATE = 128
SSD_CONV = 4
SSD_CHUNK = 128
SSD_CONV_DIM = SSD_D_INNER + 2 * SSD_GROUPS * SSD_STATE
SSD_IN_DIM = 2 * SSD_D_INNER + 2 * SSD_GROUPS * SSD_STATE + SSD_HEADS

SG_WIDTH = 2 * D_MODEL
SG_GROUPS = 16
SG_CHUNK = 128

SB_HEAD_DIM = 128
SB_HEADS = D_MODEL // SB_HEAD_DIM

XA_HEADS = 4
XA_HEAD_DIM = 128
XA_WIDTH = XA_HEADS * XA_HEAD_DIM

FFN_DIM = 5632
FFN_CONV = 3

N_SSD = (DEPTH + N_MIXERS - 1) // N_MIXERS
N_SG = (DEPTH + N_MIXERS - 2) // N_MIXERS
N_SB = DEPTH // N_MIXERS

kernel_name = "interleaved_ssd_gmlp_stickbreak_trunk"


def rmsnorm(x, g):
    xf = x.astype(jnp.float32)
    y = xf * lax.rsqrt(jnp.mean(xf * xf, axis=-1, keepdims=True) + EPS)
    return (y * g.astype(jnp.float32)).astype(x.dtype)


def layernorm(x, g, b):
    xf = x.astype(jnp.float32)
    mu = jnp.mean(xf, axis=-1, keepdims=True)
    xc = xf - mu
    y = xc * lax.rsqrt(jnp.mean(xc * xc, axis=-1, keepdims=True) + EPS)
    return (y * g.astype(jnp.float32) + b.astype(jnp.float32)).astype(x.dtype)


def causal_dwconv(x, w, b):
    K, C = w.shape
    y = lax.conv_general_dilated(
        x, w[:, None, :].astype(x.dtype), window_strides=(1,), padding=[(K - 1, 0)],
        dimension_numbers=("NWC", "WIO", "NWC"), feature_group_count=C)
    return y + b.astype(x.dtype)


def ssd_mixer(h, w_in, conv_w, conv_b, dt_bias, a_log, d_skip, norm_g, w_out):
    Bsz, L, _ = h.shape
    G, R, P, N, Q = SSD_GROUPS, SSD_HEADS_PER_GROUP, SSD_HEAD_DIM, SSD_STATE, SSD_CHUNK
    nc = L // Q
    f32 = jnp.float32
    proj = h @ w_in
    z, xbc, dt = jnp.split(proj, [SSD_D_INNER, SSD_D_INNER + SSD_CONV_DIM], axis=-1)
    xbc = jax.nn.silu(causal_dwconv(xbc, conv_w, conv_b))
    xs, Bm, Cm = jnp.split(xbc, [SSD_D_INNER, SSD_D_INNER + G * N], axis=-1)
    dt = jax.nn.softplus(dt.astype(f32) + dt_bias.astype(f32))
    A = -jnp.exp(a_log.astype(f32)).reshape(G, R)
    xs_f = xs.astype(f32).reshape(Bsz, nc, Q, G, R, P)
    dtc = dt.reshape(Bsz, nc, Q, G, R)
    Bc = Bm.astype(f32).reshape(Bsz, nc, Q, G, N)
    Cc = Cm.astype(f32).reshape(Bsz, nc, Q, G, N)
    a_cum = jnp.cumsum(dtc * A, axis=2)
    xdt = xs_f * dtc[..., None]
    causal = jnp.tril(jnp.ones((Q, Q), dtype=bool))[None, None, :, :, None, None]
    seg = a_cum[:, :, :, None] - a_cum[:, :, None, :]
    Lmat = jnp.exp(jnp.where(causal, seg, -jnp.inf))
    CB = jnp.einsum("bclgn,bcsgn->bclsg", Cc, Bc)
    y_diag = jnp.einsum("bclsg,bclsgr,bcsgrp->bclgrp", CB, Lmat, xdt)
    decay_states = jnp.exp(a_cum[:, :, -1:] - a_cum)
    states = jnp.einsum("bclgn,bclgr,bclgrp->bcgrpn", Bc, decay_states, xdt)
    chunk_decay = jnp.exp(a_cum[:, :, -1])

    def step(carry, inp):
        st, dec = inp
        return carry * dec[..., None, None] + st, carry

    init = jnp.zeros((Bsz, G, R, P, N), f32)
    _, prev = lax.scan(step, init, (jnp.moveaxis(states, 1, 0), jnp.moveaxis(chunk_decay, 1, 0)))
    prev = jnp.moveaxis(prev, 0, 1)
    y_off = jnp.einsum("bclgn,bcgrpn,bclgr->bclgrp", Cc, prev, jnp.exp(a_cum))
    y = y_diag + y_off + xs_f * d_skip.astype(f32).reshape(G, R)[..., None]
    y = y.reshape(Bsz, L, SSD_D_INNER) * jax.nn.silu(z.astype(f32))
    y = rmsnorm(y, norm_g).astype(h.dtype)
    return y @ w_out


def sgu_mixer(h, w_in, v_norm_g, v_norm_b, w_spatial, b_spatial, w_out):
    Bsz, L, _ = h.shape
    nc = L // SG_CHUNK
    uv = jax.nn.gelu(h @ w_in)
    u, v = jnp.split(uv, 2, axis=-1)
    v = layernorm(v, v_norm_g, v_norm_b)
    vc = v.reshape(Bsz, nc, SG_CHUNK, SG_GROUPS, SG_WIDTH // SG_GROUPS)
    mask = jnp.tril(jnp.ones((SG_CHUNK, SG_CHUNK), dtype=w_spatial.dtype))
    ws = w_spatial * mask
    mixed = jnp.einsum("gts,bcsgd->bctgd", ws, vc) + b_spatial.T[None, None, :, :, None]
    return (u * mixed.reshape(Bsz, L, SG_WIDTH)) @ w_out


def stick_breaking_mixer(h, w_qkv, w_out):
    Bsz, L, _ = h.shape
    f32 = jnp.float32
    qkv = (h @ w_qkv).reshape(Bsz, L, 3, SB_HEADS, SB_HEAD_DIM)
    q, k, v = qkv[:, :, 0], qkv[:, :, 1], qkv[:, :, 2]
    scale = SB_HEAD_DIM ** -0.5
    outs = []
    for i in range(L // BLOCK):
        q0 = i * BLOCK
        kend = q0 + BLOCK
        qb, kb, vb = q[:, q0:kend], k[:, :kend], v[:, :kend]
        z = jnp.einsum("bthd,bshd->bhts", qb, kb).astype(f32) * scale
        t_idx = q0 + jnp.arange(BLOCK)
        s_idx = jnp.arange(kend)
        valid = s_idx[None, :] < t_idx[:, None]
        log_beta = jax.nn.log_sigmoid(z)
        log_1mb = jnp.where(valid, jax.nn.log_sigmoid(-z), 0.0)
        tail = lax.cumsum(log_1mb, axis=3, reverse=True) - log_1mb
        A = jnp.where(valid, jnp.exp(log_beta + tail), 0.0)
        outs.append(jnp.einsum("bhts,bshd->bthd", A.astype(vb.dtype), vb))
    o = jnp.concatenate(outs, axis=1).reshape(Bsz, L, D_MODEL)
    return o @ w_out


def memory_cross_attention(h, mem_n, wq, wkv, wo):
    Bsz, L, _ = h.shape
    M = mem_n.shape[1]
    q = (h @ wq).reshape(Bsz, L, XA_HEADS, XA_HEAD_DIM)
    kv = (mem_n @ wkv).reshape(Bsz, M, 2, XA_HEADS, XA_HEAD_DIM)
    k, v = kv[:, :, 0], kv[:, :, 1]
    s = jnp.einsum("bthd,bmhd->bhtm", q, k).astype(jnp.float32) * (XA_HEAD_DIM ** -0.5)
    p = jax.nn.softmax(s, axis=-1).astype(v.dtype)
    o = jnp.einsum("bhtm,bmhd->bthd", p, v).reshape(Bsz, L, XA_WIDTH)
    return o @ wo


def conv_gated_ffn(h, w_in, conv_w, conv_b, w_out):
    gu = causal_dwconv(h @ w_in, conv_w, conv_b)
    g, u = jnp.split(gu, 2, axis=-1)
    return (jax.nn.gelu(g, approximate=True) * u) @ w_out


def _fwd_setup_inputs(seed: int = 0) -> dict:
    key = jax.random.key(seed)
    keys = iter(jax.random.split(key, 64))
    f32 = jnp.float32

    def nrm(shape, scale):
        return jax.random.normal(next(keys), shape, f32) * scale

    def gain(shape):
        return 1.0 + nrm(shape, 0.05)

    D = D_MODEL
    d = {}
    d["x"] = nrm((BATCH, SEQ, D), 1.0)
    d["mem"] = nrm((BATCH, MEM_LEN, D), 1.0)
    for name in ["ln_mix_pre", "ln_mix_post", "ln_mem", "ln_xa_pre", "ln_xa_post", "ln_ffn_pre", "ln_ffn_post"]:
        d[name] = gain((DEPTH, D))
    d["xa_wq"] = nrm((DEPTH, D, XA_WIDTH), D ** -0.5)
    d["xa_wkv"] = nrm((DEPTH, D, 2 * XA_WIDTH), D ** -0.5)
    d["xa_wo"] = nrm((DEPTH, XA_WIDTH, D), XA_WIDTH ** -0.5)
    d["ffn_w_in"] = nrm((DEPTH, D, 2 * FFN_DIM), D ** -0.5)
    d["ffn_conv_w"] = nrm((DEPTH, FFN_CONV, 2 * FFN_DIM), FFN_CONV ** -0.5)
    d["ffn_conv_b"] = nrm((DEPTH, 2 * FFN_DIM), 0.02)
    d["ffn_w_out"] = nrm((DEPTH, FFN_DIM, D), FFN_DIM ** -0.5)
    d["ssd_w_in"] = nrm((N_SSD, D, SSD_IN_DIM), D ** -0.5)
    d["ssd_conv_w"] = nrm((N_SSD, SSD_CONV, SSD_CONV_DIM), SSD_CONV ** -0.5)
    d["ssd_conv_b"] = nrm((N_SSD, SSD_CONV_DIM), 0.02)
    dt0 = jnp.exp(jax.random.uniform(next(keys), (N_SSD, SSD_HEADS), f32,
                                     minval=math_log(1e-3), maxval=math_log(1e-1)))
    d["ssd_dt_bias"] = dt0 + jnp.log(-jnp.expm1(-dt0))
    d["ssd_a_log"] = jnp.log(jax.random.uniform(next(keys), (N_SSD, SSD_HEADS), f32, minval=1.0, maxval=16.0))
    d["ssd_d"] = gain((N_SSD, SSD_HEADS))
    d["ssd_norm"] = gain((N_SSD, SSD_D_INNER))
    d["ssd_w_out"] = nrm((N_SSD, SSD_D_INNER, D), SSD_D_INNER ** -0.5)
    d["sg_w_in"] = nrm((N_SG, D, 2 * SG_WIDTH), D ** -0.5)
    d["sg_v_norm_g"] = gain((N_SG, SG_WIDTH))
    d["sg_v_norm_b"] = nrm((N_SG, SG_WIDTH), 0.02)
    d["sg_w_spatial"] = nrm((N_SG, SG_GROUPS, SG_CHUNK, SG_CHUNK), 0.5 * SG_CHUNK ** -0.5)
    d["sg_b_spatial"] = 1.0 + nrm((N_SG, SG_GROUPS, SG_CHUNK), 0.1)
    d["sg_w_out"] = nrm((N_SG, SG_WIDTH, D), SG_WIDTH ** -0.5)
    d["sb_w_qkv"] = nrm((N_SB, D, 3 * D), D ** -0.5)
    d["sb_w_out"] = nrm((N_SB, D, D), D ** -0.5)
    return d


def math_log(v):
    return float(np.log(v))


def _fwd_reference(x, mem, ln_mix_pre, ln_mix_post, ln_mem, ln_xa_pre, ln_xa_post, ln_ffn_pre, ln_ffn_post,
              xa_wq, xa_wkv, xa_wo, ffn_w_in, ffn_conv_w, ffn_conv_b, ffn_w_out,
              ssd_w_in, ssd_conv_w, ssd_conv_b, ssd_dt_bias, ssd_a_log, ssd_d, ssd_norm, ssd_w_out,
              sg_w_in, sg_v_norm_g, sg_v_norm_b, sg_w_spatial, sg_b_spatial, sg_w_out,
              sb_w_qkv, sb_w_out):
    for i in range(DEPTH):
        kind = i % N_MIXERS
        j = i // N_MIXERS
        hn = rmsnorm(x, ln_mix_pre[i])
        if kind == 0:
            m = ssd_mixer(hn, ssd_w_in[j], ssd_conv_w[j], ssd_conv_b[j], ssd_dt_bias[j],
                          ssd_a_log[j], ssd_d[j], ssd_norm[j], ssd_w_out[j])
        elif kind == 1:
            m = sgu_mixer(hn, sg_w_in[j], sg_v_norm_g[j], sg_v_norm_b[j],
                          sg_w_spatial[j], sg_b_spatial[j], sg_w_out[j])
        else:
            m = stick_breaking_mixer(hn, sb_w_qkv[j], sb_w_out[j])
        x = x + rmsnorm(m, ln_mix_post[i])
        mem_n = rmsnorm(mem, ln_mem[i])
        c = memory_cross_attention(rmsnorm(x, ln_xa_pre[i]), mem_n, xa_wq[i], xa_wkv[i], xa_wo[i])
        x = x + rmsnorm(c, ln_xa_post[i])
        f = conv_gated_ffn(rmsnorm(x, ln_ffn_pre[i]), ffn_w_in[i], ffn_conv_w[i], ffn_conv_b[i], ffn_w_out[i])
        x = x + rmsnorm(f, ln_ffn_post[i])
    return x


import jax as _jax
import jax.numpy as _jnp

TWIN_FORMAT = 'train_step'
FWD_PARAMS = ['x', 'mem', 'ln_mix_pre', 'ln_mix_post', 'ln_mem', 'ln_xa_pre', 'ln_xa_post', 'ln_ffn_pre', 'ln_ffn_post', 'xa_wq', 'xa_wkv', 'xa_wo', 'ffn_w_in', 'ffn_conv_w', 'ffn_conv_b', 'ffn_w_out', 'ssd_w_in', 'ssd_conv_w', 'ssd_conv_b', 'ssd_dt_bias', 'ssd_a_log', 'ssd_d', 'ssd_norm', 'ssd_w_out', 'sg_w_in', 'sg_v_norm_g', 'sg_v_norm_b', 'sg_w_spatial', 'sg_b_spatial', 'sg_w_out', 'sb_w_qkv', 'sb_w_out']
TWIN_WEIGHTS = ['ln_mix_pre', 'ln_mix_post', 'ln_mem', 'ln_xa_pre', 'ln_xa_post', 'ln_ffn_pre', 'ln_ffn_post', 'xa_wq', 'xa_wkv', 'xa_wo', 'ffn_w_in', 'ffn_conv_w', 'ffn_conv_b', 'ffn_w_out', 'ssd_w_in', 'ssd_conv_w', 'ssd_conv_b', 'ssd_dt_bias', 'ssd_a_log', 'ssd_d', 'ssd_norm', 'ssd_w_out', 'sg_w_in', 'sg_v_norm_g', 'sg_v_norm_b', 'sg_w_spatial', 'sg_b_spatial', 'sg_w_out', 'sb_w_qkv', 'sb_w_out']
TWIN_DIFF_INPUT = 'x'
TWIN_INPUTS = ['x', 'mem', 'ln_mix_pre', 'ln_mix_post', 'ln_mem', 'ln_xa_pre', 'ln_xa_post', 'ln_ffn_pre', 'ln_ffn_post', 'xa_wq', 'xa_wkv', 'xa_wo', 'ffn_w_in', 'ffn_conv_w', 'ffn_conv_b', 'ffn_w_out', 'ssd_w_in', 'ssd_conv_w', 'ssd_conv_b', 'ssd_dt_bias', 'ssd_a_log', 'ssd_d', 'ssd_norm', 'ssd_w_out', 'sg_w_in', 'sg_v_norm_g', 'sg_v_norm_b', 'sg_w_spatial', 'sg_b_spatial', 'sg_w_out', 'sb_w_qkv', 'sb_w_out', 'loss_target', 'm_ln_mix_pre', 'm_ln_mix_post', 'm_ln_mem', 'm_ln_xa_pre', 'm_ln_xa_post', 'm_ln_ffn_pre', 'm_ln_ffn_post', 'm_xa_wq', 'm_xa_wkv', 'm_xa_wo', 'm_ffn_w_in', 'm_ffn_conv_w', 'm_ffn_conv_b', 'm_ffn_w_out', 'm_ssd_w_in', 'm_ssd_conv_w', 'm_ssd_conv_b', 'm_ssd_dt_bias', 'm_ssd_a_log', 'm_ssd_d', 'm_ssd_norm', 'm_ssd_w_out', 'm_sg_w_in', 'm_sg_v_norm_g', 'm_sg_v_norm_b', 'm_sg_w_spatial', 'm_sg_b_spatial', 'm_sg_w_out', 'm_sb_w_qkv', 'm_sb_w_out', 'v_ln_mix_pre', 'v_ln_mix_post', 'v_ln_mem', 'v_ln_xa_pre', 'v_ln_xa_post', 'v_ln_ffn_pre', 'v_ln_ffn_post', 'v_xa_wq', 'v_xa_wkv', 'v_xa_wo', 'v_ffn_w_in', 'v_ffn_conv_w', 'v_ffn_conv_b', 'v_ffn_w_out', 'v_ssd_w_in', 'v_ssd_conv_w', 'v_ssd_conv_b', 'v_ssd_dt_bias', 'v_ssd_a_log', 'v_ssd_d', 'v_ssd_norm', 'v_ssd_w_out', 'v_sg_w_in', 'v_sg_v_norm_g', 'v_sg_v_norm_b', 'v_sg_w_spatial', 'v_sg_b_spatial', 'v_sg_w_out', 'v_sb_w_qkv', 'v_sb_w_out']
TWIN_OUTPUTS = ['loss', 'grad_x', 'grad_ln_mix_pre', 'grad_ln_mix_post', 'grad_ln_mem', 'grad_ln_xa_pre', 'grad_ln_xa_post', 'grad_ln_ffn_pre', 'grad_ln_ffn_post', 'grad_xa_wq', 'grad_xa_wkv', 'grad_xa_wo', 'grad_ffn_w_in', 'grad_ffn_conv_w', 'grad_ffn_conv_b', 'grad_ffn_w_out', 'grad_ssd_w_in', 'grad_ssd_conv_w', 'grad_ssd_conv_b', 'grad_ssd_dt_bias', 'grad_ssd_a_log', 'grad_ssd_d', 'grad_ssd_norm', 'grad_ssd_w_out', 'grad_sg_w_in', 'grad_sg_v_norm_g', 'grad_sg_v_norm_b', 'grad_sg_w_spatial', 'grad_sg_b_spatial', 'grad_sg_w_out', 'grad_sb_w_qkv', 'grad_sb_w_out', 'delta_ln_mix_pre', 'delta_ln_mix_post', 'delta_ln_mem', 'delta_ln_xa_pre', 'delta_ln_xa_post', 'delta_ln_ffn_pre', 'delta_ln_ffn_post', 'delta_xa_wq', 'delta_xa_wkv', 'delta_xa_wo', 'delta_ffn_w_in', 'delta_ffn_conv_w', 'delta_ffn_conv_b', 'delta_ffn_w_out', 'delta_ssd_w_in', 'delta_ssd_conv_w', 'delta_ssd_conv_b', 'delta_ssd_dt_bias', 'delta_ssd_a_log', 'delta_ssd_d', 'delta_ssd_norm', 'delta_ssd_w_out', 'delta_sg_w_in', 'delta_sg_v_norm_g', 'delta_sg_v_norm_b', 'delta_sg_w_spatial', 'delta_sg_b_spatial', 'delta_sg_w_out', 'delta_sb_w_qkv', 'delta_sb_w_out', 'new_m_ln_mix_pre', 'new_m_ln_mix_post', 'new_m_ln_mem', 'new_m_ln_xa_pre', 'new_m_ln_xa_post', 'new_m_ln_ffn_pre', 'new_m_ln_ffn_post', 'new_m_xa_wq', 'new_m_xa_wkv', 'new_m_xa_wo', 'new_m_ffn_w_in', 'new_m_ffn_conv_w', 'new_m_ffn_conv_b', 'new_m_ffn_w_out', 'new_m_ssd_w_in', 'new_m_ssd_conv_w', 'new_m_ssd_conv_b', 'new_m_ssd_dt_bias', 'new_m_ssd_a_log', 'new_m_ssd_d', 'new_m_ssd_norm', 'new_m_ssd_w_out', 'new_m_sg_w_in', 'new_m_sg_v_norm_g', 'new_m_sg_v_norm_b', 'new_m_sg_w_spatial', 'new_m_sg_b_spatial', 'new_m_sg_w_out', 'new_m_sb_w_qkv', 'new_m_sb_w_out', 'new_v_ln_mix_pre', 'new_v_ln_mix_post', 'new_v_ln_mem', 'new_v_ln_xa_pre', 'new_v_ln_xa_post', 'new_v_ln_ffn_pre', 'new_v_ln_ffn_post', 'new_v_xa_wq', 'new_v_xa_wkv', 'new_v_xa_wo', 'new_v_ffn_w_in', 'new_v_ffn_conv_w', 'new_v_ffn_conv_b', 'new_v_ffn_w_out', 'new_v_ssd_w_in', 'new_v_ssd_conv_w', 'new_v_ssd_conv_b', 'new_v_ssd_dt_bias', 'new_v_ssd_a_log', 'new_v_ssd_d', 'new_v_ssd_norm', 'new_v_ssd_w_out', 'new_v_sg_w_in', 'new_v_sg_v_norm_g', 'new_v_sg_v_norm_b', 'new_v_sg_w_spatial', 'new_v_sg_b_spatial', 'new_v_sg_w_out', 'new_v_sb_w_qkv', 'new_v_sb_w_out']
TWIN_LEAF_KINDS = {'loss': 'loss', 'grad_x': 'grad_x', 'grad_ln_mix_pre': 'grad_w', 'grad_ln_mix_post': 'grad_w', 'grad_ln_mem': 'grad_w', 'grad_ln_xa_pre': 'grad_w', 'grad_ln_xa_post': 'grad_w', 'grad_ln_ffn_pre': 'grad_w', 'grad_ln_ffn_post': 'grad_w', 'grad_xa_wq': 'grad_w', 'grad_xa_wkv': 'grad_w', 'grad_xa_wo': 'grad_w', 'grad_ffn_w_in': 'grad_w', 'grad_ffn_conv_w': 'grad_w', 'grad_ffn_conv_b': 'grad_w', 'grad_ffn_w_out': 'grad_w', 'grad_ssd_w_in': 'grad_w', 'grad_ssd_conv_w': 'grad_w', 'grad_ssd_conv_b': 'grad_w', 'grad_ssd_dt_bias': 'grad_w', 'grad_ssd_a_log': 'grad_w', 'grad_ssd_d': 'grad_w', 'grad_ssd_norm': 'grad_w', 'grad_ssd_w_out': 'grad_w', 'grad_sg_w_in': 'grad_w', 'grad_sg_v_norm_g': 'grad_w', 'grad_sg_v_norm_b': 'grad_w', 'grad_sg_w_spatial': 'grad_w', 'grad_sg_b_spatial': 'grad_w', 'grad_sg_w_out': 'grad_w', 'grad_sb_w_qkv': 'grad_w', 'grad_sb_w_out': 'grad_w', 'delta_ln_mix_pre': 'delta_w', 'delta_ln_mix_post': 'delta_w', 'delta_ln_mem': 'delta_w', 'delta_ln_xa_pre': 'delta_w', 'delta_ln_xa_post': 'delta_w', 'delta_ln_ffn_pre': 'delta_w', 'delta_ln_ffn_post': 'delta_w', 'delta_xa_wq': 'delta_w', 'delta_xa_wkv': 'delta_w', 'delta_xa_wo': 'delta_w', 'delta_ffn_w_in': 'delta_w', 'delta_ffn_conv_w': 'delta_w', 'delta_ffn_conv_b': 'delta_w', 'delta_ffn_w_out': 'delta_w', 'delta_ssd_w_in': 'delta_w', 'delta_ssd_conv_w': 'delta_w', 'delta_ssd_conv_b': 'delta_w', 'delta_ssd_dt_bias': 'delta_w', 'delta_ssd_a_log': 'delta_w', 'delta_ssd_d': 'delta_w', 'delta_ssd_norm': 'delta_w', 'delta_ssd_w_out': 'delta_w', 'delta_sg_w_in': 'delta_w', 'delta_sg_v_norm_g': 'delta_w', 'delta_sg_v_norm_b': 'delta_w', 'delta_sg_w_spatial': 'delta_w', 'delta_sg_b_spatial': 'delta_w', 'delta_sg_w_out': 'delta_w', 'delta_sb_w_qkv': 'delta_w', 'delta_sb_w_out': 'delta_w', 'new_m_ln_mix_pre': 'new_m', 'new_m_ln_mix_post': 'new_m', 'new_m_ln_mem': 'new_m', 'new_m_ln_xa_pre': 'new_m', 'new_m_ln_xa_post': 'new_m', 'new_m_ln_ffn_pre': 'new_m', 'new_m_ln_ffn_post': 'new_m', 'new_m_xa_wq': 'new_m', 'new_m_xa_wkv': 'new_m', 'new_m_xa_wo': 'new_m', 'new_m_ffn_w_in': 'new_m', 'new_m_ffn_conv_w': 'new_m', 'new_m_ffn_conv_b': 'new_m', 'new_m_ffn_w_out': 'new_m', 'new_m_ssd_w_in': 'new_m', 'new_m_ssd_conv_w': 'new_m', 'new_m_ssd_conv_b': 'new_m', 'new_m_ssd_dt_bias': 'new_m', 'new_m_ssd_a_log': 'new_m', 'new_m_ssd_d': 'new_m', 'new_m_ssd_norm': 'new_m', 'new_m_ssd_w_out': 'new_m', 'new_m_sg_w_in': 'new_m', 'new_m_sg_v_norm_g': 'new_m', 'new_m_sg_v_norm_b': 'new_m', 'new_m_sg_w_spatial': 'new_m', 'new_m_sg_b_spatial': 'new_m', 'new_m_sg_w_out': 'new_m', 'new_m_sb_w_qkv': 'new_m', 'new_m_sb_w_out': 'new_m', 'new_v_ln_mix_pre': 'new_v', 'new_v_ln_mix_post': 'new_v', 'new_v_ln_mem': 'new_v', 'new_v_ln_xa_pre': 'new_v', 'new_v_ln_xa_post': 'new_v', 'new_v_ln_ffn_pre': 'new_v', 'new_v_ln_ffn_post': 'new_v', 'new_v_xa_wq': 'new_v', 'new_v_xa_wkv': 'new_v', 'new_v_xa_wo': 'new_v', 'new_v_ffn_w_in': 'new_v', 'new_v_ffn_conv_w': 'new_v', 'new_v_ffn_conv_b': 'new_v', 'new_v_ffn_w_out': 'new_v', 'new_v_ssd_w_in': 'new_v', 'new_v_ssd_conv_w': 'new_v', 'new_v_ssd_conv_b': 'new_v', 'new_v_ssd_dt_bias': 'new_v', 'new_v_ssd_a_log': 'new_v', 'new_v_ssd_d': 'new_v', 'new_v_ssd_norm': 'new_v', 'new_v_ssd_w_out': 'new_v', 'new_v_sg_w_in': 'new_v', 'new_v_sg_v_norm_g': 'new_v', 'new_v_sg_v_norm_b': 'new_v', 'new_v_sg_w_spatial': 'new_v', 'new_v_sg_b_spatial': 'new_v', 'new_v_sg_w_out': 'new_v', 'new_v_sb_w_qkv': 'new_v', 'new_v_sb_w_out': 'new_v'}


def _forward(args):
    return _fwd_reference(*[args[k] for k in FWD_PARAMS])


def _output_shape():
    out = _jax.eval_shape(lambda: _forward(_fwd_setup_inputs(0)))
    return out.shape, out.dtype

N_MICROBATCH = 1
ADAM_LR = 0.001
ADAM_B1 = 0.9
ADAM_B2 = 0.999
ADAM_EPS = 1e-08
ADAM_WD = 0.01
ADAM_STEP = 10
PER_EXAMPLE_BATCH_AXIS = {'x': 0, 'mem': 0, 'loss_target': 0}
SHARED_INPUTS = []
_WEIGHT_DTYPES = {'ln_mix_pre': _jnp.float32, 'ln_mix_post': _jnp.float32, 'ln_mem': _jnp.float32, 'ln_xa_pre': _jnp.float32, 'ln_xa_post': _jnp.float32, 'ln_ffn_pre': _jnp.float32, 'ln_ffn_post': _jnp.float32, 'xa_wq': _jnp.float32, 'xa_wkv': _jnp.float32, 'xa_wo': _jnp.float32, 'ffn_w_in': _jnp.float32, 'ffn_conv_w': _jnp.float32, 'ffn_conv_b': _jnp.float32, 'ffn_w_out': _jnp.float32, 'ssd_w_in': _jnp.float32, 'ssd_conv_w': _jnp.float32, 'ssd_conv_b': _jnp.float32, 'ssd_dt_bias': _jnp.float32, 'ssd_a_log': _jnp.float32, 'ssd_d': _jnp.float32, 'ssd_norm': _jnp.float32, 'ssd_w_out': _jnp.float32, 'sg_w_in': _jnp.float32, 'sg_v_norm_g': _jnp.float32, 'sg_v_norm_b': _jnp.float32, 'sg_w_spatial': _jnp.float32, 'sg_b_spatial': _jnp.float32, 'sg_w_out': _jnp.float32, 'sb_w_qkv': _jnp.float32, 'sb_w_out': _jnp.float32}
MOMENT_SCALE = {'ln_mix_pre': 3.902777e+00, 'ln_mix_post': 1.756148e+01, 'ln_mem': 9.972660e+00, 'ln_xa_pre': 2.664239e+00, 'ln_xa_post': 1.914427e+01, 'ln_ffn_pre': 3.492630e+00, 'ln_ffn_post': 1.630570e+01, 'xa_wq': 5.388421e+00, 'xa_wkv': 1.404652e+01, 'xa_wo': 9.385488e+00, 'ffn_w_in': 1.446616e+00, 'ffn_conv_w': 1.688811e+00, 'ffn_conv_b': 4.451741e+00, 'ffn_w_out': 3.105345e+00, 'ssd_w_in': 1.160224e+00, 'ssd_conv_w': 1.644679e+00, 'ssd_conv_b': 4.357662e+00, 'ssd_dt_bias': 2.024130e+00, 'ssd_a_log': 8.113524e+00, 'ssd_d': 9.469636e+00, 'ssd_norm': 2.592579e+00, 'ssd_w_out': 3.678570e+00, 'sg_w_in': 1.771630e+00, 'sg_v_norm_g': 1.546686e-01, 'sg_v_norm_b': 1.823020e-01, 'sg_w_spatial': 4.377301e-01, 'sg_b_spatial': 7.192892e-01, 'sg_w_out': 8.456487e+00, 'sb_w_qkv': 4.390622e+00, 'sb_w_out': 7.628012e+00}


def _to_microbatches(a, axis):
    t = _jnp.moveaxis(a, axis, 0)
    t = t.reshape((N_MICROBATCH, t.shape[0] // N_MICROBATCH) + t.shape[1:])
    return _jnp.moveaxis(t, 1, axis + 1)


def setup_inputs(seed: int = 0) -> dict:
    inp = _fwd_setup_inputs(seed)
    key = _jax.random.fold_in(_jax.random.key(seed), 7919)
    shape, _ = _output_shape()
    out = dict(inp)
    out["loss_target"] = _jax.random.normal(_jax.random.fold_in(key, 0), shape, _jnp.float32)
    for i, name in enumerate(TWIN_WEIGHTS):
        w = inp[name].astype(_jnp.float32)
        if MOMENT_SCALE is None:
            s = _jnp.sqrt(_jnp.mean(_jnp.square(w)) + 1e-30)
        else:
            s = MOMENT_SCALE[name]
        km, kv = _jax.random.split(_jax.random.fold_in(key, i + 1))
        out[name] = w
        out["m_" + name] = s * _jax.random.normal(km, w.shape, _jnp.float32)
        out["v_" + name] = (s * s) * _jax.random.uniform(kv, w.shape, _jnp.float32, 0.5, 1.5)
    if N_MICROBATCH > 1:
        for name, axis in PER_EXAMPLE_BATCH_AXIS.items():
            out[name] = _to_microbatches(out[name], axis)
    return {'x': out['x'], 'mem': out['mem'], 'ln_mix_pre': out['ln_mix_pre'], 'ln_mix_post': out['ln_mix_post'], 'ln_mem': out['ln_mem'], 'ln_xa_pre': out['ln_xa_pre'], 'ln_xa_post': out['ln_xa_post'], 'ln_ffn_pre': out['ln_ffn_pre'], 'ln_ffn_post': out['ln_ffn_post'], 'xa_wq': out['xa_wq'], 'xa_wkv': out['xa_wkv'], 'xa_wo': out['xa_wo'], 'ffn_w_in': out['ffn_w_in'], 'ffn_conv_w': out['ffn_conv_w'], 'ffn_conv_b': out['ffn_conv_b'], 'ffn_w_out': out['ffn_w_out'], 'ssd_w_in': out['ssd_w_in'], 'ssd_conv_w': out['ssd_conv_w'], 'ssd_conv_b': out['ssd_conv_b'], 'ssd_dt_bias': out['ssd_dt_bias'], 'ssd_a_log': out['ssd_a_log'], 'ssd_d': out['ssd_d'], 'ssd_norm': out['ssd_norm'], 'ssd_w_out': out['ssd_w_out'], 'sg_w_in': out['sg_w_in'], 'sg_v_norm_g': out['sg_v_norm_g'], 'sg_v_norm_b': out['sg_v_norm_b'], 'sg_w_spatial': out['sg_w_spatial'], 'sg_b_spatial': out['sg_b_spatial'], 'sg_w_out': out['sg_w_out'], 'sb_w_qkv': out['sb_w_qkv'], 'sb_w_out': out['sb_w_out'], 'loss_target': out['loss_target'], 'm_ln_mix_pre': out['m_ln_mix_pre'], 'm_ln_mix_post': out['m_ln_mix_post'], 'm_ln_mem': out['m_ln_mem'], 'm_ln_xa_pre': out['m_ln_xa_pre'], 'm_ln_xa_post': out['m_ln_xa_post'], 'm_ln_ffn_pre': out['m_ln_ffn_pre'], 'm_ln_ffn_post': out['m_ln_ffn_post'], 'm_xa_wq': out['m_xa_wq'], 'm_xa_wkv': out['m_xa_wkv'], 'm_xa_wo': out['m_xa_wo'], 'm_ffn_w_in': out['m_ffn_w_in'], 'm_ffn_conv_w': out['m_ffn_conv_w'], 'm_ffn_conv_b': out['m_ffn_conv_b'], 'm_ffn_w_out': out['m_ffn_w_out'], 'm_ssd_w_in': out['m_ssd_w_in'], 'm_ssd_conv_w': out['m_ssd_conv_w'], 'm_ssd_conv_b': out['m_ssd_conv_b'], 'm_ssd_dt_bias': out['m_ssd_dt_bias'], 'm_ssd_a_log': out['m_ssd_a_log'], 'm_ssd_d': out['m_ssd_d'], 'm_ssd_norm': out['m_ssd_norm'], 'm_ssd_w_out': out['m_ssd_w_out'], 'm_sg_w_in': out['m_sg_w_in'], 'm_sg_v_norm_g': out['m_sg_v_norm_g'], 'm_sg_v_norm_b': out['m_sg_v_norm_b'], 'm_sg_w_spatial': out['m_sg_w_spatial'], 'm_sg_b_spatial': out['m_sg_b_spatial'], 'm_sg_w_out': out['m_sg_w_out'], 'm_sb_w_qkv': out['m_sb_w_qkv'], 'm_sb_w_out': out['m_sb_w_out'], 'v_ln_mix_pre': out['v_ln_mix_pre'], 'v_ln_mix_post': out['v_ln_mix_post'], 'v_ln_mem': out['v_ln_mem'], 'v_ln_xa_pre': out['v_ln_xa_pre'], 'v_ln_xa_post': out['v_ln_xa_post'], 'v_ln_ffn_pre': out['v_ln_ffn_pre'], 'v_ln_ffn_post': out['v_ln_ffn_post'], 'v_xa_wq': out['v_xa_wq'], 'v_xa_wkv': out['v_xa_wkv'], 'v_xa_wo': out['v_xa_wo'], 'v_ffn_w_in': out['v_ffn_w_in'], 'v_ffn_conv_w': out['v_ffn_conv_w'], 'v_ffn_conv_b': out['v_ffn_conv_b'], 'v_ffn_w_out': out['v_ffn_w_out'], 'v_ssd_w_in': out['v_ssd_w_in'], 'v_ssd_conv_w': out['v_ssd_conv_w'], 'v_ssd_conv_b': out['v_ssd_conv_b'], 'v_ssd_dt_bias': out['v_ssd_dt_bias'], 'v_ssd_a_log': out['v_ssd_a_log'], 'v_ssd_d': out['v_ssd_d'], 'v_ssd_norm': out['v_ssd_norm'], 'v_ssd_w_out': out['v_ssd_w_out'], 'v_sg_w_in': out['v_sg_w_in'], 'v_sg_v_norm_g': out['v_sg_v_norm_g'], 'v_sg_v_norm_b': out['v_sg_v_norm_b'], 'v_sg_w_spatial': out['v_sg_w_spatial'], 'v_sg_b_spatial': out['v_sg_b_spatial'], 'v_sg_w_out': out['v_sg_w_out'], 'v_sb_w_qkv': out['v_sb_w_qkv'], 'v_sb_w_out': out['v_sb_w_out']}


def _loss(weights, diff, rest, loss_target):
    with _jax.named_scope("forward"):
        args = {**rest, TWIN_DIFF_INPUT: diff, **{k: w.astype(_WEIGHT_DTYPES[k]) for k, w in weights.items()}}
        y = _forward(args)
    with _jax.named_scope("loss_head"):
        err = _jnp.square(y.astype(_jnp.float32) - loss_target)
        return 0.5 * _jnp.sum(_jnp.mean(err, axis=-1)) if err.ndim else 0.5 * err


def _adamw(w, g, m, v):
    m = ADAM_B1 * m + (1.0 - ADAM_B1) * g
    v = ADAM_B2 * v + (1.0 - ADAM_B2) * _jnp.square(g)
    m_hat = m / (1.0 - ADAM_B1 ** ADAM_STEP)
    v_hat = v / (1.0 - ADAM_B2 ** ADAM_STEP)
    delta = -ADAM_LR * (m_hat / (_jnp.sqrt(v_hat) + ADAM_EPS) + ADAM_WD * w)
    return delta, m, v


def reference(x, mem, ln_mix_pre, ln_mix_post, ln_mem, ln_xa_pre, ln_xa_post, ln_ffn_pre, ln_ffn_post, xa_wq, xa_wkv, xa_wo, ffn_w_in, ffn_conv_w, ffn_conv_b, ffn_w_out, ssd_w_in, ssd_conv_w, ssd_conv_b, ssd_dt_bias, ssd_a_log, ssd_d, ssd_norm, ssd_w_out, sg_w_in, sg_v_norm_g, sg_v_norm_b, sg_w_spatial, sg_b_spatial, sg_w_out, sb_w_qkv, sb_w_out, loss_target, m_ln_mix_pre, m_ln_mix_post, m_ln_mem, m_ln_xa_pre, m_ln_xa_post, m_ln_ffn_pre, m_ln_ffn_post, m_xa_wq, m_xa_wkv, m_xa_wo, m_ffn_w_in, m_ffn_conv_w, m_ffn_conv_b, m_ffn_w_out, m_ssd_w_in, m_ssd_conv_w, m_ssd_conv_b, m_ssd_dt_bias, m_ssd_a_log, m_ssd_d, m_ssd_norm, m_ssd_w_out, m_sg_w_in, m_sg_v_norm_g, m_sg_v_norm_b, m_sg_w_spatial, m_sg_b_spatial, m_sg_w_out, m_sb_w_qkv, m_sb_w_out, v_ln_mix_pre, v_ln_mix_post, v_ln_mem, v_ln_xa_pre, v_ln_xa_post, v_ln_ffn_pre, v_ln_ffn_post, v_xa_wq, v_xa_wkv, v_xa_wo, v_ffn_w_in, v_ffn_conv_w, v_ffn_conv_b, v_ffn_w_out, v_ssd_w_in, v_ssd_conv_w, v_ssd_conv_b, v_ssd_dt_bias, v_ssd_a_log, v_ssd_d, v_ssd_norm, v_ssd_w_out, v_sg_w_in, v_sg_v_norm_g, v_sg_v_norm_b, v_sg_w_spatial, v_sg_b_spatial, v_sg_w_out, v_sb_w_qkv, v_sb_w_out):
    given = dict(x=x, mem=mem, ln_mix_pre=ln_mix_pre, ln_mix_post=ln_mix_post, ln_mem=ln_mem, ln_xa_pre=ln_xa_pre, ln_xa_post=ln_xa_post, ln_ffn_pre=ln_ffn_pre, ln_ffn_post=ln_ffn_post, xa_wq=xa_wq, xa_wkv=xa_wkv, xa_wo=xa_wo, ffn_w_in=ffn_w_in, ffn_conv_w=ffn_conv_w, ffn_conv_b=ffn_conv_b, ffn_w_out=ffn_w_out, ssd_w_in=ssd_w_in, ssd_conv_w=ssd_conv_w, ssd_conv_b=ssd_conv_b, ssd_dt_bias=ssd_dt_bias, ssd_a_log=ssd_a_log, ssd_d=ssd_d, ssd_norm=ssd_norm, ssd_w_out=ssd_w_out, sg_w_in=sg_w_in, sg_v_norm_g=sg_v_norm_g, sg_v_norm_b=sg_v_norm_b, sg_w_spatial=sg_w_spatial, sg_b_spatial=sg_b_spatial, sg_w_out=sg_w_out, sb_w_qkv=sb_w_qkv, sb_w_out=sb_w_out, loss_target=loss_target, m_ln_mix_pre=m_ln_mix_pre, m_ln_mix_post=m_ln_mix_post, m_ln_mem=m_ln_mem, m_ln_xa_pre=m_ln_xa_pre, m_ln_xa_post=m_ln_xa_post, m_ln_ffn_pre=m_ln_ffn_pre, m_ln_ffn_post=m_ln_ffn_post, m_xa_wq=m_xa_wq, m_xa_wkv=m_xa_wkv, m_xa_wo=m_xa_wo, m_ffn_w_in=m_ffn_w_in, m_ffn_conv_w=m_ffn_conv_w, m_ffn_conv_b=m_ffn_conv_b, m_ffn_w_out=m_ffn_w_out, m_ssd_w_in=m_ssd_w_in, m_ssd_conv_w=m_ssd_conv_w, m_ssd_conv_b=m_ssd_conv_b, m_ssd_dt_bias=m_ssd_dt_bias, m_ssd_a_log=m_ssd_a_log, m_ssd_d=m_ssd_d, m_ssd_norm=m_ssd_norm, m_ssd_w_out=m_ssd_w_out, m_sg_w_in=m_sg_w_in, m_sg_v_norm_g=m_sg_v_norm_g, m_sg_v_norm_b=m_sg_v_norm_b, m_sg_w_spatial=m_sg_w_spatial, m_sg_b_spatial=m_sg_b_spatial, m_sg_w_out=m_sg_w_out, m_sb_w_qkv=m_sb_w_qkv, m_sb_w_out=m_sb_w_out, v_ln_mix_pre=v_ln_mix_pre, v_ln_mix_post=v_ln_mix_post, v_ln_mem=v_ln_mem, v_ln_xa_pre=v_ln_xa_pre, v_ln_xa_post=v_ln_xa_post, v_ln_ffn_pre=v_ln_ffn_pre, v_ln_ffn_post=v_ln_ffn_post, v_xa_wq=v_xa_wq, v_xa_wkv=v_xa_wkv, v_xa_wo=v_xa_wo, v_ffn_w_in=v_ffn_w_in, v_ffn_conv_w=v_ffn_conv_w, v_ffn_conv_b=v_ffn_conv_b, v_ffn_w_out=v_ffn_w_out, v_ssd_w_in=v_ssd_w_in, v_ssd_conv_w=v_ssd_conv_w, v_ssd_conv_b=v_ssd_conv_b, v_ssd_dt_bias=v_ssd_dt_bias, v_ssd_a_log=v_ssd_a_log, v_ssd_d=v_ssd_d, v_ssd_norm=v_ssd_norm, v_ssd_w_out=v_ssd_w_out, v_sg_w_in=v_sg_w_in, v_sg_v_norm_g=v_sg_v_norm_g, v_sg_v_norm_b=v_sg_v_norm_b, v_sg_w_spatial=v_sg_w_spatial, v_sg_b_spatial=v_sg_b_spatial, v_sg_w_out=v_sg_w_out, v_sb_w_qkv=v_sb_w_qkv, v_sb_w_out=v_sb_w_out)
    weights = {n: given[n] for n in TWIN_WEIGHTS}
    shared = {n: given[n] for n in SHARED_INPUTS}
    per_example = {n: given[n] for n in ['x', 'mem']}
    grad_fn = _jax.value_and_grad(_loss, argnums=(0, 1))

    def one_microbatch(ex, loss_target):
        ex = dict(ex)
        diff = ex.pop(TWIN_DIFF_INPUT)
        return grad_fn(weights, diff, {**shared, **ex}, loss_target)

    if N_MICROBATCH == 1:
        loss, (grad_w, grad_x) = one_microbatch(per_example, given["loss_target"])
    else:
        def body(carry, xs):
            loss_sum, grad_sum = carry
            l_k, (gw_k, gx_k) = one_microbatch(xs[0], xs[1])
            with _jax.named_scope("update"):
                return (loss_sum + l_k, _jax.tree.map(_jnp.add, grad_sum, gw_k)), gx_k

        init = (_jnp.zeros((), _jnp.float32), _jax.tree.map(_jnp.zeros_like, weights))
        (loss, grad_w), grad_x = _jax.lax.scan(body, init, (per_example, given["loss_target"]))
    with _jax.named_scope("update"):
        delta_w, new_m, new_v = {}, {}, {}
        for n in TWIN_WEIGHTS:
            delta_w[n], new_m[n], new_v[n] = _adamw(weights[n], grad_w[n], given["m_" + n], given["v_" + n])
    return (loss, grad_x, *[grad_w[n] for n in TWIN_WEIGHTS], *[delta_w[n] for n in TWIN_WEIGHTS],
            *[new_m[n] for n in TWIN_WEIGHTS], *[new_v[n] for n in TWIN_WEIGHTS])
```

```python
import functools
import math

import jax
import jax.numpy as jnp
from jax import lax
from jax.experimental import pallas as pl
from jax.experimental.pallas import tpu as pltpu

F32 = jnp.float32
BF16 = jnp.bfloat16
EPS = 1e-6
LANES = 128
VMEM_LIMIT = 56 * 1024 * 1024
CHUNK = 128
SSD_HEAD_DIM = 64
SSD_STATE = 128
N_CHIPS = 4
MESH = pl.DeviceIdType.MESH
ANY = pl.BlockSpec(memory_space=pl.ANY)

ADAM_LR, ADAM_B1, ADAM_B2, ADAM_EPS, ADAM_WD, ADAM_STEP = 0.001, 0.9, 0.999, 1e-08, 0.01, 10


def _params(sem):
    return pltpu.CompilerParams(dimension_semantics=sem, vmem_limit_bytes=VMEM_LIMIT)


def _sds(shape, dtype):
    return jax.ShapeDtypeStruct(tuple(shape), dtype)


def _tile(n, pref):
    if n <= pref:
        return n
    t = (pref // LANES) * LANES
    while t > LANES and n % t:
        t -= LANES
    assert n % t == 0, (n, pref)
    return t


def _split3(a):
    a1 = a.astype(BF16)
    r = a - a1.astype(F32)
    a2 = r.astype(BF16)
    a3 = (r - a2.astype(F32)).astype(BF16)
    return a1, a2, a3


def _dot(a, b, dims=(((1,), (0,)), ((), ()))):
    return lax.dot_general(a, b, dims, preferred_element_type=F32)


NN = (((1,), (0,)), ((), ()))
NT = (((1,), (1,)), ((), ()))
TN = (((0,), (0,)), ((), ()))


def _dot3r(a, m):
    p1, p2, p3 = _split3(a)
    return _dot(p1, m) + _dot(p2, m) + _dot(p3, m)


def _dot3l(m, a, dims=NN):
    p1, p2, p3 = _split3(a)
    return _dot(m, p1, dims) + _dot(m, p2, dims) + _dot(m, p3, dims)


def _iota(shape, dim):
    return lax.broadcasted_iota(jnp.int32, shape, dim)


def _tri(n, kind):
    r, c = _iota((n, n), 0), _iota((n, n), 1)
    return {"le": c <= r, "lt": c < r, "ge": c >= r, "gt": c > r}[kind]


def _sigmoid(x):
    return 1.0 / (1.0 + jnp.exp(-x))


def _silu(x):
    return x * _sigmoid(x)


def _dsilu(x):
    s = _sigmoid(x)
    return s * (1.0 + x * (1.0 - s))


_GC = math.sqrt(2.0 / math.pi)


def _gelu(x):
    return 0.5 * x * (1.0 + jnp.tanh(_GC * (x + 0.044715 * x * x * x)))


def _dgelu(x):
    th = jnp.tanh(_GC * (x + 0.044715 * x * x * x))
    return 0.5 * (1.0 + th) + 0.5 * x * (1.0 - th * th) * _GC * (1.0 + 3.0 * 0.044715 * x * x)


def _softplus(x):
    return jnp.maximum(x, 0.0) + jnp.log(1.0 + jnp.exp(-jnp.abs(x)))


def _shift_down(p, s):
    rows = _iota(p.shape, 0)
    return jnp.where(rows >= s, pltpu.roll(p, s, 0), 0.0)


def _shift_up(p, s):
    n = p.shape[0]
    rows = _iota(p.shape, 0)
    return jnp.where(rows < n - s, pltpu.roll(p, n - s, 0), 0.0)


def _mm(name, mode, a, b, out_sds, grid, a_spec, b_spec, o_spec, acc_shape, add=None, add_spec=None):
    dims = {"nn": NN, "nt": NT, "tn": TN}[mode]
    nk = grid[2]
    has_add = add is not None

    def body(*refs):
        if has_add:
            a_ref, b_ref, c_ref, o_ref = refs[:4]
        else:
            a_ref, b_ref, o_ref = refs[:3]
            c_ref = None
        part = lax.dot_general(a_ref[...], b_ref[...], dims, preferred_element_type=F32)

        def finish(r):
            if c_ref is not None:
                r = r + c_ref[...].astype(F32)
            o_ref[...] = r.astype(o_ref.dtype)

        if nk == 1:
            finish(part)
        else:
            acc = refs[-1]
            k = pl.program_id(2)

            @pl.when(k == 0)
            def _():
                acc[...] = part

            @pl.when(k > 0)
            def _():
                acc[...] += part

            @pl.when(k == nk - 1)
            def _():
                finish(acc[...])

    in_specs = [a_spec, b_spec] + ([add_spec] if has_add else [])
    args = (a, b) + ((add,) if has_add else ())
    return pl.pallas_call(
        body, name=name, grid=grid, in_specs=in_specs, out_specs=o_spec, out_shape=out_sds,
        scratch_shapes=[pltpu.VMEM(acc_shape, F32)] if nk > 1 else [],
        compiler_params=_params(("parallel", "parallel", "arbitrary")),
    )(*args)


class W:
    def __init__(self, kind, arr, layer):
        self.kind, self.arr, self.layer = kind, arr, layer
        if kind == "cols":
            s, _, k, c = arr.shape
            self.K, self.N, self.S, self.C = k, s * c, s, c
        elif kind == "rows":
            s, _, r, n = arr.shape
            self.K, self.N, self.S, self.R = s * r, n, s, r
        else:
            _, k, n = arr.shape
            self.K, self.N = k, n


def mm_fwd(name, a, w, out_dtype, tm=1024, tn=512, a_spec=None, out_sds=None, o_spec=None, add=None):
    M = a.shape[0]
    tm = min(tm, M)
    l = w.layer
    if w.kind == "cols":
        tn = _tile(w.C, tn)
        nps = w.C // tn
        tk, nk = w.K, 1
        b_spec = pl.BlockSpec((None, None, tk, tn), lambda i, j, k: (j // nps, l, 0, j % nps))
    elif w.kind == "rows":
        tn = _tile(w.N, tn)
        tk, nk = w.R, w.S
        b_spec = pl.BlockSpec((None, None, tk, tn), lambda i, j, k: (k, l, 0, j))
    else:
        tn = _tile(w.N, tn)
        tk, nk = w.K, 1
        b_spec = pl.BlockSpec((None, tk, tn), lambda i, j, k: (l, 0, j))
    grid = (M // tm, w.N // tn, nk)
    if a_spec is None:
        a_spec = pl.BlockSpec((tm, tk), lambda i, j, k: (i, k))
    if out_sds is None:
        out_sds = _sds((M, w.N), out_dtype)
        o_spec = pl.BlockSpec((tm, tn), lambda i, j, k: (i, j))
    else:
        o_spec = o_spec(tm, tn)
    add_spec = pl.BlockSpec((tm, tn), lambda i, j, k: (i, j)) if add is not None else None
    return _mm(name, "nn", a, w.arr, out_sds, grid, a_spec, b_spec, o_spec, (tm, tn), add, add_spec)


def mm_dx(name, dy, w, out_dtype, tm=1024, tn=1024, a_spec=None, add=None):
    M = dy.shape[-2]
    tm = min(tm, M)
    l = w.layer
    if w.kind == "cols":
        tn = _tile(w.K, tn)
        tk, nk = w.C, w.S
        b_spec = pl.BlockSpec((None, None, tn, tk), lambda i, j, k: (k, l, j, 0))
    elif w.kind == "rows":
        tn = _tile(w.R, tn)
        npr = w.R // tn
        tk, nk = w.N, 1
        b_spec = pl.BlockSpec((None, None, tn, tk), lambda i, j, k: (j // npr, l, j % npr, 0))
    else:
        tn = _tile(w.K, tn)
        tk, nk = _tile(w.N, 2048), w.N // _tile(w.N, 2048)
        b_spec = pl.BlockSpec((None, tn, tk), lambda i, j, k: (l, j, k))
    grid = (M // tm, w.K // tn, nk)
    if a_spec is None:
        a_spec = pl.BlockSpec((tm, tk), lambda i, j, k: (i, k))
    else:
        a_spec = a_spec(tm, tk)
    out_sds = _sds((M, w.K), out_dtype)
    o_spec = pl.BlockSpec((tm, tn), lambda i, j, k: (i, j))
    add_spec = o_spec if add is not None else None
    return _mm(name, "nt", dy, w.arr, out_sds, grid, a_spec, b_spec, o_spec, (tm, tn), add, add_spec)


def mm_dw(name, a, dy, out_dtype, n_shards=None, tm=512, tn=512, b_spec=None, N=None):
    T, K = a.shape
    N = dy.shape[-1] if N is None else N
    tm = _tile(K, tm)
    if n_shards:
        C = N // n_shards
        tn = _tile(C, tn)
        nps = C // tn
        out_sds = _sds((n_shards, K, C), out_dtype)
        o_spec = pl.BlockSpec((None, tm, tn), lambda i, j, k: (j // nps, i, j % nps))
    else:
        tn = _tile(N, tn)
        out_sds = _sds((K, N), out_dtype)
        o_spec = pl.BlockSpec((tm, tn), lambda i, j, k: (i, j))
    grid = (K // tm, N // tn, 1)
    a_spec = pl.BlockSpec((T, tm), lambda i, j, k: (0, i))
    if b_spec is None:
        b_spec = pl.BlockSpec((T, tn), lambda i, j, k: (0, j))
    else:
        b_spec = b_spec(T, tn)
    return _mm(name, "tn", a, dy, out_sds, grid, a_spec, b_spec, o_spec, (tm, tn))


def _rms(x, g):
    r = lax.rsqrt(jnp.mean(x * x, axis=-1, keepdims=True) + EPS)
    return x * r * g


def rms_fwd(name, x, g, tr=512):
    T, D = x.shape
    tr = min(tr, T)

    def body(x_ref, g_ref, o_ref):
        o_ref[...] = _rms(x_ref[...], g_ref[...]).astype(o_ref.dtype)

    row = pl.BlockSpec((tr, D), lambda i: (i, 0))
    vec = pl.BlockSpec((1, D), lambda i: (0, 0))
    return pl.pallas_call(body, name=name, grid=(T // tr,), in_specs=[row, vec], out_specs=row,
                          out_shape=_sds((T, D), BF16), compiler_params=_params(("parallel",)))(x, g)


def resid_norm(name, x, m, g_post, g_next, tr=512):
    T, D = x.shape
    tr = min(tr, T)
    has_next = g_next is not None

    def body(*refs):
        if has_next:
            x_ref, m_ref, gp_ref, gn_ref, xo_ref, h_ref = refs
        else:
            x_ref, m_ref, gp_ref, xo_ref = refs
        xn = x_ref[...] + _rms(m_ref[...], gp_ref[...])
        xo_ref[...] = xn
        if has_next:
            h_ref[...] = _rms(xn, gn_ref[...]).astype(h_ref.dtype)

    row = pl.BlockSpec((tr, D), lambda i: (i, 0))
    vec = pl.BlockSpec((1, D), lambda i: (0, 0))
    ins = [row, row, vec] + ([vec] if has_next else [])
    args = (x, m, g_post) + ((g_next,) if has_next else ())
    outs = [row, row] if has_next else row
    shp = [_sds((T, D), F32), _sds((T, D), BF16)] if has_next else _sds((T, D), F32)
    res = pl.pallas_call(body, name=name, grid=(T // tr,), in_specs=ins, out_specs=outs, out_shape=shp,
                         compiler_params=_params(("parallel",)))(*args)
    return res if has_next else (res, None)


def rms_bwd(name, xin, g, dy, resid, out_dtype, tr=512):
    T, D = xin.shape
    tr = min(tr, T)
    has_res = resid is not None

    def body(*refs):
        if has_res:
            x_ref, g_ref, dy_ref, r_ref, dx_ref, dg_ref = refs
        else:
            x_ref, g_ref, dy_ref, dx_ref, dg_ref = refs
        x = x_ref[...].astype(F32)
        dy_ = dy_ref[...].astype(F32)
        r = lax.rsqrt(jnp.mean(x * x, axis=-1, keepdims=True) + EPS)
        xh = x * r
        dxh = dy_ * g_ref[...]
        dx = r * (dxh - xh * jnp.mean(dxh * xh, axis=-1, keepdims=True))
        if has_res:
            dx = dx + r_ref[...]
        dx_ref[...] = dx.astype(dx_ref.dtype)
        part = jnp.sum(dy_ * xh, axis=0, keepdims=True)

        @pl.when(pl.program_id(0) == 0)
        def _():
            dg_ref[...] = part

        @pl.when(pl.program_id(0) > 0)
        def _():
            dg_ref[...] += part

    row = pl.BlockSpec((tr, D), lambda i: (i, 0))
    vec = pl.BlockSpec((1, D), lambda i: (0, 0))
    ins = [row, vec, row] + ([row] if has_res else [])
    args = (xin, g, dy) + ((resid,) if has_res else ())
    return pl.pallas_call(body, name=name, grid=(T // tr,), in_specs=ins, out_specs=[row, vec],
                          out_shape=[_sds((T, D), out_dtype), _sds((1, D), F32)],
                          compiler_params=_params(("arbitrary",)))(*args)


def loss_fwd_bwd(name, y, tgt, tr=512):
    T, D = y.shape
    tr = min(tr, T)

    def body(y_ref, t_ref, l_ref, d_ref):
        e = y_ref[...] - t_ref[...]
        d_ref[...] = e * (1.0 / D)
        part = 0.5 * jnp.sum(jnp.mean(e * e, axis=-1, keepdims=True), axis=0, keepdims=True)
        part = jnp.broadcast_to(part, l_ref.shape)

        @pl.when(pl.program_id(0) == 0)
        def _():
            l_ref[...] = part

        @pl.when(pl.program_id(0) > 0)
        def _():
            l_ref[...] += part

    row = pl.BlockSpec((tr, D), lambda i: (i, 0))
    return pl.pallas_call(body, name=name, grid=(T // tr,), in_specs=[row, row],
                          out_specs=[pl.BlockSpec((8, LANES), lambda i: (0, 0)), row],
                          out_shape=[_sds((8, LANES), F32), _sds((T, D), F32)],
                          compiler_params=_params(("arbitrary",)))(y, tgt)


def _conv_taps(p, w_ref, K):
    taps = [p] + [_shift_down(p, s) for s in range(1, K)]
    out = taps[0] * w_ref[pl.ds(K - 1, 1), :]
    for s in range(1, K):
        out = out + taps[s] * w_ref[pl.ds(K - 1 - s, 1), :]
    return taps, out


def _conv_bwd(dpre, taps, w_ref, dw_ref, db_ref, K):
    db_ref[...] = jnp.sum(dpre, axis=0, keepdims=True)
    dp = dpre * w_ref[pl.ds(K - 1, 1), :]
    dw_ref[pl.ds(K - 1, 1), :] = jnp.sum(dpre * taps[0], axis=0, keepdims=True)
    for s in range(1, K):
        dw_ref[pl.ds(K - 1 - s, 1), :] = jnp.sum(dpre * taps[s], axis=0, keepdims=True)
        dp = dp + _shift_up(dpre, s) * w_ref[pl.ds(K - 1 - s, 1), :]
    return dp


def ffn_gate_fwd(name, P, cw, cb, tc=LANES):
    _, T, F = P.shape
    K = cw.shape[0]
    nf = F // tc

    def body(pg_ref, pu_ref, wg_ref, wu_ref, bg_ref, bu_ref, o_ref):
        _, g = _conv_taps(pg_ref[...], wg_ref, K)
        _, u = _conv_taps(pu_ref[...], wu_ref, K)
        o_ref[...] = (_gelu(g + bg_ref[...]) * (u + bu_ref[...])).astype(o_ref.dtype)

    pg = pl.BlockSpec((None, T, tc), lambda j: (0, 0, j))
    pu = pl.BlockSpec((None, T, tc), lambda j: (1, 0, j))
    wg = pl.BlockSpec((K, tc), lambda j: (0, j))
    wu = pl.BlockSpec((K, tc), lambda j: (0, j + nf))
    bg = pl.BlockSpec((1, tc), lambda j: (0, j))
    bu = pl.BlockSpec((1, tc), lambda j: (0, j + nf))
    return pl.pallas_call(body, name=name, grid=(nf,), in_specs=[pg, pu, wg, wu, bg, bu],
                          out_specs=pl.BlockSpec((T, tc), lambda j: (0, j)), out_shape=_sds((T, F), BF16),
                          compiler_params=_params(("parallel",)))(P, P, cw, cw, cb, cb)


def ffn_gate_bwd(name, P, da, cw, cb, tc=LANES):
    _, T, F = P.shape
    K = cw.shape[0]
    nf = F // tc

    def body(pg_ref, pu_ref, da_ref, wg_ref, wu_ref, bg_ref, bu_ref, dp_ref, dwg_ref, dwu_ref, dbg_ref, dbu_ref):
        tg, g = _conv_taps(pg_ref[...], wg_ref, K)
        tu, u = _conv_taps(pu_ref[...], wu_ref, K)
        g = g + bg_ref[...]
        u = u + bu_ref[...]
        da_ = da_ref[...]
        dg = da_ * u * _dgelu(g)
        du = da_ * _gelu(g)
        dp_ref[0] = _conv_bwd(dg, tg, wg_ref, dwg_ref, dbg_ref, K).astype(dp_ref.dtype)
        dp_ref[1] = _conv_bwd(du, tu, wu_ref, dwu_ref, dbu_ref, K).astype(dp_ref.dtype)

    pg = pl.BlockSpec((None, T, tc), lambda j: (0, 0, j))
    pu = pl.BlockSpec((None, T, tc), lambda j: (1, 0, j))
    col = pl.BlockSpec((T, tc), lambda j: (0, j))
    wg = pl.BlockSpec((K, tc), lambda j: (0, j))
    wu = pl.BlockSpec((K, tc), lambda j: (0, j + nf))
    bg = pl.BlockSpec((1, tc), lambda j: (0, j))
    bu = pl.BlockSpec((1, tc), lambda j: (0, j + nf))
    return pl.pallas_call(
        body, name=name, grid=(nf,), in_specs=[pg, pu, col, wg, wu, bg, bu],
        out_specs=[pl.BlockSpec((2, T, tc), lambda j: (0, 0, j)), wg, wg, bg, bg],
        out_shape=[_sds((2, T, F), BF16), _sds((K, F), F32), _sds((K, F), F32), _sds((1, F), F32), _sds((1, F), F32)],
        compiler_params=_params(("parallel",)))(P, P, da, cw, cw, cb, cb)


def xattn_fwd(name, q, kv, n_heads, tq=512):
    T, Wd = q.shape
    Mm = kv.shape[0]
    hd = Wd // n_heads
    scale = hd ** -0.5
    tq = min(tq, T)

    def body(q_ref, kv_ref, o_ref):
        for h in range(n_heads):
            qh = q_ref[:, h * hd:(h + 1) * hd]
            kh = kv_ref[:, h * hd:(h + 1) * hd]
            vh = kv_ref[:, Wd + h * hd:Wd + (h + 1) * hd]
            s = _dot(qh, kh, NT) * scale
            s = s - jnp.max(s, axis=-1, keepdims=True)
            e = jnp.exp(s)
            p = e / jnp.sum(e, axis=-1, keepdims=True)
            o_ref[:, h * hd:(h + 1) * hd] = _dot(p.astype(BF16), vh).astype(o_ref.dtype)

    return pl.pallas_call(body, name=name, grid=(T // tq,),
                          in_specs=[pl.BlockSpec((tq, Wd), lambda i: (i, 0)), pl.BlockSpec((Mm, 2 * Wd), lambda i: (0, 0))],
                          out_specs=pl.BlockSpec((tq, Wd), lambda i: (i, 0)), out_shape=_sds((T, Wd), BF16),
                          compiler_params=_params(("parallel",)))(q, kv)


def xattn_bwd(name, q, kv, do, n_heads, tq=512):
    T, Wd = q.shape
    Mm = kv.shape[0]
    hd = Wd // n_heads
    scale = hd ** -0.5
    tq = min(tq, T)

    def body(q_ref, kv_ref, do_ref, dq_ref, dkv_ref):
        @pl.when(pl.program_id(0) == 0)
        def _():
            dkv_ref[...] = jnp.zeros_like(dkv_ref)

        for h in range(n_heads):
            sl = slice(h * hd, (h + 1) * hd)
            sv = slice(Wd + h * hd, Wd + (h + 1) * hd)
            qh, kh, vh = q_ref[:, sl], kv_ref[:, sl], kv_ref[:, sv]
            doh = do_ref[:, sl].astype(BF16)
            s = _dot(qh, kh, NT) * scale
            s = s - jnp.max(s, axis=-1, keepdims=True)
            e = jnp.exp(s)
            p = e / jnp.sum(e, axis=-1, keepdims=True)
            dp = _dot(doh, vh, NT)
            ds = (p * (dp - jnp.sum(dp * p, axis=-1, keepdims=True)) * scale).astype(BF16)
            dq_ref[:, sl] = _dot(ds, kh).astype(dq_ref.dtype)
            dkv_ref[:, sl] += _dot(ds, qh, TN)
            dkv_ref[:, sv] += _dot(p.astype(BF16), doh, TN)

    row = pl.BlockSpec((tq, Wd), lambda i: (i, 0))
    full = pl.BlockSpec((Mm, 2 * Wd), lambda i: (0, 0))
    return pl.pallas_call(body, name=name, grid=(T // tq,), in_specs=[row, full, row], out_specs=[row, full],
                          out_shape=[_sds((T, Wd), BF16), _sds((Mm, 2 * Wd), F32)],
                          compiler_params=_params(("arbitrary",)))(q, kv, do)


def ffn_fwd(hf, w_in, cw, cb, w_out):
    T = hf.shape[0]
    F = w_out.K
    tn = _tile(w_in.C, 512)
    nfp = F // tn
    P = mm_fwd("ffn_in", hf, w_in, F32, tn=tn, out_sds=_sds((2, T, F), F32),
               o_spec=lambda tm, tn_: pl.BlockSpec((None, tm, tn_), lambda i, j, k: (j // nfp, i, j % nfp)))
    a = ffn_gate_fwd("ffn_gate", P, cw, cb)
    f = mm_fwd("ffn_out", a, w_out, F32, tn=1024)
    return f, (P, a)


def ffn_bwd(hf, saved, df, w_in, cw, cb, w_out):
    P, a = saved
    T = hf.shape[0]
    F = w_out.K
    C = w_in.C
    da = mm_dx("ffn_out_dx", df, w_out, F32, tn=w_out.R)
    dw_out = mm_dw("ffn_out_dw", a, df, BF16, tn=1024)
    dP, dcw_g, dcw_u, dcb_g, dcb_u = ffn_gate_bwd("ffn_gate_bwd", P, da, cw, cb)
    dw_in = mm_dw("ffn_in_dw", hf, dP, BF16, n_shards=w_in.S, N=2 * F,
                  b_spec=lambda T_, tn: pl.BlockSpec((None, T_, tn), lambda i, j, k: (j // (F // tn), 0, j % (F // tn))))
    per = F // C
    dhf = mm_dx("ffn_in_dx", dP, w_in, F32,
                a_spec=lambda tm, tk: pl.BlockSpec((None, tm, tk), lambda i, j, k: (k // per, i, k % per)))
    grads = dict(ffn_w_in=dw_in, ffn_w_out=dw_out, ffn_conv_w=jnp.concatenate([dcw_g, dcw_u], axis=1),
                 ffn_conv_b=jnp.concatenate([dcb_g, dcb_u], axis=1))
    return dhf, grads


def xa_fwd(hq, mem_n, wq, wkv, wo, n_heads):
    q = mm_fwd("xa_q", hq, wq, BF16)
    kv = mm_fwd("xa_kv", mem_n, wkv, BF16)
    o = xattn_fwd("xa_core", q, kv, n_heads)
    c = mm_fwd("xa_o", o, wo, F32)
    return c, (q, kv, o)


def xa_bwd(hq, mem_n, saved, dc, wq, wkv, wo, n_heads):
    q, kv, o = saved
    do = mm_dx("xa_o_dx", dc, wo, F32)
    dwo = mm_dw("xa_o_dw", o, dc, BF16, n_shards=wo.S)
    dq, dkv = xattn_bwd("xa_core_bwd", q, kv, do, n_heads)
    dkv = dkv.astype(BF16)
    dwq = mm_dw("xa_q_dw", hq, dq, BF16)
    dhq = mm_dx("xa_q_dx", dq, wq, F32, tn=wq.R)
    dwkv = mm_dw("xa_kv_dw", mem_n, dkv, BF16)
    dmem_n = mm_dx("xa_kv_dx", dkv, wkv, F32, tn=wkv.R)
    return dhq, dmem_n, dict(xa_wq=dwq, xa_wkv=dwkv, xa_wo=dwo)


def _sb_logits(q, kblk, scale):
    z = _dot(q, kblk, NT) * scale
    l1 = -_softplus(z)
    return z, l1, z + l1


def sb_fwd(name, qkv, n_heads):
    T = qkv.shape[0]
    hd = qkv.shape[1] // (3 * n_heads)
    scale = hd ** -0.5
    Q = CHUNK

    def body(q_ref, k_ref, v_ref, o_ref, lt_ref):
        i = pl.program_id(1)
        q = q_ref[...]
        mrev = _tri(Q, "lt").astype(BF16)
        valid = _tri(Q, "lt")

        def block(kb, carry, masked):
            c, acc = carry
            off = pl.multiple_of(kb * Q, Q)
            kblk, vblk = k_ref[pl.ds(off, Q), :], v_ref[pl.ds(off, Q), :]
            _, l1, lb = _sb_logits(q, kblk, scale)
            if masked:
                l1 = jnp.where(valid, l1, 0.0)
            a = jnp.exp(lb + _dot3r(l1, mrev) + c)
            if masked:
                a = jnp.where(valid, a, 0.0)
            return c + jnp.sum(l1, axis=1, keepdims=True), acc + _dot(a.astype(BF16), vblk)

        carry = block(i, (jnp.zeros((Q, 1), F32), jnp.zeros((Q, hd), F32)), True)
        c, acc = lax.fori_loop(0, i, lambda r, cr: block(i - 1 - r, cr, False), carry)
        o_ref[...] = acc.astype(o_ref.dtype)
        lt_ref[...] = c

    H = n_heads
    return pl.pallas_call(
        body, name=name, grid=(H, T // Q),
        in_specs=[pl.BlockSpec((Q, hd), lambda h, i: (i, h)), pl.BlockSpec((T, hd), lambda h, i: (0, H + h)),
                  pl.BlockSpec((T, hd), lambda h, i: (0, 2 * H + h))],
        out_specs=[pl.BlockSpec((Q, hd), lambda h, i: (i, h)), pl.BlockSpec((None, Q, 1), lambda h, i: (h, i, 0))],
        out_shape=[_sds((T, H * hd), BF16), _sds((H, T, 1), F32)],
        compiler_params=_params(("parallel", "arbitrary")))(qkv, qkv, qkv)


def sb_bwd(name, qkv, do, lt, n_heads):
    T = qkv.shape[0]
    hd = qkv.shape[1] // (3 * n_heads)
    scale = hd ** -0.5
    Q = CHUNK

    def body(q_ref, k_ref, v_ref, do_ref, lt_ref, dq_ref, dk_ref, dv_ref):
        i = pl.program_id(1)

        @pl.when(i == 0)
        def _():
            dk_ref[...] = jnp.zeros_like(dk_ref)
            dv_ref[...] = jnp.zeros_like(dv_ref)

        q, do_, ltot = q_ref[...], do_ref[...], lt_ref[...]
        mrev = _tri(Q, "lt").astype(BF16)
        mfwd = _tri(Q, "gt").astype(BF16)
        valid = _tri(Q, "lt")

        def block(kb, carry, masked):
            pin, pre, dq = carry
            off = pl.multiple_of(kb * Q, Q)
            kblk, vblk = k_ref[pl.ds(off, Q), :], v_ref[pl.ds(off, Q), :]
            _, l1, lb = _sb_logits(q, kblk, scale)
            if masked:
                l1 = jnp.where(valid, l1, 0.0)
            pin = pin + jnp.sum(l1, axis=1, keepdims=True)
            a = jnp.exp(lb + _dot3r(l1, mrev) + (ltot - pin))
            if masked:
                a = jnp.where(valid, a, 0.0)
            de = _dot(do_, vblk, NT) * a
            dl1 = pre + _dot3r(de, mfwd)
            pre = pre + jnp.sum(de, axis=1, keepdims=True)
            sig = jnp.exp(lb)
            dz = (de * (1.0 - sig) - dl1 * sig) * scale
            if masked:
                dz = jnp.where(valid, dz, 0.0)
            dzb = dz.astype(BF16)
            dk_ref[pl.ds(off, Q), :] += _dot(dzb, q, TN)
            dv_ref[pl.ds(off, Q), :] += _dot(a.astype(BF16), do_, TN)
            return pin, pre, dq + _dot(dzb, kblk)

        init = (jnp.zeros((Q, 1), F32), jnp.zeros((Q, 1), F32), jnp.zeros((Q, hd), F32))
        carry = lax.fori_loop(0, i, lambda kb, cr: block(kb, cr, False), init)
        dq_ref[...] = block(i, carry, True)[2].astype(dq_ref.dtype)

    H = n_heads
    qs = pl.BlockSpec((Q, hd), lambda h, i: (i, h))
    full = pl.BlockSpec((T, hd), lambda h, i: (0, h))
    return pl.pallas_call(
        body, name=name, grid=(H, T // Q),
        in_specs=[qs, pl.BlockSpec((T, hd), lambda h, i: (0, H + h)), pl.BlockSpec((T, hd), lambda h, i: (0, 2 * H + h)),
                  qs, pl.BlockSpec((None, Q, 1), lambda h, i: (h, i, 0))],
        out_specs=[qs, full, full],
        out_shape=[_sds((T, H * hd), BF16), _sds((T, H * hd), F32), _sds((T, H * hd), F32)],
        compiler_params=_params(("parallel", "arbitrary")))(qkv, qkv, qkv, do, lt)


def sb_mixer_fwd(hn, w_qkv, w_out, n_heads):
    qkv = mm_fwd("sb_qkv", hn, w_qkv, BF16)
    o, lt = sb_fwd("sb_core", qkv, n_heads)
    m = mm_fwd("sb_out", o, w_out, F32, tn=1024)
    return m, (qkv, o, lt)


def sb_mixer_bwd(hn, saved, dm, w_qkv, w_out, n_heads):
    qkv, o, lt = saved
    do = mm_dx("sb_out_dx", dm, w_out, BF16, tn=w_out.R)
    dw_out = mm_dw("sb_out_dw", o, dm, BF16, tn=1024)
    dq, dk, dv = sb_bwd("sb_core_bwd", qkv, do, lt, n_heads)
    dqkv = jnp.concatenate([dq, dk.astype(BF16), dv.astype(BF16)], axis=1)
    dw_qkv = mm_dw("sb_qkv_dw", hn, dqkv, BF16, n_shards=w_qkv.S)
    dhn = mm_dx("sb_qkv_dx", dqkv, w_qkv, F32)
    return dhn, dict(sb_w_qkv=dw_qkv, sb_w_out=dw_out)


def _sgu_common(p_ref, vg_ref, vb_ref, Wd):
    pu, pv = p_ref[:, :Wd], p_ref[:, Wd:]
    u, v = _gelu(pu), _gelu(pv)
    xc = v - jnp.mean(v, axis=-1, keepdims=True)
    r = lax.rsqrt(jnp.mean(xc * xc, axis=-1, keepdims=True) + EPS)
    xh = xc * r
    return pu, pv, u, xh, r, xh * vg_ref[...] + vb_ref[...]


def sgu_fwd(name, P, vg, vb, ws, bexp):
    T = P.shape[0]
    Wd = P.shape[1] // 2
    G = ws.shape[0]
    gw = Wd // G
    Q = CHUNK

    def body(p_ref, vg_ref, vb_ref, ws_ref, be_ref, o_ref):
        _, _, u, _, _, vn = _sgu_common(p_ref, vg_ref, vb_ref, Wd)
        tril = _tri(Q, "le")
        for g in range(G):
            sl = slice(g * gw, (g + 1) * gw)
            wsg = jnp.where(tril, ws_ref[g], 0.0).astype(BF16)
            mixed = _dot(wsg, vn[:, sl].astype(BF16)) + be_ref[:, sl]
            o_ref[:, sl] = (u[:, sl] * mixed).astype(o_ref.dtype)

    vec = pl.BlockSpec((1, Wd), lambda c: (0, 0))
    return pl.pallas_call(
        body, name=name, grid=(T // Q,),
        in_specs=[pl.BlockSpec((Q, 2 * Wd), lambda c: (c, 0)), vec, vec, pl.BlockSpec((G, Q, Q), lambda c: (0, 0, 0)),
                  pl.BlockSpec((Q, Wd), lambda c: (0, 0))],
        out_specs=pl.BlockSpec((Q, Wd), lambda c: (c, 0)), out_shape=_sds((T, Wd), BF16),
        compiler_params=_params(("parallel",)))(P, vg, vb, ws, bexp)


def sgu_bwd(name, P, dgated, vg, vb, ws, bexp):
    T = P.shape[0]
    Wd = P.shape[1] // 2
    G = ws.shape[0]
    gw = Wd // G
    Q = CHUNK
    nc = T // Q

    def body(p_ref, dg_ref, vg_ref, vb_ref, ws_ref, be_ref, dp_ref, dws_ref, dvg_ref, dvb_ref, dbs_ref, dvn_scr, dbe_scr):
        c = pl.program_id(0)

        @pl.when(c == 0)
        def _():
            dws_ref[...] = jnp.zeros_like(dws_ref)
            dvg_ref[...] = jnp.zeros_like(dvg_ref)
            dvb_ref[...] = jnp.zeros_like(dvb_ref)
            dbe_scr[...] = jnp.zeros_like(dbe_scr)

        pu, pv, u, xh, r, vn = _sgu_common(p_ref, vg_ref, vb_ref, Wd)
        tril = _tri(Q, "le")
        for g in range(G):
            sl = slice(g * gw, (g + 1) * gw)
            wsg = jnp.where(tril, ws_ref[g], 0.0).astype(BF16)
            vng = vn[:, sl].astype(BF16)
            mixed = _dot(wsg, vng) + be_ref[:, sl]
            dgt = dg_ref[:, sl]
            dp_ref[:, sl] = (dgt * mixed * _dgelu(pu[:, sl])).astype(dp_ref.dtype)
            dmix = dgt * u[:, sl]
            dmb = dmix.astype(BF16)
            dws_ref[g] += jnp.where(tril, _dot(dmb, vng, NT), 0.0)
            dvn_scr[:, sl] = _dot(wsg, dmb, TN)
            dbe_scr[:, sl] += dmix
        dvn = dvn_scr[...]
        dvg_ref[...] += jnp.sum(dvn * xh, axis=0, keepdims=True)
        dvb_ref[...] += jnp.sum(dvn, axis=0, keepdims=True)
        dxh = dvn * vg_ref[...]
        dv = r * (dxh - jnp.mean(dxh, axis=-1, keepdims=True) - xh * jnp.mean(dxh * xh, axis=-1, keepdims=True))
        dp_ref[:, Wd:] = (dv * _dgelu(pv)).astype(dp_ref.dtype)

        @pl.when(c == nc - 1)
        def _():
            sel = (_iota((Wd, LANES), 0) // gw == _iota((Wd, LANES), 1)).astype(BF16)
            dbs_ref[...] = _dot3r(dbe_scr[...], sel)

    vec = pl.BlockSpec((1, Wd), lambda c: (0, 0))
    wsb = pl.BlockSpec((G, Q, Q), lambda c: (0, 0, 0))
    return pl.pallas_call(
        body, name=name, grid=(nc,),
        in_specs=[pl.BlockSpec((Q, 2 * Wd), lambda c: (c, 0)), pl.BlockSpec((Q, Wd), lambda c: (c, 0)), vec, vec, wsb,
                  pl.BlockSpec((Q, Wd), lambda c: (0, 0))],
        out_specs=[pl.BlockSpec((Q, 2 * Wd), lambda c: (c, 0)), wsb, vec, vec, pl.BlockSpec((Q, LANES), lambda c: (0, 0))],
        out_shape=[_sds((T, 2 * Wd), BF16), _sds((G, Q, Q), F32), _sds((1, Wd), F32), _sds((1, Wd), F32), _sds((Q, LANES), F32)],
        scratch_shapes=[pltpu.VMEM((Q, Wd), F32), pltpu.VMEM((Q, Wd), F32)],
        compiler_params=_params(("arbitrary",)))(P, dgated, vg, vb, ws, bexp)


def sg_mixer_fwd(hn, w_in, vg, vb, ws, bs, w_out):
    G = ws.shape[0]
    Wd = vg.shape[1]
    P = mm_fwd("sg_in", hn, w_in, F32)
    bexp = jnp.repeat(bs.T, Wd // G, axis=1)
    gated = sgu_fwd("sg_core", P, vg, vb, ws, bexp)
    m = mm_fwd("sg_out", gated, w_out, F32, tn=1024)
    return m, (P, bexp, gated)


def sg_mixer_bwd(hn, saved, dm, w_in, vg, vb, ws, w_out):
    P, bexp, gated = saved
    G = ws.shape[0]
    dgated = mm_dx("sg_out_dx", dm, w_out, F32, tn=w_out.R)
    dw_out = mm_dw("sg_out_dw", gated, dm, BF16, tn=1024)
    dP, dws, dvg, dvb, dbs = sgu_bwd("sg_core_bwd", P, dgated, vg, vb, ws, bexp)
    dw_in = mm_dw("sg_in_dw", hn, dP, BF16, n_shards=w_in.S)
    dhn = mm_dx("sg_in_dx", dP, w_in, F32)
    grads = dict(sg_w_in=dw_in, sg_w_out=dw_out, sg_w_spatial=dws, sg_v_norm_g=dvg, sg_v_norm_b=dvb,
                 sg_b_spatial=dbs[:, :G].T)
    return dhn, grads


def ssd_conv_fwd(name, xbc, cw, cb, tc=LANES):
    T, Cd = xbc.shape
    K = cw.shape[0]

    def body(p_ref, w_ref, b_ref, o_ref):
        _, pre = _conv_taps(p_ref[...], w_ref, K)
        o_ref[...] = _silu(pre + b_ref[...])

    col = pl.BlockSpec((T, tc), lambda j: (0, j))
    return pl.pallas_call(body, name=name, grid=(Cd // tc,),
                          in_specs=[col, pl.BlockSpec((K, tc), lambda j: (0, j)), pl.BlockSpec((1, tc), lambda j: (0, j))],
                          out_specs=col, out_shape=_sds((T, Cd), F32), compiler_params=_params(("parallel",)))(xbc, cw, cb)


def ssd_conv_bwd(name, xbc, dact, cw, cb, tc=LANES):
    T, Cd = xbc.shape
    K = cw.shape[0]

    def body(p_ref, da_ref, w_ref, b_ref, dp_ref, dw_ref, db_ref):
        taps, pre = _conv_taps(p_ref[...], w_ref, K)
        dpre = da_ref[...] * _dsilu(pre + b_ref[...])
        dp_ref[...] = _conv_bwd(dpre, taps, w_ref, dw_ref, db_ref, K).astype(dp_ref.dtype)

    col = pl.BlockSpec((T, tc), lambda j: (0, j))
    wsp = pl.BlockSpec((K, tc), lambda j: (0, j))
    bsp = pl.BlockSpec((1, tc), lambda j: (0, j))
    return pl.pallas_call(body, name=name, grid=(Cd // tc,), in_specs=[col, col, wsp, bsp], out_specs=[col, wsp, bsp],
                          out_shape=[_sds((T, Cd), BF16), _sds((K, Cd), F32), _sds((1, Cd), F32)],
                          compiler_params=_params(("parallel",)))(xbc, dact, cw, cb)


def _expand_matrix(Hd):
    return (_iota((LANES, Hd), 1) // SSD_HEAD_DIM == _iota((LANES, Hd), 0)).astype(BF16)


def ssd_dt_fwd(name, dtr, bias, Hd, tr=512):
    T = dtr.shape[0]
    tr = min(tr, T)

    def body(d_ref, b_ref, o_ref):
        o_ref[...] = _dot3r(_softplus(d_ref[...] + b_ref[...]), _expand_matrix(Hd))

    return pl.pallas_call(body, name=name, grid=(T // tr,),
                          in_specs=[pl.BlockSpec((tr, LANES), lambda i: (i, 0)), pl.BlockSpec((1, LANES), lambda i: (0, 0))],
                          out_specs=pl.BlockSpec((tr, Hd), lambda i: (i, 0)), out_shape=_sds((T, Hd), F32),
                          compiler_params=_params(("parallel",)))(dtr, bias)


def ssd_dt_bwd(name, dtr, bias, ddtx, tr=512):
    T, Hd = ddtx.shape
    tr = min(tr, T)

    def body(d_ref, b_ref, g_ref, o_ref, db_ref):
        p1, p2, p3 = _split3(g_ref[...])
        em = _expand_matrix(Hd)
        ddt = _dot(p1, em, NT) + _dot(p2, em, NT) + _dot(p3, em, NT)
        draw = ddt * _sigmoid(d_ref[...] + b_ref[...])
        o_ref[...] = draw.astype(o_ref.dtype)
        part = jnp.sum(draw, axis=0, keepdims=True)

        @pl.when(pl.program_id(0) == 0)
        def _():
            db_ref[...] = part

        @pl.when(pl.program_id(0) > 0)
        def _():
            db_ref[...] += part

    row = pl.BlockSpec((tr, LANES), lambda i: (i, 0))
    vec = pl.BlockSpec((1, LANES), lambda i: (0, 0))
    return pl.pallas_call(body, name=name, grid=(T // tr,), in_specs=[row, vec, pl.BlockSpec((tr, Hd), lambda i: (i, 0))],
                          out_specs=[row, vec], out_shape=[_sds((T, LANES), BF16), _sds((1, LANES), F32)],
                          compiler_params=_params(("arbitrary",)))(dtr, bias, ddtx)


def _ssd_head_terms(a2, a2r, half, cb, causal, lane):
    hm = (lane < SSD_HEAD_DIM) if half == 0 else (lane >= SSD_HEAD_DIM)
    ccol = jnp.where(hm, a2, a2r)
    lm = jnp.exp(jnp.where(causal, ccol - ccol.T, -jnp.inf))
    return hm, lm, cb * lm


def ssd_core_fwd(name, act, dtx, alx, dx, G):
    T, Hd = dtx.shape
    Q, N = CHUNK, SSD_STATE
    gw = Hd // G
    nc = T // Q
    nx = Hd // N

    def body(xs_ref, b_ref, c_ref, dt_ref, al_ref, d_ref, y_ref, ss_ref, st_scr):
        @pl.when(pl.program_id(1) == 0)
        def _():
            st_scr[...] = jnp.zeros_like(st_scr)

        xs, dtv = xs_ref[...], dt_ref[...]
        Bb, Cb = b_ref[...].astype(BF16), c_ref[...].astype(BF16)
        dA = dtv * (-jnp.exp(al_ref[...]))
        a = _dot3l(_tri(Q, "le").astype(BF16), dA)
        a_last = jnp.sum(dA, axis=0, keepdims=True)
        xdt = xs * dtv
        cbm = _dot(Cb, Bb, NT)
        sprev = st_scr[...]
        ss_ref[...] = sprev
        causal, lane = _tri(Q, "le"), _iota((Q, LANES), 1)
        y_rest = _dot(Cb, sprev.astype(BF16)) * jnp.exp(a) + xs * d_ref[...]
        for q in range(gw // LANES):
            sl = slice(q * LANES, (q + 1) * LANES)
            a2, x2 = a[:, sl], xdt[:, sl]
            a2r = pltpu.roll(a2, SSD_HEAD_DIM, 1)
            acc = y_rest[:, sl]
            for half in (0, 1):
                hm, _, gm = _ssd_head_terms(a2, a2r, half, cbm, causal, lane)
                acc = acc + _dot(gm.astype(BF16), jnp.where(hm, x2, 0.0).astype(BF16))
            y_ref[:, sl] = acc
        w = jnp.exp(a_last - a)
        st_scr[...] = sprev * jnp.exp(a_last) + _dot(Bb, (w * xdt).astype(BF16), TN)

    xsp = pl.BlockSpec((Q, gw), lambda g, c: (c, g))
    vec = pl.BlockSpec((1, gw), lambda g, c: (0, g))
    return pl.pallas_call(
        body, name=name, grid=(G, nc),
        in_specs=[xsp, pl.BlockSpec((Q, N), lambda g, c: (c, nx + g)), pl.BlockSpec((Q, N), lambda g, c: (c, nx + G + g)),
                  xsp, vec, vec],
        out_specs=[xsp, pl.BlockSpec((None, N, gw), lambda g, c: (c, 0, g))],
        out_shape=[_sds((T, Hd), F32), _sds((nc, N, Hd), F32)],
        scratch_shapes=[pltpu.VMEM((N, gw), F32)],
        compiler_params=_params(("parallel", "arbitrary")))(act, act, act, dtx, alx, dx)


def ssd_core_bwd(name, act, dtx, alx, dx, ssave, dy, G):
    T, Hd = dtx.shape
    Q, N = CHUNK, SSD_STATE
    gw = Hd // G
    nc = T // Q
    nx = Hd // N

    def body(xs_ref, b_ref, c_ref, dt_ref, al_ref, d_ref, ss_ref, dy_ref,
             dxs_ref, db_ref, dc_ref, ddt_ref, dal_ref, dd_ref, ds_scr, dxdt_scr, da_scr):
        @pl.when(pl.program_id(1) == 0)
        def _():
            ds_scr[...] = jnp.zeros_like(ds_scr)
            dal_ref[...] = jnp.zeros_like(dal_ref)
            dd_ref[...] = jnp.zeros_like(dd_ref)

        xs, dtv, dy_ = xs_ref[...], dt_ref[...], dy_ref[...]
        Bb, Cb = b_ref[...].astype(BF16), c_ref[...].astype(BF16)
        Ax = -jnp.exp(al_ref[...])
        dA = dtv * Ax
        a = _dot3l(_tri(Q, "le").astype(BF16), dA)
        a_last = jnp.sum(dA, axis=0, keepdims=True)
        xdt = xs * dtv
        e, w, eal = jnp.exp(a), jnp.exp(a_last - a), jnp.exp(a_last)
        sprev, dsn = ss_ref[...], ds_scr[...]
        sprevb, dsnb = sprev.astype(BF16), dsn.astype(BF16)

        dd_ref[...] += jnp.sum(dy_ * xs, axis=0, keepdims=True)
        dmb = (dy_ * e).astype(BF16)
        dC = _dot(dmb, sprevb, NT)
        ds_scr[...] = _dot(Cb, dmb, TN) + dsn * eal
        dalast = jnp.sum(dsn * sprev, axis=0, keepdims=True) * eal
        dB = _dot((w * xdt).astype(BF16), dsnb, NT)
        dwx = _dot(Bb, dsnb)
        dww = dwx * xdt * w
        dalast = dalast + jnp.sum(dww, axis=0, keepdims=True)
        da_scr[...] = dy_ * _dot(Cb, sprevb) * e - dww
        dxdt_scr[...] = w * dwx
        cbm = _dot(Cb, Bb, NT)
        dcb = jnp.zeros((Q, Q), F32)
        causal, lane = _tri(Q, "le"), _iota((Q, LANES), 1)
        for q in range(gw // LANES):
            sl = slice(q * LANES, (q + 1) * LANES)
            a2, x2, dy2 = a[:, sl], xdt[:, sl], dy_[:, sl]
            a2r = pltpu.roll(a2, SSD_HEAD_DIM, 1)
            for half in (0, 1):
                hm, lm, gm = _ssd_head_terms(a2, a2r, half, cbm, causal, lane)
                dyh = jnp.where(hm, dy2, 0.0).astype(BF16)
                dg = _dot(dyh, jnp.where(hm, x2, 0.0).astype(BF16), NT)
                dxdt_scr[:, sl] += _dot(gm.astype(BF16), dyh, TN)
                dcb = dcb + dg * lm
                dseg = dg * gm
                v = jnp.sum(dseg, axis=1, keepdims=True) - jnp.sum(dseg.T, axis=1, keepdims=True)
                da_scr[:, sl] += jnp.where(hm, v, 0.0) * (1.0 / SSD_HEAD_DIM)
        dcbb = dcb.astype(BF16)
        dc_ref[...] = dC + _dot(dcbb, Bb)
        db_ref[...] = dB + _dot(dcbb, Cb, TN)
        dxdt = dxdt_scr[...]
        dxs_ref[...] = dy_ * d_ref[...] + dxdt * dtv
        da = da_scr[...] + jnp.where(_iota((Q, gw), 0) == Q - 1, dalast, 0.0)
        dda = _dot3l(_tri(Q, "ge").astype(BF16), da)
        ddt_ref[...] = dxdt * xs + dda * Ax
        dal_ref[...] += jnp.sum(dda * dtv, axis=0, keepdims=True) * Ax

    rc = lambda c: nc - 1 - c
    xsp = pl.BlockSpec((Q, gw), lambda g, c: (rc(c), g))
    bsp = pl.BlockSpec((Q, N), lambda g, c: (rc(c), nx + g))
    csp = pl.BlockSpec((Q, N), lambda g, c: (rc(c), nx + G + g))
    gsp = pl.BlockSpec((Q, N), lambda g, c: (rc(c), g))
    vec = pl.BlockSpec((1, gw), lambda g, c: (0, g))
    return pl.pallas_call(
        body, name=name, grid=(G, nc),
        in_specs=[xsp, bsp, csp, xsp, vec, vec, pl.BlockSpec((None, N, gw), lambda g, c: (rc(c), 0, g)), xsp],
        out_specs=[xsp, gsp, gsp, xsp, vec, vec],
        out_shape=[_sds((T, Hd), F32), _sds((T, G * N), F32), _sds((T, G * N), F32), _sds((T, Hd), F32),
                   _sds((1, Hd), F32), _sds((1, Hd), F32)],
        scratch_shapes=[pltpu.VMEM((N, gw), F32), pltpu.VMEM((Q, gw), F32), pltpu.VMEM((Q, gw), F32)],
        compiler_params=_params(("parallel", "arbitrary")))(act, act, act, dtx, alx, dx, ssave, dy)


def ssd_gate_fwd(name, y, z, ng, tr=256):
    T, Hd = y.shape
    tr = min(tr, T)

    def body(y_ref, z_ref, g_ref, o_ref):
        o_ref[...] = _rms(y_ref[...] * _silu(z_ref[...]), g_ref[...]).astype(o_ref.dtype)

    row = pl.BlockSpec((tr, Hd), lambda i: (i, 0))
    return pl.pallas_call(body, name=name, grid=(T // tr,), in_specs=[row, row, pl.BlockSpec((1, Hd), lambda i: (0, 0))],
                          out_specs=row, out_shape=_sds((T, Hd), BF16), compiler_params=_params(("parallel",)))(y, z, ng)


def ssd_gate_bwd(name, y, z, ng, dyn, tr=128):
    T, Hd = y.shape
    tr = min(tr, T)

    def body(y_ref, z_ref, g_ref, dn_ref, dy_ref, dz_ref, dg_ref):
        y_, z_, dn = y_ref[...], z_ref[...], dn_ref[...]
        y2 = y_ * _silu(z_)
        r = lax.rsqrt(jnp.mean(y2 * y2, axis=-1, keepdims=True) + EPS)
        xh = y2 * r
        dxh = dn * g_ref[...]
        dy2 = r * (dxh - xh * jnp.mean(dxh * xh, axis=-1, keepdims=True))
        dy_ref[...] = dy2 * _silu(z_)
        dz_ref[...] = (dy2 * y_ * _dsilu(z_)).astype(dz_ref.dtype)
        part = jnp.sum(dn * xh, axis=0, keepdims=True)

        @pl.when(pl.program_id(0) == 0)
        def _():
            dg_ref[...] = part

        @pl.when(pl.program_id(0) > 0)
        def _():
            dg_ref[...] += part

    row = pl.BlockSpec((tr, Hd), lambda i: (i, 0))
    vec = pl.BlockSpec((1, Hd), lambda i: (0, 0))
    return pl.pallas_call(body, name=name, grid=(T // tr,), in_specs=[row, row, vec, row], out_specs=[row, row, vec],
                          out_shape=[_sds((T, Hd), F32), _sds((T, Hd), BF16), _sds((1, Hd), F32)],
                          compiler_params=_params(("arbitrary",)))(y, z, ng, dyn)


def ssd_mixer_fwd(hn, wz, wxbc, wdt, cw, cb, dtb, alx, dx, ng, w_out, G):
    Hd = wz.N
    z = mm_fwd("ssd_z", hn, wz, F32)
    xbc = mm_fwd("ssd_xbc", hn, wxbc, F32)
    dtr = mm_fwd("ssd_dt", hn, wdt, F32)
    act = ssd_conv_fwd("ssd_conv", xbc, cw, cb)
    dtx = ssd_dt_fwd("ssd_dtx", dtr, dtb, Hd)
    y, ssave = ssd_core_fwd("ssd_core", act, dtx, alx, dx, G)
    yn = ssd_gate_fwd("ssd_gate", y, z, ng)
    m = mm_fwd("ssd_out", yn, w_out, F32, tn=1024)
    return m, (z, xbc, dtr, act, dtx, y, ssave, yn)


def ssd_mixer_bwd(hn, saved, dm, wz, wxbc, wdt, cw, cb, dtb, alx, dx, ng, w_out, G):
    z, xbc, dtr, act, dtx, y, ssave, yn = saved
    dyn = mm_dx("ssd_out_dx", dm, w_out, F32, tn=w_out.R)
    dw_out = mm_dw("ssd_out_dw", yn, dm, BF16, tn=1024)
    dy, dz, dng = ssd_gate_bwd("ssd_gate_bwd", y, z, ng, dyn)
    dxs, dB, dC, ddtx, dalx, ddx = ssd_core_bwd("ssd_core_bwd", act, dtx, alx, dx, ssave, dy, G)
    dxbc, dcw, dcb = ssd_conv_bwd("ssd_conv_bwd", xbc, jnp.concatenate([dxs, dB, dC], axis=1), cw, cb)
    ddtr, ddtb = ssd_dt_bwd("ssd_dtx_bwd", dtr, dtb, ddtx)
    dwz = mm_dw("ssd_z_dw", hn, dz, BF16)
    dwxbc = mm_dw("ssd_xbc_dw", hn, dxbc, BF16)
    dwdt = mm_dw("ssd_dt_dw", hn, ddtr, BF16)
    dhn = mm_dx("ssd_z_dx", dz, wz, F32)
    dhn = mm_dx("ssd_xbc_dx", dxbc, wxbc, F32, add=dhn)
    dhn = mm_dx("ssd_dt_dx", ddtr, wdt, F32, add=dhn)
    grads = dict(ssd_w_out=dw_out, ssd_wz=dwz, ssd_wxbc=dwxbc, ssd_wdt=dwdt, ssd_conv_w=dcw, ssd_conv_b=dcb,
                 ssd_dt_bias=ddtb, ssd_alx=dalx, ssd_dx=ddx, ssd_norm=dng)
    return dhn, grads


def adamw(name, w, m, v, ga, gb=None):
    Rr, C = w.shape
    tr = 8
    while Rr % (tr * 2) == 0 and tr * 2 * C * 4 <= (1 << 20):
        tr *= 2
    if Rr % tr:
        tr = Rr
    two = gb is not None
    c1 = 1.0 - ADAM_B1 ** ADAM_STEP
    c2 = 1.0 - ADAM_B2 ** ADAM_STEP

    def body(*refs):
        if two:
            w_ref, m_ref, v_ref, a_ref, b_ref, g_ref, d_ref, mo_ref, vo_ref = refs
            g = a_ref[...] + b_ref[...]
        else:
            w_ref, m_ref, v_ref, a_ref, g_ref, d_ref, mo_ref, vo_ref = refs
            g = a_ref[...]
        m2 = ADAM_B1 * m_ref[...] + (1.0 - ADAM_B1) * g
        v2 = ADAM_B2 * v_ref[...] + (1.0 - ADAM_B2) * (g * g)
        g_ref[...] = g
        mo_ref[...] = m2
        vo_ref[...] = v2
        d_ref[...] = -ADAM_LR * ((m2 / c1) / (jnp.sqrt(v2 / c2) + ADAM_EPS) + ADAM_WD * w_ref[...])

    blk = pl.BlockSpec((tr, C), lambda i: (i, 0))
    n_in = 5 if two else 4
    args = (w, m, v, ga) + ((gb,) if two else ())
    return pl.pallas_call(body, name=name, grid=(Rr // tr,), in_specs=[blk] * n_in, out_specs=[blk] * 4,
                          out_shape=[_sds((Rr, C), F32)] * 4, compiler_params=_params(("parallel",)))(*args)


def _mesh_pos():
    return lax.axis_index("x"), lax.axis_index("y"), lax.axis_index("c")


def _peer_chips(x, y):
    return [(1 - x, y), (x, 1 - y), (1 - x, 1 - y)]


def all_gather_chips(name, tensors):
    n = len(tensors)

    def body(*refs):
        ins, outs = refs[:n], refs[n:2 * n]
        send_sems, recv_sems, loc_sems = refs[2 * n:]
        x, y, c = _mesh_pos()
        me = 2 * x + y
        peers = _peer_chips(x, y)
        sends, locs = [], []
        for t in range(n):
            lc = pltpu.make_async_copy(ins[t], outs[t].at[me], loc_sems.at[t])
            lc.start()
            locs.append(lc)
            for j, (px, py) in enumerate(peers):
                cp = pltpu.make_async_remote_copy(src_ref=ins[t], dst_ref=outs[t].at[me], send_sem=send_sems.at[t, j],
                                                  recv_sem=recv_sems.at[t, j], device_id=(px, py, c), device_id_type=MESH)
                cp.start()
                sends.append(cp)
        for t in range(n):
            for j, (px, py) in enumerate(peers):
                pltpu.make_async_remote_copy(src_ref=ins[t], dst_ref=outs[t].at[2 * px + py], send_sem=send_sems.at[t, j],
                                             recv_sem=recv_sems.at[t, j], device_id=(px, py, c), device_id_type=MESH).wait_recv()
        for cp in sends:
            cp.wait_send()
        for lc in locs:
            lc.wait()

    return pl.pallas_call(
        body, name=name, in_specs=[ANY] * n, out_specs=[ANY] * n,
        out_shape=[_sds((N_CHIPS,) + t.shape, t.dtype) for t in tensors],
        scratch_shapes=[pltpu.SemaphoreType.DMA((n, 3)), pltpu.SemaphoreType.DMA((n, 3)), pltpu.SemaphoreType.DMA((n,))],
    )(*tensors)


def scatter_to_chips(name, groups):
    flat = [g for grp in groups for g in grp]
    owner = [(ti, li) for ti, grp in enumerate(groups) for li in range(len(grp))]
    n, nt = len(flat), len(groups)

    def body(*refs):
        ins, outs = refs[:n], refs[n:n + nt]
        send_sems, recv_sems, loc_sems = refs[n + nt:]
        x, y, c = _mesh_pos()
        me = 2 * x + y
        peers = _peer_chips(x, y)
        sends, locs = [], []
        for k in range(n):
            ti, li = owner[k]
            lc = pltpu.make_async_copy(ins[k].at[me], outs[ti].at[3, li], loc_sems.at[k])
            lc.start()
            locs.append(lc)
            for j, (px, py) in enumerate(peers):
                cp = pltpu.make_async_remote_copy(src_ref=ins[k].at[2 * px + py], dst_ref=outs[ti].at[j, li],
                                                  send_sem=send_sems.at[k, j], recv_sem=recv_sems.at[k, j],
                                                  device_id=(px, py, c), device_id_type=MESH)
                cp.start()
                sends.append(cp)
        for k in range(n):
            ti, li = owner[k]
            for j, (px, py) in enumerate(peers):
                pltpu.make_async_remote_copy(src_ref=ins[k].at[me], dst_ref=outs[ti].at[j, li], send_sem=send_sems.at[k, j],
                                             recv_sem=recv_sems.at[k, j], device_id=(px, py, c), device_id_type=MESH).wait_recv()
        for cp in sends:
            cp.wait_send()
        for lc in locs:
            lc.wait()

    return pl.pallas_call(
        body, name=name, in_specs=[ANY] * n, out_specs=[ANY] * nt,
        out_shape=[_sds((4, len(grp)) + grp[0].shape[1:], grp[0].dtype) for grp in groups],
        scratch_shapes=[pltpu.SemaphoreType.DMA((n, 3)), pltpu.SemaphoreType.DMA((n, 3)), pltpu.SemaphoreType.DMA((n,))],
    )(*flat)


def sum_slots(name, r):
    _, Rr, C = r.shape
    tr = 8
    while Rr % (tr * 2) == 0 and tr * 2 * C * 4 <= (1 << 20):
        tr *= 2

    def body(r_ref, o_ref):
        o_ref[...] = ((r_ref[3].astype(F32) + r_ref[0].astype(F32)) + r_ref[1].astype(F32)) + r_ref[2].astype(F32)

    return pl.pallas_call(body, name=name, grid=(Rr // tr,), in_specs=[pl.BlockSpec((4, tr, C), lambda i: (0, i, 0))],
                          out_specs=pl.BlockSpec((tr, C), lambda i: (i, 0)), out_shape=_sds((Rr, C), F32),
                          compiler_params=_params(("parallel",)))(r)


def swap_with_sibling(name, tensors):
    n = len(tensors)

    def body(*refs):
        ins, outs = refs[:n], refs[n:2 * n]
        send_sems, recv_sems = refs[2 * n:]
        x, y, c = _mesh_pos()
        cps = []
        for t in range(n):
            cp = pltpu.make_async_remote_copy(src_ref=ins[t], dst_ref=outs[t], send_sem=send_sems.at[t], recv_sem=recv_sems.at[t],
                                              device_id=(x, y, 1 - c), device_id_type=MESH)
            cp.start()
            cps.append(cp)
        for cp in cps:
            cp.wait()

    return pl.pallas_call(
        body, name=name, in_specs=[ANY] * n, out_specs=[ANY] * n, out_shape=[_sds(t.shape, t.dtype) for t in tensors],
        scratch_shapes=[pltpu.SemaphoreType.DMA((n,)), pltpu.SemaphoreType.DMA((n,))],
    )(*tensors)


def all_reduce_small(name, v):
    Rr, C = v.shape
    nd = 8

    def body(v_ref, o_ref, gath, send_sems, recv_sems):
        x, y, c = _mesh_pos()
        me = 4 * x + 2 * y + c
        cps = []
        for d in range(1, nd):
            bx, by, bc = (d >> 2) & 1, (d >> 1) & 1, d & 1
            tgt = (1 - x if bx else x, 1 - y if by else y, 1 - c if bc else c)
            cp = pltpu.make_async_remote_copy(src_ref=v_ref, dst_ref=gath.at[me], send_sem=send_sems.at[d - 1],
                                              recv_sem=recv_sems.at[d - 1], device_id=tgt, device_id_type=MESH)
            cp.start()
            cps.append((cp, tgt))
        gath[me] = v_ref[...]
        for d in range(1, nd):
            _, (tx, ty, tc) = cps[d - 1]
            pltpu.make_async_remote_copy(src_ref=v_ref, dst_ref=gath.at[4 * tx + 2 * ty + tc], send_sem=send_sems.at[d - 1],
                                         recv_sem=recv_sems.at[d - 1], device_id=(tx, ty, tc), device_id_type=MESH).wait_recv()
        acc = gath[0]
        for d in range(1, nd):
            acc = acc + gath[d]
        o_ref[...] = acc
        for cp, _ in cps:
            cp.wait_send()

    vm = pl.BlockSpec(memory_space=pltpu.VMEM)
    return pl.pallas_call(
        body, name=name, in_specs=[vm], out_specs=vm, out_shape=_sds((Rr, C), F32),
        scratch_shapes=[pltpu.VMEM((nd, Rr, C), F32), pltpu.SemaphoreType.DMA((nd - 1,)), pltpu.SemaphoreType.DMA((nd - 1,))],
        compiler_params=pltpu.CompilerParams(vmem_limit_bytes=VMEM_LIMIT),
    )(v)


def _pack(arrs):
    flat = jnp.concatenate([a.reshape(-1) for a in arrs])
    pad = (-flat.shape[0]) % (8 * LANES)
    return jnp.pad(flat, (0, pad)).reshape(-1, LANES)


def _unpack(buf, shapes):
    flat = buf.reshape(-1)
    out, off = [], 0
    for s in shapes:
        n = math.prod(s)
        out.append(flat[off:off + n].reshape(s))
        off += n
    return out


WEIGHTS = ["ln_mix_pre", "ln_mix_post", "ln_mem", "ln_xa_pre", "ln_xa_post", "ln_ffn_pre", "ln_ffn_post", "xa_wq", "xa_wkv",
           "xa_wo", "ffn_w_in", "ffn_conv_w", "ffn_conv_b", "ffn_w_out", "ssd_w_in", "ssd_conv_w", "ssd_conv_b", "ssd_dt_bias",
           "ssd_a_log", "ssd_d", "ssd_norm", "ssd_w_out", "sg_w_in", "sg_v_norm_g", "sg_v_norm_b", "sg_w_spatial",
           "sg_b_spatial", "sg_w_out", "sb_w_qkv", "sb_w_out"]
BIG = {"xa_wq": "rows", "xa_wkv": "rows", "xa_wo": "cols", "ffn_w_in": "cols", "ffn_w_out": "rows", "ssd_w_in": "cols",
       "ssd_w_out": "rows", "sg_w_in": "cols", "sg_w_out": "rows", "sb_w_qkv": "cols", "sb_w_out": "rows"}
SHARDED_SMALL = {"ffn_conv_w": 2, "ssd_conv_w": 2, "ssd_conv_b": 1, "ssd_norm": 1}
SMALL = [n for n in WEIGHTS if n not in BIG]
N_MIXERS = 3
HEAD = 128


def _unshard(a, axis):
    a = jnp.moveaxis(a, 0, axis)
    s = a.shape
    return a.reshape(s[:axis] + (s[axis] * s[axis + 1],) + s[axis + 2:])


def _step(p):
    x, mem, tgt = p["x"][0], p["mem"][0], p["loss_target"][0]
    T, D = x.shape
    depth = p["ln_mix_pre"].shape[0]
    S = N_CHIPS

    names = list(BIG) + list(SHARDED_SMALL)
    gathered = dict(zip(names, all_gather_chips(
        "gather_weights", [p[n].astype(BF16) for n in BIG] + [p[n] for n in SHARDED_SMALL])))
    w_of = lambda n, l: W(BIG[n], gathered[n], l)
    ffn_cw = _unshard(gathered["ffn_conv_w"], 2)
    ssd_cw = _unshard(gathered["ssd_conv_w"], 2)
    ssd_cb = _unshard(gathered["ssd_conv_b"], 1)
    ssd_ng = _unshard(gathered["ssd_norm"], 1)
    ssd_in = _unshard(gathered["ssd_w_in"], 2)
    Hd, Cd, nh = ssd_ng.shape[1], ssd_cb.shape[1], p["ssd_dt_bias"].shape[1]
    G = (Cd - Hd) // (2 * SSD_STATE)
    ssd_wz = ssd_in[:, :, :Hd]
    ssd_wxbc = ssd_in[:, :, Hd:Hd + Cd]
    ssd_wdt = jnp.pad(ssd_in[:, :, Hd + Cd:], ((0, 0), (0, 0), (0, LANES - nh)))
    xa_heads = gathered["xa_wo"].shape[2] // HEAD
    sb_heads = D // HEAD

    def ssd_args(j):
        return (W("full", ssd_wz, j), W("full", ssd_wxbc, j), W("full", ssd_wdt, j), ssd_cw[j], ssd_cb[j:j + 1],
                jnp.pad(p["ssd_dt_bias"][j], (0, LANES - nh))[None], jnp.repeat(p["ssd_a_log"][j], SSD_HEAD_DIM)[None],
                jnp.repeat(p["ssd_d"][j], SSD_HEAD_DIM)[None], ssd_ng[j:j + 1], w_of("ssd_w_out", j), G)

    def sg_args(j):
        return (w_of("sg_w_in", j), p["sg_v_norm_g"][j:j + 1], p["sg_v_norm_b"][j:j + 1], p["sg_w_spatial"][j])

    ln = lambda n, i: p[n][i:i + 1]

    h = rms_fwd("rms_first", x, ln("ln_mix_pre", 0))
    saved = []
    for i in range(depth):
        kind, j = i % N_MIXERS, i // N_MIXERS
        if kind == 0:
            m, ms = ssd_mixer_fwd(h, *ssd_args(j))
        elif kind == 1:
            m, ms = sg_mixer_fwd(h, *sg_args(j), p["sg_b_spatial"][j], w_of("sg_w_out", j))
        else:
            m, ms = sb_mixer_fwd(h, w_of("sb_w_qkv", j), w_of("sb_w_out", j), sb_heads)
        x1, hq = resid_norm("resid_norm", x, m, ln("ln_mix_post", i), ln("ln_xa_pre", i))
        mem_n = rms_fwd("rms_mem", mem, ln("ln_mem", i))
        c, cs = xa_fwd(hq, mem_n, w_of("xa_wq", i), w_of("xa_wkv", i), w_of("xa_wo", i), xa_heads)
        x2, hf = resid_norm("resid_norm", x1, c, ln("ln_xa_post", i), ln("ln_ffn_pre", i))
        f, fs = ffn_fwd(hf, w_of("ffn_w_in", i), ffn_cw[i], p["ffn_conv_b"][i:i + 1], w_of("ffn_w_out", i))
        x3, hn = resid_norm("resid_norm", x2, f, ln("ln_ffn_post", i), ln("ln_mix_pre", i + 1) if i + 1 < depth else None)
        saved.append(dict(x=x, h=h, m=m, ms=ms, x1=x1, hq=hq, mem_n=mem_n, c=c, cs=cs, x2=x2, hf=hf, f=f, fs=fs))
        x, h = x3, hn
    loss_tile, dx = loss_fwd_bwd("loss", x, tgt)
    loss = lax.psum(loss_tile[0, 0], ("x", "y", "c"))

    gs = {n: [None] * p[n].shape[0] for n in WEIGHTS}
    for i in reversed(range(depth)):
        kind, j = i % N_MIXERS, i // N_MIXERS
        s = saved[i]
        df, gs["ln_ffn_post"][i] = rms_bwd("rms_bwd_post", s["f"], ln("ln_ffn_post", i), dx, None, BF16)
        dhf, g = ffn_bwd(s["hf"], s["fs"], df, w_of("ffn_w_in", i), ffn_cw[i], p["ffn_conv_b"][i:i + 1], w_of("ffn_w_out", i))
        gs["ffn_w_in"][i], gs["ffn_conv_w"][i], gs["ffn_conv_b"][i] = g["ffn_w_in"], g["ffn_conv_w"], g["ffn_conv_b"]
        gs["ffn_w_out"][i] = g["ffn_w_out"].reshape(S, -1, D)
        dx, gs["ln_ffn_pre"][i] = rms_bwd("rms_bwd_pre", s["x2"], ln("ln_ffn_pre", i), dhf, dx, F32)

        dc, gs["ln_xa_post"][i] = rms_bwd("rms_bwd_post", s["c"], ln("ln_xa_post", i), dx, None, BF16)
        dhq, dmem_n, g = xa_bwd(s["hq"], s["mem_n"], s["cs"], dc, w_of("xa_wq", i), w_of("xa_wkv", i), w_of("xa_wo", i), xa_heads)
        gs["xa_wq"][i] = g["xa_wq"].reshape(S, D // S, -1)
        gs["xa_wkv"][i] = g["xa_wkv"].reshape(S, D // S, -1)
        gs["xa_wo"][i] = g["xa_wo"]
        _, gs["ln_mem"][i] = rms_bwd("rms_bwd_mem", mem, ln("ln_mem", i), dmem_n, None, BF16)
        dx, gs["ln_xa_pre"][i] = rms_bwd("rms_bwd_pre", s["x1"], ln("ln_xa_pre", i), dhq, dx, F32)

        dm, gs["ln_mix_post"][i] = rms_bwd("rms_bwd_post", s["m"], ln("ln_mix_post", i), dx, None, BF16)
        if kind == 0:
            dhn, g = ssd_mixer_bwd(s["h"], s["ms"], dm, *ssd_args(j))
            full = jnp.concatenate([g["ssd_wz"], g["ssd_wxbc"], g["ssd_wdt"][:, :nh]], axis=1)
            gs["ssd_w_in"][j] = full.reshape(D, S, -1).transpose(1, 0, 2)
            gs["ssd_w_out"][j] = g["ssd_w_out"].reshape(S, Hd // S, D)
            gs["ssd_conv_w"][j], gs["ssd_conv_b"][j], gs["ssd_norm"][j] = g["ssd_conv_w"], g["ssd_conv_b"], g["ssd_norm"]
            gs["ssd_dt_bias"][j] = g["ssd_dt_bias"][:, :nh]
            gs["ssd_a_log"][j] = g["ssd_alx"].reshape(nh, SSD_HEAD_DIM).sum(-1)[None]
            gs["ssd_d"][j] = g["ssd_dx"].reshape(nh, SSD_HEAD_DIM).sum(-1)[None]
        elif kind == 1:
            dhn, g = sg_mixer_bwd(s["h"], s["ms"], dm, *sg_args(j), w_of("sg_w_out", j))
            gs["sg_w_in"][j] = g["sg_w_in"]
            gs["sg_w_out"][j] = g["sg_w_out"].reshape(S, -1, D)
            for n in ("sg_v_norm_g", "sg_v_norm_b", "sg_w_spatial", "sg_b_spatial"):
                gs[n][j] = g[n]
        else:
            dhn, g = sb_mixer_bwd(s["h"], s["ms"], dm, w_of("sb_w_qkv", j), w_of("sb_w_out", j), sb_heads)
            gs["sb_w_qkv"][j] = g["sb_w_qkv"]
            gs["sb_w_out"][j] = g["sb_w_out"].reshape(S, -1, D)
        dx, gs["ln_mix_pre"][i] = rms_bwd("rms_bwd_pre", s["x"], ln("ln_mix_pre", i), dhn, dx, F32)

    recv = scatter_to_chips("scatter_grads", [gs[n] for n in BIG])
    sums = [sum_slots("sum_grad_slots", r.reshape(4, -1, r.shape[-1])) for r in recv]
    sib = swap_with_sibling("swap_grads", sums)
    out = {}
    for n, q, q2 in zip(BIG, sums, sib):
        two_d = lambda a: a.reshape(-1, a.shape[-1])
        res = adamw("adamw_big", two_d(p[n]), two_d(p["m_" + n]), two_d(p["v_" + n]), q, q2)
        out[n] = [r.reshape(p[n].shape) for r in res]

    stack = lambda n: jnp.stack([a.reshape(p[n].shape[1:]) if n not in SHARDED_SMALL else a.reshape(a.shape[-len(p[n].shape) + 1:])
                                 for a in gs[n]])
    small_full = [stack(n) for n in SMALL]
    red = _unpack(all_reduce_small("reduce_small", _pack(small_full)), [a.shape for a in small_full])
    me = 2 * lax.axis_index("x") + lax.axis_index("y")
    small_g = []
    for n, a in zip(SMALL, red):
        if n in SHARDED_SMALL:
            ax = SHARDED_SMALL[n]
            a = lax.dynamic_slice_in_dim(a, me * p[n].shape[ax], p[n].shape[ax], axis=ax)
        small_g.append(a)
    shapes = [p[n].shape for n in SMALL]
    res = adamw("adamw_small", _pack([p[n] for n in SMALL]), _pack([p["m_" + n] for n in SMALL]),
                _pack([p["v_" + n] for n in SMALL]), _pack(small_g))
    for k, r in enumerate(res):
        for n, a in zip(SMALL, _unpack(r, shapes)):
            out.setdefault(n, [None] * 4)[k] = a

    return (loss, dx[None]) + tuple(out[n][k] for k in range(4) for n in WEIGHTS)


def kernel(x, mem, ln_mix_pre, ln_mix_post, ln_mem, ln_xa_pre, ln_xa_post, ln_ffn_pre, ln_ffn_post, xa_wq, xa_wkv, xa_wo, ffn_w_in, ffn_conv_w, ffn_conv_b, ffn_w_out, ssd_w_in, ssd_conv_w, ssd_conv_b, ssd_dt_bias, ssd_a_log, ssd_d, ssd_norm, ssd_w_out, sg_w_in, sg_v_norm_g, sg_v_norm_b, sg_w_spatial, sg_b_spatial, sg_w_out, sb_w_qkv, sb_w_out, loss_target, m_ln_mix_pre, m_ln_mix_post, m_ln_mem, m_ln_xa_pre, m_ln_xa_post, m_ln_ffn_pre, m_ln_ffn_post, m_xa_wq, m_xa_wkv, m_xa_wo, m_ffn_w_in, m_ffn_conv_w, m_ffn_conv_b, m_ffn_w_out, m_ssd_w_in, m_ssd_conv_w, m_ssd_conv_b, m_ssd_dt_bias, m_ssd_a_log, m_ssd_d, m_ssd_norm, m_ssd_w_out, m_sg_w_in, m_sg_v_norm_g, m_sg_v_norm_b, m_sg_w_spatial, m_sg_b_spatial, m_sg_w_out, m_sb_w_qkv, m_sb_w_out, v_ln_mix_pre, v_ln_mix_post, v_ln_mem, v_ln_xa_pre, v_ln_xa_post, v_ln_ffn_pre, v_ln_ffn_post, v_xa_wq, v_xa_wkv, v_xa_wo, v_ffn_w_in, v_ffn_conv_w, v_ffn_conv_b, v_ffn_w_out, v_ssd_w_in, v_ssd_conv_w, v_ssd_conv_b, v_ssd_dt_bias, v_ssd_a_log, v_ssd_d, v_ssd_norm, v_ssd_w_out, v_sg_w_in, v_sg_v_norm_g, v_sg_v_norm_b, v_sg_w_spatial, v_sg_b_spatial, v_sg_w_out, v_sb_w_qkv, v_sb_w_out):
    return _step(dict(locals()))
```

```python
import functools
import math

import jax
import jax.numpy as jnp
from jax import lax
from jax.experimental import pallas as pl
from jax.experimental.pallas import tpu as pltpu

F32 = jnp.float32
BF16 = jnp.bfloat16
EPS = 1e-6
LANES = 128
VMEM_LIMIT = 56 * 1024 * 1024
CHUNK = 128
SSD_HEAD_DIM = 64
SSD_STATE = 128
N_CHIPS = 4
MESH = pl.DeviceIdType.MESH
ANY = pl.BlockSpec(memory_space=pl.ANY)

ADAM_LR, ADAM_B1, ADAM_B2, ADAM_EPS, ADAM_WD, ADAM_STEP = 0.001, 0.9, 0.999, 1e-08, 0.01, 10


def _params(sem):
    return pltpu.CompilerParams(dimension_semantics=sem, vmem_limit_bytes=VMEM_LIMIT)


def _sds(shape, dtype):
    return jax.ShapeDtypeStruct(tuple(shape), dtype)


def _tile(n, pref):
    if n <= pref:
        return n
    t = (pref // LANES) * LANES
    while t > LANES and n % t:
        t -= LANES
    assert n % t == 0, (n, pref)
    return t


def _split3(a):
    a1 = a.astype(BF16)
    r = a - a1.astype(F32)
    a2 = r.astype(BF16)
    a3 = (r - a2.astype(F32)).astype(BF16)
    return a1, a2, a3


def _dot(a, b, dims=(((1,), (0,)), ((), ()))):
    return lax.dot_general(a, b, dims, preferred_element_type=F32)


NN = (((1,), (0,)), ((), ()))
NT = (((1,), (1,)), ((), ()))
TN = (((0,), (0,)), ((), ()))


def _dot3r(a, m):
    p1, p2, p3 = _split3(a)
    return _dot(p1, m) + _dot(p2, m) + _dot(p3, m)


def _dot3l(m, a, dims=NN):
    p1, p2, p3 = _split3(a)
    return _dot(m, p1, dims) + _dot(m, p2, dims) + _dot(m, p3, dims)


def _iota(shape, dim):
    return lax.broadcasted_iota(jnp.int32, shape, dim)


def _tri(n, kind):
    r, c = _iota((n, n), 0), _iota((n, n), 1)
    return {"le": c <= r, "lt": c < r, "ge": c >= r, "gt": c > r}[kind]


def _sigmoid(x):
    return 1.0 / (1.0 + jnp.exp(-x))


def _silu(x):
    return x * _sigmoid(x)


def _dsilu(x):
    s = _sigmoid(x)
    return s * (1.0 + x * (1.0 - s))


_GC = math.sqrt(2.0 / math.pi)


def _gelu(x):
    return 0.5 * x * (1.0 + jnp.tanh(_GC * (x + 0.044715 * x * x * x)))


def _dgelu(x):
    th = jnp.tanh(_GC * (x + 0.044715 * x * x * x))
    return 0.5 * (1.0 + th) + 0.5 * x * (1.0 - th * th) * _GC * (1.0 + 3.0 * 0.044715 * x * x)


def _softplus(x):
    return jnp.maximum(x, 0.0) + jnp.log(1.0 + jnp.exp(-jnp.abs(x)))


def _shift_down(p, s):
    rows = _iota(p.shape, 0)
    return jnp.where(rows >= s, pltpu.roll(p, s, 0), 0.0)


def _shift_up(p, s):
    n = p.shape[0]
    rows = _iota(p.shape, 0)
    return jnp.where(rows < n - s, pltpu.roll(p, n - s, 0), 0.0)


def _mm(name, mode, a, b, out_sds, grid, a_spec, b_spec, o_spec, acc_shape, add=None, add_spec=None):
    dims = {"nn": NN, "nt": NT, "tn": TN}[mode]
    nk = grid[2]
    has_add = add is not None

    def body(*refs):
        if has_add:
            a_ref, b_ref, c_ref, o_ref = refs[:4]
        else:
            a_ref, b_ref, o_ref = refs[:3]
            c_ref = None
        part = lax.dot_general(a_ref[...], b_ref[...], dims, preferred_element_type=F32)

        def finish(r):
            if c_ref is not None:
                r = r + c_ref[...].astype(F32)
            o_ref[...] = r.astype(o_ref.dtype)

        if nk == 1:
            finish(part)
        else:
            acc = refs[-1]
            k = pl.program_id(2)

            @pl.when(k == 0)
            def _():
                acc[...] = part

            @pl.when(k > 0)
            def _():
                acc[...] += part

            @pl.when(k == nk - 1)
            def _():
                finish(acc[...])

    in_specs = [a_spec, b_spec] + ([add_spec] if has_add else [])
    args = (a, b) + ((add,) if has_add else ())
    return pl.pallas_call(
        body, name=name, grid=grid, in_specs=in_specs, out_specs=o_spec, out_shape=out_sds,
        scratch_shapes=[pltpu.VMEM(acc_shape, F32)] if nk > 1 else [],
        compiler_params=_params(("parallel", "parallel", "arbitrary")),
    )(*args)


class W:
    def __init__(self, kind, arr, layer):
        self.kind, self.arr, self.layer = kind, arr, layer
        if kind == "cols":
            s, _, k, c = arr.shape
            self.K, self.N, self.S, self.C = k, s * c, s, c
        elif kind == "rows":
            s, _, r, n = arr.shape
            self.K, self.N, self.S, self.R = s * r, n, s, r
        else:
            _, k, n = arr.shape
            self.K, self.N = k, n


def mm_fwd(name, a, w, out_dtype, tm=1024, tn=512, a_spec=None, out_sds=None, o_spec=None, add=None):
    M = a.shape[0]
    tm = min(tm, M)
    l = w.layer
    if w.kind == "cols":
        tn = _tile(w.C, tn)
        nps = w.C // tn
        tk, nk = w.K, 1
        b_spec = pl.BlockSpec((None, None, tk, tn), lambda i, j, k: (j // nps, l, 0, j % nps))
    elif w.kind == "rows":
        tn = _tile(w.N, tn)
        tk, nk = w.R, w.S
        b_spec = pl.BlockSpec((None, None, tk, tn), lambda i, j, k: (k, l, 0, j))
    else:
        tn = _tile(w.N, tn)
        tk, nk = w.K, 1
        b_spec = pl.BlockSpec((None, tk, tn), lambda i, j, k: (l, 0, j))
    grid = (M // tm, w.N // tn, nk)
    if a_spec is None:
        a_spec = pl.BlockSpec((tm, tk), lambda i, j, k: (i, k))
    if out_sds is None:
        out_sds = _sds((M, w.N), out_dtype)
        o_spec = pl.BlockSpec((tm, tn), lambda i, j, k: (i, j))
    else:
        o_spec = o_spec(tm, tn)
    add_spec = pl.BlockSpec((tm, tn), lambda i, j, k: (i, j)) if add is not None else None
    return _mm(name, "nn", a, w.arr, out_sds, grid, a_spec, b_spec, o_spec, (tm, tn), add, add_spec)


def mm_dx(name, dy, w, out_dtype, tm=1024, tn=1024, a_spec=None, add=None):
    M = dy.shape[-2]
    tm = min(tm, M)
    l = w.layer
    if w.kind == "cols":
        tn = _tile(w.K, tn)
        tk, nk = w.C, w.S
        b_spec = pl.BlockSpec((None, None, tn, tk), lambda i, j, k: (k, l, j, 0))
    elif w.kind == "rows":
        tn = _tile(w.R, tn)
        npr = w.R // tn
        tk, nk = w.N, 1
        b_spec = pl.BlockSpec((None, None, tn, tk), lambda i, j, k: (j // npr, l, j % npr, 0))
    else:
        tn = _tile(w.K, tn)
        tk, nk = _tile(w.N, 2048), w.N // _tile(w.N, 2048)
        b_spec = pl.BlockSpec((None, tn, tk), lambda i, j, k: (l, j, k))
    grid = (M // tm, w.K // tn, nk)
    if a_spec is None:
        a_spec = pl.BlockSpec((tm, tk), lambda i, j, k: (i, k))
    else:
        a_spec = a_spec(tm, tk)
    out_sds = _sds((M, w.K), out_dtype)
    o_spec = pl.BlockSpec((tm, tn), lambda i, j, k: (i, j))
    add_spec = o_spec if add is not None else None
    return _mm(name, "nt", dy, w.arr, out_sds, grid, a_spec, b_spec, o_spec, (tm, tn), add, add_spec)


def mm_dw(name, a, dy, out_dtype, n_shards=None, tm=512, tn=512, b_spec=None, N=None):
    T, K = a.shape
    N = dy.shape[-1] if N is None else N
    tm = _tile(K, tm)
    if n_shards:
        C = N // n_shards
        tn = _tile(C, tn)
        nps = C // tn
        out_sds = _sds((n_shards, K, C), out_dtype)
        o_spec = pl.BlockSpec((None, tm, tn), lambda i, j, k: (j // nps, i, j % nps))
    else:
        tn = _tile(N, tn)
        out_sds = _sds((K, N), out_dtype)
        o_spec = pl.BlockSpec((tm, tn), lambda i, j, k: (i, j))
    grid = (K // tm, N // tn, 1)
    a_spec = pl.BlockSpec((T, tm), lambda i, j, k: (0, i))
    if b_spec is None:
        b_spec = pl.BlockSpec((T, tn), lambda i, j, k: (0, j))
    else:
        b_spec = b_spec(T, tn)
    return _mm(name, "tn", a, dy, out_sds, grid, a_spec, b_spec, o_spec, (tm, tn))


def _rms(x, g):
    r = lax.rsqrt(jnp.mean(x * x, axis=-1, keepdims=True) + EPS)
    return x * r * g


def rms_fwd(name, x, g, tr=512, after=None):
    T, D = x.shape
    tr = min(tr, T)

    def body(x_ref, g_ref, *rest):
        o_ref = rest[-1]
        o_ref[...] = _rms(x_ref[...], g_ref[...]).astype(o_ref.dtype)

    row = pl.BlockSpec((tr, D), lambda i: (i, 0))
    vec = pl.BlockSpec((1, D), lambda i: (0, 0))
    extra = [] if after is None else [after]
    return pl.pallas_call(body, name=name, grid=(T // tr,), in_specs=[row, vec] + [ANY] * len(extra), out_specs=row,
                          out_shape=_sds((T, D), BF16), compiler_params=_params(("parallel",)))(x, g, *extra)


def resid_norm(name, x, m, g_post, g_next, tr=512):
    T, D = x.shape
    tr = min(tr, T)
    has_next = g_next is not None

    def body(*refs):
        if has_next:
            x_ref, m_ref, gp_ref, gn_ref, xo_ref, h_ref = refs
        else:
            x_ref, m_ref, gp_ref, xo_ref = refs
        xn = x_ref[...] + _rms(m_ref[...], gp_ref[...])
        xo_ref[...] = xn
        if has_next:
            h_ref[...] = _rms(xn, gn_ref[...]).astype(h_ref.dtype)

    row = pl.BlockSpec((tr, D), lambda i: (i, 0))
    vec = pl.BlockSpec((1, D), lambda i: (0, 0))
    ins = [row, row, vec] + ([vec] if has_next else [])
    args = (x, m, g_post) + ((g_next,) if has_next else ())
    outs = [row, row] if has_next else row
    shp = [_sds((T, D), F32), _sds((T, D), BF16)] if has_next else _sds((T, D), F32)
    res = pl.pallas_call(body, name=name, grid=(T // tr,), in_specs=ins, out_specs=outs, out_shape=shp,
                         compiler_params=_params(("parallel",)))(*args)
    return res if has_next else (res, None)


def rms_bwd(name, xin, g, dy, resid, out_dtype, tr=512, after=None):
    T, D = xin.shape
    tr = min(tr, T)
    has_res = resid is not None

    def body(*refs):
        dx_ref, dg_ref = refs[-2:]
        if has_res:
            x_ref, g_ref, dy_ref, r_ref = refs[:4]
        else:
            x_ref, g_ref, dy_ref = refs[:3]
        x = x_ref[...].astype(F32)
        dy_ = dy_ref[...].astype(F32)
        r = lax.rsqrt(jnp.mean(x * x, axis=-1, keepdims=True) + EPS)
        xh = x * r
        dxh = dy_ * g_ref[...]
        dx = r * (dxh - xh * jnp.mean(dxh * xh, axis=-1, keepdims=True))
        if has_res:
            dx = dx + r_ref[...]
        dx_ref[...] = dx.astype(dx_ref.dtype)
        part = jnp.sum(dy_ * xh, axis=0, keepdims=True)

        @pl.when(pl.program_id(0) == 0)
        def _():
            dg_ref[...] = part

        @pl.when(pl.program_id(0) > 0)
        def _():
            dg_ref[...] += part

    row = pl.BlockSpec((tr, D), lambda i: (i, 0))
    vec = pl.BlockSpec((1, D), lambda i: (0, 0))
    ins = [row, vec, row] + ([row] if has_res else []) + ([] if after is None else [ANY])
    args = (xin, g, dy) + ((resid,) if has_res else ()) + (() if after is None else (after,))
    return pl.pallas_call(body, name=name, grid=(T // tr,), in_specs=ins, out_specs=[row, vec],
                          out_shape=[_sds((T, D), out_dtype), _sds((1, D), F32)],
                          compiler_params=_params(("arbitrary",)))(*args)


def loss_fwd_bwd(name, y, tgt, tr=512):
    T, D = y.shape
    tr = min(tr, T)

    def body(y_ref, t_ref, l_ref, d_ref):
        e = y_ref[...] - t_ref[...]
        d_ref[...] = e * (1.0 / D)
        part = 0.5 * jnp.sum(jnp.mean(e * e, axis=-1, keepdims=True), axis=0, keepdims=True)
        part = jnp.broadcast_to(part, l_ref.shape)

        @pl.when(pl.program_id(0) == 0)
        def _():
            l_ref[...] = part

        @pl.when(pl.program_id(0) > 0)
        def _():
            l_ref[...] += part

    row = pl.BlockSpec((tr, D), lambda i: (i, 0))
    return pl.pallas_call(body, name=name, grid=(T // tr,), in_specs=[row, row],
                          out_specs=[pl.BlockSpec((8, LANES), lambda i: (0, 0)), row],
                          out_shape=[_sds((8, LANES), F32), _sds((T, D), F32)],
                          compiler_params=_params(("arbitrary",)))(y, tgt)


def _conv_taps(p, w_ref, K):
    taps = [p] + [_shift_down(p, s) for s in range(1, K)]
    out = taps[0] * w_ref[pl.ds(K - 1, 1), :]
    for s in range(1, K):
        out = out + taps[s] * w_ref[pl.ds(K - 1 - s, 1), :]
    return taps, out


def _conv_bwd(dpre, taps, w_ref, dw_ref, db_ref, K):
    db_ref[...] = jnp.sum(dpre, axis=0, keepdims=True)
    dp = dpre * w_ref[pl.ds(K - 1, 1), :]
    dw_ref[pl.ds(K - 1, 1), :] = jnp.sum(dpre * taps[0], axis=0, keepdims=True)
    for s in range(1, K):
        dw_ref[pl.ds(K - 1 - s, 1), :] = jnp.sum(dpre * taps[s], axis=0, keepdims=True)
        dp = dp + _shift_up(dpre, s) * w_ref[pl.ds(K - 1 - s, 1), :]
    return dp


def ffn_gate_fwd(name, P, cw, cb, tc=LANES):
    _, T, F = P.shape
    K = cw.shape[0]
    nf = F // tc

    def body(pg_ref, pu_ref, wg_ref, wu_ref, bg_ref, bu_ref, o_ref):
        _, g = _conv_taps(pg_ref[...], wg_ref, K)
        _, u = _conv_taps(pu_ref[...], wu_ref, K)
        o_ref[...] = (_gelu(g + bg_ref[...]) * (u + bu_ref[...])).astype(o_ref.dtype)

    pg = pl.BlockSpec((None, T, tc), lambda j: (0, 0, j))
    pu = pl.BlockSpec((None, T, tc), lambda j: (1, 0, j))
    wg = pl.BlockSpec((K, tc), lambda j: (0, j))
    wu = pl.BlockSpec((K, tc), lambda j: (0, j + nf))
    bg = pl.BlockSpec((1, tc), lambda j: (0, j))
    bu = pl.BlockSpec((1, tc), lambda j: (0, j + nf))
    return pl.pallas_call(body, name=name, grid=(nf,), in_specs=[pg, pu, wg, wu, bg, bu],
                          out_specs=pl.BlockSpec((T, tc), lambda j: (0, j)), out_shape=_sds((T, F), BF16),
                          compiler_params=_params(("parallel",)))(P, P, cw, cw, cb, cb)


def ffn_gate_bwd(name, P, da, cw, cb, tc=LANES):
    _, T, F = P.shape
    K = cw.shape[0]
    nf = F // tc

    def body(pg_ref, pu_ref, da_ref, wg_ref, wu_ref, bg_ref, bu_ref, dp_ref, dwg_ref, dwu_ref, dbg_ref, dbu_ref):
        tg, g = _conv_taps(pg_ref[...], wg_ref, K)
        tu, u = _conv_taps(pu_ref[...], wu_ref, K)
        g = g + bg_ref[...]
        u = u + bu_ref[...]
        da_ = da_ref[...]
        dg = da_ * u * _dgelu(g)
        du = da_ * _gelu(g)
        dp_ref[0] = _conv_bwd(dg, tg, wg_ref, dwg_ref, dbg_ref, K).astype(dp_ref.dtype)
        dp_ref[1] = _conv_bwd(du, tu, wu_ref, dwu_ref, dbu_ref, K).astype(dp_ref.dtype)

    pg = pl.BlockSpec((None, T, tc), lambda j: (0, 0, j))
    pu = pl.BlockSpec((None, T, tc), lambda j: (1, 0, j))
    col = pl.BlockSpec((T, tc), lambda j: (0, j))
    wg = pl.BlockSpec((K, tc), lambda j: (0, j))
    wu = pl.BlockSpec((K, tc), lambda j: (0, j + nf))
    bg = pl.BlockSpec((1, tc), lambda j: (0, j))
    bu = pl.BlockSpec((1, tc), lambda j: (0, j + nf))
    return pl.pallas_call(
        body, name=name, grid=(nf,), in_specs=[pg, pu, col, wg, wu, bg, bu],
        out_specs=[pl.BlockSpec((2, T, tc), lambda j: (0, 0, j)), wg, wg, bg, bg],
        out_shape=[_sds((2, T, F), BF16), _sds((K, F), F32), _sds((K, F), F32), _sds((1, F), F32), _sds((1, F), F32)],
        compiler_params=_params(("parallel",)))(P, P, da, cw, cw, cb, cb)


def xattn_fwd(name, q, kv, n_heads, tq=512):
    T, Wd = q.shape
    Mm = kv.shape[0]
    hd = Wd // n_heads
    scale = hd ** -0.5
    tq = min(tq, T)

    def body(q_ref, kv_ref, o_ref):
        for h in range(n_heads):
            qh = q_ref[:, h * hd:(h + 1) * hd]
            kh = kv_ref[:, h * hd:(h + 1) * hd]
            vh = kv_ref[:, Wd + h * hd:Wd + (h + 1) * hd]
            s = _dot(qh, kh, NT) * scale
            s = s - jnp.max(s, axis=-1, keepdims=True)
            e = jnp.exp(s)
            p = e / jnp.sum(e, axis=-1, keepdims=True)
            o_ref[:, h * hd:(h + 1) * hd] = _dot(p.astype(BF16), vh).astype(o_ref.dtype)

    return pl.pallas_call(body, name=name, grid=(T // tq,),
                          in_specs=[pl.BlockSpec((tq, Wd), lambda i: (i, 0)), pl.BlockSpec((Mm, 2 * Wd), lambda i: (0, 0))],
                          out_specs=pl.BlockSpec((tq, Wd), lambda i: (i, 0)), out_shape=_sds((T, Wd), BF16),
                          compiler_params=_params(("parallel",)))(q, kv)


def xattn_bwd(name, q, kv, do, n_heads, tq=512):
    T, Wd = q.shape
    Mm = kv.shape[0]
    hd = Wd // n_heads
    scale = hd ** -0.5
    tq = min(tq, T)

    def body(q_ref, kv_ref, do_ref, dq_ref, dkv_ref):
        @pl.when(pl.program_id(0) == 0)
        def _():
            dkv_ref[...] = jnp.zeros_like(dkv_ref)

        for h in range(n_heads):
            sl = slice(h * hd, (h + 1) * hd)
            sv = slice(Wd + h * hd, Wd + (h + 1) * hd)
            qh, kh, vh = q_ref[:, sl], kv_ref[:, sl], kv_ref[:, sv]
            doh = do_ref[:, sl].astype(BF16)
            s = _dot(qh, kh, NT) * scale
            s = s - jnp.max(s, axis=-1, keepdims=True)
            e = jnp.exp(s)
            p = e / jnp.sum(e, axis=-1, keepdims=True)
            dp = _dot(doh, vh, NT)
            ds = (p * (dp - jnp.sum(dp * p, axis=-1, keepdims=True)) * scale).astype(BF16)
            dq_ref[:, sl] = _dot(ds, kh).astype(dq_ref.dtype)
            dkv_ref[:, sl] += _dot(ds, qh, TN)
            dkv_ref[:, sv] += _dot(p.astype(BF16), doh, TN)

    row = pl.BlockSpec((tq, Wd), lambda i: (i, 0))
    full = pl.BlockSpec((Mm, 2 * Wd), lambda i: (0, 0))
    return pl.pallas_call(body, name=name, grid=(T // tq,), in_specs=[row, full, row], out_specs=[row, full],
                          out_shape=[_sds((T, Wd), BF16), _sds((Mm, 2 * Wd), F32)],
                          compiler_params=_params(("arbitrary",)))(q, kv, do)


def ffn_fwd(hf, w_in, cw, cb, w_out):
    T = hf.shape[0]
    F = w_out.K
    tn = _tile(w_in.C, 512)
    nfp = F // tn
    P = mm_fwd("ffn_in", hf, w_in, F32, tn=tn, out_sds=_sds((2, T, F), F32),
               o_spec=lambda tm, tn_: pl.BlockSpec((None, tm, tn_), lambda i, j, k: (j // nfp, i, j % nfp)))
    a = ffn_gate_fwd("ffn_gate", P, cw, cb)
    f = mm_fwd("ffn_out", a, w_out, F32, tn=1024)
    return f, (P, a)


def ffn_bwd(hf, saved, df, w_in, cw, cb, w_out):
    P, a = saved
    T = hf.shape[0]
    F = w_out.K
    C = w_in.C
    da = mm_dx("ffn_out_dx", df, w_out, F32, tn=w_out.R)
    dw_out = mm_dw("ffn_out_dw", a, df, BF16, tn=1024)
    dP, dcw_g, dcw_u, dcb_g, dcb_u = ffn_gate_bwd("ffn_gate_bwd", P, da, cw, cb)
    dw_in = mm_dw("ffn_in_dw", hf, dP, BF16, n_shards=w_in.S, N=2 * F,
                  b_spec=lambda T_, tn: pl.BlockSpec((None, T_, tn), lambda i, j, k: (j // (F // tn), 0, j % (F // tn))))
    per = F // C
    dhf = mm_dx("ffn_in_dx", dP, w_in, F32,
                a_spec=lambda tm, tk: pl.BlockSpec((None, tm, tk), lambda i, j, k: (k // per, i, k % per)))
    grads = dict(ffn_w_in=dw_in, ffn_w_out=dw_out, ffn_conv_w=jnp.concatenate([dcw_g, dcw_u], axis=1),
                 ffn_conv_b=jnp.concatenate([dcb_g, dcb_u], axis=1))
    return dhf, grads


def xa_fwd(hq, mem_n, wq, wkv, wo, n_heads):
    q = mm_fwd("xa_q", hq, wq, BF16)
    kv = mm_fwd("xa_kv", mem_n, wkv, BF16)
    o = xattn_fwd("xa_core", q, kv, n_heads)
    c = mm_fwd("xa_o", o, wo, F32)
    return c, (q, kv, o)


def xa_bwd(hq, mem_n, saved, dc, wq, wkv, wo, n_heads):
    q, kv, o = saved
    do = mm_dx("xa_o_dx", dc, wo, F32)
    dwo = mm_dw("xa_o_dw", o, dc, BF16, n_shards=wo.S)
    dq, dkv = xattn_bwd("xa_core_bwd", q, kv, do, n_heads)
    dkv = dkv.astype(BF16)
    dwq = mm_dw("xa_q_dw", hq, dq, BF16)
    dhq = mm_dx("xa_q_dx", dq, wq, F32, tn=wq.R)
    dwkv = mm_dw("xa_kv_dw", mem_n, dkv, BF16)
    dmem_n = mm_dx("xa_kv_dx", dkv, wkv, F32, tn=wkv.R)
    return dhq, dmem_n, dict(xa_wq=dwq, xa_wkv=dwkv, xa_wo=dwo)


def _sb_logits(q, kblk, scale):
    z = _dot(q, kblk, NT) * scale
    l1 = -_softplus(z)
    return z, l1, z + l1


def _split2(a):
    a1 = a.astype(BF16)
    return a1, (a - a1.astype(F32)).astype(BF16)


def _dot2r(a, m):
    p1, p2 = _split2(a)
    return _dot(p1, m) + _dot(p2, m)


SB_TQ = 512


def sb_fwd(name, qkv, n_heads, tq=SB_TQ):
    T = qkv.shape[0]
    hd = qkv.shape[1] // (3 * n_heads)
    scale = hd ** -0.5
    Q = CHUNK
    tq = min(tq, T)
    nb = tq // Q

    def body(q_ref, k_ref, v_ref, o_ref, lt_ref):
        i = pl.program_id(1)
        q = q_ref[...]
        mcat = jnp.concatenate([_tri(Q, "lt"), jnp.ones((Q, Q), jnp.bool_)], axis=1).astype(BF16)
        t_idx, s_idx = _iota((tq, Q), 0), _iota((tq, Q), 1)

        def block(kb, carry, band):
            c, acc = carry
            off = pl.multiple_of(kb * Q, Q)
            kblk, vblk = k_ref[pl.ds(off, Q), :], v_ref[pl.ds(off, Q), :]
            _, l1, lb = _sb_logits(q, kblk, scale)
            if band is not None:
                valid = band * Q + s_idx < t_idx
                l1 = jnp.where(valid, l1, 0.0)
            r = _dot2r(l1, mcat)
            a = jnp.exp(lb + r[:, :Q] + c)
            if band is not None:
                a = jnp.where(valid, a, 0.0)
            return c + r[:, Q:], acc + _dot(a.astype(BF16), vblk)

        carry = (jnp.zeros((tq, Q), F32), jnp.zeros((tq, hd), F32))
        for b in reversed(range(nb)):
            carry = block(i * nb + b, carry, b)
        c, acc = lax.fori_loop(0, i * nb, lambda r, cr: block(i * nb - 1 - r, cr, None), carry)
        o_ref[...] = acc.astype(o_ref.dtype)
        lt_ref[...] = c

    H = n_heads
    return pl.pallas_call(
        body, name=name, grid=(H, T // tq),
        in_specs=[pl.BlockSpec((tq, hd), lambda h, i: (i, h)), pl.BlockSpec((T, hd), lambda h, i: (0, H + h)),
                  pl.BlockSpec((T, hd), lambda h, i: (0, 2 * H + h))],
        out_specs=[pl.BlockSpec((tq, hd), lambda h, i: (i, h)), pl.BlockSpec((None, tq, Q), lambda h, i: (h, i, 0))],
        out_shape=[_sds((T, H * hd), BF16), _sds((H, T, Q), F32)],
        compiler_params=_params(("parallel", "arbitrary")))(qkv, qkv, qkv)


def sb_bwd(name, qkv, do, lt, n_heads, tq=SB_TQ):
    T = qkv.shape[0]
    hd = qkv.shape[1] // (3 * n_heads)
    scale = hd ** -0.5
    Q = CHUNK
    tq = min(tq, T)
    nb = tq // Q

    def body(q_ref, k_ref, v_ref, do_ref, lt_ref, dq_ref, dk_ref, dv_ref):
        i = pl.program_id(1)

        @pl.when(i == 0)
        def _():
            dk_ref[...] = jnp.zeros_like(dk_ref)
            dv_ref[...] = jnp.zeros_like(dv_ref)

        q, do_, ltot = q_ref[...], do_ref[...], lt_ref[...]
        ones = jnp.ones((Q, Q), jnp.bool_)
        mrev = jnp.concatenate([_tri(Q, "lt"), ones], axis=1).astype(BF16)
        mfwd = jnp.concatenate([_tri(Q, "gt"), ones], axis=1).astype(BF16)
        t_idx, s_idx = _iota((tq, Q), 0), _iota((tq, Q), 1)

        def block(kb, carry, band):
            pin, pre, dq = carry
            off = pl.multiple_of(kb * Q, Q)
            kblk, vblk = k_ref[pl.ds(off, Q), :], v_ref[pl.ds(off, Q), :]
            _, l1, lb = _sb_logits(q, kblk, scale)
            if band is not None:
                valid = band * Q + s_idx < t_idx
                l1 = jnp.where(valid, l1, 0.0)
            r = _dot2r(l1, mrev)
            pin = pin + r[:, Q:]
            a = jnp.exp(lb + r[:, :Q] + (ltot - pin))
            if band is not None:
                a = jnp.where(valid, a, 0.0)
            de = _dot(do_, vblk, NT) * a
            r2 = _dot2r(de, mfwd)
            dl1 = pre + r2[:, :Q]
            pre = pre + r2[:, Q:]
            sig = jnp.exp(lb)
            dz = (de * (1.0 - sig) - dl1 * sig) * scale
            if band is not None:
                dz = jnp.where(valid, dz, 0.0)
            dzb = dz.astype(BF16)
            dk_ref[pl.ds(off, Q), :] += _dot(dzb, q, TN)
            dv_ref[pl.ds(off, Q), :] += _dot(a.astype(BF16), do_, TN)
            return pin, pre, dq + _dot(dzb, kblk)

        init = (jnp.zeros((tq, Q), F32), jnp.zeros((tq, Q), F32), jnp.zeros((tq, hd), F32))
        carry = lax.fori_loop(0, i * nb, lambda kb, cr: block(kb, cr, None), init)
        for b in range(nb):
            carry = block(i * nb + b, carry, b)
        dq_ref[...] = carry[2].astype(dq_ref.dtype)

    H = n_heads
    qs = pl.BlockSpec((tq, hd), lambda h, i: (i, h))
    full = pl.BlockSpec((T, hd), lambda h, i: (0, h))
    return pl.pallas_call(
        body, name=name, grid=(H, T // tq),
        in_specs=[qs, pl.BlockSpec((T, hd), lambda h, i: (0, H + h)), pl.BlockSpec((T, hd), lambda h, i: (0, 2 * H + h)),
                  qs, pl.BlockSpec((None, tq, Q), lambda h, i: (h, i, 0))],
        out_specs=[qs, full, full],
        out_shape=[_sds((T, H * hd), BF16), _sds((T, H * hd), F32), _sds((T, H * hd), F32)],
        compiler_params=_params(("parallel", "arbitrary")))(qkv, qkv, qkv, do, lt)


def sb_mixer_fwd(hn, w_qkv, w_out, n_heads):
    qkv = mm_fwd("sb_qkv", hn, w_qkv, BF16)
    o, lt = sb_fwd("sb_core", qkv, n_heads)
    m = mm_fwd("sb_out", o, w_out, F32, tn=1024)
    return m, (qkv, o, lt)


def sb_mixer_bwd(hn, saved, dm, w_qkv, w_out, n_heads):
    qkv, o, lt = saved
    do = mm_dx("sb_out_dx", dm, w_out, BF16, tn=w_out.R)
    dw_out = mm_dw("sb_out_dw", o, dm, BF16, tn=1024)
    dq, dk, dv = sb_bwd("sb_core_bwd", qkv, do, lt, n_heads)
    dqkv = jnp.concatenate([dq, dk.astype(BF16), dv.astype(BF16)], axis=1)
    dw_qkv = mm_dw("sb_qkv_dw", hn, dqkv, BF16, n_shards=w_qkv.S)
    dhn = mm_dx("sb_qkv_dx", dqkv, w_qkv, F32)
    return dhn, dict(sb_w_qkv=dw_qkv, sb_w_out=dw_out)


def _sgu_common(p_ref, vg_ref, vb_ref, Wd):
    pu, pv = p_ref[:, :Wd], p_ref[:, Wd:]
    u, v = _gelu(pu), _gelu(pv)
    xc = v - jnp.mean(v, axis=-1, keepdims=True)
    r = lax.rsqrt(jnp.mean(xc * xc, axis=-1, keepdims=True) + EPS)
    xh = xc * r
    return pu, pv, u, xh, r, xh * vg_ref[...] + vb_ref[...]


def sgu_fwd(name, P, vg, vb, ws, bexp):
    T = P.shape[0]
    Wd = P.shape[1] // 2
    G = ws.shape[0]
    gw = Wd // G
    Q = CHUNK

    def body(p_ref, vg_ref, vb_ref, ws_ref, be_ref, o_ref):
        _, _, u, _, _, vn = _sgu_common(p_ref, vg_ref, vb_ref, Wd)
        tril = _tri(Q, "le")
        for g in range(G):
            sl = slice(g * gw, (g + 1) * gw)
            wsg = jnp.where(tril, ws_ref[g], 0.0).astype(BF16)
            mixed = _dot(wsg, vn[:, sl].astype(BF16)) + be_ref[:, sl]
            o_ref[:, sl] = (u[:, sl] * mixed).astype(o_ref.dtype)

    vec = pl.BlockSpec((1, Wd), lambda c: (0, 0))
    return pl.pallas_call(
        body, name=name, grid=(T // Q,),
        in_specs=[pl.BlockSpec((Q, 2 * Wd), lambda c: (c, 0)), vec, vec, pl.BlockSpec((G, Q, Q), lambda c: (0, 0, 0)),
                  pl.BlockSpec((Q, Wd), lambda c: (0, 0))],
        out_specs=pl.BlockSpec((Q, Wd), lambda c: (c, 0)), out_shape=_sds((T, Wd), BF16),
        compiler_params=_params(("parallel",)))(P, vg, vb, ws, bexp)


def sgu_bwd(name, P, dgated, vg, vb, ws, bexp):
    T = P.shape[0]
    Wd = P.shape[1] // 2
    G = ws.shape[0]
    gw = Wd // G
    Q = CHUNK
    nc = T // Q

    def body(p_ref, dg_ref, vg_ref, vb_ref, ws_ref, be_ref, dp_ref, dws_ref, dvg_ref, dvb_ref, dbs_ref, dvn_scr, dbe_scr):
        c = pl.program_id(0)

        @pl.when(c == 0)
        def _():
            dws_ref[...] = jnp.zeros_like(dws_ref)
            dvg_ref[...] = jnp.zeros_like(dvg_ref)
            dvb_ref[...] = jnp.zeros_like(dvb_ref)
            dbe_scr[...] = jnp.zeros_like(dbe_scr)

        pu, pv, u, xh, r, vn = _sgu_common(p_ref, vg_ref, vb_ref, Wd)
        tril = _tri(Q, "le")
        for g in range(G):
            sl = slice(g * gw, (g + 1) * gw)
            wsg = jnp.where(tril, ws_ref[g], 0.0).astype(BF16)
            vng = vn[:, sl].astype(BF16)
            mixed = _dot(wsg, vng) + be_ref[:, sl]
            dgt = dg_ref[:, sl]
            dp_ref[:, sl] = (dgt * mixed * _dgelu(pu[:, sl])).astype(dp_ref.dtype)
            dmix = dgt * u[:, sl]
            dmb = dmix.astype(BF16)
            dws_ref[g] += jnp.where(tril, _dot(dmb, vng, NT), 0.0)
            dvn_scr[:, sl] = _dot(wsg, dmb, TN)
            dbe_scr[:, sl] += dmix
        dvn = dvn_scr[...]
        dvg_ref[...] += jnp.sum(dvn * xh, axis=0, keepdims=True)
        dvb_ref[...] += jnp.sum(dvn, axis=0, keepdims=True)
        dxh = dvn * vg_ref[...]
        dv = r * (dxh - jnp.mean(dxh, axis=-1, keepdims=True) - xh * jnp.mean(dxh * xh, axis=-1, keepdims=True))
        dp_ref[:, Wd:] = (dv * _dgelu(pv)).astype(dp_ref.dtype)

        @pl.when(c == nc - 1)
        def _():
            sel = (_iota((Wd, LANES), 0) // gw == _iota((Wd, LANES), 1)).astype(BF16)
            dbs_ref[...] = _dot3r(dbe_scr[...], sel)

    vec = pl.BlockSpec((1, Wd), lambda c: (0, 0))
    wsb = pl.BlockSpec((G, Q, Q), lambda c: (0, 0, 0))
    return pl.pallas_call(
        body, name=name, grid=(nc,),
        in_specs=[pl.BlockSpec((Q, 2 * Wd), lambda c: (c, 0)), pl.BlockSpec((Q, Wd), lambda c: (c, 0)), vec, vec, wsb,
                  pl.BlockSpec((Q, Wd), lambda c: (0, 0))],
        out_specs=[pl.BlockSpec((Q, 2 * Wd), lambda c: (c, 0)), wsb, vec, vec, pl.BlockSpec((Q, LANES), lambda c: (0, 0))],
        out_shape=[_sds((T, 2 * Wd), BF16), _sds((G, Q, Q), F32), _sds((1, Wd), F32), _sds((1, Wd), F32), _sds((Q, LANES), F32)],
        scratch_shapes=[pltpu.VMEM((Q, Wd), F32), pltpu.VMEM((Q, Wd), F32)],
        compiler_params=_params(("arbitrary",)))(P, dgated, vg, vb, ws, bexp)


def sg_mixer_fwd(hn, w_in, vg, vb, ws, bs, w_out):
    G = ws.shape[0]
    Wd = vg.shape[1]
    P = mm_fwd("sg_in", hn, w_in, F32)
    bexp = jnp.repeat(bs.T, Wd // G, axis=1)
    gated = sgu_fwd("sg_core", P, vg, vb, ws, bexp)
    m = mm_fwd("sg_out", gated, w_out, F32, tn=1024)
    return m, (P, bexp, gated)


def sg_mixer_bwd(hn, saved, dm, w_in, vg, vb, ws, w_out):
    P, bexp, gated = saved
    G = ws.shape[0]
    dgated = mm_dx("sg_out_dx", dm, w_out, F32, tn=w_out.R)
    dw_out = mm_dw("sg_out_dw", gated, dm, BF16, tn=1024)
    dP, dws, dvg, dvb, dbs = sgu_bwd("sg_core_bwd", P, dgated, vg, vb, ws, bexp)
    dw_in = mm_dw("sg_in_dw", hn, dP, BF16, n_shards=w_in.S)
    dhn = mm_dx("sg_in_dx", dP, w_in, F32)
    grads = dict(sg_w_in=dw_in, sg_w_out=dw_out, sg_w_spatial=dws, sg_v_norm_g=dvg, sg_v_norm_b=dvb,
                 sg_b_spatial=dbs[:, :G].T)
    return dhn, grads


def ssd_conv_fwd(name, xbc, cw, cb, tc=LANES):
    T, Cd = xbc.shape
    K = cw.shape[0]

    def body(p_ref, w_ref, b_ref, o_ref):
        _, pre = _conv_taps(p_ref[...], w_ref, K)
        o_ref[...] = _silu(pre + b_ref[...])

    col = pl.BlockSpec((T, tc), lambda j: (0, j))
    return pl.pallas_call(body, name=name, grid=(Cd // tc,),
                          in_specs=[col, pl.BlockSpec((K, tc), lambda j: (0, j)), pl.BlockSpec((1, tc), lambda j: (0, j))],
                          out_specs=col, out_shape=_sds((T, Cd), F32), compiler_params=_params(("parallel",)))(xbc, cw, cb)


def ssd_conv_bwd(name, xbc, dact, cw, cb, tc=LANES):
    T, Cd = xbc.shape
    K = cw.shape[0]

    def body(p_ref, da_ref, w_ref, b_ref, dp_ref, dw_ref, db_ref):
        taps, pre = _conv_taps(p_ref[...], w_ref, K)
        dpre = da_ref[...] * _dsilu(pre + b_ref[...])
        dp_ref[...] = _conv_bwd(dpre, taps, w_ref, dw_ref, db_ref, K).astype(dp_ref.dtype)

    col = pl.BlockSpec((T, tc), lambda j: (0, j))
    wsp = pl.BlockSpec((K, tc), lambda j: (0, j))
    bsp = pl.BlockSpec((1, tc), lambda j: (0, j))
    return pl.pallas_call(body, name=name, grid=(Cd // tc,), in_specs=[col, col, wsp, bsp], out_specs=[col, wsp, bsp],
                          out_shape=[_sds((T, Cd), BF16), _sds((K, Cd), F32), _sds((1, Cd), F32)],
                          compiler_params=_params(("parallel",)))(xbc, dact, cw, cb)


def _expand_matrix(Hd):
    return (_iota((LANES, Hd), 1) // SSD_HEAD_DIM == _iota((LANES, Hd), 0)).astype(BF16)


def ssd_dt_fwd(name, dtr, bias, Hd, tr=512):
    T = dtr.shape[0]
    tr = min(tr, T)

    def body(d_ref, b_ref, o_ref):
        o_ref[...] = _dot3r(_softplus(d_ref[...] + b_ref[...]), _expand_matrix(Hd))

    return pl.pallas_call(body, name=name, grid=(T // tr,),
                          in_specs=[pl.BlockSpec((tr, LANES), lambda i: (i, 0)), pl.BlockSpec((1, LANES), lambda i: (0, 0))],
                          out_specs=pl.BlockSpec((tr, Hd), lambda i: (i, 0)), out_shape=_sds((T, Hd), F32),
                          compiler_params=_params(("parallel",)))(dtr, bias)


def ssd_dt_bwd(name, dtr, bias, ddtx, tr=512):
    T, Hd = ddtx.shape
    tr = min(tr, T)

    def body(d_ref, b_ref, g_ref, o_ref, db_ref):
        p1, p2, p3 = _split3(g_ref[...])
        em = _expand_matrix(Hd)
        ddt = _dot(p1, em, NT) + _dot(p2, em, NT) + _dot(p3, em, NT)
        draw = ddt * _sigmoid(d_ref[...] + b_ref[...])
        o_ref[...] = draw.astype(o_ref.dtype)
        part = jnp.sum(draw, axis=0, keepdims=True)

        @pl.when(pl.program_id(0) == 0)
        def _():
            db_ref[...] = part

        @pl.when(pl.program_id(0) > 0)
        def _():
            db_ref[...] += part

    row = pl.BlockSpec((tr, LANES), lambda i: (i, 0))
    vec = pl.BlockSpec((1, LANES), lambda i: (0, 0))
    return pl.pallas_call(body, name=name, grid=(T // tr,), in_specs=[row, vec, pl.BlockSpec((tr, Hd), lambda i: (i, 0))],
                          out_specs=[row, vec], out_shape=[_sds((T, LANES), BF16), _sds((1, LANES), F32)],
                          compiler_params=_params(("arbitrary",)))(dtr, bias, ddtx)


def _ssd_head_terms(a2, a2r, half, cb, causal, lane):
    hm = (lane < SSD_HEAD_DIM) if half == 0 else (lane >= SSD_HEAD_DIM)
    ccol = jnp.where(hm, a2, a2r)
    lm = jnp.exp(jnp.where(causal, ccol - ccol.T, -jnp.inf))
    return hm, lm, cb * lm


def ssd_core_fwd(name, act, dtx, alx, dx, G):
    T, Hd = dtx.shape
    Q, N = CHUNK, SSD_STATE
    gw = Hd // G
    nc = T // Q
    nx = Hd // N

    def body(xs_ref, b_ref, c_ref, dt_ref, al_ref, d_ref, y_ref, ss_ref, st_scr):
        @pl.when(pl.program_id(1) == 0)
        def _():
            st_scr[...] = jnp.zeros_like(st_scr)

        xs, dtv = xs_ref[...], dt_ref[...]
        Bb, Cb = b_ref[...].astype(BF16), c_ref[...].astype(BF16)
        dA = dtv * (-jnp.exp(al_ref[...]))
        a = _dot3l(_tri(Q, "le").astype(BF16), dA)
        a_last = jnp.sum(dA, axis=0, keepdims=True)
        xdt = xs * dtv
        cbm = _dot(Cb, Bb, NT)
        sprev = st_scr[...]
        ss_ref[...] = sprev
        causal, lane = _tri(Q, "le"), _iota((Q, LANES), 1)
        y_rest = _dot(Cb, sprev.astype(BF16)) * jnp.exp(a) + xs * d_ref[...]
        for q in range(gw // LANES):
            sl = slice(q * LANES, (q + 1) * LANES)
            a2, x2 = a[:, sl], xdt[:, sl]
            a2r = pltpu.roll(a2, SSD_HEAD_DIM, 1)
            acc = y_rest[:, sl]
            for half in (0, 1):
                hm, _, gm = _ssd_head_terms(a2, a2r, half, cbm, causal, lane)
                acc = acc + _dot(gm.astype(BF16), jnp.where(hm, x2, 0.0).astype(BF16))
            y_ref[:, sl] = acc
        w = jnp.exp(a_last - a)
        st_scr[...] = sprev * jnp.exp(a_last) + _dot(Bb, (w * xdt).astype(BF16), TN)

    xsp = pl.BlockSpec((Q, gw), lambda g, c: (c, g))
    vec = pl.BlockSpec((1, gw), lambda g, c: (0, g))
    return pl.pallas_call(
        body, name=name, grid=(G, nc),
        in_specs=[xsp, pl.BlockSpec((Q, N), lambda g, c: (c, nx + g)), pl.BlockSpec((Q, N), lambda g, c: (c, nx + G + g)),
                  xsp, vec, vec],
        out_specs=[xsp, pl.BlockSpec((None, N, gw), lambda g, c: (c, 0, g))],
        out_shape=[_sds((T, Hd), F32), _sds((nc, N, Hd), F32)],
        scratch_shapes=[pltpu.VMEM((N, gw), F32)],
        compiler_params=_params(("parallel", "arbitrary")))(act, act, act, dtx, alx, dx)


def ssd_core_bwd(name, act, dtx, alx, dx, ssave, dy, G):
    T, Hd = dtx.shape
    Q, N = CHUNK, SSD_STATE
    gw = Hd // G
    nc = T // Q
    nx = Hd // N

    def body(xs_ref, b_ref, c_ref, dt_ref, al_ref, d_ref, ss_ref, dy_ref,
             dxs_ref, db_ref, dc_ref, ddt_ref, dal_ref, dd_ref, ds_scr, dxdt_scr, da_scr):
        @pl.when(pl.program_id(1) == 0)
        def _():
            ds_scr[...] = jnp.zeros_like(ds_scr)
            dal_ref[...] = jnp.zeros_like(dal_ref)
            dd_ref[...] = jnp.zeros_like(dd_ref)

        xs, dtv, dy_ = xs_ref[...], dt_ref[...], dy_ref[...]
        Bb, Cb = b_ref[...].astype(BF16), c_ref[...].astype(BF16)
        Ax = -jnp.exp(al_ref[...])
        dA = dtv * Ax
        a = _dot3l(_tri(Q, "le").astype(BF16), dA)
        a_last = jnp.sum(dA, axis=0, keepdims=True)
        xdt = xs * dtv
        e, w, eal = jnp.exp(a), jnp.exp(a_last - a), jnp.exp(a_last)
        sprev, dsn = ss_ref[...], ds_scr[...]
        sprevb, dsnb = sprev.astype(BF16), dsn.astype(BF16)

        dd_ref[...] += jnp.sum(dy_ * xs, axis=0, keepdims=True)
        dmb = (dy_ * e).astype(BF16)
        dC = _dot(dmb, sprevb, NT)
        ds_scr[...] = _dot(Cb, dmb, TN) + dsn * eal
        dalast = jnp.sum(dsn * sprev, axis=0, keepdims=True) * eal
        dB = _dot((w * xdt).astype(BF16), dsnb, NT)
        dwx = _dot(Bb, dsnb)
        dww = dwx * xdt * w
        dalast = dalast + jnp.sum(dww, axis=0, keepdims=True)
        da_scr[...] = dy_ * _dot(Cb, sprevb) * e - dww
        dxdt_scr[...] = w * dwx
        cbm = _dot(Cb, Bb, NT)
        dcb = jnp.zeros((Q, Q), F32)
        causal, lane = _tri(Q, "le"), _iota((Q, LANES), 1)
        for q in range(gw // LANES):
            sl = slice(q * LANES, (q + 1) * LANES)
            a2, x2, dy2 = a[:, sl], xdt[:, sl], dy_[:, sl]
            a2r = pltpu.roll(a2, SSD_HEAD_DIM, 1)
            for half in (0, 1):
                hm, lm, gm = _ssd_head_terms(a2, a2r, half, cbm, causal, lane)
                dyh = jnp.where(hm, dy2, 0.0).astype(BF16)
                dg = _dot(dyh, jnp.where(hm, x2, 0.0).astype(BF16), NT)
                dxdt_scr[:, sl] += _dot(gm.astype(BF16), dyh, TN)
                dcb = dcb + dg * lm
                dseg = dg * gm
                v = jnp.sum(dseg, axis=1, keepdims=True) - jnp.sum(dseg.T, axis=1, keepdims=True)
                da_scr[:, sl] += jnp.where(hm, v, 0.0) * (1.0 / SSD_HEAD_DIM)
        dcbb = dcb.astype(BF16)
        dc_ref[...] = dC + _dot(dcbb, Bb)
        db_ref[...] = dB + _dot(dcbb, Cb, TN)
        dxdt = dxdt_scr[...]
        dxs_ref[...] = dy_ * d_ref[...] + dxdt * dtv
        da = da_scr[...] + jnp.where(_iota((Q, gw), 0) == Q - 1, dalast, 0.0)
        dda = _dot3l(_tri(Q, "ge").astype(BF16), da)
        ddt_ref[...] = dxdt * xs + dda * Ax
        dal_ref[...] += jnp.sum(dda * dtv, axis=0, keepdims=True) * Ax

    rc = lambda c: nc - 1 - c
    xsp = pl.BlockSpec((Q, gw), lambda g, c: (rc(c), g))
    bsp = pl.BlockSpec((Q, N), lambda g, c: (rc(c), nx + g))
    csp = pl.BlockSpec((Q, N), lambda g, c: (rc(c), nx + G + g))
    gsp = pl.BlockSpec((Q, N), lambda g, c: (rc(c), g))
    vec = pl.BlockSpec((1, gw), lambda g, c: (0, g))
    return pl.pallas_call(
        body, name=name, grid=(G, nc),
        in_specs=[xsp, bsp, csp, xsp, vec, vec, pl.BlockSpec((None, N, gw), lambda g, c: (rc(c), 0, g)), xsp],
        out_specs=[xsp, gsp, gsp, xsp, vec, vec],
        out_shape=[_sds((T, Hd), F32), _sds((T, G * N), F32), _sds((T, G * N), F32), _sds((T, Hd), F32),
                   _sds((1, Hd), F32), _sds((1, Hd), F32)],
        scratch_shapes=[pltpu.VMEM((N, gw), F32), pltpu.VMEM((Q, gw), F32), pltpu.VMEM((Q, gw), F32)],
        compiler_params=_params(("parallel", "arbitrary")))(act, act, act, dtx, alx, dx, ssave, dy)


def ssd_gate_fwd(name, y, z, ng, tr=256):
    T, Hd = y.shape
    tr = min(tr, T)

    def body(y_ref, z_ref, g_ref, o_ref):
        o_ref[...] = _rms(y_ref[...] * _silu(z_ref[...]), g_ref[...]).astype(o_ref.dtype)

    row = pl.BlockSpec((tr, Hd), lambda i: (i, 0))
    return pl.pallas_call(body, name=name, grid=(T // tr,), in_specs=[row, row, pl.BlockSpec((1, Hd), lambda i: (0, 0))],
                          out_specs=row, out_shape=_sds((T, Hd), BF16), compiler_params=_params(("parallel",)))(y, z, ng)


def ssd_gate_bwd(name, y, z, ng, dyn, tr=128):
    T, Hd = y.shape
    tr = min(tr, T)

    def body(y_ref, z_ref, g_ref, dn_ref, dy_ref, dz_ref, dg_ref):
        y_, z_, dn = y_ref[...], z_ref[...], dn_ref[...]
        y2 = y_ * _silu(z_)
        r = lax.rsqrt(jnp.mean(y2 * y2, axis=-1, keepdims=True) + EPS)
        xh = y2 * r
        dxh = dn * g_ref[...]
        dy2 = r * (dxh - xh * jnp.mean(dxh * xh, axis=-1, keepdims=True))
        dy_ref[...] = dy2 * _silu(z_)
        dz_ref[...] = (dy2 * y_ * _dsilu(z_)).astype(dz_ref.dtype)
        part = jnp.sum(dn * xh, axis=0, keepdims=True)

        @pl.when(pl.program_id(0) == 0)
        def _():
            dg_ref[...] = part

        @pl.when(pl.program_id(0) > 0)
        def _():
            dg_ref[...] += part

    row = pl.BlockSpec((tr, Hd), lambda i: (i, 0))
    vec = pl.BlockSpec((1, Hd), lambda i: (0, 0))
    return pl.pallas_call(body, name=name, grid=(T // tr,), in_specs=[row, row, vec, row], out_specs=[row, row, vec],
                          out_shape=[_sds((T, Hd), F32), _sds((T, Hd), BF16), _sds((1, Hd), F32)],
                          compiler_params=_params(("arbitrary",)))(y, z, ng, dyn)


def ssd_mixer_fwd(hn, wz, wxbc, wdt, cw, cb, dtb, alx, dx, ng, w_out, G):
    Hd = wz.N
    z = mm_fwd("ssd_z", hn, wz, F32)
    xbc = mm_fwd("ssd_xbc", hn, wxbc, F32)
    dtr = mm_fwd("ssd_dt", hn, wdt, F32)
    act = ssd_conv_fwd("ssd_conv", xbc, cw, cb)
    dtx = ssd_dt_fwd("ssd_dtx", dtr, dtb, Hd)
    y, ssave = ssd_core_fwd("ssd_core", act, dtx, alx, dx, G)
    yn = ssd_gate_fwd("ssd_gate", y, z, ng)
    m = mm_fwd("ssd_out", yn, w_out, F32, tn=1024)
    return m, (z, xbc, dtr, act, dtx, y, ssave, yn)


def ssd_mixer_bwd(hn, saved, dm, wz, wxbc, wdt, cw, cb, dtb, alx, dx, ng, w_out, G):
    z, xbc, dtr, act, dtx, y, ssave, yn = saved
    dyn = mm_dx("ssd_out_dx", dm, w_out, F32, tn=w_out.R)
    dw_out = mm_dw("ssd_out_dw", yn, dm, BF16, tn=1024)
    dy, dz, dng = ssd_gate_bwd("ssd_gate_bwd", y, z, ng, dyn)
    dxs, dB, dC, ddtx, dalx, ddx = ssd_core_bwd("ssd_core_bwd", act, dtx, alx, dx, ssave, dy, G)
    dxbc, dcw, dcb = ssd_conv_bwd("ssd_conv_bwd", xbc, jnp.concatenate([dxs, dB, dC], axis=1), cw, cb)
    ddtr, ddtb = ssd_dt_bwd("ssd_dtx_bwd", dtr, dtb, ddtx)
    dwz = mm_dw("ssd_z_dw", hn, dz, BF16)
    dwxbc = mm_dw("ssd_xbc_dw", hn, dxbc, BF16)
    dwdt = mm_dw("ssd_dt_dw", hn, ddtr, BF16)
    dhn = mm_dx("ssd_z_dx", dz, wz, F32)
    dhn = mm_dx("ssd_xbc_dx", dxbc, wxbc, F32, add=dhn)
    dhn = mm_dx("ssd_dt_dx", ddtr, wdt, F32, add=dhn)
    grads = dict(ssd_w_out=dw_out, ssd_wz=dwz, ssd_wxbc=dwxbc, ssd_wdt=dwdt, ssd_conv_w=dcw, ssd_conv_b=dcb,
                 ssd_dt_bias=ddtb, ssd_alx=dalx, ssd_dx=ddx, ssd_norm=dng)
    return dhn, grads


def adamw(name, w, m, v, ga, gb=None):
    Rr, C = w.shape
    tr = 8
    while Rr % (tr * 2) == 0 and tr * 2 * C * 4 <= (1 << 20):
        tr *= 2
    if Rr % tr:
        tr = Rr
    two = gb is not None
    c1 = 1.0 - ADAM_B1 ** ADAM_STEP
    c2 = 1.0 - ADAM_B2 ** ADAM_STEP

    def body(*refs):
        if two:
            w_ref, m_ref, v_ref, a_ref, b_ref, g_ref, d_ref, mo_ref, vo_ref = refs
            g = a_ref[...] + b_ref[...]
        else:
            w_ref, m_ref, v_ref, a_ref, g_ref, d_ref, mo_ref, vo_ref = refs
            g = a_ref[...]
        m2 = ADAM_B1 * m_ref[...] + (1.0 - ADAM_B1) * g
        v2 = ADAM_B2 * v_ref[...] + (1.0 - ADAM_B2) * (g * g)
        g_ref[...] = g
        mo_ref[...] = m2
        vo_ref[...] = v2
        d_ref[...] = -ADAM_LR * ((m2 / c1) / (jnp.sqrt(v2 / c2) + ADAM_EPS) + ADAM_WD * w_ref[...])

    blk = pl.BlockSpec((tr, C), lambda i: (i, 0))
    n_in = 5 if two else 4
    args = (w, m, v, ga) + ((gb,) if two else ())
    return pl.pallas_call(body, name=name, grid=(Rr // tr,), in_specs=[blk] * n_in, out_specs=[blk] * 4,
                          out_shape=[_sds((Rr, C), F32)] * 4, compiler_params=_params(("parallel",)))(*args)


def _mesh_pos():
    return lax.axis_index("x"), lax.axis_index("y"), lax.axis_index("c")


def _peer_chips(x, y):
    return [(1 - x, y), (x, 1 - y), (1 - x, 1 - y)]


def sum_slots(name, own, r):
    _, Rr, C = r.shape
    tr = 8
    while Rr % (tr * 2) == 0 and tr * 2 * C * 4 <= (1 << 20):
        tr *= 2

    def body(o_in, r_ref, o_ref):
        o_ref[...] = ((o_in[...].astype(F32) + r_ref[0].astype(F32)) + r_ref[1].astype(F32)) + r_ref[2].astype(F32)

    blk = pl.BlockSpec((tr, C), lambda i: (i, 0))
    return pl.pallas_call(body, name=name, grid=(Rr // tr,), in_specs=[blk, pl.BlockSpec((3, tr, C), lambda i: (0, i, 0))],
                          out_specs=blk, out_shape=_sds((Rr, C), F32), compiler_params=_params(("parallel",)))(own, r)


HBM = pl.BlockSpec(memory_space=pltpu.HBM)
SEM = pl.BlockSpec(memory_space=pltpu.SEMAPHORE)
EFFECT = pltpu.SideEffectType.DATAFLOW_SIDE_EFFECTING


def _xchg_copy(mode, side, src, land, send, recv, t, j, peer, me, c):
    px, py = peer
    pidx = 2 * px + py
    if mode == "gather":
        s, dst = src, land.at[me if side == "out" else pidx]
    else:
        s, dst = src.at[pidx], land.at[j]
    k = 3 * t + j
    return pltpu.make_async_remote_copy(src_ref=s, dst_ref=dst, send_sem=send.at[k], recv_sem=recv.at[k],
                                        device_id=(px, py, c), device_id_type=MESH)


def xchg_start(name, mode, srcs, lands):
    counts = [len(g) for g in srcs]
    ng = len(counts)
    fs = [a for g in srcs for a in g]
    fl = [a for g in lands for a in g]
    n = len(fs)

    def body(*refs):
        src, land = refs[:n], refs[n:2 * n]
        send, recv = refs[2 * n:2 * n + ng], refs[2 * n + ng:2 * n + 2 * ng]
        token = refs[-1]
        x, y, c = _mesh_pos()
        me = 2 * x + y
        k = 0
        for gi in range(ng):
            for t in range(counts[gi]):
                for j, peer in enumerate(_peer_chips(x, y)):
                    _xchg_copy(mode, "out", src[k], land[k], send[gi], recv[gi], t, j, peer, me, c).start()
                k += 1
        token[...] = jnp.zeros_like(token)

    sems = tuple(pltpu.SemaphoreType.DMA((3 * cnt,)) for cnt in counts)
    thru = tuple(pltpu.HBM(a.shape, a.dtype) for a in fs + fl)
    out = pl.pallas_call(
        body, name=name, in_specs=[HBM] * (2 * n),
        out_specs=(SEM,) * (2 * ng) + (HBM,) * (2 * n) + (pl.BlockSpec(memory_space=pltpu.VMEM),),
        out_shape=sems + sems + thru + (_sds((8, LANES), F32),),
        input_output_aliases={i: 2 * ng + i for i in range(2 * n)},
        compiler_params=pltpu.CompilerParams(has_side_effects=EFFECT),
    )(*[pltpu.with_memory_space_constraint(a, pltpu.HBM) for a in fs + fl])
    send, recv = out[:ng], out[ng:2 * ng]
    thru_s, thru_l = out[2 * ng:2 * ng + n], out[2 * ng + n:2 * ng + 2 * n]
    groups, k = [], 0
    for gi, cnt in enumerate(counts):
        groups.append(dict(send=send[gi], recv=recv[gi], src=list(thru_s[k:k + cnt]), land=list(thru_l[k:k + cnt])))
        k += cnt
    return groups, out[-1]


def xchg_wait(name, mode, grp, after):
    src, land = grp["src"], grp["land"]
    n = len(src)

    def body(*refs):
        s_ref, l_ref = refs[:n], refs[n:2 * n]
        send, recv = refs[2 * n], refs[2 * n + 1]
        x, y, c = _mesh_pos()
        me = 2 * x + y
        for t in range(n):
            for j, peer in enumerate(_peer_chips(x, y)):
                _xchg_copy(mode, "out", s_ref[t], l_ref[t], send, recv, t, j, peer, me, c).wait_send()
                _xchg_copy(mode, "in", s_ref[t], l_ref[t], send, recv, t, j, peer, me, c).wait_recv()

    res = pl.pallas_call(
        body, name=name, in_specs=[HBM] * (2 * n) + [SEM, SEM, ANY],
        out_specs=(HBM,) * (2 * n), out_shape=tuple(pltpu.HBM(a.shape, a.dtype) for a in src + land),
        input_output_aliases={i: i for i in range(2 * n)},
        compiler_params=pltpu.CompilerParams(has_side_effects=EFFECT),
    )(*src, *land, grp["send"], grp["recv"], after)
    return list(res[:n]), list(res[n:])


def swap_with_sibling(name, tensors):
    n = len(tensors)

    def body(*refs):
        ins, outs = refs[:n], refs[n:2 * n]
        send_sems, recv_sems = refs[2 * n:]
        x, y, c = _mesh_pos()
        cps = []
        for t in range(n):
            cp = pltpu.make_async_remote_copy(src_ref=ins[t], dst_ref=outs[t], send_sem=send_sems.at[t], recv_sem=recv_sems.at[t],
                                              device_id=(x, y, 1 - c), device_id_type=MESH)
            cp.start()
            cps.append(cp)
        for cp in cps:
            cp.wait()

    return pl.pallas_call(
        body, name=name, in_specs=[ANY] * n, out_specs=[ANY] * n, out_shape=[_sds(t.shape, t.dtype) for t in tensors],
        scratch_shapes=[pltpu.SemaphoreType.DMA((n,)), pltpu.SemaphoreType.DMA((n,))],
    )(*tensors)


def all_reduce_small(name, v):
    Rr, C = v.shape
    nd = 8

    def body(v_ref, o_ref, gath, send_sems, recv_sems):
        x, y, c = _mesh_pos()
        me = 4 * x + 2 * y + c
        cps = []
        for d in range(1, nd):
            bx, by, bc = (d >> 2) & 1, (d >> 1) & 1, d & 1
            tgt = (1 - x if bx else x, 1 - y if by else y, 1 - c if bc else c)
            cp = pltpu.make_async_remote_copy(src_ref=v_ref, dst_ref=gath.at[me], send_sem=send_sems.at[d - 1],
                                              recv_sem=recv_sems.at[d - 1], device_id=tgt, device_id_type=MESH)
            cp.start()
            cps.append((cp, tgt))
        gath[me] = v_ref[...]
        for d in range(1, nd):
            _, (tx, ty, tc) = cps[d - 1]
            pltpu.make_async_remote_copy(src_ref=v_ref, dst_ref=gath.at[4 * tx + 2 * ty + tc], send_sem=send_sems.at[d - 1],
                                         recv_sem=recv_sems.at[d - 1], device_id=(tx, ty, tc), device_id_type=MESH).wait_recv()
        acc = gath[0]
        for d in range(1, nd):
            acc = acc + gath[d]
        o_ref[...] = acc
        for cp, _ in cps:
            cp.wait_send()

    vm = pl.BlockSpec(memory_space=pltpu.VMEM)
    return pl.pallas_call(
        body, name=name, in_specs=[vm], out_specs=vm, out_shape=_sds((Rr, C), F32),
        scratch_shapes=[pltpu.VMEM((nd, Rr, C), F32), pltpu.SemaphoreType.DMA((nd - 1,)), pltpu.SemaphoreType.DMA((nd - 1,))],
        compiler_params=pltpu.CompilerParams(vmem_limit_bytes=VMEM_LIMIT),
    )(v)


def _pack(arrs):
    flat = jnp.concatenate([a.reshape(-1) for a in arrs])
    pad = (-flat.shape[0]) % (8 * LANES)
    return jnp.pad(flat, (0, pad)).reshape(-1, LANES)


def _unpack(buf, shapes):
    flat = buf.reshape(-1)
    out, off = [], 0
    for s in shapes:
        n = math.prod(s)
        out.append(flat[off:off + n].reshape(s))
        off += n
    return out


WEIGHTS = ["ln_mix_pre", "ln_mix_post", "ln_mem", "ln_xa_pre", "ln_xa_post", "ln_ffn_pre", "ln_ffn_post", "xa_wq", "xa_wkv",
           "xa_wo", "ffn_w_in", "ffn_conv_w", "ffn_conv_b", "ffn_w_out", "ssd_w_in", "ssd_conv_w", "ssd_conv_b", "ssd_dt_bias",
           "ssd_a_log", "ssd_d", "ssd_norm", "ssd_w_out", "sg_w_in", "sg_v_norm_g", "sg_v_norm_b", "sg_w_spatial",
           "sg_b_spatial", "sg_w_out", "sb_w_qkv", "sb_w_out"]
BIG = {"xa_wq": "rows", "xa_wkv": "rows", "xa_wo": "cols", "ffn_w_in": "cols", "ffn_w_out": "rows", "ssd_w_in": "cols",
       "ssd_w_out": "rows", "sg_w_in": "cols", "sg_w_out": "rows", "sb_w_qkv": "cols", "sb_w_out": "rows"}
SHARDED_SMALL = {"ffn_conv_w": 2, "ssd_conv_w": 2, "ssd_conv_b": 1, "ssd_norm": 1}
SMALL = [n for n in WEIGHTS if n not in BIG]
N_MIXERS = 3
HEAD = 128


def _unshard(a, axis):
    a = jnp.moveaxis(a, 0, axis)
    s = a.shape
    return a.reshape(s[:axis] + (s[axis] * s[axis + 1],) + s[axis + 2:])


def _step(p):
    x, mem, tgt = p["x"][0], p["mem"][0], p["loss_target"][0]
    T, D = x.shape
    depth = p["ln_mix_pre"].shape[0]
    S = N_CHIPS

    me = 2 * lax.axis_index("x") + lax.axis_index("y")
    Hd, Cd = S * p["ssd_norm"].shape[1], S * p["ssd_conv_b"].shape[1]
    nh = p["ssd_dt_bias"].shape[1]
    G = (Cd - Hd) // (2 * SSD_STATE)
    xa_heads = p["xa_wo"].shape[1] // HEAD
    sb_heads = D // HEAD

    def layer_tensors(i):
        kind, j = i % N_MIXERS, i // N_MIXERS
        items = [("xa_wq", i), ("xa_wkv", i), ("xa_wo", i), ("ffn_w_in", i), ("ffn_w_out", i), ("ffn_conv_w", i)]
        if kind == 0:
            items += [("ssd_w_in", j), ("ssd_w_out", j), ("ssd_conv_w", j), ("ssd_conv_b", j), ("ssd_norm", j)]
        elif kind == 1:
            items += [("sg_w_in", j), ("sg_w_out", j)]
        else:
            items += [("sb_w_qkv", j), ("sb_w_out", j)]
        return items

    srcs, lands = [], []
    for i in range(depth):
        s_i, l_i = [], []
        for n, k in layer_tensors(i):
            a = p[n][k].astype(BF16) if n in BIG else p[n][k]
            a = a.reshape((1,) * (2 - a.ndim) + a.shape)
            s_i.append(a)
            l_i.append(lax.dynamic_update_index_in_dim(jnp.zeros((S,) + a.shape, a.dtype), a, me, 0))
        srcs.append(s_i)
        lands.append(l_i)
    gather_groups, gather_token = xchg_start("gather_start", "gather", srcs, lands)

    def layer_weights(i, after):
        _, zones = xchg_wait("gather_wait_%d" % i, "gather", gather_groups[i], after)
        return dict(zip(layer_tensors(i), zones))

    def layer_args(i, g):
        kind, j = i % N_MIXERS, i // N_MIXERS
        w_of = lambda n, k: W(BIG[n], g[(n, k)][:, None], 0)
        a = dict(xa=(w_of("xa_wq", i), w_of("xa_wkv", i), w_of("xa_wo", i), xa_heads),
                 ffn=(w_of("ffn_w_in", i), _unshard(g[("ffn_conv_w", i)], 1), p["ffn_conv_b"][i:i + 1], w_of("ffn_w_out", i)))
        if kind == 0:
            w_in = _unshard(g[("ssd_w_in", j)], 1)[None]
            a["mix"] = (W("full", w_in[:, :, :Hd], 0), W("full", w_in[:, :, Hd:Hd + Cd], 0),
                        W("full", jnp.pad(w_in[:, :, Hd + Cd:], ((0, 0), (0, 0), (0, LANES - nh))), 0),
                        _unshard(g[("ssd_conv_w", j)], 1), _unshard(g[("ssd_conv_b", j)], 1),
                        jnp.pad(p["ssd_dt_bias"][j], (0, LANES - nh))[None], jnp.repeat(p["ssd_a_log"][j], SSD_HEAD_DIM)[None],
                        jnp.repeat(p["ssd_d"][j], SSD_HEAD_DIM)[None], _unshard(g[("ssd_norm", j)], 1), w_of("ssd_w_out", j), G)
        elif kind == 1:
            a["mix"] = (w_of("sg_w_in", j), p["sg_v_norm_g"][j:j + 1], p["sg_v_norm_b"][j:j + 1], p["sg_w_spatial"][j],
                        w_of("sg_w_out", j))
        else:
            a["mix"] = (w_of("sb_w_qkv", j), w_of("sb_w_out", j), sb_heads)
        return a

    ln = lambda n, i: p[n][i:i + 1]

    h = rms_fwd("rms_first", x, ln("ln_mix_pre", 0), after=gather_token)
    saved, largs = [], []
    for i in range(depth):
        kind, j = i % N_MIXERS, i // N_MIXERS
        la = layer_args(i, layer_weights(i, x))
        largs.append(la)
        if kind == 0:
            m, ms = ssd_mixer_fwd(h, *la["mix"])
        elif kind == 1:
            m, ms = sg_mixer_fwd(h, *la["mix"][:4], p["sg_b_spatial"][j], la["mix"][4])
        else:
            m, ms = sb_mixer_fwd(h, *la["mix"])
        x1, hq = resid_norm("resid_norm", x, m, ln("ln_mix_post", i), ln("ln_xa_pre", i))
        mem_n = rms_fwd("rms_mem", mem, ln("ln_mem", i))
        c, cs = xa_fwd(hq, mem_n, *la["xa"])
        x2, hf = resid_norm("resid_norm", x1, c, ln("ln_xa_post", i), ln("ln_ffn_pre", i))
        f, fs = ffn_fwd(hf, *la["ffn"])
        x3, hn = resid_norm("resid_norm", x2, f, ln("ln_ffn_post", i), ln("ln_mix_pre", i + 1) if i + 1 < depth else None)
        saved.append(dict(x=x, h=h, m=m, ms=ms, x1=x1, hq=hq, mem_n=mem_n, c=c, cs=cs, x2=x2, hf=hf, f=f, fs=fs))
        x, h = x3, hn
    loss_tile, dx = loss_fwd_bwd("loss", x, tgt)
    loss = lax.psum(loss_tile[0, 0], ("x", "y", "c"))

    gs = {n: [None] * p[n].shape[0] for n in WEIGHTS}
    scatter_groups = [None] * depth
    token = None
    for i in reversed(range(depth)):
        kind, j = i % N_MIXERS, i // N_MIXERS
        s, la = saved[i], largs[i]
        df, gs["ln_ffn_post"][i] = rms_bwd("rms_bwd_post", s["f"], ln("ln_ffn_post", i), dx, None, BF16, after=token)
        dhf, g = ffn_bwd(s["hf"], s["fs"], df, *la["ffn"])
        gs["ffn_w_in"][i], gs["ffn_conv_w"][i], gs["ffn_conv_b"][i] = g["ffn_w_in"], g["ffn_conv_w"], g["ffn_conv_b"]
        gs["ffn_w_out"][i] = g["ffn_w_out"].reshape(S, -1, D)
        dx, gs["ln_ffn_pre"][i] = rms_bwd("rms_bwd_pre", s["x2"], ln("ln_ffn_pre", i), dhf, dx, F32)

        dc, gs["ln_xa_post"][i] = rms_bwd("rms_bwd_post", s["c"], ln("ln_xa_post", i), dx, None, BF16)
        dhq, dmem_n, g = xa_bwd(s["hq"], s["mem_n"], s["cs"], dc, *la["xa"])
        gs["xa_wq"][i] = g["xa_wq"].reshape(S, D // S, -1)
        gs["xa_wkv"][i] = g["xa_wkv"].reshape(S, D // S, -1)
        gs["xa_wo"][i] = g["xa_wo"]
        _, gs["ln_mem"][i] = rms_bwd("rms_bwd_mem", mem, ln("ln_mem", i), dmem_n, None, BF16)
        dx, gs["ln_xa_pre"][i] = rms_bwd("rms_bwd_pre", s["x1"], ln("ln_xa_pre", i), dhq, dx, F32)

        dm, gs["ln_mix_post"][i] = rms_bwd("rms_bwd_post", s["m"], ln("ln_mix_post", i), dx, None, BF16)
        if kind == 0:
            dhn, g = ssd_mixer_bwd(s["h"], s["ms"], dm, *la["mix"])
            full = jnp.concatenate([g["ssd_wz"], g["ssd_wxbc"], g["ssd_wdt"][:, :nh]], axis=1)
            gs["ssd_w_in"][j] = full.reshape(D, S, -1).transpose(1, 0, 2)
            gs["ssd_w_out"][j] = g["ssd_w_out"].reshape(S, Hd // S, D)
            gs["ssd_conv_w"][j], gs["ssd_conv_b"][j], gs["ssd_norm"][j] = g["ssd_conv_w"], g["ssd_conv_b"], g["ssd_norm"]
            gs["ssd_dt_bias"][j] = g["ssd_dt_bias"][:, :nh]
            gs["ssd_a_log"][j] = g["ssd_alx"].reshape(nh, SSD_HEAD_DIM).sum(-1)[None]
            gs["ssd_d"][j] = g["ssd_dx"].reshape(nh, SSD_HEAD_DIM).sum(-1)[None]
        elif kind == 1:
            dhn, g = sg_mixer_bwd(s["h"], s["ms"], dm, *la["mix"])
            gs["sg_w_in"][j] = g["sg_w_in"]
            gs["sg_w_out"][j] = g["sg_w_out"].reshape(S, -1, D)
            for n in ("sg_v_norm_g", "sg_v_norm_b", "sg_w_spatial", "sg_b_spatial"):
                gs[n][j] = g[n]
        else:
            dhn, g = sb_mixer_bwd(s["h"], s["ms"], dm, *la["mix"])
            gs["sb_w_qkv"][j] = g["sb_w_qkv"]
            gs["sb_w_out"][j] = g["sb_w_out"].reshape(S, -1, D)
        dx, gs["ln_mix_pre"][i] = rms_bwd("rms_bwd_pre", s["x"], ln("ln_mix_pre", i), dhn, dx, F32)

        big_i = [(n, k) for n, k in layer_tensors(i) if n in BIG]
        g_src = [gs[n][k] for n, k in big_i]
        g_land = [jnp.zeros((3,) + a.shape[1:], a.dtype) for a in g_src]
        grp, token = xchg_start("scatter_start_%d" % i, "scatter", [g_src], [g_land])
        scatter_groups[i] = (big_i, grp[0])

    sums, last = {}, token
    for i in reversed(range(depth)):
        big_i, grp = scatter_groups[i]
        sent, got = xchg_wait("scatter_wait_%d" % i, "scatter", grp, last if i == 0 else token)
        for (n, k), own, r in zip(big_i, sent, got):
            mine = lax.dynamic_index_in_dim(own, me, 0, keepdims=False)
            last = sums[(n, k)] = sum_slots("sum_grad_slots", mine.reshape(-1, mine.shape[-1]), r.reshape(3, -1, r.shape[-1]))
    stacked = [jnp.concatenate([sums[(n, k)] for k in range(p[n].shape[0])], axis=0) for n in BIG]
    sib = swap_with_sibling("swap_grads", stacked)
    out = {}
    for n, q, q2 in zip(BIG, stacked, sib):
        two_d = lambda a: a.reshape(-1, a.shape[-1])
        res = adamw("adamw_big", two_d(p[n]), two_d(p["m_" + n]), two_d(p["v_" + n]), q, q2)
        out[n] = [r.reshape(p[n].shape) for r in res]

    stack = lambda n: jnp.stack([a.reshape(p[n].shape[1:]) if n not in SHARDED_SMALL else a.reshape(a.shape[-len(p[n].shape) + 1:])
                                 for a in gs[n]])
    small_full = [stack(n) for n in SMALL]
    red = _unpack(all_reduce_small("reduce_small", _pack(small_full)), [a.shape for a in small_full])
    small_g = []
    for n, a in zip(SMALL, red):
        if n in SHARDED_SMALL:
            ax = SHARDED_SMALL[n]
            a = lax.dynamic_slice_in_dim(a, me * p[n].shape[ax], p[n].shape[ax], axis=ax)
        small_g.append(a)
    shapes = [p[n].shape for n in SMALL]
    res = adamw("adamw_small", _pack([p[n] for n in SMALL]), _pack([p["m_" + n] for n in SMALL]),
                _pack([p["v_" + n] for n in SMALL]), _pack(small_g))
    for k, r in enumerate(res):
        for n, a in zip(SMALL, _unpack(r, shapes)):
            out.setdefault(n, [None] * 4)[k] = a

    return (loss, dx[None]) + tuple(out[n][k] for k in range(4) for n in WEIGHTS)


def kernel(x, mem, ln_mix_pre, ln_mix_post, ln_mem, ln_xa_pre, ln_xa_post, ln_ffn_pre, ln_ffn_post, xa_wq, xa_wkv, xa_wo, ffn_w_in, ffn_conv_w, ffn_conv_b, ffn_w_out, ssd_w_in, ssd_conv_w, ssd_conv_b, ssd_dt_bias, ssd_a_log, ssd_d, ssd_norm, ssd_w_out, sg_w_in, sg_v_norm_g, sg_v_norm_b, sg_w_spatial, sg_b_spatial, sg_w_out, sb_w_qkv, sb_w_out, loss_target, m_ln_mix_pre, m_ln_mix_post, m_ln_mem, m_ln_xa_pre, m_ln_xa_post, m_ln_ffn_pre, m_ln_ffn_post, m_xa_wq, m_xa_wkv, m_xa_wo, m_ffn_w_in, m_ffn_conv_w, m_ffn_conv_b, m_ffn_w_out, m_ssd_w_in, m_ssd_conv_w, m_ssd_conv_b, m_ssd_dt_bias, m_ssd_a_log, m_ssd_d, m_ssd_norm, m_ssd_w_out, m_sg_w_in, m_sg_v_norm_g, m_sg_v_norm_b, m_sg_w_spatial, m_sg_b_spatial, m_sg_w_out, m_sb_w_qkv, m_sb_w_out, v_ln_mix_pre, v_ln_mix_post, v_ln_mem, v_ln_xa_pre, v_ln_xa_post, v_ln_ffn_pre, v_ln_ffn_post, v_xa_wq, v_xa_wkv, v_xa_wo, v_ffn_w_in, v_ffn_conv_w, v_ffn_conv_b, v_ffn_w_out, v_ssd_w_in, v_ssd_conv_w, v_ssd_conv_b, v_ssd_dt_bias, v_ssd_a_log, v_ssd_d, v_ssd_norm, v_ssd_w_out, v_sg_w_in, v_sg_v_norm_g, v_sg_v_norm_b, v_sg_w_spatial, v_sg_b_spatial, v_sg_w_out, v_sb_w_qkv, v_sb_w_out):
    return _step(dict(locals()))
```

```python
import functools
import math

import jax
import jax.numpy as jnp
from jax import lax
from jax.experimental import pallas as pl
from jax.experimental.pallas import tpu as pltpu

F32 = jnp.float32
BF16 = jnp.bfloat16
EPS = 1e-6
LANES = 128
VMEM_LIMIT = 56 * 1024 * 1024
CHUNK = 128
SSD_HEAD_DIM = 64
SSD_STATE = 128
N_CHIPS = 4
MESH = pl.DeviceIdType.MESH
ANY = pl.BlockSpec(memory_space=pl.ANY)

ADAM_LR, ADAM_B1, ADAM_B2, ADAM_EPS, ADAM_WD, ADAM_STEP = 0.001, 0.9, 0.999, 1e-08, 0.01, 10


def _params(sem):
    return pltpu.CompilerParams(dimension_semantics=sem, vmem_limit_bytes=VMEM_LIMIT)


def _sds(shape, dtype):
    return jax.ShapeDtypeStruct(tuple(shape), dtype)


def _tile(n, pref):
    if n <= pref:
        return n
    t = (pref // LANES) * LANES
    while t > LANES and n % t:
        t -= LANES
    assert n % t == 0, (n, pref)
    return t


def _split3(a):
    a1 = a.astype(BF16)
    r = a - a1.astype(F32)
    a2 = r.astype(BF16)
    a3 = (r - a2.astype(F32)).astype(BF16)
    return a1, a2, a3


def _dot(a, b, dims=(((1,), (0,)), ((), ()))):
    return lax.dot_general(a, b, dims, preferred_element_type=F32)


NN = (((1,), (0,)), ((), ()))
NT = (((1,), (1,)), ((), ()))
TN = (((0,), (0,)), ((), ()))


def _dot3r(a, m):
    p1, p2, p3 = _split3(a)
    return _dot(p1, m) + _dot(p2, m) + _dot(p3, m)


def _dot3l(m, a, dims=NN):
    p1, p2, p3 = _split3(a)
    return _dot(m, p1, dims) + _dot(m, p2, dims) + _dot(m, p3, dims)


def _iota(shape, dim):
    return lax.broadcasted_iota(jnp.int32, shape, dim)


def _tri(n, kind):
    r, c = _iota((n, n), 0), _iota((n, n), 1)
    return {"le": c <= r, "lt": c < r, "ge": c >= r, "gt": c > r}[kind]


def _sigmoid(x):
    return 1.0 / (1.0 + jnp.exp(-x))


def _silu(x):
    return x * _sigmoid(x)


def _dsilu(x):
    s = _sigmoid(x)
    return s * (1.0 + x * (1.0 - s))


_GC = math.sqrt(2.0 / math.pi)


def _gelu(x):
    return 0.5 * x * (1.0 + jnp.tanh(_GC * (x + 0.044715 * x * x * x)))


def _dgelu(x):
    th = jnp.tanh(_GC * (x + 0.044715 * x * x * x))
    return 0.5 * (1.0 + th) + 0.5 * x * (1.0 - th * th) * _GC * (1.0 + 3.0 * 0.044715 * x * x)


def _softplus(x):
    return jnp.maximum(x, 0.0) + jnp.log(1.0 + jnp.exp(-jnp.abs(x)))


def _shift_down(p, s):
    rows = _iota(p.shape, 0)
    return jnp.where(rows >= s, pltpu.roll(p, s, 0), 0.0)


def _shift_up(p, s):
    n = p.shape[0]
    rows = _iota(p.shape, 0)
    return jnp.where(rows < n - s, pltpu.roll(p, n - s, 0), 0.0)


def _mm(name, mode, a, b, out_sds, grid, a_spec, b_spec, o_spec, acc_shape, add=None, add_spec=None):
    dims = {"nn": NN, "nt": NT, "tn": TN}[mode]
    nk = grid[2]
    has_add = add is not None

    def body(*refs):
        if has_add:
            a_ref, b_ref, c_ref, o_ref = refs[:4]
        else:
            a_ref, b_ref, o_ref = refs[:3]
            c_ref = None
        part = lax.dot_general(a_ref[...], b_ref[...], dims, preferred_element_type=F32)

        def finish(r):
            if c_ref is not None:
                r = r + c_ref[...].astype(F32)
            o_ref[...] = r.astype(o_ref.dtype)

        if nk == 1:
            finish(part)
        else:
            acc = refs[-1]
            k = pl.program_id(2)

            @pl.when(k == 0)
            def _():
                acc[...] = part

            @pl.when(k > 0)
            def _():
                acc[...] += part

            @pl.when(k == nk - 1)
            def _():
                finish(acc[...])

    in_specs = [a_spec, b_spec] + ([add_spec] if has_add else [])
    args = (a, b) + ((add,) if has_add else ())
    return pl.pallas_call(
        body, name=name, grid=grid, in_specs=in_specs, out_specs=o_spec, out_shape=out_sds,
        scratch_shapes=[pltpu.VMEM(acc_shape, F32)] if nk > 1 else [],
        compiler_params=_params(("parallel", "parallel", "arbitrary")),
    )(*args)


class W:
    def __init__(self, kind, arr, layer):
        self.kind, self.arr, self.layer = kind, arr, layer
        if kind == "cols":
            s, _, k, c = arr.shape
            self.K, self.N, self.S, self.C = k, s * c, s, c
        elif kind == "rows":
            s, _, r, n = arr.shape
            self.K, self.N, self.S, self.R = s * r, n, s, r
        else:
            _, k, n = arr.shape
            self.K, self.N = k, n


def mm_fwd(name, a, w, out_dtype, tm=1024, tn=1536, a_spec=None, out_sds=None, o_spec=None, add=None):
    M = a.shape[0]
    tm = min(tm, M)
    l = w.layer
    if w.kind == "cols":
        tn = _tile(w.C, tn)
        nps = w.C // tn
        tk, nk = w.K, 1
        b_spec = pl.BlockSpec((None, None, tk, tn), lambda i, j, k: (j // nps, l, 0, j % nps))
    elif w.kind == "rows":
        tn = _tile(w.N, tn)
        tk, nk = w.R, w.S
        b_spec = pl.BlockSpec((None, None, tk, tn), lambda i, j, k: (k, l, 0, j))
    else:
        tn = _tile(w.N, tn)
        tk, nk = w.K, 1
        b_spec = pl.BlockSpec((None, tk, tn), lambda i, j, k: (l, 0, j))
    grid = (M // tm, w.N // tn, nk)
    if a_spec is None:
        a_spec = pl.BlockSpec((tm, tk), lambda i, j, k: (i, k))
    if out_sds is None:
        out_sds = _sds((M, w.N), out_dtype)
        o_spec = pl.BlockSpec((tm, tn), lambda i, j, k: (i, j))
    else:
        o_spec = o_spec(tm, tn)
    add_spec = pl.BlockSpec((tm, tn), lambda i, j, k: (i, j)) if add is not None else None
    return _mm(name, "nn", a, w.arr, out_sds, grid, a_spec, b_spec, o_spec, (tm, tn), add, add_spec)


def mm_dx(name, dy, w, out_dtype, tm=1024, tn=1024, a_spec=None, add=None):
    M = dy.shape[-2]
    tm = min(tm, M)
    l = w.layer
    if w.kind == "cols":
        tn = _tile(w.K, tn)
        tk, nk = w.C, w.S
        b_spec = pl.BlockSpec((None, None, tn, tk), lambda i, j, k: (k, l, j, 0))
    elif w.kind == "rows":
        tn = _tile(w.R, tn)
        npr = w.R // tn
        tk, nk = w.N, 1
        b_spec = pl.BlockSpec((None, None, tn, tk), lambda i, j, k: (j // npr, l, j % npr, 0))
    else:
        tn = _tile(w.K, tn)
        tk, nk = _tile(w.N, 2048), w.N // _tile(w.N, 2048)
        b_spec = pl.BlockSpec((None, tn, tk), lambda i, j, k: (l, j, k))
    grid = (M // tm, w.K // tn, nk)
    if a_spec is None:
        a_spec = pl.BlockSpec((tm, tk), lambda i, j, k: (i, k))
    else:
        a_spec = a_spec(tm, tk)
    out_sds = _sds((M, w.K), out_dtype)
    o_spec = pl.BlockSpec((tm, tn), lambda i, j, k: (i, j))
    add_spec = o_spec if add is not None else None
    return _mm(name, "nt", dy, w.arr, out_sds, grid, a_spec, b_spec, o_spec, (tm, tn), add, add_spec)


def mm_dw(name, a, dy, out_dtype, n_shards=None, tm=512, tn=1536, b_spec=None, N=None):
    T, K = a.shape
    N = dy.shape[-1] if N is None else N
    tm = _tile(K, tm)
    if n_shards:
        C = N // n_shards
        tn = _tile(C, tn)
        nps = C // tn
        out_sds = _sds((n_shards, K, C), out_dtype)
        o_spec = pl.BlockSpec((None, tm, tn), lambda i, j, k: (j // nps, i, j % nps))
    else:
        tn = _tile(N, tn)
        out_sds = _sds((K, N), out_dtype)
        o_spec = pl.BlockSpec((tm, tn), lambda i, j, k: (i, j))
    grid = (K // tm, N // tn, 1)
    a_spec = pl.BlockSpec((T, tm), lambda i, j, k: (0, i))
    if b_spec is None:
        b_spec = pl.BlockSpec((T, tn), lambda i, j, k: (0, j))
    else:
        b_spec = b_spec(T, tn)
    return _mm(name, "tn", a, dy, out_sds, grid, a_spec, b_spec, o_spec, (tm, tn))


def _rms(x, g):
    r = lax.rsqrt(jnp.mean(x * x, axis=-1, keepdims=True) + EPS)
    return x * r * g


def rms_fwd(name, x, g, tr=512, after=None):
    T, D = x.shape
    tr = min(tr, T)

    def body(x_ref, g_ref, *rest):
        o_ref = rest[-1]
        o_ref[...] = _rms(x_ref[...], g_ref[...]).astype(o_ref.dtype)

    row = pl.BlockSpec((tr, D), lambda i: (i, 0))
    vec = pl.BlockSpec((1, D), lambda i: (0, 0))
    extra = [] if after is None else [after]
    return pl.pallas_call(body, name=name, grid=(T // tr,), in_specs=[row, vec] + [ANY] * len(extra), out_specs=row,
                          out_shape=_sds((T, D), BF16), compiler_params=_params(("parallel",)))(x, g, *extra)


def resid_norm(name, x, m, g_post, g_next, tr=512):
    T, D = x.shape
    tr = min(tr, T)
    has_next = g_next is not None

    def body(*refs):
        if has_next:
            x_ref, m_ref, gp_ref, gn_ref, xo_ref, h_ref = refs
        else:
            x_ref, m_ref, gp_ref, xo_ref = refs
        xn = x_ref[...] + _rms(m_ref[...], gp_ref[...])
        xo_ref[...] = xn
        if has_next:
            h_ref[...] = _rms(xn, gn_ref[...]).astype(h_ref.dtype)

    row = pl.BlockSpec((tr, D), lambda i: (i, 0))
    vec = pl.BlockSpec((1, D), lambda i: (0, 0))
    ins = [row, row, vec] + ([vec] if has_next else [])
    args = (x, m, g_post) + ((g_next,) if has_next else ())
    outs = [row, row] if has_next else row
    shp = [_sds((T, D), F32), _sds((T, D), BF16)] if has_next else _sds((T, D), F32)
    res = pl.pallas_call(body, name=name, grid=(T // tr,), in_specs=ins, out_specs=outs, out_shape=shp,
                         compiler_params=_params(("parallel",)))(*args)
    return res if has_next else (res, None)


def rms_bwd(name, xin, g, dy, resid, out_dtype, tr=512, after=None):
    T, D = xin.shape
    tr = min(tr, T)
    has_res = resid is not None

    def body(*refs):
        dx_ref, dg_ref = refs[-2:]
        if has_res:
            x_ref, g_ref, dy_ref, r_ref = refs[:4]
        else:
            x_ref, g_ref, dy_ref = refs[:3]
        x = x_ref[...].astype(F32)
        dy_ = dy_ref[...].astype(F32)
        r = lax.rsqrt(jnp.mean(x * x, axis=-1, keepdims=True) + EPS)
        xh = x * r
        dxh = dy_ * g_ref[...]
        dx = r * (dxh - xh * jnp.mean(dxh * xh, axis=-1, keepdims=True))
        if has_res:
            dx = dx + r_ref[...]
        dx_ref[...] = dx.astype(dx_ref.dtype)
        part = jnp.sum(dy_ * xh, axis=0, keepdims=True)

        @pl.when(pl.program_id(0) == 0)
        def _():
            dg_ref[...] = part

        @pl.when(pl.program_id(0) > 0)
        def _():
            dg_ref[...] += part

    row = pl.BlockSpec((tr, D), lambda i: (i, 0))
    vec = pl.BlockSpec((1, D), lambda i: (0, 0))
    ins = [row, vec, row] + ([row] if has_res else []) + ([] if after is None else [ANY])
    args = (xin, g, dy) + ((resid,) if has_res else ()) + (() if after is None else (after,))
    return pl.pallas_call(body, name=name, grid=(T // tr,), in_specs=ins, out_specs=[row, vec],
                          out_shape=[_sds((T, D), out_dtype), _sds((1, D), F32)],
                          compiler_params=_params(("arbitrary",)))(*args)


def loss_fwd_bwd(name, y, tgt, tr=512):
    T, D = y.shape
    tr = min(tr, T)

    def body(y_ref, t_ref, l_ref, d_ref):
        e = y_ref[...] - t_ref[...]
        d_ref[...] = e * (1.0 / D)
        part = 0.5 * jnp.sum(jnp.mean(e * e, axis=-1, keepdims=True), axis=0, keepdims=True)
        part = jnp.broadcast_to(part, l_ref.shape)

        @pl.when(pl.program_id(0) == 0)
        def _():
            l_ref[...] = part

        @pl.when(pl.program_id(0) > 0)
        def _():
            l_ref[...] += part

    row = pl.BlockSpec((tr, D), lambda i: (i, 0))
    return pl.pallas_call(body, name=name, grid=(T // tr,), in_specs=[row, row],
                          out_specs=[pl.BlockSpec((8, LANES), lambda i: (0, 0)), row],
                          out_shape=[_sds((8, LANES), F32), _sds((T, D), F32)],
                          compiler_params=_params(("arbitrary",)))(y, tgt)


def _conv_taps(p, w_ref, K):
    taps = [p] + [_shift_down(p, s) for s in range(1, K)]
    out = taps[0] * w_ref[pl.ds(K - 1, 1), :]
    for s in range(1, K):
        out = out + taps[s] * w_ref[pl.ds(K - 1 - s, 1), :]
    return taps, out


def _conv_bwd(dpre, taps, w_ref, dw_ref, db_ref, K):
    db_ref[...] = jnp.sum(dpre, axis=0, keepdims=True)
    dp = dpre * w_ref[pl.ds(K - 1, 1), :]
    dw_ref[pl.ds(K - 1, 1), :] = jnp.sum(dpre * taps[0], axis=0, keepdims=True)
    for s in range(1, K):
        dw_ref[pl.ds(K - 1 - s, 1), :] = jnp.sum(dpre * taps[s], axis=0, keepdims=True)
        dp = dp + _shift_up(dpre, s) * w_ref[pl.ds(K - 1 - s, 1), :]
    return dp


def ffn_gate_fwd(name, P, cw, cb, tc=LANES):
    _, T, F = P.shape
    K = cw.shape[0]
    nf = F // tc

    def body(pg_ref, pu_ref, wg_ref, wu_ref, bg_ref, bu_ref, o_ref):
        _, g = _conv_taps(pg_ref[...], wg_ref, K)
        _, u = _conv_taps(pu_ref[...], wu_ref, K)
        o_ref[...] = (_gelu(g + bg_ref[...]) * (u + bu_ref[...])).astype(o_ref.dtype)

    pg = pl.BlockSpec((None, T, tc), lambda j: (0, 0, j))
    pu = pl.BlockSpec((None, T, tc), lambda j: (1, 0, j))
    wg = pl.BlockSpec((K, tc), lambda j: (0, j))
    wu = pl.BlockSpec((K, tc), lambda j: (0, j + nf))
    bg = pl.BlockSpec((1, tc), lambda j: (0, j))
    bu = pl.BlockSpec((1, tc), lambda j: (0, j + nf))
    return pl.pallas_call(body, name=name, grid=(nf,), in_specs=[pg, pu, wg, wu, bg, bu],
                          out_specs=pl.BlockSpec((T, tc), lambda j: (0, j)), out_shape=_sds((T, F), BF16),
                          compiler_params=_params(("parallel",)))(P, P, cw, cw, cb, cb)


def ffn_gate_bwd(name, P, da, cw, cb, tc=LANES):
    _, T, F = P.shape
    K = cw.shape[0]
    nf = F // tc

    def body(pg_ref, pu_ref, da_ref, wg_ref, wu_ref, bg_ref, bu_ref, dp_ref, dwg_ref, dwu_ref, dbg_ref, dbu_ref):
        tg, g = _conv_taps(pg_ref[...], wg_ref, K)
        tu, u = _conv_taps(pu_ref[...], wu_ref, K)
        g = g + bg_ref[...]
        u = u + bu_ref[...]
        da_ = da_ref[...]
        dg = da_ * u * _dgelu(g)
        du = da_ * _gelu(g)
        dp_ref[0] = _conv_bwd(dg, tg, wg_ref, dwg_ref, dbg_ref, K).astype(dp_ref.dtype)
        dp_ref[1] = _conv_bwd(du, tu, wu_ref, dwu_ref, dbu_ref, K).astype(dp_ref.dtype)

    pg = pl.BlockSpec((None, T, tc), lambda j: (0, 0, j))
    pu = pl.BlockSpec((None, T, tc), lambda j: (1, 0, j))
    col = pl.BlockSpec((T, tc), lambda j: (0, j))
    wg = pl.BlockSpec((K, tc), lambda j: (0, j))
    wu = pl.BlockSpec((K, tc), lambda j: (0, j + nf))
    bg = pl.BlockSpec((1, tc), lambda j: (0, j))
    bu = pl.BlockSpec((1, tc), lambda j: (0, j + nf))
    return pl.pallas_call(
        body, name=name, grid=(nf,), in_specs=[pg, pu, col, wg, wu, bg, bu],
        out_specs=[pl.BlockSpec((2, T, tc), lambda j: (0, 0, j)), wg, wg, bg, bg],
        out_shape=[_sds((2, T, F), BF16), _sds((K, F), F32), _sds((K, F), F32), _sds((1, F), F32), _sds((1, F), F32)],
        compiler_params=_params(("parallel",)))(P, P, da, cw, cw, cb, cb)


def xattn_fwd(name, q, kv, n_heads, tq=512):
    T, Wd = q.shape
    Mm = kv.shape[0]
    hd = Wd // n_heads
    scale = hd ** -0.5
    tq = min(tq, T)

    def body(q_ref, kv_ref, o_ref):
        for h in range(n_heads):
            qh = q_ref[:, h * hd:(h + 1) * hd]
            kh = kv_ref[:, h * hd:(h + 1) * hd]
            vh = kv_ref[:, Wd + h * hd:Wd + (h + 1) * hd]
            s = _dot(qh, kh, NT) * scale
            s = s - jnp.max(s, axis=-1, keepdims=True)
            e = jnp.exp(s)
            p = e / jnp.sum(e, axis=-1, keepdims=True)
            o_ref[:, h * hd:(h + 1) * hd] = _dot(p.astype(BF16), vh).astype(o_ref.dtype)

    return pl.pallas_call(body, name=name, grid=(T // tq,),
                          in_specs=[pl.BlockSpec((tq, Wd), lambda i: (i, 0)), pl.BlockSpec((Mm, 2 * Wd), lambda i: (0, 0))],
                          out_specs=pl.BlockSpec((tq, Wd), lambda i: (i, 0)), out_shape=_sds((T, Wd), BF16),
                          compiler_params=_params(("parallel",)))(q, kv)


def xattn_bwd(name, q, kv, do, n_heads, tq=512):
    T, Wd = q.shape
    Mm = kv.shape[0]
    hd = Wd // n_heads
    scale = hd ** -0.5
    tq = min(tq, T)

    def body(q_ref, kv_ref, do_ref, dq_ref, dkv_ref):
        @pl.when(pl.program_id(0) == 0)
        def _():
            dkv_ref[...] = jnp.zeros_like(dkv_ref)

        for h in range(n_heads):
            sl = slice(h * hd, (h + 1) * hd)
            sv = slice(Wd + h * hd, Wd + (h + 1) * hd)
            qh, kh, vh = q_ref[:, sl], kv_ref[:, sl], kv_ref[:, sv]
            doh = do_ref[:, sl].astype(BF16)
            s = _dot(qh, kh, NT) * scale
            s = s - jnp.max(s, axis=-1, keepdims=True)
            e = jnp.exp(s)
            p = e / jnp.sum(e, axis=-1, keepdims=True)
            dp = _dot(doh, vh, NT)
            ds = (p * (dp - jnp.sum(dp * p, axis=-1, keepdims=True)) * scale).astype(BF16)
            dq_ref[:, sl] = _dot(ds, kh).astype(dq_ref.dtype)
            dkv_ref[:, sl] += _dot(ds, qh, TN)
            dkv_ref[:, sv] += _dot(p.astype(BF16), doh, TN)

    row = pl.BlockSpec((tq, Wd), lambda i: (i, 0))
    full = pl.BlockSpec((Mm, 2 * Wd), lambda i: (0, 0))
    return pl.pallas_call(body, name=name, grid=(T // tq,), in_specs=[row, full, row], out_specs=[row, full],
                          out_shape=[_sds((T, Wd), BF16), _sds((Mm, 2 * Wd), F32)],
                          compiler_params=_params(("arbitrary",)))(q, kv, do)


def ffn_fwd(hf, w_in, cw, cb, w_out):
    T = hf.shape[0]
    F = w_out.K
    tn = _tile(w_in.C, 1536)
    nfp = F // tn
    P = mm_fwd("ffn_in", hf, w_in, F32, tn=tn, out_sds=_sds((2, T, F), F32),
               o_spec=lambda tm, tn_: pl.BlockSpec((None, tm, tn_), lambda i, j, k: (j // nfp, i, j % nfp)))
    a = ffn_gate_fwd("ffn_gate", P, cw, cb)
    f = mm_fwd("ffn_out", a, w_out, F32, tn=2048)
    return f, (P, a)


def ffn_bwd(hf, saved, df, w_in, cw, cb, w_out):
    P, a = saved
    T = hf.shape[0]
    F = w_out.K
    C = w_in.C
    da = mm_dx("ffn_out_dx", df, w_out, F32, tn=w_out.R)
    dw_out = mm_dw("ffn_out_dw", a, df, BF16, tn=1024)
    dP, dcw_g, dcw_u, dcb_g, dcb_u = ffn_gate_bwd("ffn_gate_bwd", P, da, cw, cb)
    dw_in = mm_dw("ffn_in_dw", hf, dP, BF16, n_shards=w_in.S, N=2 * F,
                  b_spec=lambda T_, tn: pl.BlockSpec((None, T_, tn), lambda i, j, k: (j // (F // tn), 0, j % (F // tn))))
    per = F // C
    dhf = mm_dx("ffn_in_dx", dP, w_in, F32,
                a_spec=lambda tm, tk: pl.BlockSpec((None, tm, tk), lambda i, j, k: (k // per, i, k % per)))
    grads = dict(ffn_w_in=dw_in, ffn_w_out=dw_out, ffn_conv_w=jnp.concatenate([dcw_g, dcw_u], axis=1),
                 ffn_conv_b=jnp.concatenate([dcb_g, dcb_u], axis=1))
    return dhf, grads


def xa_fwd(hq, mem_n, wq, wkv, wo, n_heads):
    q = mm_fwd("xa_q", hq, wq, BF16)
    kv = mm_fwd("xa_kv", mem_n, wkv, BF16)
    o = xattn_fwd("xa_core", q, kv, n_heads)
    c = mm_fwd("xa_o", o, wo, F32)
    return c, (q, kv, o)


def xa_bwd(hq, mem_n, saved, dc, wq, wkv, wo, n_heads):
    q, kv, o = saved
    do = mm_dx("xa_o_dx", dc, wo, F32)
    dwo = mm_dw("xa_o_dw", o, dc, BF16, n_shards=wo.S)
    dq, dkv = xattn_bwd("xa_core_bwd", q, kv, do, n_heads)
    dkv = dkv.astype(BF16)
    dwq = mm_dw("xa_q_dw", hq, dq, BF16)
    dhq = mm_dx("xa_q_dx", dq, wq, F32, tn=wq.R)
    dwkv = mm_dw("xa_kv_dw", mem_n, dkv, BF16)
    dmem_n = mm_dx("xa_kv_dx", dkv, wkv, F32, tn=wkv.R)
    return dhq, dmem_n, dict(xa_wq=dwq, xa_wkv=dwkv, xa_wo=dwo)


def _sb_logits(q, kblk, scale):
    z = _dot(q, kblk, NT) * scale
    l1 = -_softplus(z)
    return z, l1, z + l1


def _split2(a):
    a1 = a.astype(BF16)
    return a1, (a - a1.astype(F32)).astype(BF16)


def _dot2r(a, m):
    p1, p2 = _split2(a)
    return _dot(p1, m) + _dot(p2, m)


SB_TQ = 1024


def sb_fwd(name, qkv, n_heads, tq=SB_TQ):
    T = qkv.shape[0]
    hd = qkv.shape[1] // (3 * n_heads)
    scale = hd ** -0.5
    Q = CHUNK
    tq = min(tq, T)
    nb = tq // Q
    unroll = 2 if nb % 2 == 0 else 1

    def body(q_ref, k_ref, v_ref, o_ref, lt_ref):
        i = pl.program_id(1)
        q = q_ref[...]
        mcat = jnp.concatenate([_tri(Q, "lt"), jnp.ones((Q, Q), jnp.bool_)], axis=1).astype(BF16)

        def block(kb, carry, q_, masked):
            c, acc = carry
            off = pl.multiple_of(kb * Q, Q)
            kblk, vblk = k_ref[pl.ds(off, Q), :], v_ref[pl.ds(off, Q), :]
            _, l1, lb = _sb_logits(q_, kblk, scale)
            if masked:
                valid = _iota(l1.shape, 1) < _iota(l1.shape, 0)
                l1 = jnp.where(valid, l1, 0.0)
            r = _dot2r(l1, mcat)
            a = jnp.exp(lb + r[:, :Q] + c)
            if masked:
                a = jnp.where(valid, a, 0.0)
            return c + r[:, Q:], acc + _dot(a.astype(BF16), vblk)

        c, acc = jnp.zeros((tq, Q), F32), jnp.zeros((tq, hd), F32)
        for b in reversed(range(nb)):
            lo = b * Q
            cb, ab = block(i * nb + b, (c[lo:], acc[lo:]), q[lo:], True)
            c = cb if lo == 0 else jnp.concatenate([c[:lo], cb], axis=0)
            acc = ab if lo == 0 else jnp.concatenate([acc[:lo], ab], axis=0)

        def step(r, cr):
            for u in range(unroll):
                cr = block(i * nb - 1 - unroll * r - u, cr, q, False)
            return cr

        c, acc = lax.fori_loop(0, i * (nb // unroll), step, (c, acc))
        o_ref[...] = acc.astype(o_ref.dtype)
        lt_ref[...] = c

    H = n_heads
    return pl.pallas_call(
        body, name=name, grid=(H, T // tq),
        in_specs=[pl.BlockSpec((tq, hd), lambda h, i: (i, h)), pl.BlockSpec((T, hd), lambda h, i: (0, H + h)),
                  pl.BlockSpec((T, hd), lambda h, i: (0, 2 * H + h))],
        out_specs=[pl.BlockSpec((tq, hd), lambda h, i: (i, h)), pl.BlockSpec((None, tq, Q), lambda h, i: (h, i, 0))],
        out_shape=[_sds((T, H * hd), BF16), _sds((H, T, Q), F32)],
        compiler_params=_params(("parallel", "arbitrary")))(qkv, qkv, qkv)


def sb_bwd(name, qkv, do, lt, n_heads, tq=SB_TQ):
    T = qkv.shape[0]
    hd = qkv.shape[1] // (3 * n_heads)
    scale = hd ** -0.5
    Q = CHUNK
    tq = min(tq, T)
    nb = tq // Q
    unroll = 2 if nb % 2 == 0 else 1

    def body(q_ref, k_ref, v_ref, do_ref, lt_ref, dq_ref, dk_ref, dv_ref):
        i = pl.program_id(1)

        @pl.when(i == 0)
        def _():
            dk_ref[...] = jnp.zeros_like(dk_ref)
            dv_ref[...] = jnp.zeros_like(dv_ref)

        q, do_, ltot = q_ref[...], do_ref[...], lt_ref[...]
        ones = jnp.ones((Q, Q), jnp.bool_)
        mrev = jnp.concatenate([_tri(Q, "lt"), ones], axis=1).astype(BF16)
        mfwd = jnp.concatenate([_tri(Q, "gt"), ones], axis=1).astype(BF16)
        def block(kb, carry, q_, d_, lt_, masked):
            pin, pre, dq = carry
            off = pl.multiple_of(kb * Q, Q)
            kblk, vblk = k_ref[pl.ds(off, Q), :], v_ref[pl.ds(off, Q), :]
            _, l1, lb = _sb_logits(q_, kblk, scale)
            if masked:
                valid = _iota(l1.shape, 1) < _iota(l1.shape, 0)
                l1 = jnp.where(valid, l1, 0.0)
            r = _dot2r(l1, mrev)
            pin = pin + r[:, Q:]
            a = jnp.exp(lb + r[:, :Q] + (lt_ - pin))
            if masked:
                a = jnp.where(valid, a, 0.0)
            de = _dot(d_, vblk, NT) * a
            r2 = _dot2r(de, mfwd)
            dl1 = pre + r2[:, :Q]
            pre = pre + r2[:, Q:]
            sig = jnp.exp(lb)
            dz = (de * (1.0 - sig) - dl1 * sig) * scale
            if masked:
                dz = jnp.where(valid, dz, 0.0)
            dzb = dz.astype(BF16)
            dk_ref[pl.ds(off, Q), :] += _dot(dzb, q_, TN)
            dv_ref[pl.ds(off, Q), :] += _dot(a.astype(BF16), d_, TN)
            return pin, pre, dq + _dot(dzb, kblk)

        def step(r, cr):
            for u in range(unroll):
                cr = block(unroll * r + u, cr, q, do_, ltot, False)
            return cr

        init = (jnp.zeros((tq, Q), F32), jnp.zeros((tq, Q), F32), jnp.zeros((tq, hd), F32))
        carry = lax.fori_loop(0, i * (nb // unroll), step, init)
        for b in range(nb):
            lo = b * Q
            part = block(i * nb + b, tuple(a[lo:] for a in carry), q[lo:], do_[lo:], ltot[lo:], True)
            carry = part if lo == 0 else tuple(jnp.concatenate([a[:lo], pb], axis=0) for a, pb in zip(carry, part))
        dq_ref[...] = carry[2].astype(dq_ref.dtype)

    H = n_heads
    qs = pl.BlockSpec((tq, hd), lambda h, i: (i, h))
    full = pl.BlockSpec((T, hd), lambda h, i: (0, h))
    return pl.pallas_call(
        body, name=name, grid=(H, T // tq),
        in_specs=[qs, pl.BlockSpec((T, hd), lambda h, i: (0, H + h)), pl.BlockSpec((T, hd), lambda h, i: (0, 2 * H + h)),
                  qs, pl.BlockSpec((None, tq, Q), lambda h, i: (h, i, 0))],
        out_specs=[qs, full, full],
        out_shape=[_sds((T, H * hd), BF16), _sds((T, H * hd), F32), _sds((T, H * hd), F32)],
        compiler_params=_params(("parallel", "arbitrary")))(qkv, qkv, qkv, do, lt)


def sb_mixer_fwd(hn, w_qkv, w_out, n_heads):
    qkv = mm_fwd("sb_qkv", hn, w_qkv, BF16)
    o, lt = sb_fwd("sb_core", qkv, n_heads)
    m = mm_fwd("sb_out", o, w_out, F32, tn=1024)
    return m, (qkv, o, lt)


def sb_mixer_bwd(hn, saved, dm, w_qkv, w_out, n_heads):
    qkv, o, lt = saved
    do = mm_dx("sb_out_dx", dm, w_out, BF16, tn=w_out.R)
    dw_out = mm_dw("sb_out_dw", o, dm, BF16, tn=1024)
    dq, dk, dv = sb_bwd("sb_core_bwd", qkv, do, lt, n_heads)
    dqkv = jnp.concatenate([dq, dk.astype(BF16), dv.astype(BF16)], axis=1)
    dw_qkv = mm_dw("sb_qkv_dw", hn, dqkv, BF16, n_shards=w_qkv.S)
    dhn = mm_dx("sb_qkv_dx", dqkv, w_qkv, F32)
    return dhn, dict(sb_w_qkv=dw_qkv, sb_w_out=dw_out)


def _sgu_common(p_ref, vg_ref, vb_ref, Wd):
    pu, pv = p_ref[:, :Wd], p_ref[:, Wd:]
    u, v = _gelu(pu), _gelu(pv)
    xc = v - jnp.mean(v, axis=-1, keepdims=True)
    r = lax.rsqrt(jnp.mean(xc * xc, axis=-1, keepdims=True) + EPS)
    xh = xc * r
    return pu, pv, u, xh, r, xh * vg_ref[...] + vb_ref[...]


def sgu_fwd(name, P, vg, vb, ws, bexp):
    T = P.shape[0]
    Wd = P.shape[1] // 2
    G = ws.shape[0]
    gw = Wd // G
    Q = CHUNK

    def body(p_ref, vg_ref, vb_ref, ws_ref, be_ref, o_ref):
        _, _, u, _, _, vn = _sgu_common(p_ref, vg_ref, vb_ref, Wd)
        tril = _tri(Q, "le")
        for g in range(G):
            sl = slice(g * gw, (g + 1) * gw)
            wsg = jnp.where(tril, ws_ref[g], 0.0).astype(BF16)
            mixed = _dot(wsg, vn[:, sl].astype(BF16)) + be_ref[:, sl]
            o_ref[:, sl] = (u[:, sl] * mixed).astype(o_ref.dtype)

    vec = pl.BlockSpec((1, Wd), lambda c: (0, 0))
    return pl.pallas_call(
        body, name=name, grid=(T // Q,),
        in_specs=[pl.BlockSpec((Q, 2 * Wd), lambda c: (c, 0)), vec, vec, pl.BlockSpec((G, Q, Q), lambda c: (0, 0, 0)),
                  pl.BlockSpec((Q, Wd), lambda c: (0, 0))],
        out_specs=pl.BlockSpec((Q, Wd), lambda c: (c, 0)), out_shape=_sds((T, Wd), BF16),
        compiler_params=_params(("parallel",)))(P, vg, vb, ws, bexp)


def sgu_bwd(name, P, dgated, vg, vb, ws, bexp):
    T = P.shape[0]
    Wd = P.shape[1] // 2
    G = ws.shape[0]
    gw = Wd // G
    Q = CHUNK
    nc = T // Q

    def body(p_ref, dg_ref, vg_ref, vb_ref, ws_ref, be_ref, dp_ref, dws_ref, dvg_ref, dvb_ref, dbs_ref, dvn_scr, dbe_scr):
        c = pl.program_id(0)

        @pl.when(c == 0)
        def _():
            dws_ref[...] = jnp.zeros_like(dws_ref)
            dvg_ref[...] = jnp.zeros_like(dvg_ref)
            dvb_ref[...] = jnp.zeros_like(dvb_ref)
            dbe_scr[...] = jnp.zeros_like(dbe_scr)

        pu, pv, u, xh, r, vn = _sgu_common(p_ref, vg_ref, vb_ref, Wd)
        tril = _tri(Q, "le")
        for g in range(G):
            sl = slice(g * gw, (g + 1) * gw)
            wsg = jnp.where(tril, ws_ref[g], 0.0).astype(BF16)
            vng = vn[:, sl].astype(BF16)
            mixed = _dot(wsg, vng) + be_ref[:, sl]
            dgt = dg_ref[:, sl]
            dp_ref[:, sl] = (dgt * mixed * _dgelu(pu[:, sl])).astype(dp_ref.dtype)
            dmix = dgt * u[:, sl]
            dmb = dmix.astype(BF16)
            dws_ref[g] += jnp.where(tril, _dot(dmb, vng, NT), 0.0)
            dvn_scr[:, sl] = _dot(wsg, dmb, TN)
            dbe_scr[:, sl] += dmix
        dvn = dvn_scr[...]
        dvg_ref[...] += jnp.sum(dvn * xh, axis=0, keepdims=True)
        dvb_ref[...] += jnp.sum(dvn, axis=0, keepdims=True)
        dxh = dvn * vg_ref[...]
        dv = r * (dxh - jnp.mean(dxh, axis=-1, keepdims=True) - xh * jnp.mean(dxh * xh, axis=-1, keepdims=True))
        dp_ref[:, Wd:] = (dv * _dgelu(pv)).astype(dp_ref.dtype)

        @pl.when(c == nc - 1)
        def _():
            sel = (_iota((Wd, LANES), 0) // gw == _iota((Wd, LANES), 1)).astype(BF16)
            dbs_ref[...] = _dot3r(dbe_scr[...], sel)

    vec = pl.BlockSpec((1, Wd), lambda c: (0, 0))
    wsb = pl.BlockSpec((G, Q, Q), lambda c: (0, 0, 0))
    return pl.pallas_call(
        body, name=name, grid=(nc,),
        in_specs=[pl.BlockSpec((Q, 2 * Wd), lambda c: (c, 0)), pl.BlockSpec((Q, Wd), lambda c: (c, 0)), vec, vec, wsb,
                  pl.BlockSpec((Q, Wd), lambda c: (0, 0))],
        out_specs=[pl.BlockSpec((Q, 2 * Wd), lambda c: (c, 0)), wsb, vec, vec, pl.BlockSpec((Q, LANES), lambda c: (0, 0))],
        out_shape=[_sds((T, 2 * Wd), BF16), _sds((G, Q, Q), F32), _sds((1, Wd), F32), _sds((1, Wd), F32), _sds((Q, LANES), F32)],
        scratch_shapes=[pltpu.VMEM((Q, Wd), F32), pltpu.VMEM((Q, Wd), F32)],
        compiler_params=_params(("arbitrary",)))(P, dgated, vg, vb, ws, bexp)


def sg_mixer_fwd(hn, w_in, vg, vb, ws, bs, w_out):
    G = ws.shape[0]
    Wd = vg.shape[1]
    P = mm_fwd("sg_in", hn, w_in, F32)
    bexp = jnp.repeat(bs.T, Wd // G, axis=1)
    gated = sgu_fwd("sg_core", P, vg, vb, ws, bexp)
    m = mm_fwd("sg_out", gated, w_out, F32, tn=1024)
    return m, (P, bexp, gated)


def sg_mixer_bwd(hn, saved, dm, w_in, vg, vb, ws, w_out):
    P, bexp, gated = saved
    G = ws.shape[0]
    dgated = mm_dx("sg_out_dx", dm, w_out, F32, tn=w_out.R)
    dw_out = mm_dw("sg_out_dw", gated, dm, BF16, tn=1024)
    dP, dws, dvg, dvb, dbs = sgu_bwd("sg_core_bwd", P, dgated, vg, vb, ws, bexp)
    dw_in = mm_dw("sg_in_dw", hn, dP, BF16, n_shards=w_in.S)
    dhn = mm_dx("sg_in_dx", dP, w_in, F32)
    grads = dict(sg_w_in=dw_in, sg_w_out=dw_out, sg_w_spatial=dws, sg_v_norm_g=dvg, sg_v_norm_b=dvb,
                 sg_b_spatial=dbs[:, :G].T)
    return dhn, grads


def ssd_conv_fwd(name, xbc, cw, cb, tc=LANES):
    T, Cd = xbc.shape
    K = cw.shape[0]

    def body(p_ref, w_ref, b_ref, o_ref):
        _, pre = _conv_taps(p_ref[...], w_ref, K)
        o_ref[...] = _silu(pre + b_ref[...])

    col = pl.BlockSpec((T, tc), lambda j: (0, j))
    return pl.pallas_call(body, name=name, grid=(Cd // tc,),
                          in_specs=[col, pl.BlockSpec((K, tc), lambda j: (0, j)), pl.BlockSpec((1, tc), lambda j: (0, j))],
                          out_specs=col, out_shape=_sds((T, Cd), F32), compiler_params=_params(("parallel",)))(xbc, cw, cb)


def ssd_conv_bwd(name, xbc, dact, cw, cb, tc=LANES):
    T, Cd = xbc.shape
    K = cw.shape[0]

    def body(p_ref, da_ref, w_ref, b_ref, dp_ref, dw_ref, db_ref):
        taps, pre = _conv_taps(p_ref[...], w_ref, K)
        dpre = da_ref[...] * _dsilu(pre + b_ref[...])
        dp_ref[...] = _conv_bwd(dpre, taps, w_ref, dw_ref, db_ref, K).astype(dp_ref.dtype)

    col = pl.BlockSpec((T, tc), lambda j: (0, j))
    wsp = pl.BlockSpec((K, tc), lambda j: (0, j))
    bsp = pl.BlockSpec((1, tc), lambda j: (0, j))
    return pl.pallas_call(body, name=name, grid=(Cd // tc,), in_specs=[col, col, wsp, bsp], out_specs=[col, wsp, bsp],
                          out_shape=[_sds((T, Cd), BF16), _sds((K, Cd), F32), _sds((1, Cd), F32)],
                          compiler_params=_params(("parallel",)))(xbc, dact, cw, cb)


def _expand_matrix(Hd):
    return (_iota((LANES, Hd), 1) // SSD_HEAD_DIM == _iota((LANES, Hd), 0)).astype(BF16)


def ssd_dt_fwd(name, dtr, bias, Hd, tr=512):
    T = dtr.shape[0]
    tr = min(tr, T)

    def body(d_ref, b_ref, o_ref):
        o_ref[...] = _dot3r(_softplus(d_ref[...] + b_ref[...]), _expand_matrix(Hd))

    return pl.pallas_call(body, name=name, grid=(T // tr,),
                          in_specs=[pl.BlockSpec((tr, LANES), lambda i: (i, 0)), pl.BlockSpec((1, LANES), lambda i: (0, 0))],
                          out_specs=pl.BlockSpec((tr, Hd), lambda i: (i, 0)), out_shape=_sds((T, Hd), F32),
                          compiler_params=_params(("parallel",)))(dtr, bias)


def ssd_dt_bwd(name, dtr, bias, ddtx, tr=512):
    T, Hd = ddtx.shape
    tr = min(tr, T)

    def body(d_ref, b_ref, g_ref, o_ref, db_ref):
        p1, p2, p3 = _split3(g_ref[...])
        em = _expand_matrix(Hd)
        ddt = _dot(p1, em, NT) + _dot(p2, em, NT) + _dot(p3, em, NT)
        draw = ddt * _sigmoid(d_ref[...] + b_ref[...])
        o_ref[...] = draw.astype(o_ref.dtype)
        part = jnp.sum(draw, axis=0, keepdims=True)

        @pl.when(pl.program_id(0) == 0)
        def _():
            db_ref[...] = part

        @pl.when(pl.program_id(0) > 0)
        def _():
            db_ref[...] += part

    row = pl.BlockSpec((tr, LANES), lambda i: (i, 0))
    vec = pl.BlockSpec((1, LANES), lambda i: (0, 0))
    return pl.pallas_call(body, name=name, grid=(T // tr,), in_specs=[row, vec, pl.BlockSpec((tr, Hd), lambda i: (i, 0))],
                          out_specs=[row, vec], out_shape=[_sds((T, LANES), BF16), _sds((1, LANES), F32)],
                          compiler_params=_params(("arbitrary",)))(dtr, bias, ddtx)


def _ssd_head_terms(a2, a2r, half, cb, causal, lane):
    hm = (lane < SSD_HEAD_DIM) if half == 0 else (lane >= SSD_HEAD_DIM)
    ccol = jnp.where(hm, a2, a2r)
    lm = jnp.exp(jnp.where(causal, ccol - ccol.T, -jnp.inf))
    return hm, lm, cb * lm


def ssd_core_fwd(name, act, dtx, alx, dx, G):
    T, Hd = dtx.shape
    Q, N = CHUNK, SSD_STATE
    gw = Hd // G
    nc = T // Q
    nx = Hd // N

    def body(xs_ref, b_ref, c_ref, dt_ref, al_ref, d_ref, y_ref, ss_ref, st_scr):
        @pl.when(pl.program_id(1) == 0)
        def _():
            st_scr[...] = jnp.zeros_like(st_scr)

        xs, dtv = xs_ref[...], dt_ref[...]
        Bb, Cb = b_ref[...].astype(BF16), c_ref[...].astype(BF16)
        dA = dtv * (-jnp.exp(al_ref[...]))
        a = _dot3l(_tri(Q, "le").astype(BF16), dA)
        a_last = jnp.sum(dA, axis=0, keepdims=True)
        xdt = xs * dtv
        cbm = _dot(Cb, Bb, NT)
        sprev = st_scr[...]
        ss_ref[...] = sprev
        causal, lane = _tri(Q, "le"), _iota((Q, LANES), 1)
        y_rest = _dot(Cb, sprev.astype(BF16)) * jnp.exp(a) + xs * d_ref[...]
        for q in range(gw // LANES):
            sl = slice(q * LANES, (q + 1) * LANES)
            a2, x2 = a[:, sl], xdt[:, sl]
            a2r = pltpu.roll(a2, SSD_HEAD_DIM, 1)
            acc = y_rest[:, sl]
            for half in (0, 1):
                hm, _, gm = _ssd_head_terms(a2, a2r, half, cbm, causal, lane)
                acc = acc + _dot(gm.astype(BF16), jnp.where(hm, x2, 0.0).astype(BF16))
            y_ref[:, sl] = acc
        w = jnp.exp(a_last - a)
        st_scr[...] = sprev * jnp.exp(a_last) + _dot(Bb, (w * xdt).astype(BF16), TN)

    xsp = pl.BlockSpec((Q, gw), lambda g, c: (c, g))
    vec = pl.BlockSpec((1, gw), lambda g, c: (0, g))
    return pl.pallas_call(
        body, name=name, grid=(G, nc),
        in_specs=[xsp, pl.BlockSpec((Q, N), lambda g, c: (c, nx + g)), pl.BlockSpec((Q, N), lambda g, c: (c, nx + G + g)),
                  xsp, vec, vec],
        out_specs=[xsp, pl.BlockSpec((None, N, gw), lambda g, c: (c, 0, g))],
        out_shape=[_sds((T, Hd), F32), _sds((nc, N, Hd), F32)],
        scratch_shapes=[pltpu.VMEM((N, gw), F32)],
        compiler_params=_params(("parallel", "arbitrary")))(act, act, act, dtx, alx, dx)


def ssd_core_bwd(name, act, dtx, alx, dx, ssave, dy, G):
    T, Hd = dtx.shape
    Q, N = CHUNK, SSD_STATE
    gw = Hd // G
    nc = T // Q
    nx = Hd // N

    def body(xs_ref, b_ref, c_ref, dt_ref, al_ref, d_ref, ss_ref, dy_ref,
             dxs_ref, db_ref, dc_ref, ddt_ref, dal_ref, dd_ref, ds_scr, dxdt_scr, da_scr):
        @pl.when(pl.program_id(1) == 0)
        def _():
            ds_scr[...] = jnp.zeros_like(ds_scr)
            dal_ref[...] = jnp.zeros_like(dal_ref)
            dd_ref[...] = jnp.zeros_like(dd_ref)

        xs, dtv, dy_ = xs_ref[...], dt_ref[...], dy_ref[...]
        Bb, Cb = b_ref[...].astype(BF16), c_ref[...].astype(BF16)
        Ax = -jnp.exp(al_ref[...])
        dA = dtv * Ax
        a = _dot3l(_tri(Q, "le").astype(BF16), dA)
        a_last = jnp.sum(dA, axis=0, keepdims=True)
        xdt = xs * dtv
        e, w, eal = jnp.exp(a), jnp.exp(a_last - a), jnp.exp(a_last)
        sprev, dsn = ss_ref[...], ds_scr[...]
        sprevb, dsnb = sprev.astype(BF16), dsn.astype(BF16)

        dd_ref[...] += jnp.sum(dy_ * xs, axis=0, keepdims=True)
        dmb = (dy_ * e).astype(BF16)
        dC = _dot(dmb, sprevb, NT)
        ds_scr[...] = _dot(Cb, dmb, TN) + dsn * eal
        dalast = jnp.sum(dsn * sprev, axis=0, keepdims=True) * eal
        dB = _dot((w * xdt).astype(BF16), dsnb, NT)
        dwx = _dot(Bb, dsnb)
        dww = dwx * xdt * w
        dalast = dalast + jnp.sum(dww, axis=0, keepdims=True)
        da_scr[...] = dy_ * _dot(Cb, sprevb) * e - dww
        dxdt_scr[...] = w * dwx
        cbm = _dot(Cb, Bb, NT)
        dcb = jnp.zeros((Q, Q), F32)
        causal, lane = _tri(Q, "le"), _iota((Q, LANES), 1)
        for q in range(gw // LANES):
            sl = slice(q * LANES, (q + 1) * LANES)
            a2, x2, dy2 = a[:, sl], xdt[:, sl], dy_[:, sl]
            a2r = pltpu.roll(a2, SSD_HEAD_DIM, 1)
            for half in (0, 1):
                hm, lm, gm = _ssd_head_terms(a2, a2r, half, cbm, causal, lane)
                dyh = jnp.where(hm, dy2, 0.0).astype(BF16)
                dg = _dot(dyh, jnp.where(hm, x2, 0.0).astype(BF16), NT)
                dxdt_scr[:, sl] += _dot(gm.astype(BF16), dyh, TN)
                dcb = dcb + dg * lm
                dseg = dg * gm
                v = jnp.sum(dseg, axis=1, keepdims=True) - jnp.sum(dseg.T, axis=1, keepdims=True)
                da_scr[:, sl] += jnp.where(hm, v, 0.0) * (1.0 / SSD_HEAD_DIM)
        dcbb = dcb.astype(BF16)
        dc_ref[...] = dC + _dot(dcbb, Bb)
        db_ref[...] = dB + _dot(dcbb, Cb, TN)
        dxdt = dxdt_scr[...]
        dxs_ref[...] = dy_ * d_ref[...] + dxdt * dtv
        da = da_scr[...] + jnp.where(_iota((Q, gw), 0) == Q - 1, dalast, 0.0)
        dda = _dot3l(_tri(Q, "ge").astype(BF16), da)
        ddt_ref[...] = dxdt * xs + dda * Ax
        dal_ref[...] += jnp.sum(dda * dtv, axis=0, keepdims=True) * Ax

    rc = lambda c: nc - 1 - c
    xsp = pl.BlockSpec((Q, gw), lambda g, c: (rc(c), g))
    bsp = pl.BlockSpec((Q, N), lambda g, c: (rc(c), nx + g))
    csp = pl.BlockSpec((Q, N), lambda g, c: (rc(c), nx + G + g))
    gsp = pl.BlockSpec((Q, N), lambda g, c: (rc(c), g))
    vec = pl.BlockSpec((1, gw), lambda g, c: (0, g))
    return pl.pallas_call(
        body, name=name, grid=(G, nc),
        in_specs=[xsp, bsp, csp, xsp, vec, vec, pl.BlockSpec((None, N, gw), lambda g, c: (rc(c), 0, g)), xsp],
        out_specs=[xsp, gsp, gsp, xsp, vec, vec],
        out_shape=[_sds((T, Hd), F32), _sds((T, G * N), F32), _sds((T, G * N), F32), _sds((T, Hd), F32),
                   _sds((1, Hd), F32), _sds((1, Hd), F32)],
        scratch_shapes=[pltpu.VMEM((N, gw), F32), pltpu.VMEM((Q, gw), F32), pltpu.VMEM((Q, gw), F32)],
        compiler_params=_params(("parallel", "arbitrary")))(act, act, act, dtx, alx, dx, ssave, dy)


def ssd_gate_fwd(name, y, z, ng, tr=256):
    T, Hd = y.shape
    tr = min(tr, T)

    def body(y_ref, z_ref, g_ref, o_ref):
        o_ref[...] = _rms(y_ref[...] * _silu(z_ref[...]), g_ref[...]).astype(o_ref.dtype)

    row = pl.BlockSpec((tr, Hd), lambda i: (i, 0))
    return pl.pallas_call(body, name=name, grid=(T // tr,), in_specs=[row, row, pl.BlockSpec((1, Hd), lambda i: (0, 0))],
                          out_specs=row, out_shape=_sds((T, Hd), BF16), compiler_params=_params(("parallel",)))(y, z, ng)


def ssd_gate_bwd(name, y, z, ng, dyn, tr=128):
    T, Hd = y.shape
    tr = min(tr, T)

    def body(y_ref, z_ref, g_ref, dn_ref, dy_ref, dz_ref, dg_ref):
        y_, z_, dn = y_ref[...], z_ref[...], dn_ref[...]
        y2 = y_ * _silu(z_)
        r = lax.rsqrt(jnp.mean(y2 * y2, axis=-1, keepdims=True) + EPS)
        xh = y2 * r
        dxh = dn * g_ref[...]
        dy2 = r * (dxh - xh * jnp.mean(dxh * xh, axis=-1, keepdims=True))
        dy_ref[...] = dy2 * _silu(z_)
        dz_ref[...] = (dy2 * y_ * _dsilu(z_)).astype(dz_ref.dtype)
        part = jnp.sum(dn * xh, axis=0, keepdims=True)

        @pl.when(pl.program_id(0) == 0)
        def _():
            dg_ref[...] = part

        @pl.when(pl.program_id(0) > 0)
        def _():
            dg_ref[...] += part

    row = pl.BlockSpec((tr, Hd), lambda i: (i, 0))
    vec = pl.BlockSpec((1, Hd), lambda i: (0, 0))
    return pl.pallas_call(body, name=name, grid=(T // tr,), in_specs=[row, row, vec, row], out_specs=[row, row, vec],
                          out_shape=[_sds((T, Hd), F32), _sds((T, Hd), BF16), _sds((1, Hd), F32)],
                          compiler_params=_params(("arbitrary",)))(y, z, ng, dyn)


def ssd_mixer_fwd(hn, wz, wxbc, wdt, cw, cb, dtb, alx, dx, ng, w_out, G):
    Hd = wz.N
    z = mm_fwd("ssd_z", hn, wz, F32)
    xbc = mm_fwd("ssd_xbc", hn, wxbc, F32)
    dtr = mm_fwd("ssd_dt", hn, wdt, F32)
    act = ssd_conv_fwd("ssd_conv", xbc, cw, cb)
    dtx = ssd_dt_fwd("ssd_dtx", dtr, dtb, Hd)
    y, ssave = ssd_core_fwd("ssd_core", act, dtx, alx, dx, G)
    yn = ssd_gate_fwd("ssd_gate", y, z, ng)
    m = mm_fwd("ssd_out", yn, w_out, F32, tn=1024)
    return m, (z, xbc, dtr, act, dtx, y, ssave, yn)


def ssd_mixer_bwd(hn, saved, dm, wz, wxbc, wdt, cw, cb, dtb, alx, dx, ng, w_out, G):
    z, xbc, dtr, act, dtx, y, ssave, yn = saved
    dyn = mm_dx("ssd_out_dx", dm, w_out, F32, tn=w_out.R)
    dw_out = mm_dw("ssd_out_dw", yn, dm, BF16, tn=1024)
    dy, dz, dng = ssd_gate_bwd("ssd_gate_bwd", y, z, ng, dyn)
    dxs, dB, dC, ddtx, dalx, ddx = ssd_core_bwd("ssd_core_bwd", act, dtx, alx, dx, ssave, dy, G)
    dxbc, dcw, dcb = ssd_conv_bwd("ssd_conv_bwd", xbc, jnp.concatenate([dxs, dB, dC], axis=1), cw, cb)
    ddtr, ddtb = ssd_dt_bwd("ssd_dtx_bwd", dtr, dtb, ddtx)
    dwz = mm_dw("ssd_z_dw", hn, dz, BF16)
    dwxbc = mm_dw("ssd_xbc_dw", hn, dxbc, BF16)
    dwdt = mm_dw("ssd_dt_dw", hn, ddtr, BF16)
    dhn = mm_dx("ssd_z_dx", dz, wz, F32)
    dhn = mm_dx("ssd_xbc_dx", dxbc, wxbc, F32, add=dhn)
    dhn = mm_dx("ssd_dt_dx", ddtr, wdt, F32, add=dhn)
    grads = dict(ssd_w_out=dw_out, ssd_wz=dwz, ssd_wxbc=dwxbc, ssd_wdt=dwdt, ssd_conv_w=dcw, ssd_conv_b=dcb,
                 ssd_dt_bias=ddtb, ssd_alx=dalx, ssd_dx=ddx, ssd_norm=dng)
    return dhn, grads


def adamw(name, w, m, v, ga, gb=None):
    Rr, C = w.shape
    tr = 8
    while Rr % (tr * 2) == 0 and tr * 2 * C * 4 <= (1 << 20):
        tr *= 2
    if Rr % tr:
        tr = Rr
    two = gb is not None
    c1 = 1.0 - ADAM_B1 ** ADAM_STEP
    c2 = 1.0 - ADAM_B2 ** ADAM_STEP

    def body(*refs):
        if two:
            w_ref, m_ref, v_ref, a_ref, b_ref, g_ref, d_ref, mo_ref, vo_ref = refs
            g = a_ref[...] + b_ref[...]
        else:
            w_ref, m_ref, v_ref, a_ref, g_ref, d_ref, mo_ref, vo_ref = refs
            g = a_ref[...]
        m2 = ADAM_B1 * m_ref[...] + (1.0 - ADAM_B1) * g
        v2 = ADAM_B2 * v_ref[...] + (1.0 - ADAM_B2) * (g * g)
        g_ref[...] = g
        mo_ref[...] = m2
        vo_ref[...] = v2
        d_ref[...] = -ADAM_LR * ((m2 / c1) / (jnp.sqrt(v2 / c2) + ADAM_EPS) + ADAM_WD * w_ref[...])

    blk = pl.BlockSpec((tr, C), lambda i: (i, 0))
    n_in = 5 if two else 4
    args = (w, m, v, ga) + ((gb,) if two else ())
    return pl.pallas_call(body, name=name, grid=(Rr // tr,), in_specs=[blk] * n_in, out_specs=[blk] * 4,
                          out_shape=[_sds((Rr, C), F32)] * 4, compiler_params=_params(("parallel",)))(*args)


def _mesh_pos():
    return lax.axis_index("x"), lax.axis_index("y"), lax.axis_index("c")


def _peer_chips(x, y):
    return [(1 - x, y), (x, 1 - y), (1 - x, 1 - y)]


def sum_slots(name, own, r):
    _, Rr, C = r.shape
    tr = 8
    while Rr % (tr * 2) == 0 and tr * 2 * C * 4 <= (1 << 20):
        tr *= 2

    def body(o_in, r_ref, o_ref):
        o_ref[...] = ((o_in[...].astype(F32) + r_ref[0].astype(F32)) + r_ref[1].astype(F32)) + r_ref[2].astype(F32)

    blk = pl.BlockSpec((tr, C), lambda i: (i, 0))
    return pl.pallas_call(body, name=name, grid=(Rr // tr,), in_specs=[blk, pl.BlockSpec((3, tr, C), lambda i: (0, i, 0))],
                          out_specs=blk, out_shape=_sds((Rr, C), F32), compiler_params=_params(("parallel",)))(own, r)


HBM = pl.BlockSpec(memory_space=pltpu.HBM)
SEM = pl.BlockSpec(memory_space=pltpu.SEMAPHORE)
EFFECT = pltpu.SideEffectType.DATAFLOW_SIDE_EFFECTING


def _xchg_copy(mode, side, src, land, send, recv, t, j, peer, me, c):
    px, py = peer
    pidx = 2 * px + py
    if mode == "gather":
        s, dst = src, land.at[me if side == "out" else pidx]
    else:
        s, dst = src.at[pidx], land.at[j]
    k = 3 * t + j
    return pltpu.make_async_remote_copy(src_ref=s, dst_ref=dst, send_sem=send.at[k], recv_sem=recv.at[k],
                                        device_id=(px, py, c), device_id_type=MESH)


def xchg_start(name, mode, srcs, lands):
    counts = [len(g) for g in srcs]
    ng = len(counts)
    fs = [a for g in srcs for a in g]
    fl = [a for g in lands for a in g]
    n = len(fs)

    def body(*refs):
        src, land = refs[:n], refs[n:2 * n]
        send, recv = refs[2 * n:2 * n + ng], refs[2 * n + ng:2 * n + 2 * ng]
        token = refs[-1]
        x, y, c = _mesh_pos()
        me = 2 * x + y
        k = 0
        for gi in range(ng):
            for t in range(counts[gi]):
                for j, peer in enumerate(_peer_chips(x, y)):
                    _xchg_copy(mode, "out", src[k], land[k], send[gi], recv[gi], t, j, peer, me, c).start()
                k += 1
        token[...] = jnp.zeros_like(token)

    sems = tuple(pltpu.SemaphoreType.DMA((3 * cnt,)) for cnt in counts)
    thru = tuple(pltpu.HBM(a.shape, a.dtype) for a in fs + fl)
    out = pl.pallas_call(
        body, name=name, in_specs=[HBM] * (2 * n),
        out_specs=(SEM,) * (2 * ng) + (HBM,) * (2 * n) + (pl.BlockSpec(memory_space=pltpu.VMEM),),
        out_shape=sems + sems + thru + (_sds((8, LANES), F32),),
        input_output_aliases={i: 2 * ng + i for i in range(2 * n)},
        compiler_params=pltpu.CompilerParams(has_side_effects=EFFECT),
    )(*[pltpu.with_memory_space_constraint(a, pltpu.HBM) for a in fs + fl])
    send, recv = out[:ng], out[ng:2 * ng]
    thru_s, thru_l = out[2 * ng:2 * ng + n], out[2 * ng + n:2 * ng + 2 * n]
    groups, k = [], 0
    for gi, cnt in enumerate(counts):
        groups.append(dict(send=send[gi], recv=recv[gi], src=list(thru_s[k:k + cnt]), land=list(thru_l[k:k + cnt])))
        k += cnt
    return groups, out[-1]


def xchg_wait(name, mode, grp, after):
    src, land = grp["src"], grp["land"]
    n = len(src)

    def body(*refs):
        s_ref, l_ref = refs[:n], refs[n:2 * n]
        send, recv = refs[2 * n], refs[2 * n + 1]
        x, y, c = _mesh_pos()
        me = 2 * x + y
        for t in range(n):
            for j, peer in enumerate(_peer_chips(x, y)):
                _xchg_copy(mode, "out", s_ref[t], l_ref[t], send, recv, t, j, peer, me, c).wait_send()
                _xchg_copy(mode, "in", s_ref[t], l_ref[t], send, recv, t, j, peer, me, c).wait_recv()

    res = pl.pallas_call(
        body, name=name, in_specs=[HBM] * (2 * n) + [SEM, SEM, ANY],
        out_specs=(HBM,) * (2 * n), out_shape=tuple(pltpu.HBM(a.shape, a.dtype) for a in src + land),
        input_output_aliases={i: i for i in range(2 * n)},
        compiler_params=pltpu.CompilerParams(has_side_effects=EFFECT),
    )(*src, *land, grp["send"], grp["recv"], after)
    return list(res[:n]), list(res[n:])


def swap_with_sibling(name, tensors):
    n = len(tensors)

    def body(*refs):
        ins, outs = refs[:n], refs[n:2 * n]
        send_sems, recv_sems = refs[2 * n:]
        x, y, c = _mesh_pos()
        cps = []
        for t in range(n):
            cp = pltpu.make_async_remote_copy(src_ref=ins[t], dst_ref=outs[t], send_sem=send_sems.at[t], recv_sem=recv_sems.at[t],
                                              device_id=(x, y, 1 - c), device_id_type=MESH)
            cp.start()
            cps.append(cp)
        for cp in cps:
            cp.wait()

    return pl.pallas_call(
        body, name=name, in_specs=[ANY] * n, out_specs=[ANY] * n, out_shape=[_sds(t.shape, t.dtype) for t in tensors],
        scratch_shapes=[pltpu.SemaphoreType.DMA((n,)), pltpu.SemaphoreType.DMA((n,))],
    )(*tensors)


def all_reduce_small(name, v):
    Rr, C = v.shape
    nd = 8

    def body(v_ref, o_ref, gath, send_sems, recv_sems):
        x, y, c = _mesh_pos()
        me = 4 * x + 2 * y + c
        cps = []
        for d in range(1, nd):
            bx, by, bc = (d >> 2) & 1, (d >> 1) & 1, d & 1
            tgt = (1 - x if bx else x, 1 - y if by else y, 1 - c if bc else c)
            cp = pltpu.make_async_remote_copy(src_ref=v_ref, dst_ref=gath.at[me], send_sem=send_sems.at[d - 1],
                                              recv_sem=recv_sems.at[d - 1], device_id=tgt, device_id_type=MESH)
            cp.start()
            cps.append((cp, tgt))
        gath[me] = v_ref[...]
        for d in range(1, nd):
            _, (tx, ty, tc) = cps[d - 1]
            pltpu.make_async_remote_copy(src_ref=v_ref, dst_ref=gath.at[4 * tx + 2 * ty + tc], send_sem=send_sems.at[d - 1],
                                         recv_sem=recv_sems.at[d - 1], device_id=(tx, ty, tc), device_id_type=MESH).wait_recv()
        acc = gath[0]
        for d in range(1, nd):
            acc = acc + gath[d]
        o_ref[...] = acc
        for cp, _ in cps:
            cp.wait_send()

    vm = pl.BlockSpec(memory_space=pltpu.VMEM)
    return pl.pallas_call(
        body, name=name, in_specs=[vm], out_specs=vm, out_shape=_sds((Rr, C), F32),
        scratch_shapes=[pltpu.VMEM((nd, Rr, C), F32), pltpu.SemaphoreType.DMA((nd - 1,)), pltpu.SemaphoreType.DMA((nd - 1,))],
        compiler_params=pltpu.CompilerParams(vmem_limit_bytes=VMEM_LIMIT),
    )(v)


def _pack(arrs):
    flat = jnp.concatenate([a.reshape(-1) for a in arrs])
    pad = (-flat.shape[0]) % (8 * LANES)
    return jnp.pad(flat, (0, pad)).reshape(-1, LANES)


def _unpack(buf, shapes):
    flat = buf.reshape(-1)
    out, off = [], 0
    for s in shapes:
        n = math.prod(s)
        out.append(flat[off:off + n].reshape(s))
        off += n
    return out


WEIGHTS = ["ln_mix_pre", "ln_mix_post", "ln_mem", "ln_xa_pre", "ln_xa_post", "ln_ffn_pre", "ln_ffn_post", "xa_wq", "xa_wkv",
           "xa_wo", "ffn_w_in", "ffn_conv_w", "ffn_conv_b", "ffn_w_out", "ssd_w_in", "ssd_conv_w", "ssd_conv_b", "ssd_dt_bias",
           "ssd_a_log", "ssd_d", "ssd_norm", "ssd_w_out", "sg_w_in", "sg_v_norm_g", "sg_v_norm_b", "sg_w_spatial",
           "sg_b_spatial", "sg_w_out", "sb_w_qkv", "sb_w_out"]
BIG = {"xa_wq": "rows", "xa_wkv": "rows", "xa_wo": "cols", "ffn_w_in": "cols", "ffn_w_out": "rows", "ssd_w_in": "cols",
       "ssd_w_out": "rows", "sg_w_in": "cols", "sg_w_out": "rows", "sb_w_qkv": "cols", "sb_w_out": "rows"}
SHARDED_SMALL = {"ffn_conv_w": 2, "ssd_conv_w": 2, "ssd_conv_b": 1, "ssd_norm": 1}
SMALL = [n for n in WEIGHTS if n not in BIG]
N_MIXERS = 3
HEAD = 128


def _unshard(a, axis):
    a = jnp.moveaxis(a, 0, axis)
    s = a.shape
    return a.reshape(s[:axis] + (s[axis] * s[axis + 1],) + s[axis + 2:])


def _step(p):
    x, mem, tgt = p["x"][0], p["mem"][0], p["loss_target"][0]
    T, D = x.shape
    depth = p["ln_mix_pre"].shape[0]
    S = N_CHIPS

    me = 2 * lax.axis_index("x") + lax.axis_index("y")
    Hd, Cd = S * p["ssd_norm"].shape[1], S * p["ssd_conv_b"].shape[1]
    nh = p["ssd_dt_bias"].shape[1]
    G = (Cd - Hd) // (2 * SSD_STATE)
    xa_heads = p["xa_wo"].shape[1] // HEAD
    sb_heads = D // HEAD

    def layer_tensors(i):
        kind, j = i % N_MIXERS, i // N_MIXERS
        items = [("xa_wq", i), ("xa_wkv", i), ("xa_wo", i), ("ffn_w_in", i), ("ffn_w_out", i), ("ffn_conv_w", i)]
        if kind == 0:
            items += [("ssd_w_in", j), ("ssd_w_out", j), ("ssd_conv_w", j), ("ssd_conv_b", j), ("ssd_norm", j)]
        elif kind == 1:
            items += [("sg_w_in", j), ("sg_w_out", j)]
        else:
            items += [("sb_w_qkv", j), ("sb_w_out", j)]
        return items

    srcs, lands = [], []
    for i in range(depth):
        s_i, l_i = [], []
        for n, k in layer_tensors(i):
            a = p[n][k].astype(BF16) if n in BIG else p[n][k]
            a = a.reshape((1,) * (2 - a.ndim) + a.shape)
            s_i.append(a)
            l_i.append(lax.dynamic_update_index_in_dim(lax.empty((S,) + a.shape, a.dtype), a, me, 0))
        srcs.append(s_i)
        lands.append(l_i)
    gather_groups, gather_token = xchg_start("gather_start", "gather", srcs, lands)

    def layer_weights(i, after):
        _, zones = xchg_wait("gather_wait_%d" % i, "gather", gather_groups[i], after)
        return dict(zip(layer_tensors(i), zones))

    def layer_args(i, g):
        kind, j = i % N_MIXERS, i // N_MIXERS
        w_of = lambda n, k: W(BIG[n], g[(n, k)][:, None], 0)
        a = dict(xa=(w_of("xa_wq", i), w_of("xa_wkv", i), w_of("xa_wo", i), xa_heads),
                 ffn=(w_of("ffn_w_in", i), _unshard(g[("ffn_conv_w", i)], 1), p["ffn_conv_b"][i:i + 1], w_of("ffn_w_out", i)))
        if kind == 0:
            w_in = _unshard(g[("ssd_w_in", j)], 1)[None]
            a["mix"] = (W("full", w_in[:, :, :Hd], 0), W("full", w_in[:, :, Hd:Hd + Cd], 0),
                        W("full", jnp.pad(w_in[:, :, Hd + Cd:], ((0, 0), (0, 0), (0, LANES - nh))), 0),
                        _unshard(g[("ssd_conv_w", j)], 1), _unshard(g[("ssd_conv_b", j)], 1),
                        jnp.pad(p["ssd_dt_bias"][j], (0, LANES - nh))[None], jnp.repeat(p["ssd_a_log"][j], SSD_HEAD_DIM)[None],
                        jnp.repeat(p["ssd_d"][j], SSD_HEAD_DIM)[None], _unshard(g[("ssd_norm", j)], 1), w_of("ssd_w_out", j), G)
        elif kind == 1:
            a["mix"] = (w_of("sg_w_in", j), p["sg_v_norm_g"][j:j + 1], p["sg_v_norm_b"][j:j + 1], p["sg_w_spatial"][j],
                        w_of("sg_w_out", j))
        else:
            a["mix"] = (w_of("sb_w_qkv", j), w_of("sb_w_out", j), sb_heads)
        return a

    ln = lambda n, i: p[n][i:i + 1]

    h = rms_fwd("rms_first", x, ln("ln_mix_pre", 0), after=gather_token)
    saved, largs = [], []
    for i in range(depth):
        kind, j = i % N_MIXERS, i // N_MIXERS
        la = layer_args(i, layer_weights(i, x))
        largs.append(la)
        if kind == 0:
            m, ms = ssd_mixer_fwd(h, *la["mix"])
        elif kind == 1:
            m, ms = sg_mixer_fwd(h, *la["mix"][:4], p["sg_b_spatial"][j], la["mix"][4])
        else:
            m, ms = sb_mixer_fwd(h, *la["mix"])
        x1, hq = resid_norm("resid_norm", x, m, ln("ln_mix_post", i), ln("ln_xa_pre", i))
        mem_n = rms_fwd("rms_mem", mem, ln("ln_mem", i))
        c, cs = xa_fwd(hq, mem_n, *la["xa"])
        x2, hf = resid_norm("resid_norm", x1, c, ln("ln_xa_post", i), ln("ln_ffn_pre", i))
        f, fs = ffn_fwd(hf, *la["ffn"])
        x3, hn = resid_norm("resid_norm", x2, f, ln("ln_ffn_post", i), ln("ln_mix_pre", i + 1) if i + 1 < depth else None)
        saved.append(dict(x=x, h=h, m=m, ms=ms, x1=x1, hq=hq, mem_n=mem_n, c=c, cs=cs, x2=x2, hf=hf, f=f, fs=fs))
        x, h = x3, hn
    loss_tile, dx = loss_fwd_bwd("loss", x, tgt)
    loss = lax.psum(loss_tile[0, 0], ("x", "y", "c"))

    gs = {n: [None] * p[n].shape[0] for n in WEIGHTS}
    scatter_groups = [None] * depth
    token = None
    for i in reversed(range(depth)):
        kind, j = i % N_MIXERS, i // N_MIXERS
        s, la = saved[i], largs[i]
        df, gs["ln_ffn_post"][i] = rms_bwd("rms_bwd_post", s["f"], ln("ln_ffn_post", i), dx, None, BF16, after=token)
        dhf, g = ffn_bwd(s["hf"], s["fs"], df, *la["ffn"])
        gs["ffn_w_in"][i], gs["ffn_conv_w"][i], gs["ffn_conv_b"][i] = g["ffn_w_in"], g["ffn_conv_w"], g["ffn_conv_b"]
        gs["ffn_w_out"][i] = g["ffn_w_out"].reshape(S, -1, D)
        dx, gs["ln_ffn_pre"][i] = rms_bwd("rms_bwd_pre", s["x2"], ln("ln_ffn_pre", i), dhf, dx, F32)

        dc, gs["ln_xa_post"][i] = rms_bwd("rms_bwd_post", s["c"], ln("ln_xa_post", i), dx, None, BF16)
        dhq, dmem_n, g = xa_bwd(s["hq"], s["mem_n"], s["cs"], dc, *la["xa"])
        gs["xa_wq"][i] = g["xa_wq"].reshape(S, D // S, -1)
        gs["xa_wkv"][i] = g["xa_wkv"].reshape(S, D // S, -1)
        gs["xa_wo"][i] = g["xa_wo"]
        _, gs["ln_mem"][i] = rms_bwd("rms_bwd_mem", mem, ln("ln_mem", i), dmem_n, None, BF16)
        dx, gs["ln_xa_pre"][i] = rms_bwd("rms_bwd_pre", s["x1"], ln("ln_xa_pre", i), dhq, dx, F32)

        dm, gs["ln_mix_post"][i] = rms_bwd("rms_bwd_post", s["m"], ln("ln_mix_post", i), dx, None, BF16)
        if kind == 0:
            dhn, g = ssd_mixer_bwd(s["h"], s["ms"], dm, *la["mix"])
            full = jnp.concatenate([g["ssd_wz"], g["ssd_wxbc"], g["ssd_wdt"][:, :nh]], axis=1)
            gs["ssd_w_in"][j] = full.reshape(D, S, -1).transpose(1, 0, 2)
            gs["ssd_w_out"][j] = g["ssd_w_out"].reshape(S, Hd // S, D)
            gs["ssd_conv_w"][j], gs["ssd_conv_b"][j], gs["ssd_norm"][j] = g["ssd_conv_w"], g["ssd_conv_b"], g["ssd_norm"]
            gs["ssd_dt_bias"][j] = g["ssd_dt_bias"][:, :nh]
            gs["ssd_a_log"][j] = g["ssd_alx"].reshape(nh, SSD_HEAD_DIM).sum(-1)[None]
            gs["ssd_d"][j] = g["ssd_dx"].reshape(nh, SSD_HEAD_DIM).sum(-1)[None]
        elif kind == 1:
            dhn, g = sg_mixer_bwd(s["h"], s["ms"], dm, *la["mix"])
            gs["sg_w_in"][j] = g["sg_w_in"]
            gs["sg_w_out"][j] = g["sg_w_out"].reshape(S, -1, D)
            for n in ("sg_v_norm_g", "sg_v_norm_b", "sg_w_spatial", "sg_b_spatial"):
                gs[n][j] = g[n]
        else:
            dhn, g = sb_mixer_bwd(s["h"], s["ms"], dm, *la["mix"])
            gs["sb_w_qkv"][j] = g["sb_w_qkv"]
            gs["sb_w_out"][j] = g["sb_w_out"].reshape(S, -1, D)
        dx, gs["ln_mix_pre"][i] = rms_bwd("rms_bwd_pre", s["x"], ln("ln_mix_pre", i), dhn, dx, F32)

        big_i = [(n, k) for n, k in layer_tensors(i) if n in BIG]
        g_src = [gs[n][k] for n, k in big_i]
        g_land = [lax.empty((3,) + a.shape[1:], a.dtype) for a in g_src]
        grp, token = xchg_start("scatter_start_%d" % i, "scatter", [g_src], [g_land])
        scatter_groups[i] = (big_i, grp[0])

    out = {}
    stack = lambda n: jnp.stack([a.reshape(p[n].shape[1:]) if n not in SHARDED_SMALL else a.reshape(a.shape[-len(p[n].shape) + 1:])
                                 for a in gs[n]])
    small_full = [stack(n) for n in SMALL]
    red = _unpack(all_reduce_small("reduce_small", _pack(small_full) + token[0, 0]), [a.shape for a in small_full])
    small_g = []
    for n, a in zip(SMALL, red):
        if n in SHARDED_SMALL:
            ax = SHARDED_SMALL[n]
            a = lax.dynamic_slice_in_dim(a, me * p[n].shape[ax], p[n].shape[ax], axis=ax)
        small_g.append(a)
    shapes = [p[n].shape for n in SMALL]
    res = adamw("adamw_small", _pack([p[n] for n in SMALL]), _pack([p["m_" + n] for n in SMALL]),
                _pack([p["v_" + n] for n in SMALL]), _pack(small_g))
    for k, r in enumerate(res):
        for n, a in zip(SMALL, _unpack(r, shapes)):
            out.setdefault(n, [None] * 4)[k] = a

    sums, last = {}, token
    for i in reversed(range(depth)):
        big_i, grp = scatter_groups[i]
        sent, got = xchg_wait("scatter_wait_%d" % i, "scatter", grp, last[:1, :1] + res[0][:1, :1] if i == 0 else token)
        for (n, k), own, r in zip(big_i, sent, got):
            mine = lax.dynamic_index_in_dim(own, me, 0, keepdims=False)
            last = sums[(n, k)] = sum_slots("sum_grad_slots", mine.reshape(-1, mine.shape[-1]), r.reshape(3, -1, r.shape[-1]))
    stacked = [jnp.concatenate([sums[(n, k)] for k in range(p[n].shape[0])], axis=0) for n in BIG]
    sib = swap_with_sibling("swap_grads", stacked)
    for n, q, q2 in zip(BIG, stacked, sib):
        two_d = lambda a: a.reshape(-1, a.shape[-1])
        big = adamw("adamw_big", two_d(p[n]), two_d(p["m_" + n]), two_d(p["v_" + n]), q, q2)
        out[n] = [r.reshape(p[n].shape) for r in big]

    return (loss, dx[None]) + tuple(out[n][k] for k in range(4) for n in WEIGHTS)


def kernel(x, mem, ln_mix_pre, ln_mix_post, ln_mem, ln_xa_pre, ln_xa_post, ln_ffn_pre, ln_ffn_post, xa_wq, xa_wkv, xa_wo, ffn_w_in, ffn_conv_w, ffn_conv_b, ffn_w_out, ssd_w_in, ssd_conv_w, ssd_conv_b, ssd_dt_bias, ssd_a_log, ssd_d, ssd_norm, ssd_w_out, sg_w_in, sg_v_norm_g, sg_v_norm_b, sg_w_spatial, sg_b_spatial, sg_w_out, sb_w_qkv, sb_w_out, loss_target, m_ln_mix_pre, m_ln_mix_post, m_ln_mem, m_ln_xa_pre, m_ln_xa_post, m_ln_ffn_pre, m_ln_ffn_post, m_xa_wq, m_xa_wkv, m_xa_wo, m_ffn_w_in, m_ffn_conv_w, m_ffn_conv_b, m_ffn_w_out, m_ssd_w_in, m_ssd_conv_w, m_ssd_conv_b, m_ssd_dt_bias, m_ssd_a_log, m_ssd_d, m_ssd_norm, m_ssd_w_out, m_sg_w_in, m_sg_v_norm_g, m_sg_v_norm_b, m_sg_w_spatial, m_sg_b_spatial, m_sg_w_out, m_sb_w_qkv, m_sb_w_out, v_ln_mix_pre, v_ln_mix_post, v_ln_mem, v_ln_xa_pre, v_ln_xa_post, v_ln_ffn_pre, v_ln_ffn_post, v_xa_wq, v_xa_wkv, v_xa_wo, v_ffn_w_in, v_ffn_conv_w, v_ffn_conv_b, v_ffn_w_out, v_ssd_w_in, v_ssd_conv_w, v_ssd_conv_b, v_ssd_dt_bias, v_ssd_a_log, v_ssd_d, v_ssd_norm, v_ssd_w_out, v_sg_w_in, v_sg_v_norm_g, v_sg_v_norm_b, v_sg_w_spatial, v_sg_b_spatial, v_sg_w_out, v_sb_w_qkv, v_sb_w_out):
    return _step(dict(locals()))
```

```python
import functools
import math

import jax
import jax.numpy as jnp
from jax import lax
from jax.experimental import pallas as pl
from jax.experimental.pallas import tpu as pltpu

F32 = jnp.float32
BF16 = jnp.bfloat16
EPS = 1e-6
LANES = 128
VMEM_LIMIT = 56 * 1024 * 1024
CHUNK = 128
SSD_HEAD_DIM = 64
SSD_STATE = 128
N_CHIPS = 4
MESH = pl.DeviceIdType.MESH
ANY = pl.BlockSpec(memory_space=pl.ANY)

ADAM_LR, ADAM_B1, ADAM_B2, ADAM_EPS, ADAM_WD, ADAM_STEP = 0.001, 0.9, 0.999, 1e-08, 0.01, 10


def _params(sem):
    return pltpu.CompilerParams(dimension_semantics=sem, vmem_limit_bytes=VMEM_LIMIT)


def _sds(shape, dtype):
    return jax.ShapeDtypeStruct(tuple(shape), dtype)


def _tile(n, pref):
    if n <= pref:
        return n
    t = (pref // LANES) * LANES
    while t > LANES and n % t:
        t -= LANES
    assert n % t == 0, (n, pref)
    return t


def _split3(a):
    a1 = a.astype(BF16)
    r = a - a1.astype(F32)
    a2 = r.astype(BF16)
    a3 = (r - a2.astype(F32)).astype(BF16)
    return a1, a2, a3


def _dot(a, b, dims=(((1,), (0,)), ((), ()))):
    return lax.dot_general(a, b, dims, preferred_element_type=F32)


NN = (((1,), (0,)), ((), ()))
NT = (((1,), (1,)), ((), ()))
TN = (((0,), (0,)), ((), ()))


def _dot3r(a, m):
    p1, p2, p3 = _split3(a)
    return _dot(p1, m) + _dot(p2, m) + _dot(p3, m)


def _dot3l(m, a, dims=NN):
    p1, p2, p3 = _split3(a)
    return _dot(m, p1, dims) + _dot(m, p2, dims) + _dot(m, p3, dims)


def _iota(shape, dim):
    return lax.broadcasted_iota(jnp.int32, shape, dim)


def _tri(n, kind):
    r, c = _iota((n, n), 0), _iota((n, n), 1)
    return {"le": c <= r, "lt": c < r, "ge": c >= r, "gt": c > r}[kind]


def _sigmoid(x):
    return 1.0 / (1.0 + jnp.exp(-x))


def _silu(x):
    return x * _sigmoid(x)


def _dsilu(x):
    s = _sigmoid(x)
    return s * (1.0 + x * (1.0 - s))


_GC = math.sqrt(2.0 / math.pi)


def _gelu(x):
    return 0.5 * x * (1.0 + jnp.tanh(_GC * (x + 0.044715 * x * x * x)))


def _dgelu(x):
    th = jnp.tanh(_GC * (x + 0.044715 * x * x * x))
    return 0.5 * (1.0 + th) + 0.5 * x * (1.0 - th * th) * _GC * (1.0 + 3.0 * 0.044715 * x * x)


def _softplus(x):
    return jnp.maximum(x, 0.0) + jnp.log(1.0 + jnp.exp(-jnp.abs(x)))


def _shift_down(p, s):
    rows = _iota(p.shape, 0)
    return jnp.where(rows >= s, pltpu.roll(p, s, 0), 0.0)


def _shift_up(p, s):
    n = p.shape[0]
    rows = _iota(p.shape, 0)
    return jnp.where(rows < n - s, pltpu.roll(p, n - s, 0), 0.0)


def _mm(name, mode, a, b, out_sds, grid, a_spec, b_spec, o_spec, acc_shape, add=None, add_spec=None):
    dims = {"nn": NN, "nt": NT, "tn": TN}[mode]
    nk = grid[2]
    has_add = add is not None

    def body(*refs):
        if has_add:
            a_ref, b_ref, c_ref, o_ref = refs[:4]
        else:
            a_ref, b_ref, o_ref = refs[:3]
            c_ref = None
        part = lax.dot_general(a_ref[...], b_ref[...], dims, preferred_element_type=F32)

        def finish(r):
            if c_ref is not None:
                r = r + c_ref[...].astype(F32)
            o_ref[...] = r.astype(o_ref.dtype)

        if nk == 1:
            finish(part)
        else:
            acc = refs[-1]
            k = pl.program_id(2)

            @pl.when(k == 0)
            def _():
                acc[...] = part

            @pl.when(k > 0)
            def _():
                acc[...] += part

            @pl.when(k == nk - 1)
            def _():
                finish(acc[...])

    in_specs = [a_spec, b_spec] + ([add_spec] if has_add else [])
    args = (a, b) + ((add,) if has_add else ())
    return pl.pallas_call(
        body, name=name, grid=grid, in_specs=in_specs, out_specs=o_spec, out_shape=out_sds,
        scratch_shapes=[pltpu.VMEM(acc_shape, F32)] if nk > 1 else [],
        compiler_params=_params(("parallel", "parallel", "arbitrary")),
    )(*args)


class W:
    def __init__(self, kind, arr, layer, shape=None):
        self.kind, self._arr, self.layer = kind, arr, layer
        shape = arr.shape if shape is None else shape
        if kind == "cols":
            s, _, k, c = shape
            self.K, self.N, self.S, self.C = k, s * c, s, c
        elif kind == "rows":
            s, _, r, n = shape
            self.K, self.N, self.S, self.R = s * r, n, s, r
        else:
            _, k, n = shape
            self.K, self.N = k, n

    def get(self, operand):
        if callable(self._arr):
            self._arr = self._arr(operand)
        return self._arr


def mm_fwd(name, a, w, out_dtype, tm=1024, tn=1536, a_spec=None, out_sds=None, o_spec=None, add=None):
    M = a.shape[0]
    tm = min(tm, M)
    l = w.layer
    if w.kind == "cols":
        tn = _tile(w.C, tn)
        nps = w.C // tn
        tk, nk = w.K, 1
        b_spec = pl.BlockSpec((None, None, tk, tn), lambda i, j, k: (j // nps, l, 0, j % nps))
    elif w.kind == "rows":
        tn = _tile(w.N, tn)
        tk, nk = w.R, w.S
        b_spec = pl.BlockSpec((None, None, tk, tn), lambda i, j, k: (k, l, 0, j))
    else:
        tn = _tile(w.N, tn)
        tk, nk = w.K, 1
        b_spec = pl.BlockSpec((None, tk, tn), lambda i, j, k: (l, 0, j))
    grid = (M // tm, w.N // tn, nk)
    if a_spec is None:
        a_spec = pl.BlockSpec((tm, tk), lambda i, j, k: (i, k))
    if out_sds is None:
        out_sds = _sds((M, w.N), out_dtype)
        o_spec = pl.BlockSpec((tm, tn), lambda i, j, k: (i, j))
    else:
        o_spec = o_spec(tm, tn)
    add_spec = pl.BlockSpec((tm, tn), lambda i, j, k: (i, j)) if add is not None else None
    return _mm(name, "nn", a, w.get(a), out_sds, grid, a_spec, b_spec, o_spec, (tm, tn), add, add_spec)


def mm_dx(name, dy, w, out_dtype, tm=1024, tn=1024, a_spec=None, add=None):
    M = dy.shape[-2]
    tm = min(tm, M)
    l = w.layer
    if w.kind == "cols":
        tn = _tile(w.K, tn)
        tk, nk = w.C, w.S
        b_spec = pl.BlockSpec((None, None, tn, tk), lambda i, j, k: (k, l, j, 0))
    elif w.kind == "rows":
        tn = _tile(w.R, tn)
        npr = w.R // tn
        tk, nk = w.N, 1
        b_spec = pl.BlockSpec((None, None, tn, tk), lambda i, j, k: (j // npr, l, j % npr, 0))
    else:
        tn = _tile(w.K, tn)
        tk, nk = _tile(w.N, 2048), w.N // _tile(w.N, 2048)
        b_spec = pl.BlockSpec((None, tn, tk), lambda i, j, k: (l, j, k))
    grid = (M // tm, w.K // tn, nk)
    if a_spec is None:
        a_spec = pl.BlockSpec((tm, tk), lambda i, j, k: (i, k))
    else:
        a_spec = a_spec(tm, tk)
    out_sds = _sds((M, w.K), out_dtype)
    o_spec = pl.BlockSpec((tm, tn), lambda i, j, k: (i, j))
    add_spec = o_spec if add is not None else None
    return _mm(name, "nt", dy, w.get(dy), out_sds, grid, a_spec, b_spec, o_spec, (tm, tn), add, add_spec)


def mm_dw(name, a, dy, out_dtype, n_shards=None, tm=512, tn=1536, b_spec=None, N=None):
    T, K = a.shape
    N = dy.shape[-1] if N is None else N
    tm = _tile(K, tm)
    if n_shards:
        C = N // n_shards
        tn = _tile(C, tn)
        nps = C // tn
        out_sds = _sds((n_shards, K, C), out_dtype)
        o_spec = pl.BlockSpec((None, tm, tn), lambda i, j, k: (j // nps, i, j % nps))
    else:
        tn = _tile(N, tn)
        out_sds = _sds((K, N), out_dtype)
        o_spec = pl.BlockSpec((tm, tn), lambda i, j, k: (i, j))
    grid = (K // tm, N // tn, 1)
    a_spec = pl.BlockSpec((T, tm), lambda i, j, k: (0, i))
    if b_spec is None:
        b_spec = pl.BlockSpec((T, tn), lambda i, j, k: (0, j))
    else:
        b_spec = b_spec(T, tn)
    return _mm(name, "tn", a, dy, out_sds, grid, a_spec, b_spec, o_spec, (tm, tn))


def _rms(x, g):
    r = lax.rsqrt(jnp.mean(x * x, axis=-1, keepdims=True) + EPS)
    return x * r * g


def rms_fwd(name, x, g, tr=512, after=None):
    T, D = x.shape
    tr = min(tr, T)

    def body(x_ref, g_ref, *rest):
        o_ref = rest[-1]
        o_ref[...] = _rms(x_ref[...], g_ref[...]).astype(o_ref.dtype)

    row = pl.BlockSpec((tr, D), lambda i: (i, 0))
    vec = pl.BlockSpec((1, D), lambda i: (0, 0))
    extra = [] if after is None else [after]
    return pl.pallas_call(body, name=name, grid=(T // tr,), in_specs=[row, vec] + [ANY] * len(extra), out_specs=row,
                          out_shape=_sds((T, D), BF16), compiler_params=_params(("parallel",)))(x, g, *extra)


def resid_norm(name, x, m, g_post, g_next, tr=512):
    T, D = x.shape
    tr = min(tr, T)
    has_next = g_next is not None

    def body(*refs):
        if has_next:
            x_ref, m_ref, gp_ref, gn_ref, xo_ref, h_ref = refs
        else:
            x_ref, m_ref, gp_ref, xo_ref = refs
        xn = x_ref[...] + _rms(m_ref[...], gp_ref[...])
        xo_ref[...] = xn
        if has_next:
            h_ref[...] = _rms(xn, gn_ref[...]).astype(h_ref.dtype)

    row = pl.BlockSpec((tr, D), lambda i: (i, 0))
    vec = pl.BlockSpec((1, D), lambda i: (0, 0))
    ins = [row, row, vec] + ([vec] if has_next else [])
    args = (x, m, g_post) + ((g_next,) if has_next else ())
    outs = [row, row] if has_next else row
    shp = [_sds((T, D), F32), _sds((T, D), BF16)] if has_next else _sds((T, D), F32)
    res = pl.pallas_call(body, name=name, grid=(T // tr,), in_specs=ins, out_specs=outs, out_shape=shp,
                         compiler_params=_params(("parallel",)))(*args)
    return res if has_next else (res, None)


def rms_bwd(name, xin, g, dy, resid, out_dtype, tr=512, after=None):
    T, D = xin.shape
    tr = min(tr, T)
    has_res = resid is not None

    def body(*refs):
        dx_ref, dg_ref = refs[-2:]
        if has_res:
            x_ref, g_ref, dy_ref, r_ref = refs[:4]
        else:
            x_ref, g_ref, dy_ref = refs[:3]
        x = x_ref[...].astype(F32)
        dy_ = dy_ref[...].astype(F32)
        r = lax.rsqrt(jnp.mean(x * x, axis=-1, keepdims=True) + EPS)
        xh = x * r
        dxh = dy_ * g_ref[...]
        dx = r * (dxh - xh * jnp.mean(dxh * xh, axis=-1, keepdims=True))
        if has_res:
            dx = dx + r_ref[...]
        dx_ref[...] = dx.astype(dx_ref.dtype)
        part = jnp.sum(dy_ * xh, axis=0, keepdims=True)

        @pl.when(pl.program_id(0) == 0)
        def _():
            dg_ref[...] = part

        @pl.when(pl.program_id(0) > 0)
        def _():
            dg_ref[...] += part

    row = pl.BlockSpec((tr, D), lambda i: (i, 0))
    vec = pl.BlockSpec((1, D), lambda i: (0, 0))
    ins = [row, vec, row] + ([row] if has_res else []) + ([] if after is None else [ANY])
    args = (xin, g, dy) + ((resid,) if has_res else ()) + (() if after is None else (after,))
    return pl.pallas_call(body, name=name, grid=(T // tr,), in_specs=ins, out_specs=[row, vec],
                          out_shape=[_sds((T, D), out_dtype), _sds((1, D), F32)],
                          compiler_params=_params(("arbitrary",)))(*args)


def loss_fwd_bwd(name, y, tgt, tr=512):
    T, D = y.shape
    tr = min(tr, T)

    def body(y_ref, t_ref, l_ref, d_ref):
        e = y_ref[...] - t_ref[...]
        d_ref[...] = e * (1.0 / D)
        part = 0.5 * jnp.sum(jnp.mean(e * e, axis=-1, keepdims=True), axis=0, keepdims=True)
        part = jnp.broadcast_to(part, l_ref.shape)

        @pl.when(pl.program_id(0) == 0)
        def _():
            l_ref[...] = part

        @pl.when(pl.program_id(0) > 0)
        def _():
            l_ref[...] += part

    row = pl.BlockSpec((tr, D), lambda i: (i, 0))
    return pl.pallas_call(body, name=name, grid=(T // tr,), in_specs=[row, row],
                          out_specs=[pl.BlockSpec((8, LANES), lambda i: (0, 0)), row],
                          out_shape=[_sds((8, LANES), F32), _sds((T, D), F32)],
                          compiler_params=_params(("arbitrary",)))(y, tgt)


def _conv_taps(p, w_ref, K):
    taps = [p] + [_shift_down(p, s) for s in range(1, K)]
    out = taps[0] * w_ref[pl.ds(K - 1, 1), :]
    for s in range(1, K):
        out = out + taps[s] * w_ref[pl.ds(K - 1 - s, 1), :]
    return taps, out


def _conv_bwd(dpre, taps, w_ref, dw_ref, db_ref, K):
    db_ref[...] = jnp.sum(dpre, axis=0, keepdims=True)
    dp = dpre * w_ref[pl.ds(K - 1, 1), :]
    dw_ref[pl.ds(K - 1, 1), :] = jnp.sum(dpre * taps[0], axis=0, keepdims=True)
    for s in range(1, K):
        dw_ref[pl.ds(K - 1 - s, 1), :] = jnp.sum(dpre * taps[s], axis=0, keepdims=True)
        dp = dp + _shift_up(dpre, s) * w_ref[pl.ds(K - 1 - s, 1), :]
    return dp


def ffn_gate_fwd(name, P, cw, cb, tc=LANES):
    _, T, F = P.shape
    K = cw.shape[0]
    nf = F // tc

    def body(pg_ref, pu_ref, wg_ref, wu_ref, bg_ref, bu_ref, o_ref):
        _, g = _conv_taps(pg_ref[...], wg_ref, K)
        _, u = _conv_taps(pu_ref[...], wu_ref, K)
        o_ref[...] = (_gelu(g + bg_ref[...]) * (u + bu_ref[...])).astype(o_ref.dtype)

    pg = pl.BlockSpec((None, T, tc), lambda j: (0, 0, j))
    pu = pl.BlockSpec((None, T, tc), lambda j: (1, 0, j))
    wg = pl.BlockSpec((K, tc), lambda j: (0, j))
    wu = pl.BlockSpec((K, tc), lambda j: (0, j + nf))
    bg = pl.BlockSpec((1, tc), lambda j: (0, j))
    bu = pl.BlockSpec((1, tc), lambda j: (0, j + nf))
    return pl.pallas_call(body, name=name, grid=(nf,), in_specs=[pg, pu, wg, wu, bg, bu],
                          out_specs=pl.BlockSpec((T, tc), lambda j: (0, j)), out_shape=_sds((T, F), BF16),
                          compiler_params=_params(("parallel",)))(P, P, cw, cw, cb, cb)


def ffn_gate_bwd(name, P, da, cw, cb, tc=LANES):
    _, T, F = P.shape
    K = cw.shape[0]
    nf = F // tc

    def body(pg_ref, pu_ref, da_ref, wg_ref, wu_ref, bg_ref, bu_ref, dp_ref, dwg_ref, dwu_ref, dbg_ref, dbu_ref):
        tg, g = _conv_taps(pg_ref[...], wg_ref, K)
        tu, u = _conv_taps(pu_ref[...], wu_ref, K)
        g = g + bg_ref[...]
        u = u + bu_ref[...]
        da_ = da_ref[...]
        dg = da_ * u * _dgelu(g)
        du = da_ * _gelu(g)
        dp_ref[0] = _conv_bwd(dg, tg, wg_ref, dwg_ref, dbg_ref, K).astype(dp_ref.dtype)
        dp_ref[1] = _conv_bwd(du, tu, wu_ref, dwu_ref, dbu_ref, K).astype(dp_ref.dtype)

    pg = pl.BlockSpec((None, T, tc), lambda j: (0, 0, j))
    pu = pl.BlockSpec((None, T, tc), lambda j: (1, 0, j))
    col = pl.BlockSpec((T, tc), lambda j: (0, j))
    wg = pl.BlockSpec((K, tc), lambda j: (0, j))
    wu = pl.BlockSpec((K, tc), lambda j: (0, j + nf))
    bg = pl.BlockSpec((1, tc), lambda j: (0, j))
    bu = pl.BlockSpec((1, tc), lambda j: (0, j + nf))
    return pl.pallas_call(
        body, name=name, grid=(nf,), in_specs=[pg, pu, col, wg, wu, bg, bu],
        out_specs=[pl.BlockSpec((2, T, tc), lambda j: (0, 0, j)), wg, wg, bg, bg],
        out_shape=[_sds((2, T, F), BF16), _sds((K, F), F32), _sds((K, F), F32), _sds((1, F), F32), _sds((1, F), F32)],
        compiler_params=_params(("parallel",)))(P, P, da, cw, cw, cb, cb)


def xattn_fwd(name, q, kv, n_heads, tq=512):
    T, Wd = q.shape
    Mm = kv.shape[0]
    hd = Wd // n_heads
    scale = hd ** -0.5
    tq = min(tq, T)

    def body(q_ref, kv_ref, o_ref):
        for h in range(n_heads):
            qh = q_ref[:, h * hd:(h + 1) * hd]
            kh = kv_ref[:, h * hd:(h + 1) * hd]
            vh = kv_ref[:, Wd + h * hd:Wd + (h + 1) * hd]
            s = _dot(qh, kh, NT) * scale
            s = s - jnp.max(s, axis=-1, keepdims=True)
            e = jnp.exp(s)
            p = e / jnp.sum(e, axis=-1, keepdims=True)
            o_ref[:, h * hd:(h + 1) * hd] = _dot(p.astype(BF16), vh).astype(o_ref.dtype)

    return pl.pallas_call(body, name=name, grid=(T // tq,),
                          in_specs=[pl.BlockSpec((tq, Wd), lambda i: (i, 0)), pl.BlockSpec((Mm, 2 * Wd), lambda i: (0, 0))],
                          out_specs=pl.BlockSpec((tq, Wd), lambda i: (i, 0)), out_shape=_sds((T, Wd), BF16),
                          compiler_params=_params(("parallel",)))(q, kv)


def xattn_bwd(name, q, kv, do, n_heads, tq=512):
    T, Wd = q.shape
    Mm = kv.shape[0]
    hd = Wd // n_heads
    scale = hd ** -0.5
    tq = min(tq, T)

    def body(q_ref, kv_ref, do_ref, dq_ref, dkv_ref):
        @pl.when(pl.program_id(0) == 0)
        def _():
            dkv_ref[...] = jnp.zeros_like(dkv_ref)

        for h in range(n_heads):
            sl = slice(h * hd, (h + 1) * hd)
            sv = slice(Wd + h * hd, Wd + (h + 1) * hd)
            qh, kh, vh = q_ref[:, sl], kv_ref[:, sl], kv_ref[:, sv]
            doh = do_ref[:, sl].astype(BF16)
            s = _dot(qh, kh, NT) * scale
            s = s - jnp.max(s, axis=-1, keepdims=True)
            e = jnp.exp(s)
            p = e / jnp.sum(e, axis=-1, keepdims=True)
            dp = _dot(doh, vh, NT)
            ds = (p * (dp - jnp.sum(dp * p, axis=-1, keepdims=True)) * scale).astype(BF16)
            dq_ref[:, sl] = _dot(ds, kh).astype(dq_ref.dtype)
            dkv_ref[:, sl] += _dot(ds, qh, TN)
            dkv_ref[:, sv] += _dot(p.astype(BF16), doh, TN)

    row = pl.BlockSpec((tq, Wd), lambda i: (i, 0))
    full = pl.BlockSpec((Mm, 2 * Wd), lambda i: (0, 0))
    return pl.pallas_call(body, name=name, grid=(T // tq,), in_specs=[row, full, row], out_specs=[row, full],
                          out_shape=[_sds((T, Wd), BF16), _sds((Mm, 2 * Wd), F32)],
                          compiler_params=_params(("arbitrary",)))(q, kv, do)


def ffn_fwd(hf, w_in, cw, cb, w_out):
    T = hf.shape[0]
    F = w_out.K
    tn = _tile(w_in.C, 1536)
    nfp = F // tn
    P = mm_fwd("ffn_in", hf, w_in, F32, tn=tn, out_sds=_sds((2, T, F), F32),
               o_spec=lambda tm, tn_: pl.BlockSpec((None, tm, tn_), lambda i, j, k: (j // nfp, i, j % nfp)))
    a = ffn_gate_fwd("ffn_gate", P, cw, cb)
    f = mm_fwd("ffn_out", a, w_out, F32, tn=2048)
    return f, (P, a)


def ffn_bwd(hf, saved, df, w_in, cw, cb, w_out):
    P, a = saved
    T = hf.shape[0]
    F = w_out.K
    C = w_in.C
    da = mm_dx("ffn_out_dx", df, w_out, F32, tn=w_out.R)
    dw_out = mm_dw("ffn_out_dw", a, df, BF16, tn=1024)
    dP, dcw_g, dcw_u, dcb_g, dcb_u = ffn_gate_bwd("ffn_gate_bwd", P, da, cw, cb)
    dw_in = mm_dw("ffn_in_dw", hf, dP, BF16, n_shards=w_in.S, N=2 * F,
                  b_spec=lambda T_, tn: pl.BlockSpec((None, T_, tn), lambda i, j, k: (j // (F // tn), 0, j % (F // tn))))
    per = F // C
    dhf = mm_dx("ffn_in_dx", dP, w_in, F32,
                a_spec=lambda tm, tk: pl.BlockSpec((None, tm, tk), lambda i, j, k: (k // per, i, k % per)))
    grads = dict(ffn_w_in=dw_in, ffn_w_out=dw_out, ffn_conv_w=jnp.concatenate([dcw_g, dcw_u], axis=1),
                 ffn_conv_b=jnp.concatenate([dcb_g, dcb_u], axis=1))
    return dhf, grads


def xa_fwd(hq, mem_n, wq, wkv, wo, n_heads):
    q = mm_fwd("xa_q", hq, wq, BF16)
    kv = mm_fwd("xa_kv", mem_n, wkv, BF16)
    o = xattn_fwd("xa_core", q, kv, n_heads)
    c = mm_fwd("xa_o", o, wo, F32)
    return c, (q, kv, o)


def xa_bwd(hq, mem_n, saved, dc, wq, wkv, wo, n_heads):
    q, kv, o = saved
    do = mm_dx("xa_o_dx", dc, wo, F32)
    dwo = mm_dw("xa_o_dw", o, dc, BF16, n_shards=wo.S)
    dq, dkv = xattn_bwd("xa_core_bwd", q, kv, do, n_heads)
    dkv = dkv.astype(BF16)
    dwq = mm_dw("xa_q_dw", hq, dq, BF16)
    dhq = mm_dx("xa_q_dx", dq, wq, F32, tn=wq.R)
    dwkv = mm_dw("xa_kv_dw", mem_n, dkv, BF16)
    dmem_n = mm_dx("xa_kv_dx", dkv, wkv, F32, tn=wkv.R)
    return dhq, dmem_n, dict(xa_wq=dwq, xa_wkv=dwkv, xa_wo=dwo)


def _sb_logits(q, kblk, scale):
    z = _dot(q, kblk, NT) * scale
    l1 = -_softplus(z)
    return z, l1, z + l1


def _split2(a):
    a1 = a.astype(BF16)
    return a1, (a - a1.astype(F32)).astype(BF16)


def _dot2r(a, m):
    p1, p2 = _split2(a)
    return _dot(p1, m) + _dot(p2, m)


SB_TQ = 1024


def sb_fwd(name, qkv, n_heads, tq=SB_TQ):
    T = qkv.shape[0]
    hd = qkv.shape[1] // (3 * n_heads)
    scale = hd ** -0.5
    Q = CHUNK
    tq = min(tq, T)
    nb = tq // Q
    unroll = 2 if nb % 2 == 0 else 1

    def body(q_ref, k_ref, v_ref, o_ref, lt_ref):
        i = pl.program_id(1)
        q = q_ref[...]
        mcat = jnp.concatenate([_tri(Q, "lt"), jnp.ones((Q, Q), jnp.bool_)], axis=1).astype(BF16)

        def block(kb, carry, q_, masked):
            c, acc = carry
            off = pl.multiple_of(kb * Q, Q)
            kblk, vblk = k_ref[pl.ds(off, Q), :], v_ref[pl.ds(off, Q), :]
            _, l1, lb = _sb_logits(q_, kblk, scale)
            if masked:
                valid = _iota(l1.shape, 1) < _iota(l1.shape, 0)
                l1 = jnp.where(valid, l1, 0.0)
            r = _dot2r(l1, mcat)
            a = jnp.exp(lb + r[:, :Q] + c)
            if masked:
                a = jnp.where(valid, a, 0.0)
            return c + r[:, Q:], acc + _dot(a.astype(BF16), vblk)

        c, acc = jnp.zeros((tq, Q), F32), jnp.zeros((tq, hd), F32)
        for b in reversed(range(nb)):
            lo = b * Q
            cb, ab = block(i * nb + b, (c[lo:], acc[lo:]), q[lo:], True)
            c = cb if lo == 0 else jnp.concatenate([c[:lo], cb], axis=0)
            acc = ab if lo == 0 else jnp.concatenate([acc[:lo], ab], axis=0)

        def step(r, cr):
            for u in range(unroll):
                cr = block(i * nb - 1 - unroll * r - u, cr, q, False)
            return cr

        c, acc = lax.fori_loop(0, i * (nb // unroll), step, (c, acc))
        o_ref[...] = acc.astype(o_ref.dtype)
        lt_ref[...] = c

    H = n_heads
    return pl.pallas_call(
        body, name=name, grid=(H, T // tq),
        in_specs=[pl.BlockSpec((tq, hd), lambda h, i: (i, h)), pl.BlockSpec((T, hd), lambda h, i: (0, H + h)),
                  pl.BlockSpec((T, hd), lambda h, i: (0, 2 * H + h))],
        out_specs=[pl.BlockSpec((tq, hd), lambda h, i: (i, h)), pl.BlockSpec((None, tq, Q), lambda h, i: (h, i, 0))],
        out_shape=[_sds((T, H * hd), BF16), _sds((H, T, Q), F32)],
        compiler_params=_params(("parallel", "arbitrary")))(qkv, qkv, qkv)


def sb_bwd(name, qkv, do, lt, n_heads, tq=SB_TQ):
    T = qkv.shape[0]
    hd = qkv.shape[1] // (3 * n_heads)
    scale = hd ** -0.5
    Q = CHUNK
    tq = min(tq, T)
    nb = tq // Q
    unroll = 2 if nb % 2 == 0 else 1

    def body(q_ref, k_ref, v_ref, do_ref, lt_ref, dq_ref, dk_ref, dv_ref):
        i = pl.program_id(1)

        @pl.when(i == 0)
        def _():
            dk_ref[...] = jnp.zeros_like(dk_ref)
            dv_ref[...] = jnp.zeros_like(dv_ref)

        q, do_, ltot = q_ref[...], do_ref[...], lt_ref[...]
        ones = jnp.ones((Q, Q), jnp.bool_)
        mrev = jnp.concatenate([_tri(Q, "lt"), ones], axis=1).astype(BF16)
        mfwd = jnp.concatenate([_tri(Q, "gt"), ones], axis=1).astype(BF16)
        def block(kb, carry, q_, d_, lt_, masked):
            pin, pre, dq = carry
            off = pl.multiple_of(kb * Q, Q)
            kblk, vblk = k_ref[pl.ds(off, Q), :], v_ref[pl.ds(off, Q), :]
            _, l1, lb = _sb_logits(q_, kblk, scale)
            if masked:
                valid = _iota(l1.shape, 1) < _iota(l1.shape, 0)
                l1 = jnp.where(valid, l1, 0.0)
            r = _dot2r(l1, mrev)
            pin = pin + r[:, Q:]
            a = jnp.exp(lb + r[:, :Q] + (lt_ - pin))
            if masked:
                a = jnp.where(valid, a, 0.0)
            de = _dot(d_, vblk, NT) * a
            r2 = _dot2r(de, mfwd)
            dl1 = pre + r2[:, :Q]
            pre = pre + r2[:, Q:]
            sig = jnp.exp(lb)
            dz = (de * (1.0 - sig) - dl1 * sig) * scale
            if masked:
                dz = jnp.where(valid, dz, 0.0)
            dzb = dz.astype(BF16)
            dk_ref[pl.ds(off, Q), :] += _dot(dzb, q_, TN)
            dv_ref[pl.ds(off, Q), :] += _dot(a.astype(BF16), d_, TN)
            return pin, pre, dq + _dot(dzb, kblk)

        def step(r, cr):
            for u in range(unroll):
                cr = block(unroll * r + u, cr, q, do_, ltot, False)
            return cr

        init = (jnp.zeros((tq, Q), F32), jnp.zeros((tq, Q), F32), jnp.zeros((tq, hd), F32))
        carry = lax.fori_loop(0, i * (nb // unroll), step, init)
        for b in range(nb):
            lo = b * Q
            part = block(i * nb + b, tuple(a[lo:] for a in carry), q[lo:], do_[lo:], ltot[lo:], True)
            carry = part if lo == 0 else tuple(jnp.concatenate([a[:lo], pb], axis=0) for a, pb in zip(carry, part))
        dq_ref[...] = carry[2].astype(dq_ref.dtype)

    H = n_heads
    qs = pl.BlockSpec((tq, hd), lambda h, i: (i, h))
    full = pl.BlockSpec((T, hd), lambda h, i: (0, h))
    return pl.pallas_call(
        body, name=name, grid=(H, T // tq),
        in_specs=[qs, pl.BlockSpec((T, hd), lambda h, i: (0, H + h)), pl.BlockSpec((T, hd), lambda h, i: (0, 2 * H + h)),
                  qs, pl.BlockSpec((None, tq, Q), lambda h, i: (h, i, 0))],
        out_specs=[qs, full, full],
        out_shape=[_sds((T, H * hd), BF16), _sds((T, H * hd), F32), _sds((T, H * hd), F32)],
        compiler_params=_params(("parallel", "arbitrary")))(qkv, qkv, qkv, do, lt)


def sb_mixer_fwd(hn, w_qkv, w_out, n_heads):
    qkv = mm_fwd("sb_qkv", hn, w_qkv, BF16)
    o, lt = sb_fwd("sb_core", qkv, n_heads)
    m = mm_fwd("sb_out", o, w_out, F32, tn=1024)
    return m, (qkv, o, lt)


def sb_mixer_bwd(hn, saved, dm, w_qkv, w_out, n_heads):
    qkv, o, lt = saved
    do = mm_dx("sb_out_dx", dm, w_out, BF16, tn=w_out.R)
    dw_out = mm_dw("sb_out_dw", o, dm, BF16, tn=1024)
    dq, dk, dv = sb_bwd("sb_core_bwd", qkv, do, lt, n_heads)
    dqkv = jnp.concatenate([dq, dk.astype(BF16), dv.astype(BF16)], axis=1)
    dw_qkv = mm_dw("sb_qkv_dw", hn, dqkv, BF16, n_shards=w_qkv.S)
    dhn = mm_dx("sb_qkv_dx", dqkv, w_qkv, F32)
    return dhn, dict(sb_w_qkv=dw_qkv, sb_w_out=dw_out)


def _sgu_common(p_ref, vg_ref, vb_ref, Wd):
    pu, pv = p_ref[:, :Wd], p_ref[:, Wd:]
    u, v = _gelu(pu), _gelu(pv)
    xc = v - jnp.mean(v, axis=-1, keepdims=True)
    r = lax.rsqrt(jnp.mean(xc * xc, axis=-1, keepdims=True) + EPS)
    xh = xc * r
    return pu, pv, u, xh, r, xh * vg_ref[...] + vb_ref[...]


def sgu_fwd(name, P, vg, vb, ws, bexp):
    T = P.shape[0]
    Wd = P.shape[1] // 2
    G = ws.shape[0]
    gw = Wd // G
    Q = CHUNK

    def body(p_ref, vg_ref, vb_ref, ws_ref, be_ref, o_ref):
        _, _, u, _, _, vn = _sgu_common(p_ref, vg_ref, vb_ref, Wd)
        tril = _tri(Q, "le")
        for g in range(G):
            sl = slice(g * gw, (g + 1) * gw)
            wsg = jnp.where(tril, ws_ref[g], 0.0).astype(BF16)
            mixed = _dot(wsg, vn[:, sl].astype(BF16)) + be_ref[:, sl]
            o_ref[:, sl] = (u[:, sl] * mixed).astype(o_ref.dtype)

    vec = pl.BlockSpec((1, Wd), lambda c: (0, 0))
    return pl.pallas_call(
        body, name=name, grid=(T // Q,),
        in_specs=[pl.BlockSpec((Q, 2 * Wd), lambda c: (c, 0)), vec, vec, pl.BlockSpec((G, Q, Q), lambda c: (0, 0, 0)),
                  pl.BlockSpec((Q, Wd), lambda c: (0, 0))],
        out_specs=pl.BlockSpec((Q, Wd), lambda c: (c, 0)), out_shape=_sds((T, Wd), BF16),
        compiler_params=_params(("parallel",)))(P, vg, vb, ws, bexp)


def sgu_bwd(name, P, dgated, vg, vb, ws, bexp):
    T = P.shape[0]
    Wd = P.shape[1] // 2
    G = ws.shape[0]
    gw = Wd // G
    Q = CHUNK
    nc = T // Q

    def body(p_ref, dg_ref, vg_ref, vb_ref, ws_ref, be_ref, dp_ref, dws_ref, dvg_ref, dvb_ref, dbs_ref, dvn_scr, dbe_scr):
        c = pl.program_id(0)

        @pl.when(c == 0)
        def _():
            dws_ref[...] = jnp.zeros_like(dws_ref)
            dvg_ref[...] = jnp.zeros_like(dvg_ref)
            dvb_ref[...] = jnp.zeros_like(dvb_ref)
            dbe_scr[...] = jnp.zeros_like(dbe_scr)

        pu, pv, u, xh, r, vn = _sgu_common(p_ref, vg_ref, vb_ref, Wd)
        tril = _tri(Q, "le")
        for g in range(G):
            sl = slice(g * gw, (g + 1) * gw)
            wsg = jnp.where(tril, ws_ref[g], 0.0).astype(BF16)
            vng = vn[:, sl].astype(BF16)
            mixed = _dot(wsg, vng) + be_ref[:, sl]
            dgt = dg_ref[:, sl]
            dp_ref[:, sl] = (dgt * mixed * _dgelu(pu[:, sl])).astype(dp_ref.dtype)
            dmix = dgt * u[:, sl]
            dmb = dmix.astype(BF16)
            dws_ref[g] += jnp.where(tril, _dot(dmb, vng, NT), 0.0)
            dvn_scr[:, sl] = _dot(wsg, dmb, TN)
            dbe_scr[:, sl] += dmix
        dvn = dvn_scr[...]
        dvg_ref[...] += jnp.sum(dvn * xh, axis=0, keepdims=True)
        dvb_ref[...] += jnp.sum(dvn, axis=0, keepdims=True)
        dxh = dvn * vg_ref[...]
        dv = r * (dxh - jnp.mean(dxh, axis=-1, keepdims=True) - xh * jnp.mean(dxh * xh, axis=-1, keepdims=True))
        dp_ref[:, Wd:] = (dv * _dgelu(pv)).astype(dp_ref.dtype)

        @pl.when(c == nc - 1)
        def _():
            sel = (_iota((Wd, LANES), 0) // gw == _iota((Wd, LANES), 1)).astype(BF16)
            dbs_ref[...] = _dot3r(dbe_scr[...], sel)

    vec = pl.BlockSpec((1, Wd), lambda c: (0, 0))
    wsb = pl.BlockSpec((G, Q, Q), lambda c: (0, 0, 0))
    return pl.pallas_call(
        body, name=name, grid=(nc,),
        in_specs=[pl.BlockSpec((Q, 2 * Wd), lambda c: (c, 0)), pl.BlockSpec((Q, Wd), lambda c: (c, 0)), vec, vec, wsb,
                  pl.BlockSpec((Q, Wd), lambda c: (0, 0))],
        out_specs=[pl.BlockSpec((Q, 2 * Wd), lambda c: (c, 0)), wsb, vec, vec, pl.BlockSpec((Q, LANES), lambda c: (0, 0))],
        out_shape=[_sds((T, 2 * Wd), BF16), _sds((G, Q, Q), F32), _sds((1, Wd), F32), _sds((1, Wd), F32), _sds((Q, LANES), F32)],
        scratch_shapes=[pltpu.VMEM((Q, Wd), F32), pltpu.VMEM((Q, Wd), F32)],
        compiler_params=_params(("arbitrary",)))(P, dgated, vg, vb, ws, bexp)


def sg_mixer_fwd(hn, w_in, vg, vb, ws, bs, w_out):
    G = ws.shape[0]
    Wd = vg.shape[1]
    P = mm_fwd("sg_in", hn, w_in, F32)
    bexp = jnp.repeat(bs.T, Wd // G, axis=1)
    gated = sgu_fwd("sg_core", P, vg, vb, ws, bexp)
    m = mm_fwd("sg_out", gated, w_out, F32, tn=1024)
    return m, (P, bexp, gated)


def sg_mixer_bwd(hn, saved, dm, w_in, vg, vb, ws, w_out):
    P, bexp, gated = saved
    G = ws.shape[0]
    dgated = mm_dx("sg_out_dx", dm, w_out, F32, tn=w_out.R)
    dw_out = mm_dw("sg_out_dw", gated, dm, BF16, tn=1024)
    dP, dws, dvg, dvb, dbs = sgu_bwd("sg_core_bwd", P, dgated, vg, vb, ws, bexp)
    dw_in = mm_dw("sg_in_dw", hn, dP, BF16, n_shards=w_in.S)
    dhn = mm_dx("sg_in_dx", dP, w_in, F32)
    grads = dict(sg_w_in=dw_in, sg_w_out=dw_out, sg_w_spatial=dws, sg_v_norm_g=dvg, sg_v_norm_b=dvb,
                 sg_b_spatial=dbs[:, :G].T)
    return dhn, grads


def ssd_conv_fwd(name, xbc, cw, cb, tc=LANES):
    T, Cd = xbc.shape
    K = cw.shape[0]

    def body(p_ref, w_ref, b_ref, o_ref):
        _, pre = _conv_taps(p_ref[...], w_ref, K)
        o_ref[...] = _silu(pre + b_ref[...])

    col = pl.BlockSpec((T, tc), lambda j: (0, j))
    return pl.pallas_call(body, name=name, grid=(Cd // tc,),
                          in_specs=[col, pl.BlockSpec((K, tc), lambda j: (0, j)), pl.BlockSpec((1, tc), lambda j: (0, j))],
                          out_specs=col, out_shape=_sds((T, Cd), F32), compiler_params=_params(("parallel",)))(xbc, cw, cb)


def ssd_conv_bwd(name, xbc, dact, cw, cb, tc=LANES):
    T, Cd = xbc.shape
    K = cw.shape[0]

    def body(p_ref, da_ref, w_ref, b_ref, dp_ref, dw_ref, db_ref):
        taps, pre = _conv_taps(p_ref[...], w_ref, K)
        dpre = da_ref[...] * _dsilu(pre + b_ref[...])
        dp_ref[...] = _conv_bwd(dpre, taps, w_ref, dw_ref, db_ref, K).astype(dp_ref.dtype)

    col = pl.BlockSpec((T, tc), lambda j: (0, j))
    wsp = pl.BlockSpec((K, tc), lambda j: (0, j))
    bsp = pl.BlockSpec((1, tc), lambda j: (0, j))
    return pl.pallas_call(body, name=name, grid=(Cd // tc,), in_specs=[col, col, wsp, bsp], out_specs=[col, wsp, bsp],
                          out_shape=[_sds((T, Cd), BF16), _sds((K, Cd), F32), _sds((1, Cd), F32)],
                          compiler_params=_params(("parallel",)))(xbc, dact, cw, cb)


def _expand_matrix(Hd):
    return (_iota((LANES, Hd), 1) // SSD_HEAD_DIM == _iota((LANES, Hd), 0)).astype(BF16)


def ssd_dt_fwd(name, dtr, bias, Hd, tr=512):
    T = dtr.shape[0]
    tr = min(tr, T)

    def body(d_ref, b_ref, o_ref):
        o_ref[...] = _dot3r(_softplus(d_ref[...] + b_ref[...]), _expand_matrix(Hd))

    return pl.pallas_call(body, name=name, grid=(T // tr,),
                          in_specs=[pl.BlockSpec((tr, LANES), lambda i: (i, 0)), pl.BlockSpec((1, LANES), lambda i: (0, 0))],
                          out_specs=pl.BlockSpec((tr, Hd), lambda i: (i, 0)), out_shape=_sds((T, Hd), F32),
                          compiler_params=_params(("parallel",)))(dtr, bias)


def ssd_dt_bwd(name, dtr, bias, ddtx, tr=512):
    T, Hd = ddtx.shape
    tr = min(tr, T)

    def body(d_ref, b_ref, g_ref, o_ref, db_ref):
        p1, p2, p3 = _split3(g_ref[...])
        em = _expand_matrix(Hd)
        ddt = _dot(p1, em, NT) + _dot(p2, em, NT) + _dot(p3, em, NT)
        draw = ddt * _sigmoid(d_ref[...] + b_ref[...])
        o_ref[...] = draw.astype(o_ref.dtype)
        part = jnp.sum(draw, axis=0, keepdims=True)

        @pl.when(pl.program_id(0) == 0)
        def _():
            db_ref[...] = part

        @pl.when(pl.program_id(0) > 0)
        def _():
            db_ref[...] += part

    row = pl.BlockSpec((tr, LANES), lambda i: (i, 0))
    vec = pl.BlockSpec((1, LANES), lambda i: (0, 0))
    return pl.pallas_call(body, name=name, grid=(T // tr,), in_specs=[row, vec, pl.BlockSpec((tr, Hd), lambda i: (i, 0))],
                          out_specs=[row, vec], out_shape=[_sds((T, LANES), BF16), _sds((1, LANES), F32)],
                          compiler_params=_params(("arbitrary",)))(dtr, bias, ddtx)


def _ssd_head_terms(a2, a2r, half, cb, causal, lane):
    hm = (lane < SSD_HEAD_DIM) if half == 0 else (lane >= SSD_HEAD_DIM)
    ccol = jnp.where(hm, a2, a2r)
    lm = jnp.exp(jnp.where(causal, ccol - ccol.T, -jnp.inf))
    return hm, lm, cb * lm


def ssd_core_fwd(name, act, dtx, alx, dx, G):
    T, Hd = dtx.shape
    Q, N = CHUNK, SSD_STATE
    gw = Hd // G
    nc = T // Q
    nx = Hd // N

    def body(xs_ref, b_ref, c_ref, dt_ref, al_ref, d_ref, y_ref, ss_ref, st_scr):
        @pl.when(pl.program_id(1) == 0)
        def _():
            st_scr[...] = jnp.zeros_like(st_scr)

        xs, dtv = xs_ref[...], dt_ref[...]
        Bb, Cb = b_ref[...].astype(BF16), c_ref[...].astype(BF16)
        dA = dtv * (-jnp.exp(al_ref[...]))
        a = _dot3l(_tri(Q, "le").astype(BF16), dA)
        a_last = jnp.sum(dA, axis=0, keepdims=True)
        xdt = xs * dtv
        cbm = _dot(Cb, Bb, NT)
        sprev = st_scr[...]
        ss_ref[...] = sprev
        causal, lane = _tri(Q, "le"), _iota((Q, LANES), 1)
        y_rest = _dot(Cb, sprev.astype(BF16)) * jnp.exp(a) + xs * d_ref[...]
        for q in range(gw // LANES):
            sl = slice(q * LANES, (q + 1) * LANES)
            a2, x2 = a[:, sl], xdt[:, sl]
            a2r = pltpu.roll(a2, SSD_HEAD_DIM, 1)
            acc = y_rest[:, sl]
            for half in (0, 1):
                hm, _, gm = _ssd_head_terms(a2, a2r, half, cbm, causal, lane)
                acc = acc + _dot(gm.astype(BF16), jnp.where(hm, x2, 0.0).astype(BF16))
            y_ref[:, sl] = acc
        w = jnp.exp(a_last - a)
        st_scr[...] = sprev * jnp.exp(a_last) + _dot(Bb, (w * xdt).astype(BF16), TN)

    xsp = pl.BlockSpec((Q, gw), lambda g, c: (c, g))
    vec = pl.BlockSpec((1, gw), lambda g, c: (0, g))
    return pl.pallas_call(
        body, name=name, grid=(G, nc),
        in_specs=[xsp, pl.BlockSpec((Q, N), lambda g, c: (c, nx + g)), pl.BlockSpec((Q, N), lambda g, c: (c, nx + G + g)),
                  xsp, vec, vec],
        out_specs=[xsp, pl.BlockSpec((None, N, gw), lambda g, c: (c, 0, g))],
        out_shape=[_sds((T, Hd), F32), _sds((nc, N, Hd), F32)],
        scratch_shapes=[pltpu.VMEM((N, gw), F32)],
        compiler_params=_params(("parallel", "arbitrary")))(act, act, act, dtx, alx, dx)


def ssd_core_bwd(name, act, dtx, alx, dx, ssave, dy, G):
    T, Hd = dtx.shape
    Q, N = CHUNK, SSD_STATE
    gw = Hd // G
    nc = T // Q
    nx = Hd // N

    def body(xs_ref, b_ref, c_ref, dt_ref, al_ref, d_ref, ss_ref, dy_ref,
             dxs_ref, db_ref, dc_ref, ddt_ref, dal_ref, dd_ref, ds_scr, dxdt_scr, da_scr):
        @pl.when(pl.program_id(1) == 0)
        def _():
            ds_scr[...] = jnp.zeros_like(ds_scr)
            dal_ref[...] = jnp.zeros_like(dal_ref)
            dd_ref[...] = jnp.zeros_like(dd_ref)

        xs, dtv, dy_ = xs_ref[...], dt_ref[...], dy_ref[...]
        Bb, Cb = b_ref[...].astype(BF16), c_ref[...].astype(BF16)
        Ax = -jnp.exp(al_ref[...])
        dA = dtv * Ax
        a = _dot3l(_tri(Q, "le").astype(BF16), dA)
        a_last = jnp.sum(dA, axis=0, keepdims=True)
        xdt = xs * dtv
        e, w, eal = jnp.exp(a), jnp.exp(a_last - a), jnp.exp(a_last)
        sprev, dsn = ss_ref[...], ds_scr[...]
        sprevb, dsnb = sprev.astype(BF16), dsn.astype(BF16)

        dd_ref[...] += jnp.sum(dy_ * xs, axis=0, keepdims=True)
        dmb = (dy_ * e).astype(BF16)
        dC = _dot(dmb, sprevb, NT)
        ds_scr[...] = _dot(Cb, dmb, TN) + dsn * eal
        dalast = jnp.sum(dsn * sprev, axis=0, keepdims=True) * eal
        dB = _dot((w * xdt).astype(BF16), dsnb, NT)
        dwx = _dot(Bb, dsnb)
        dww = dwx * xdt * w
        dalast = dalast + jnp.sum(dww, axis=0, keepdims=True)
        da_scr[...] = dy_ * _dot(Cb, sprevb) * e - dww
        dxdt_scr[...] = w * dwx
        cbm = _dot(Cb, Bb, NT)
        dcb = jnp.zeros((Q, Q), F32)
        causal, lane = _tri(Q, "le"), _iota((Q, LANES), 1)
        for q in range(gw // LANES):
            sl = slice(q * LANES, (q + 1) * LANES)
            a2, x2, dy2 = a[:, sl], xdt[:, sl], dy_[:, sl]
            a2r = pltpu.roll(a2, SSD_HEAD_DIM, 1)
            for half in (0, 1):
                hm, lm, gm = _ssd_head_terms(a2, a2r, half, cbm, causal, lane)
                dyh = jnp.where(hm, dy2, 0.0).astype(BF16)
                dg = _dot(dyh, jnp.where(hm, x2, 0.0).astype(BF16), NT)
                dxdt_scr[:, sl] += _dot(gm.astype(BF16), dyh, TN)
                dcb = dcb + dg * lm
                dseg = dg * gm
                v = jnp.sum(dseg, axis=1, keepdims=True) - jnp.sum(dseg.T, axis=1, keepdims=True)
                da_scr[:, sl] += jnp.where(hm, v, 0.0) * (1.0 / SSD_HEAD_DIM)
        dcbb = dcb.astype(BF16)
        dc_ref[...] = dC + _dot(dcbb, Bb)
        db_ref[...] = dB + _dot(dcbb, Cb, TN)
        dxdt = dxdt_scr[...]
        dxs_ref[...] = dy_ * d_ref[...] + dxdt * dtv
        da = da_scr[...] + jnp.where(_iota((Q, gw), 0) == Q - 1, dalast, 0.0)
        dda = _dot3l(_tri(Q, "ge").astype(BF16), da)
        ddt_ref[...] = dxdt * xs + dda * Ax
        dal_ref[...] += jnp.sum(dda * dtv, axis=0, keepdims=True) * Ax

    rc = lambda c: nc - 1 - c
    xsp = pl.BlockSpec((Q, gw), lambda g, c: (rc(c), g))
    bsp = pl.BlockSpec((Q, N), lambda g, c: (rc(c), nx + g))
    csp = pl.BlockSpec((Q, N), lambda g, c: (rc(c), nx + G + g))
    gsp = pl.BlockSpec((Q, N), lambda g, c: (rc(c), g))
    vec = pl.BlockSpec((1, gw), lambda g, c: (0, g))
    return pl.pallas_call(
        body, name=name, grid=(G, nc),
        in_specs=[xsp, bsp, csp, xsp, vec, vec, pl.BlockSpec((None, N, gw), lambda g, c: (rc(c), 0, g)), xsp],
        out_specs=[xsp, gsp, gsp, xsp, vec, vec],
        out_shape=[_sds((T, Hd), F32), _sds((T, G * N), F32), _sds((T, G * N), F32), _sds((T, Hd), F32),
                   _sds((1, Hd), F32), _sds((1, Hd), F32)],
        scratch_shapes=[pltpu.VMEM((N, gw), F32), pltpu.VMEM((Q, gw), F32), pltpu.VMEM((Q, gw), F32)],
        compiler_params=_params(("parallel", "arbitrary")))(act, act, act, dtx, alx, dx, ssave, dy)


def ssd_gate_fwd(name, y, z, ng, tr=256):
    T, Hd = y.shape
    tr = min(tr, T)

    def body(y_ref, z_ref, g_ref, o_ref):
        o_ref[...] = _rms(y_ref[...] * _silu(z_ref[...]), g_ref[...]).astype(o_ref.dtype)

    row = pl.BlockSpec((tr, Hd), lambda i: (i, 0))
    return pl.pallas_call(body, name=name, grid=(T // tr,), in_specs=[row, row, pl.BlockSpec((1, Hd), lambda i: (0, 0))],
                          out_specs=row, out_shape=_sds((T, Hd), BF16), compiler_params=_params(("parallel",)))(y, z, ng)


def ssd_gate_bwd(name, y, z, ng, dyn, tr=128):
    T, Hd = y.shape
    tr = min(tr, T)

    def body(y_ref, z_ref, g_ref, dn_ref, dy_ref, dz_ref, dg_ref):
        y_, z_, dn = y_ref[...], z_ref[...], dn_ref[...]
        y2 = y_ * _silu(z_)
        r = lax.rsqrt(jnp.mean(y2 * y2, axis=-1, keepdims=True) + EPS)
        xh = y2 * r
        dxh = dn * g_ref[...]
        dy2 = r * (dxh - xh * jnp.mean(dxh * xh, axis=-1, keepdims=True))
        dy_ref[...] = dy2 * _silu(z_)
        dz_ref[...] = (dy2 * y_ * _dsilu(z_)).astype(dz_ref.dtype)
        part = jnp.sum(dn * xh, axis=0, keepdims=True)

        @pl.when(pl.program_id(0) == 0)
        def _():
            dg_ref[...] = part

        @pl.when(pl.program_id(0) > 0)
        def _():
            dg_ref[...] += part

    row = pl.BlockSpec((tr, Hd), lambda i: (i, 0))
    vec = pl.BlockSpec((1, Hd), lambda i: (0, 0))
    return pl.pallas_call(body, name=name, grid=(T // tr,), in_specs=[row, row, vec, row], out_specs=[row, row, vec],
                          out_shape=[_sds((T, Hd), F32), _sds((T, Hd), BF16), _sds((1, Hd), F32)],
                          compiler_params=_params(("arbitrary",)))(y, z, ng, dyn)


def ssd_mixer_fwd(hn, wz, wxbc, wdt, cw, cb, dtb, alx, dx, ng, w_out, G):
    Hd = wz.N
    z = mm_fwd("ssd_z", hn, wz, F32)
    xbc = mm_fwd("ssd_xbc", hn, wxbc, F32)
    dtr = mm_fwd("ssd_dt", hn, wdt, F32)
    act = ssd_conv_fwd("ssd_conv", xbc, cw, cb)
    dtx = ssd_dt_fwd("ssd_dtx", dtr, dtb, Hd)
    y, ssave = ssd_core_fwd("ssd_core", act, dtx, alx, dx, G)
    yn = ssd_gate_fwd("ssd_gate", y, z, ng)
    m = mm_fwd("ssd_out", yn, w_out, F32, tn=1024)
    return m, (z, xbc, dtr, act, dtx, y, ssave, yn)


def ssd_mixer_bwd(hn, saved, dm, wz, wxbc, wdt, cw, cb, dtb, alx, dx, ng, w_out, G):
    z, xbc, dtr, act, dtx, y, ssave, yn = saved
    dyn = mm_dx("ssd_out_dx", dm, w_out, F32, tn=w_out.R)
    dw_out = mm_dw("ssd_out_dw", yn, dm, BF16, tn=1024)
    dy, dz, dng = ssd_gate_bwd("ssd_gate_bwd", y, z, ng, dyn)
    dxs, dB, dC, ddtx, dalx, ddx = ssd_core_bwd("ssd_core_bwd", act, dtx, alx, dx, ssave, dy, G)
    dxbc, dcw, dcb = ssd_conv_bwd("ssd_conv_bwd", xbc, jnp.concatenate([dxs, dB, dC], axis=1), cw, cb)
    ddtr, ddtb = ssd_dt_bwd("ssd_dtx_bwd", dtr, dtb, ddtx)
    dwz = mm_dw("ssd_z_dw", hn, dz, BF16)
    dwxbc = mm_dw("ssd_xbc_dw", hn, dxbc, BF16)
    dwdt = mm_dw("ssd_dt_dw", hn, ddtr, BF16)
    dhn = mm_dx("ssd_z_dx", dz, wz, F32)
    dhn = mm_dx("ssd_xbc_dx", dxbc, wxbc, F32, add=dhn)
    dhn = mm_dx("ssd_dt_dx", ddtr, wdt, F32, add=dhn)
    grads = dict(ssd_w_out=dw_out, ssd_wz=dwz, ssd_wxbc=dwxbc, ssd_wdt=dwdt, ssd_conv_w=dcw, ssd_conv_b=dcb,
                 ssd_dt_bias=ddtb, ssd_alx=dalx, ssd_dx=ddx, ssd_norm=dng)
    return dhn, grads


def adamw(name, w, m, v, ga, gb=None):
    Rr, C = w.shape
    tr = 8
    while Rr % (tr * 2) == 0 and tr * 2 * C * 4 <= (1 << 20):
        tr *= 2
    if Rr % tr:
        tr = Rr
    two = gb is not None
    c1 = 1.0 - ADAM_B1 ** ADAM_STEP
    c2 = 1.0 - ADAM_B2 ** ADAM_STEP

    def body(*refs):
        if two:
            w_ref, m_ref, v_ref, a_ref, b_ref, g_ref, d_ref, mo_ref, vo_ref = refs
            g = a_ref[...] + b_ref[...]
        else:
            w_ref, m_ref, v_ref, a_ref, g_ref, d_ref, mo_ref, vo_ref = refs
            g = a_ref[...]
        m2 = ADAM_B1 * m_ref[...] + (1.0 - ADAM_B1) * g
        v2 = ADAM_B2 * v_ref[...] + (1.0 - ADAM_B2) * (g * g)
        g_ref[...] = g
        mo_ref[...] = m2
        vo_ref[...] = v2
        d_ref[...] = -ADAM_LR * ((m2 / c1) / (jnp.sqrt(v2 / c2) + ADAM_EPS) + ADAM_WD * w_ref[...])

    blk = pl.BlockSpec((tr, C), lambda i: (i, 0))
    n_in = 5 if two else 4
    args = (w, m, v, ga) + ((gb,) if two else ())
    return pl.pallas_call(body, name=name, grid=(Rr // tr,), in_specs=[blk] * n_in, out_specs=[blk] * 4,
                          out_shape=[_sds((Rr, C), F32)] * 4, compiler_params=_params(("parallel",)))(*args)


def adamw_rows(name, w, m, v, ga, gb, row0, prev):
    Rr, C = ga.shape
    tr = 8
    while Rr % (tr * 2) == 0 and row0 % (tr * 2) == 0 and tr * 2 * C * 4 <= (1 << 20):
        tr *= 2
    assert Rr % tr == 0 and row0 % tr == 0, (Rr, row0, tr)
    off = row0 // tr
    c1 = 1.0 - ADAM_B1 ** ADAM_STEP
    c2 = 1.0 - ADAM_B2 ** ADAM_STEP

    def body(w_ref, m_ref, v_ref, a_ref, b_ref, *rest):
        g_ref, d_ref, mo_ref, vo_ref = rest[-4:]
        g = a_ref[...] + b_ref[...]
        m2 = ADAM_B1 * m_ref[...] + (1.0 - ADAM_B1) * g
        v2 = ADAM_B2 * v_ref[...] + (1.0 - ADAM_B2) * (g * g)
        g_ref[...] = g
        mo_ref[...] = m2
        vo_ref[...] = v2
        d_ref[...] = -ADAM_LR * ((m2 / c1) / (jnp.sqrt(v2 / c2) + ADAM_EPS) + ADAM_WD * w_ref[...])

    rows = pl.BlockSpec((tr, C), lambda i: (off + i, 0))
    part = pl.BlockSpec((tr, C), lambda i: (i, 0))
    carried = [] if prev is None else list(prev)
    return pl.pallas_call(body, name=name, grid=(Rr // tr,), in_specs=[rows] * 3 + [part] * 2 + [ANY] * len(carried),
                          out_specs=[rows] * 4, out_shape=[_sds(w.shape, F32)] * 4,
                          input_output_aliases={5 + k: k for k in range(len(carried))},
                          compiler_params=_params(("parallel",)))(w, m, v, ga, gb, *carried)


def _mesh_pos():
    return lax.axis_index("x"), lax.axis_index("y"), lax.axis_index("c")


def _peer_chips(x, y):
    return [(1 - x, y), (x, 1 - y), (1 - x, 1 - y)]


def sum_slots(name, own, r):
    _, Rr, C = r.shape
    tr = 8
    while Rr % (tr * 2) == 0 and tr * 2 * C * 4 <= (1 << 20):
        tr *= 2

    def body(o_in, r_ref, o_ref):
        o_ref[...] = ((o_in[...].astype(F32) + r_ref[0].astype(F32)) + r_ref[1].astype(F32)) + r_ref[2].astype(F32)

    blk = pl.BlockSpec((tr, C), lambda i: (i, 0))
    return pl.pallas_call(body, name=name, grid=(Rr // tr,), in_specs=[blk, pl.BlockSpec((3, tr, C), lambda i: (0, i, 0))],
                          out_specs=blk, out_shape=_sds((Rr, C), F32), compiler_params=_params(("parallel",)))(own, r)


HBM = pl.BlockSpec(memory_space=pltpu.HBM)
SEM = pl.BlockSpec(memory_space=pltpu.SEMAPHORE)
EFFECT = pltpu.SideEffectType.DATAFLOW_SIDE_EFFECTING


def _xchg_copy(mode, side, src, land, send, recv, t, j, peer, me, c):
    px, py = peer
    pidx = 2 * px + py
    if mode == "gather":
        s, dst = src, land.at[me if side == "out" else pidx]
    else:
        s, dst = src.at[pidx], land.at[j]
    k = 3 * t + j
    return pltpu.make_async_remote_copy(src_ref=s, dst_ref=dst, send_sem=send.at[k], recv_sem=recv.at[k],
                                        device_id=(px, py, c), device_id_type=MESH)


def xchg_start(name, mode, srcs, lands):
    counts = [len(g) for g in srcs]
    ng = len(counts)
    fs = [a for g in srcs for a in g]
    fl = [a for g in lands for a in g]
    n = len(fs)

    def body(*refs):
        src, land = refs[:n], refs[n:2 * n]
        send, recv = refs[2 * n:2 * n + ng], refs[2 * n + ng:2 * n + 2 * ng]
        token = refs[-1]
        x, y, c = _mesh_pos()
        me = 2 * x + y
        k = 0
        for gi in range(ng):
            for t in range(counts[gi]):
                for j, peer in enumerate(_peer_chips(x, y)):
                    _xchg_copy(mode, "out", src[k], land[k], send[gi], recv[gi], t, j, peer, me, c).start()
                k += 1
        token[...] = jnp.zeros_like(token)

    sems = tuple(pltpu.SemaphoreType.DMA((3 * cnt,)) for cnt in counts)
    thru = tuple(pltpu.HBM(a.shape, a.dtype) for a in fs + fl)
    out = pl.pallas_call(
        body, name=name, in_specs=[HBM] * (2 * n),
        out_specs=(SEM,) * (2 * ng) + (HBM,) * (2 * n) + (pl.BlockSpec(memory_space=pltpu.VMEM),),
        out_shape=sems + sems + thru + (_sds((8, LANES), F32),),
        input_output_aliases={i: 2 * ng + i for i in range(2 * n)},
        compiler_params=pltpu.CompilerParams(has_side_effects=EFFECT),
    )(*[pltpu.with_memory_space_constraint(a, pltpu.HBM) for a in fs + fl])
    send, recv = out[:ng], out[ng:2 * ng]
    thru_s, thru_l = out[2 * ng:2 * ng + n], out[2 * ng + n:2 * ng + 2 * n]
    groups, k = [], 0
    for gi, cnt in enumerate(counts):
        groups.append(dict(send=send[gi], recv=recv[gi], src=list(thru_s[k:k + cnt]), land=list(thru_l[k:k + cnt])))
        k += cnt
    return groups, out[-1]


def xchg_wait(name, mode, grp, after):
    src, land = grp["src"], grp["land"]
    n = len(src)

    def body(*refs):
        s_ref, l_ref = refs[:n], refs[n:2 * n]
        send, recv = refs[2 * n], refs[2 * n + 1]
        x, y, c = _mesh_pos()
        me = 2 * x + y
        for t in range(n):
            for j, peer in enumerate(_peer_chips(x, y)):
                _xchg_copy(mode, "out", s_ref[t], l_ref[t], send, recv, t, j, peer, me, c).wait_send()
                _xchg_copy(mode, "in", s_ref[t], l_ref[t], send, recv, t, j, peer, me, c).wait_recv()

    res = pl.pallas_call(
        body, name=name, in_specs=[HBM] * (2 * n) + [SEM, SEM, ANY],
        out_specs=(HBM,) * (2 * n), out_shape=tuple(pltpu.HBM(a.shape, a.dtype) for a in src + land),
        input_output_aliases={i: i for i in range(2 * n)},
        compiler_params=pltpu.CompilerParams(has_side_effects=EFFECT),
    )(*src, *land, grp["send"], grp["recv"], after)
    return list(res[:n]), list(res[n:])


def swap_with_sibling(name, tensors):
    n = len(tensors)

    def body(*refs):
        ins, outs = refs[:n], refs[n:2 * n]
        send_sems, recv_sems = refs[2 * n:]
        x, y, c = _mesh_pos()
        cps = []
        for t in range(n):
            cp = pltpu.make_async_remote_copy(src_ref=ins[t], dst_ref=outs[t], send_sem=send_sems.at[t], recv_sem=recv_sems.at[t],
                                              device_id=(x, y, 1 - c), device_id_type=MESH)
            cp.start()
            cps.append(cp)
        for cp in cps:
            cp.wait()

    return pl.pallas_call(
        body, name=name, in_specs=[ANY] * n, out_specs=[ANY] * n, out_shape=[_sds(t.shape, t.dtype) for t in tensors],
        scratch_shapes=[pltpu.SemaphoreType.DMA((n,)), pltpu.SemaphoreType.DMA((n,))],
    )(*tensors)


def all_reduce_small(name, v, after):
    Rr, C = v.shape
    nd = 8

    def body(v_ref, after_ref, o_ref, gath, send_sems, recv_sems):
        x, y, c = _mesh_pos()
        me = 4 * x + 2 * y + c
        cps = []
        for d in range(1, nd):
            bx, by, bc = (d >> 2) & 1, (d >> 1) & 1, d & 1
            tgt = (1 - x if bx else x, 1 - y if by else y, 1 - c if bc else c)
            cp = pltpu.make_async_remote_copy(src_ref=v_ref, dst_ref=gath.at[me], send_sem=send_sems.at[d - 1],
                                              recv_sem=recv_sems.at[d - 1], device_id=tgt, device_id_type=MESH)
            cp.start()
            cps.append((cp, tgt))
        gath[me] = v_ref[...]
        for d in range(1, nd):
            _, (tx, ty, tc) = cps[d - 1]
            pltpu.make_async_remote_copy(src_ref=v_ref, dst_ref=gath.at[4 * tx + 2 * ty + tc], send_sem=send_sems.at[d - 1],
                                         recv_sem=recv_sems.at[d - 1], device_id=(tx, ty, tc), device_id_type=MESH).wait_recv()
        acc = gath[0]
        for d in range(1, nd):
            acc = acc + gath[d]
        o_ref[...] = acc
        for cp, _ in cps:
            cp.wait_send()

    vm = pl.BlockSpec(memory_space=pltpu.VMEM)
    return pl.pallas_call(
        body, name=name, in_specs=[vm, ANY], out_specs=vm, out_shape=_sds((Rr, C), F32),
        scratch_shapes=[pltpu.VMEM((nd, Rr, C), F32), pltpu.SemaphoreType.DMA((nd - 1,)), pltpu.SemaphoreType.DMA((nd - 1,))],
        compiler_params=pltpu.CompilerParams(vmem_limit_bytes=VMEM_LIMIT),
    )(v, after)


def _pack(arrs):
    flat = jnp.concatenate([a.reshape(-1) for a in arrs])
    pad = (-flat.shape[0]) % (8 * LANES)
    return jnp.pad(flat, (0, pad)).reshape(-1, LANES)


def _unpack(buf, shapes):
    flat = buf.reshape(-1)
    out, off = [], 0
    for s in shapes:
        n = math.prod(s)
        out.append(flat[off:off + n].reshape(s))
        off += n
    return out


WEIGHTS = ["ln_mix_pre", "ln_mix_post", "ln_mem", "ln_xa_pre", "ln_xa_post", "ln_ffn_pre", "ln_ffn_post", "xa_wq", "xa_wkv",
           "xa_wo", "ffn_w_in", "ffn_conv_w", "ffn_conv_b", "ffn_w_out", "ssd_w_in", "ssd_conv_w", "ssd_conv_b", "ssd_dt_bias",
           "ssd_a_log", "ssd_d", "ssd_norm", "ssd_w_out", "sg_w_in", "sg_v_norm_g", "sg_v_norm_b", "sg_w_spatial",
           "sg_b_spatial", "sg_w_out", "sb_w_qkv", "sb_w_out"]
BIG = {"xa_wq": "rows", "xa_wkv": "rows", "xa_wo": "cols", "ffn_w_in": "cols", "ffn_w_out": "rows", "ssd_w_in": "cols",
       "ssd_w_out": "rows", "sg_w_in": "cols", "sg_w_out": "rows", "sb_w_qkv": "cols", "sb_w_out": "rows"}
SHARDED_SMALL = {"ffn_conv_w": 2, "ssd_conv_w": 2, "ssd_conv_b": 1, "ssd_norm": 1}
SMALL = [n for n in WEIGHTS if n not in BIG]
N_MIXERS = 3
HEAD = 128


def _unshard(a, axis):
    a = jnp.moveaxis(a, 0, axis)
    s = a.shape
    return a.reshape(s[:axis] + (s[axis] * s[axis + 1],) + s[axis + 2:])


def _step(p):
    x, mem, tgt = p["x"][0], p["mem"][0], p["loss_target"][0]
    T, D = x.shape
    depth = p["ln_mix_pre"].shape[0]
    S = N_CHIPS

    me = 2 * lax.axis_index("x") + lax.axis_index("y")
    Hd, Cd = S * p["ssd_norm"].shape[1], S * p["ssd_conv_b"].shape[1]
    nh = p["ssd_dt_bias"].shape[1]
    G = (Cd - Hd) // (2 * SSD_STATE)
    xa_heads = p["xa_wo"].shape[1] // HEAD
    sb_heads = D // HEAD

    def layer_parts(i):
        kind, j = i % N_MIXERS, i // N_MIXERS
        first = {0: [("ssd_w_in", j), ("ssd_conv_w", j), ("ssd_conv_b", j), ("ssd_norm", j)], 1: [("sg_w_in", j)],
                 2: [("sb_w_qkv", j)]}[kind]
        w_out = {0: "ssd_w_out", 1: "sg_w_out", 2: "sb_w_out"}[kind]
        return [first + [("ffn_conv_w", i)], [(w_out, j), ("xa_wq", i), ("xa_wkv", i), ("xa_wo", i)],
                [("ffn_w_in", i), ("ffn_w_out", i)]]

    srcs, lands = [], []
    for i in range(depth):
        for part in layer_parts(i):
            s_i, l_i = [], []
            for n, k in part:
                a = p[n][k].astype(BF16) if n in BIG else p[n][k]
                a = a.reshape((1,) * (2 - a.ndim) + a.shape)
                s_i.append(a)
                l_i.append(lax.dynamic_update_index_in_dim(lax.empty((S,) + a.shape, a.dtype), a, me, 0))
            srcs.append(s_i)
            lands.append(l_i)
    gather_groups, gather_token = xchg_start("gather_start", "gather", srcs, lands)

    class Gathered:
        def __init__(self, i):
            self.i, self.parts, self.got = i, layer_parts(i), {}

        def get(self, key, after):
            if key not in self.got:
                k = next(idx for idx, part in enumerate(self.parts) if key in part)
                _, zones = xchg_wait("gather_wait_%d_%d" % (self.i, k), "gather", gather_groups[3 * self.i + k], after)
                self.got.update(zip(self.parts[k], zones))
            return self.got[key]

    def layer_args(i, gz, x_in):
        kind, j = i % N_MIXERS, i // N_MIXERS
        w_of = lambda n, k: W(BIG[n], lambda operand: gz.get((n, k), operand)[:, None], 0, shape=(S, 1) + p[n].shape[1:])
        now = lambda n, k: gz.get((n, k), x_in)
        a = dict(xa=(w_of("xa_wq", i), w_of("xa_wkv", i), w_of("xa_wo", i), xa_heads),
                 ffn=(w_of("ffn_w_in", i), _unshard(now("ffn_conv_w", i), 1), p["ffn_conv_b"][i:i + 1], w_of("ffn_w_out", i)))
        if kind == 0:
            w_in = _unshard(now("ssd_w_in", j), 1)[None]
            a["mix"] = (W("full", w_in[:, :, :Hd], 0), W("full", w_in[:, :, Hd:Hd + Cd], 0),
                        W("full", jnp.pad(w_in[:, :, Hd + Cd:], ((0, 0), (0, 0), (0, LANES - nh))), 0),
                        _unshard(now("ssd_conv_w", j), 1), _unshard(now("ssd_conv_b", j), 1),
                        jnp.pad(p["ssd_dt_bias"][j], (0, LANES - nh))[None], jnp.repeat(p["ssd_a_log"][j], SSD_HEAD_DIM)[None],
                        jnp.repeat(p["ssd_d"][j], SSD_HEAD_DIM)[None], _unshard(now("ssd_norm", j), 1), w_of("ssd_w_out", j), G)
        elif kind == 1:
            a["mix"] = (w_of("sg_w_in", j), p["sg_v_norm_g"][j:j + 1], p["sg_v_norm_b"][j:j + 1], p["sg_w_spatial"][j],
                        w_of("sg_w_out", j))
        else:
            a["mix"] = (w_of("sb_w_qkv", j), w_of("sb_w_out", j), sb_heads)
        return a

    ln = lambda n, i: p[n][i:i + 1]

    h = rms_fwd("rms_first", x, ln("ln_mix_pre", 0), after=gather_token)
    saved, largs = [], []
    for i in range(depth):
        kind, j = i % N_MIXERS, i // N_MIXERS
        la = layer_args(i, Gathered(i), x)
        largs.append(la)
        if kind == 0:
            m, ms = ssd_mixer_fwd(h, *la["mix"])
        elif kind == 1:
            m, ms = sg_mixer_fwd(h, *la["mix"][:4], p["sg_b_spatial"][j], la["mix"][4])
        else:
            m, ms = sb_mixer_fwd(h, *la["mix"])
        x1, hq = resid_norm("resid_norm", x, m, ln("ln_mix_post", i), ln("ln_xa_pre", i))
        mem_n = rms_fwd("rms_mem", mem, ln("ln_mem", i))
        c, cs = xa_fwd(hq, mem_n, *la["xa"])
        x2, hf = resid_norm("resid_norm", x1, c, ln("ln_xa_post", i), ln("ln_ffn_pre", i))
        f, fs = ffn_fwd(hf, *la["ffn"])
        x3, hn = resid_norm("resid_norm", x2, f, ln("ln_ffn_post", i), ln("ln_mix_pre", i + 1) if i + 1 < depth else None)
        saved.append(dict(x=x, h=h, m=m, ms=ms, x1=x1, hq=hq, mem_n=mem_n, c=c, cs=cs, x2=x2, hf=hf, f=f, fs=fs))
        x, h = x3, hn
    loss_tile, dx = loss_fwd_bwd("loss", x, tgt)
    loss = lax.psum(loss_tile[0, 0], ("x", "y", "c"))

    gs = {n: [None] * p[n].shape[0] for n in WEIGHTS}
    scatter_groups = [[] for _ in range(depth)]

    def send_grads(i, part, keys):
        g_src = [gs[n][k] for n, k in keys]
        g_land = [lax.empty((3,) + a.shape[1:], a.dtype) for a in g_src]
        grp, tok = xchg_start("scatter_start_%d_%d" % (i, part), "scatter", [g_src], [g_land])
        scatter_groups[i].append((part, keys, grp[0]))
        return tok

    token = None
    for i in reversed(range(depth)):
        kind, j = i % N_MIXERS, i // N_MIXERS
        s, la = saved[i], largs[i]
        df, gs["ln_ffn_post"][i] = rms_bwd("rms_bwd_post", s["f"], ln("ln_ffn_post", i), dx, None, BF16, after=token)
        dhf, g = ffn_bwd(s["hf"], s["fs"], df, *la["ffn"])
        gs["ffn_w_in"][i], gs["ffn_conv_w"][i], gs["ffn_conv_b"][i] = g["ffn_w_in"], g["ffn_conv_w"], g["ffn_conv_b"]
        gs["ffn_w_out"][i] = g["ffn_w_out"].reshape(S, -1, D)
        dx, gs["ln_ffn_pre"][i] = rms_bwd("rms_bwd_pre", s["x2"], ln("ln_ffn_pre", i), dhf, dx, F32)
        token = send_grads(i, 2, [("ffn_w_in", i), ("ffn_w_out", i)])

        dc, gs["ln_xa_post"][i] = rms_bwd("rms_bwd_post", s["c"], ln("ln_xa_post", i), dx, None, BF16, after=token)
        dhq, dmem_n, g = xa_bwd(s["hq"], s["mem_n"], s["cs"], dc, *la["xa"])
        gs["xa_wq"][i] = g["xa_wq"].reshape(S, D // S, -1)
        gs["xa_wkv"][i] = g["xa_wkv"].reshape(S, D // S, -1)
        gs["xa_wo"][i] = g["xa_wo"]
        _, gs["ln_mem"][i] = rms_bwd("rms_bwd_mem", mem, ln("ln_mem", i), dmem_n, None, BF16)
        dx, gs["ln_xa_pre"][i] = rms_bwd("rms_bwd_pre", s["x1"], ln("ln_xa_pre", i), dhq, dx, F32)
        token = send_grads(i, 1, [("xa_wq", i), ("xa_wkv", i), ("xa_wo", i)])

        dm, gs["ln_mix_post"][i] = rms_bwd("rms_bwd_post", s["m"], ln("ln_mix_post", i), dx, None, BF16, after=token)
        if kind == 0:
            dhn, g = ssd_mixer_bwd(s["h"], s["ms"], dm, *la["mix"])
            full = jnp.concatenate([g["ssd_wz"], g["ssd_wxbc"], g["ssd_wdt"][:, :nh]], axis=1)
            gs["ssd_w_in"][j] = full.reshape(D, S, -1).transpose(1, 0, 2)
            gs["ssd_w_out"][j] = g["ssd_w_out"].reshape(S, Hd // S, D)
            gs["ssd_conv_w"][j], gs["ssd_conv_b"][j], gs["ssd_norm"][j] = g["ssd_conv_w"], g["ssd_conv_b"], g["ssd_norm"]
            gs["ssd_dt_bias"][j] = g["ssd_dt_bias"][:, :nh]
            gs["ssd_a_log"][j] = g["ssd_alx"].reshape(nh, SSD_HEAD_DIM).sum(-1)[None]
            gs["ssd_d"][j] = g["ssd_dx"].reshape(nh, SSD_HEAD_DIM).sum(-1)[None]
        elif kind == 1:
            dhn, g = sg_mixer_bwd(s["h"], s["ms"], dm, *la["mix"])
            gs["sg_w_in"][j] = g["sg_w_in"]
            gs["sg_w_out"][j] = g["sg_w_out"].reshape(S, -1, D)
            for n in ("sg_v_norm_g", "sg_v_norm_b", "sg_w_spatial", "sg_b_spatial"):
                gs[n][j] = g[n]
        else:
            dhn, g = sb_mixer_bwd(s["h"], s["ms"], dm, *la["mix"])
            gs["sb_w_qkv"][j] = g["sb_w_qkv"]
            gs["sb_w_out"][j] = g["sb_w_out"].reshape(S, -1, D)
        dx, gs["ln_mix_pre"][i] = rms_bwd("rms_bwd_pre", s["x"], ln("ln_mix_pre", i), dhn, dx, F32)

        mix_w = {0: ("ssd_w_in", "ssd_w_out"), 1: ("sg_w_in", "sg_w_out"), 2: ("sb_w_qkv", "sb_w_out")}[kind]
        token = send_grads(i, 0, [(n, j) for n in mix_w])

    out, running, behind = {}, {n: None for n in BIG}, token
    for i in reversed(range(depth)):
        keys, qs = [], []
        for part, part_keys, grp in scatter_groups[i]:
            sent, got = xchg_wait("scatter_wait_%d_%d" % (i, part), "scatter", grp, behind if i == 0 else token)
            for (n, k), own, r in zip(part_keys, sent, got):
                mine = lax.dynamic_index_in_dim(own, me, 0, keepdims=False)
                keys.append((n, k))
                qs.append(sum_slots("sum_grad_slots", mine.reshape(-1, mine.shape[-1]), r.reshape(3, -1, r.shape[-1])))
        sib = swap_with_sibling("swap_grads_%d" % i, qs)
        for (n, k), q, q2 in zip(keys, qs, sib):
            two_d = lambda a: a.reshape(-1, a.shape[-1])
            running[n] = adamw_rows("adamw_big", two_d(p[n]), two_d(p["m_" + n]), two_d(p["v_" + n]), q, q2,
                                    k * q.shape[0], running[n])
            behind = running[n][0]
    for n in BIG:
        out[n] = [r.reshape(p[n].shape) for r in running[n]]

    stack = lambda n: jnp.stack([a.reshape(p[n].shape[1:]) if n not in SHARDED_SMALL else a.reshape(a.shape[-len(p[n].shape) + 1:])
                                 for a in gs[n]])
    small_full = [stack(n) for n in SMALL]
    red = _unpack(all_reduce_small("reduce_small", _pack(small_full), after=behind), [a.shape for a in small_full])
    small_g = []
    for n, a in zip(SMALL, red):
        if n in SHARDED_SMALL:
            ax = SHARDED_SMALL[n]
            a = lax.dynamic_slice_in_dim(a, me * p[n].shape[ax], p[n].shape[ax], axis=ax)
        small_g.append(a)
    shapes = [p[n].shape for n in SMALL]
    res = adamw("adamw_small", _pack([p[n] for n in SMALL]), _pack([p["m_" + n] for n in SMALL]),
                _pack([p["v_" + n] for n in SMALL]), _pack(small_g))
    for k, r in enumerate(res):
        for n, a in zip(SMALL, _unpack(r, shapes)):
            out.setdefault(n, [None] * 4)[k] = a

    return (loss, dx[None]) + tuple(out[n][k] for k in range(4) for n in WEIGHTS)


def kernel(x, mem, ln_mix_pre, ln_mix_post, ln_mem, ln_xa_pre, ln_xa_post, ln_ffn_pre, ln_ffn_post, xa_wq, xa_wkv, xa_wo, ffn_w_in, ffn_conv_w, ffn_conv_b, ffn_w_out, ssd_w_in, ssd_conv_w, ssd_conv_b, ssd_dt_bias, ssd_a_log, ssd_d, ssd_norm, ssd_w_out, sg_w_in, sg_v_norm_g, sg_v_norm_b, sg_w_spatial, sg_b_spatial, sg_w_out, sb_w_qkv, sb_w_out, loss_target, m_ln_mix_pre, m_ln_mix_post, m_ln_mem, m_ln_xa_pre, m_ln_xa_post, m_ln_ffn_pre, m_ln_ffn_post, m_xa_wq, m_xa_wkv, m_xa_wo, m_ffn_w_in, m_ffn_conv_w, m_ffn_conv_b, m_ffn_w_out, m_ssd_w_in, m_ssd_conv_w, m_ssd_conv_b, m_ssd_dt_bias, m_ssd_a_log, m_ssd_d, m_ssd_norm, m_ssd_w_out, m_sg_w_in, m_sg_v_norm_g, m_sg_v_norm_b, m_sg_w_spatial, m_sg_b_spatial, m_sg_w_out, m_sb_w_qkv, m_sb_w_out, v_ln_mix_pre, v_ln_mix_post, v_ln_mem, v_ln_xa_pre, v_ln_xa_post, v_ln_ffn_pre, v_ln_ffn_post, v_xa_wq, v_xa_wkv, v_xa_wo, v_ffn_w_in, v_ffn_conv_w, v_ffn_conv_b, v_ffn_w_out, v_ssd_w_in, v_ssd_conv_w, v_ssd_conv_b, v_ssd_dt_bias, v_ssd_a_log, v_ssd_d, v_ssd_norm, v_ssd_w_out, v_sg_w_in, v_sg_v_norm_g, v_sg_v_norm_b, v_sg_w_spatial, v_sg_b_spatial, v_sg_w_out, v_sb_w_qkv, v_sb_w_out):
    return _step(dict(locals()))
```

```python
import functools
import math

import jax
import jax.numpy as jnp
from jax import lax
from jax.experimental import pallas as pl
from jax.experimental.pallas import tpu as pltpu

F32 = jnp.float32
BF16 = jnp.bfloat16
EPS = 1e-6
LANES = 128
VMEM_LIMIT = 56 * 1024 * 1024
CHUNK = 128
SSD_HEAD_DIM = 64
SSD_STATE = 128
N_CHIPS = 4
MESH = pl.DeviceIdType.MESH
ANY = pl.BlockSpec(memory_space=pl.ANY)

ADAM_LR, ADAM_B1, ADAM_B2, ADAM_EPS, ADAM_WD, ADAM_STEP = 0.001, 0.9, 0.999, 1e-08, 0.01, 10


def _params(sem):
    return pltpu.CompilerParams(dimension_semantics=sem, vmem_limit_bytes=VMEM_LIMIT)


def _sds(shape, dtype):
    return jax.ShapeDtypeStruct(tuple(shape), dtype)


def _tile(n, pref):
    if n <= pref:
        return n
    t = (pref // LANES) * LANES
    while t > LANES and n % t:
        t -= LANES
    assert n % t == 0, (n, pref)
    return t


def _split3(a):
    a1 = a.astype(BF16)
    r = a - a1.astype(F32)
    a2 = r.astype(BF16)
    a3 = (r - a2.astype(F32)).astype(BF16)
    return a1, a2, a3


def _dot(a, b, dims=(((1,), (0,)), ((), ()))):
    return lax.dot_general(a, b, dims, preferred_element_type=F32)


NN = (((1,), (0,)), ((), ()))
NT = (((1,), (1,)), ((), ()))
TN = (((0,), (0,)), ((), ()))


def _dot3r(a, m):
    p1, p2, p3 = _split3(a)
    return _dot(p1, m) + _dot(p2, m) + _dot(p3, m)


def _dot3l(m, a, dims=NN):
    p1, p2, p3 = _split3(a)
    return _dot(m, p1, dims) + _dot(m, p2, dims) + _dot(m, p3, dims)


def _iota(shape, dim):
    return lax.broadcasted_iota(jnp.int32, shape, dim)


def _tri(n, kind):
    r, c = _iota((n, n), 0), _iota((n, n), 1)
    return {"le": c <= r, "lt": c < r, "ge": c >= r, "gt": c > r}[kind]


def _sigmoid(x):
    return 1.0 / (1.0 + jnp.exp(-x))


def _silu(x):
    return x * _sigmoid(x)


def _dsilu(x):
    s = _sigmoid(x)
    return s * (1.0 + x * (1.0 - s))


_GC = math.sqrt(2.0 / math.pi)


def _gelu(x):
    return 0.5 * x * (1.0 + jnp.tanh(_GC * (x + 0.044715 * x * x * x)))


def _dgelu(x):
    th = jnp.tanh(_GC * (x + 0.044715 * x * x * x))
    return 0.5 * (1.0 + th) + 0.5 * x * (1.0 - th * th) * _GC * (1.0 + 3.0 * 0.044715 * x * x)


def _softplus(x):
    return jnp.maximum(x, 0.0) + jnp.log(1.0 + jnp.exp(-jnp.abs(x)))


def _mm(name, mode, a, b, out_sds, grid, a_spec, b_spec, o_spec, acc_shape, add=None, add_spec=None):
    dims = {"nn": NN, "nt": NT, "tn": TN}[mode]
    nk = grid[2]
    has_add = add is not None

    def body(*refs):
        if has_add:
            a_ref, b_ref, c_ref, o_ref = refs[:4]
        else:
            a_ref, b_ref, o_ref = refs[:3]
            c_ref = None
        part = lax.dot_general(a_ref[...], b_ref[...], dims, preferred_element_type=F32)

        def finish(r):
            if c_ref is not None:
                r = r + c_ref[...].astype(F32)
            o_ref[...] = r.astype(o_ref.dtype)

        if nk == 1:
            finish(part)
        else:
            acc = refs[-1]
            k = pl.program_id(2)

            @pl.when(k == 0)
            def _():
                acc[...] = part

            @pl.when(k > 0)
            def _():
                acc[...] += part

            @pl.when(k == nk - 1)
            def _():
                finish(acc[...])

    in_specs = [a_spec, b_spec] + ([add_spec] if has_add else [])
    args = (a, b) + ((add,) if has_add else ())
    return pl.pallas_call(
        body, name=name, grid=grid, in_specs=in_specs, out_specs=o_spec, out_shape=out_sds,
        scratch_shapes=[pltpu.VMEM(acc_shape, F32)] if nk > 1 else [],
        compiler_params=_params(("parallel", "parallel", "arbitrary")),
    )(*args)


class W:
    def __init__(self, kind, arr, layer, shape=None):
        self.kind, self._arr, self.layer = kind, arr, layer
        shape = arr.shape if shape is None else shape
        if kind == "cols":
            s, _, k, c = shape
            self.K, self.N, self.S, self.C = k, s * c, s, c
        elif kind == "rows":
            s, _, r, n = shape
            self.K, self.N, self.S, self.R = s * r, n, s, r
        else:
            _, k, n = shape
            self.K, self.N = k, n

    def get(self, operand):
        if callable(self._arr):
            self._arr = self._arr(operand)
        return self._arr


def mm_fwd(name, a, w, out_dtype, tm=1024, tn=1536, a_spec=None, out_sds=None, o_spec=None, add=None):
    M = a.shape[0]
    tm = min(tm, M)
    l = w.layer
    if w.kind == "cols":
        tn = _tile(w.C, tn)
        nps = w.C // tn
        tk, nk = w.K, 1
        b_spec = pl.BlockSpec((None, None, tk, tn), lambda i, j, k: (j // nps, l, 0, j % nps))
    elif w.kind == "rows":
        tn = _tile(w.N, tn)
        tk, nk = w.R, w.S
        b_spec = pl.BlockSpec((None, None, tk, tn), lambda i, j, k: (k, l, 0, j))
    else:
        tn = _tile(w.N, tn)
        tk, nk = w.K, 1
        b_spec = pl.BlockSpec((None, tk, tn), lambda i, j, k: (l, 0, j))
    grid = (M // tm, w.N // tn, nk)
    if a_spec is None:
        a_spec = pl.BlockSpec((tm, tk), lambda i, j, k: (i, k))
    if out_sds is None:
        out_sds = _sds((M, w.N), out_dtype)
        o_spec = pl.BlockSpec((tm, tn), lambda i, j, k: (i, j))
    else:
        o_spec = o_spec(tm, tn)
    add_spec = pl.BlockSpec((tm, tn), lambda i, j, k: (i, j)) if add is not None else None
    return _mm(name, "nn", a, w.get(a), out_sds, grid, a_spec, b_spec, o_spec, (tm, tn), add, add_spec)


def mm_dx(name, dy, w, out_dtype, tm=1024, tn=1024, a_spec=None, add=None):
    M = dy.shape[-2]
    tm = min(tm, M)
    l = w.layer
    if w.kind == "cols":
        tn = _tile(w.K, tn)
        tk, nk = w.C, w.S
        b_spec = pl.BlockSpec((None, None, tn, tk), lambda i, j, k: (k, l, j, 0))
    elif w.kind == "rows":
        tn = _tile(w.R, tn)
        npr = w.R // tn
        tk, nk = w.N, 1
        b_spec = pl.BlockSpec((None, None, tn, tk), lambda i, j, k: (j // npr, l, j % npr, 0))
    else:
        tn = _tile(w.K, tn)
        tk, nk = _tile(w.N, 2048), w.N // _tile(w.N, 2048)
        b_spec = pl.BlockSpec((None, tn, tk), lambda i, j, k: (l, j, k))
    grid = (M // tm, w.K // tn, nk)
    if a_spec is None:
        a_spec = pl.BlockSpec((tm, tk), lambda i, j, k: (i, k))
    else:
        a_spec = a_spec(tm, tk)
    out_sds = _sds((M, w.K), out_dtype)
    o_spec = pl.BlockSpec((tm, tn), lambda i, j, k: (i, j))
    add_spec = o_spec if add is not None else None
    return _mm(name, "nt", dy, w.get(dy), out_sds, grid, a_spec, b_spec, o_spec, (tm, tn), add, add_spec)


def mm_dw(name, a, dy, out_dtype, n_shards=None, tm=512, tn=1536, b_spec=None, N=None):
    T, K = a.shape
    N = dy.shape[-1] if N is None else N
    tm = _tile(K, tm)
    if n_shards:
        C = N // n_shards
        tn = _tile(C, tn)
        nps = C // tn
        out_sds = _sds((n_shards, K, C), out_dtype)
        o_spec = pl.BlockSpec((None, tm, tn), lambda i, j, k: (j // nps, i, j % nps))
    else:
        tn = _tile(N, tn)
        out_sds = _sds((K, N), out_dtype)
        o_spec = pl.BlockSpec((tm, tn), lambda i, j, k: (i, j))
    grid = (K // tm, N // tn, 1)
    a_spec = pl.BlockSpec((T, tm), lambda i, j, k: (0, i))
    if b_spec is None:
        b_spec = pl.BlockSpec((T, tn), lambda i, j, k: (0, j))
    else:
        b_spec = b_spec(T, tn)
    return _mm(name, "tn", a, dy, out_sds, grid, a_spec, b_spec, o_spec, (tm, tn))


def _rms(x, g):
    r = lax.rsqrt(jnp.mean(x * x, axis=-1, keepdims=True) + EPS)
    return x * r * g


def rms_fwd(name, x, g, tr=512, after=None):
    T, D = x.shape
    tr = min(tr, T)

    def body(x_ref, g_ref, *rest):
        o_ref = rest[-1]
        o_ref[...] = _rms(x_ref[...], g_ref[...]).astype(o_ref.dtype)

    row = pl.BlockSpec((tr, D), lambda i: (i, 0))
    vec = pl.BlockSpec((1, D), lambda i: (0, 0))
    extra = [] if after is None else [after]
    return pl.pallas_call(body, name=name, grid=(T // tr,), in_specs=[row, vec] + [ANY] * len(extra), out_specs=row,
                          out_shape=_sds((T, D), BF16), compiler_params=_params(("parallel",)))(x, g, *extra)


def resid_norm(name, x, m, g_post, g_next, tr=512):
    T, D = x.shape
    tr = min(tr, T)
    has_next = g_next is not None

    def body(*refs):
        if has_next:
            x_ref, m_ref, gp_ref, gn_ref, xo_ref, h_ref = refs
        else:
            x_ref, m_ref, gp_ref, xo_ref = refs
        xn = x_ref[...] + _rms(m_ref[...], gp_ref[...])
        xo_ref[...] = xn
        if has_next:
            h_ref[...] = _rms(xn, gn_ref[...]).astype(h_ref.dtype)

    row = pl.BlockSpec((tr, D), lambda i: (i, 0))
    vec = pl.BlockSpec((1, D), lambda i: (0, 0))
    ins = [row, row, vec] + ([vec] if has_next else [])
    args = (x, m, g_post) + ((g_next,) if has_next else ())
    outs = [row, row] if has_next else row
    shp = [_sds((T, D), F32), _sds((T, D), BF16)] if has_next else _sds((T, D), F32)
    res = pl.pallas_call(body, name=name, grid=(T // tr,), in_specs=ins, out_specs=outs, out_shape=shp,
                         compiler_params=_params(("parallel",)))(*args)
    return res if has_next else (res, None)


def rms_bwd(name, xin, g, dy, resid, out_dtype, tr=512, after=None):
    T, D = xin.shape
    tr = min(tr, T)
    has_res = resid is not None

    def body(*refs):
        dx_ref, dg_ref = refs[-2:]
        if has_res:
            x_ref, g_ref, dy_ref, r_ref = refs[:4]
        else:
            x_ref, g_ref, dy_ref = refs[:3]
        x = x_ref[...].astype(F32)
        dy_ = dy_ref[...].astype(F32)
        r = lax.rsqrt(jnp.mean(x * x, axis=-1, keepdims=True) + EPS)
        xh = x * r
        dxh = dy_ * g_ref[...]
        dx = r * (dxh - xh * jnp.mean(dxh * xh, axis=-1, keepdims=True))
        if has_res:
            dx = dx + r_ref[...]
        dx_ref[...] = dx.astype(dx_ref.dtype)
        part = jnp.sum(dy_ * xh, axis=0, keepdims=True)

        @pl.when(pl.program_id(0) == 0)
        def _():
            dg_ref[...] = part

        @pl.when(pl.program_id(0) > 0)
        def _():
            dg_ref[...] += part

    row = pl.BlockSpec((tr, D), lambda i: (i, 0))
    vec = pl.BlockSpec((1, D), lambda i: (0, 0))
    ins = [row, vec, row] + ([row] if has_res else []) + ([] if after is None else [ANY])
    args = (xin, g, dy) + ((resid,) if has_res else ()) + (() if after is None else (after,))
    return pl.pallas_call(body, name=name, grid=(T // tr,), in_specs=ins, out_specs=[row, vec],
                          out_shape=[_sds((T, D), out_dtype), _sds((1, D), F32)],
                          compiler_params=_params(("arbitrary",)))(*args)


def loss_fwd_bwd(name, y, tgt, tr=512):
    T, D = y.shape
    tr = min(tr, T)

    def body(y_ref, t_ref, l_ref, d_ref):
        e = y_ref[...] - t_ref[...]
        d_ref[...] = e * (1.0 / D)
        part = 0.5 * jnp.sum(jnp.mean(e * e, axis=-1, keepdims=True), axis=0, keepdims=True)
        part = jnp.broadcast_to(part, l_ref.shape)

        @pl.when(pl.program_id(0) == 0)
        def _():
            l_ref[...] = part

        @pl.when(pl.program_id(0) > 0)
        def _():
            l_ref[...] += part

    row = pl.BlockSpec((tr, D), lambda i: (i, 0))
    return pl.pallas_call(body, name=name, grid=(T // tr,), in_specs=[row, row],
                          out_specs=[pl.BlockSpec((8, LANES), lambda i: (0, 0)), row],
                          out_shape=[_sds((8, LANES), F32), _sds((T, D), F32)],
                          compiler_params=_params(("arbitrary",)))(y, tgt)


CONV_ROWS = 256
HALO = 8


def _ext_rows(ref, r0, rb, T):
    top = ref[pl.ds(pl.multiple_of(jnp.maximum(r0 - HALO, 0), HALO), HALO), :]
    bot = ref[pl.ds(pl.multiple_of(jnp.minimum(r0 + rb, T - HALO), HALO), HALO), :]
    return jnp.concatenate([jnp.where(r0 > 0, top, 0.0), ref[pl.ds(r0, rb), :], jnp.where(r0 + rb < T, bot, 0.0)], axis=0)


def _conv_ext(e, w_ref, K):
    taps = [e] + [pltpu.roll(e, s, 0) for s in range(1, K)]
    out = taps[0] * w_ref[pl.ds(K - 1, 1), :]
    for s in range(1, K):
        out = out + taps[s] * w_ref[pl.ds(K - 1 - s, 1), :]
    return taps, out


def _conv_ext_bwd(dpre, taps, w_ref, K, rb):
    n = dpre.shape[0]
    own = slice(HALO, HALO + rb)
    d_own = dpre[own]
    dws = [jnp.sum(d_own * taps[s][own], axis=0, keepdims=True) for s in range(K)]
    dp = d_own * w_ref[pl.ds(K - 1, 1), :]
    for s in range(1, K):
        dp = dp + pltpu.roll(dpre, n - s, 0)[own] * w_ref[pl.ds(K - 1 - s, 1), :]
    return dp, dws, jnp.sum(d_own, axis=0, keepdims=True)


def _store_conv_grads(acc, dw_ref, db_ref, K):
    for s in range(K):
        dw_ref[pl.ds(K - 1 - s, 1), :] = acc[s]
    db_ref[...] = acc[K]


def ffn_gate_fwd(name, P, cw, cb, tc=LANES):
    _, T, F = P.shape
    K = cw.shape[0]
    nf = F // tc
    rb = min(CONV_ROWS, T)

    def body(pg_ref, pu_ref, wg_ref, wu_ref, bg_ref, bu_ref, o_ref):
        def blk(bi, carry):
            r0 = pl.multiple_of(bi * rb, rb)
            _, g = _conv_ext(_ext_rows(pg_ref, r0, rb, T), wg_ref, K)
            _, u = _conv_ext(_ext_rows(pu_ref, r0, rb, T), wu_ref, K)
            own = slice(HALO, HALO + rb)
            o_ref[pl.ds(r0, rb), :] = (_gelu(g[own] + bg_ref[...]) * (u[own] + bu_ref[...])).astype(o_ref.dtype)
            return carry

        lax.fori_loop(0, T // rb, blk, 0)

    pg = pl.BlockSpec((None, T, tc), lambda j: (0, 0, j))
    pu = pl.BlockSpec((None, T, tc), lambda j: (1, 0, j))
    wg = pl.BlockSpec((K, tc), lambda j: (0, j))
    wu = pl.BlockSpec((K, tc), lambda j: (0, j + nf))
    bg = pl.BlockSpec((1, tc), lambda j: (0, j))
    bu = pl.BlockSpec((1, tc), lambda j: (0, j + nf))
    return pl.pallas_call(body, name=name, grid=(nf,), in_specs=[pg, pu, wg, wu, bg, bu],
                          out_specs=pl.BlockSpec((T, tc), lambda j: (0, j)), out_shape=_sds((T, F), BF16),
                          compiler_params=_params(("parallel",)))(P, P, cw, cw, cb, cb)


def ffn_gate_bwd(name, P, da, cw, cb, tc=LANES):
    _, T, F = P.shape
    K = cw.shape[0]
    nf = F // tc

    rb = min(CONV_ROWS, T)

    def body(pg_ref, pu_ref, da_ref, wg_ref, wu_ref, bg_ref, bu_ref, dp_ref, dwg_ref, dwu_ref, dbg_ref, dbu_ref):
        def blk(bi, acc):
            r0 = pl.multiple_of(bi * rb, rb)
            tg, g = _conv_ext(_ext_rows(pg_ref, r0, rb, T), wg_ref, K)
            tu, u = _conv_ext(_ext_rows(pu_ref, r0, rb, T), wu_ref, K)
            g = g + bg_ref[...]
            u = u + bu_ref[...]
            da_ = _ext_rows(da_ref, r0, rb, T)
            dpg, dwg, dbg = _conv_ext_bwd(da_ * u * _dgelu(g), tg, wg_ref, K, rb)
            dpu, dwu, dbu = _conv_ext_bwd(da_ * _gelu(g), tu, wu_ref, K, rb)
            dp_ref[0, pl.ds(r0, rb), :] = dpg.astype(dp_ref.dtype)
            dp_ref[1, pl.ds(r0, rb), :] = dpu.astype(dp_ref.dtype)
            return tuple(a + b for a, b in zip(acc, dwg + [dbg] + dwu + [dbu]))

        acc = lax.fori_loop(0, T // rb, blk, tuple(jnp.zeros((1, tc), F32) for _ in range(2 * K + 2)))
        _store_conv_grads(acc[:K + 1], dwg_ref, dbg_ref, K)
        _store_conv_grads(acc[K + 1:], dwu_ref, dbu_ref, K)

    pg = pl.BlockSpec((None, T, tc), lambda j: (0, 0, j))
    pu = pl.BlockSpec((None, T, tc), lambda j: (1, 0, j))
    col = pl.BlockSpec((T, tc), lambda j: (0, j))
    wg = pl.BlockSpec((K, tc), lambda j: (0, j))
    wu = pl.BlockSpec((K, tc), lambda j: (0, j + nf))
    bg = pl.BlockSpec((1, tc), lambda j: (0, j))
    bu = pl.BlockSpec((1, tc), lambda j: (0, j + nf))
    return pl.pallas_call(
        body, name=name, grid=(nf,), in_specs=[pg, pu, col, wg, wu, bg, bu],
        out_specs=[pl.BlockSpec((2, T, tc), lambda j: (0, 0, j)), wg, wg, bg, bg],
        out_shape=[_sds((2, T, F), BF16), _sds((K, F), F32), _sds((K, F), F32), _sds((1, F), F32), _sds((1, F), F32)],
        compiler_params=_params(("parallel",)))(P, P, da, cw, cw, cb, cb)


def xattn_fwd(name, q, kv, n_heads, tq=512):
    T, Wd = q.shape
    Mm = kv.shape[0]
    hd = Wd // n_heads
    scale = hd ** -0.5
    tq = min(tq, T)

    def body(q_ref, kv_ref, o_ref):
        for h in range(n_heads):
            qh = q_ref[:, h * hd:(h + 1) * hd]
            kh = kv_ref[:, h * hd:(h + 1) * hd]
            vh = kv_ref[:, Wd + h * hd:Wd + (h + 1) * hd]
            s = _dot(qh, kh, NT) * scale
            s = s - jnp.max(s, axis=-1, keepdims=True)
            e = jnp.exp(s)
            p = e / jnp.sum(e, axis=-1, keepdims=True)
            o_ref[:, h * hd:(h + 1) * hd] = _dot(p.astype(BF16), vh).astype(o_ref.dtype)

    return pl.pallas_call(body, name=name, grid=(T // tq,),
                          in_specs=[pl.BlockSpec((tq, Wd), lambda i: (i, 0)), pl.BlockSpec((Mm, 2 * Wd), lambda i: (0, 0))],
                          out_specs=pl.BlockSpec((tq, Wd), lambda i: (i, 0)), out_shape=_sds((T, Wd), BF16),
                          compiler_params=_params(("parallel",)))(q, kv)


def xattn_bwd(name, q, kv, do, n_heads, tq=512):
    T, Wd = q.shape
    Mm = kv.shape[0]
    hd = Wd // n_heads
    scale = hd ** -0.5
    tq = min(tq, T)

    def body(q_ref, kv_ref, do_ref, dq_ref, dkv_ref):
        @pl.when(pl.program_id(0) == 0)
        def _():
            dkv_ref[...] = jnp.zeros_like(dkv_ref)

        for h in range(n_heads):
            sl = slice(h * hd, (h + 1) * hd)
            sv = slice(Wd + h * hd, Wd + (h + 1) * hd)
            qh, kh, vh = q_ref[:, sl], kv_ref[:, sl], kv_ref[:, sv]
            doh = do_ref[:, sl].astype(BF16)
            s = _dot(qh, kh, NT) * scale
            s = s - jnp.max(s, axis=-1, keepdims=True)
            e = jnp.exp(s)
            p = e / jnp.sum(e, axis=-1, keepdims=True)
            dp = _dot(doh, vh, NT)
            ds = (p * (dp - jnp.sum(dp * p, axis=-1, keepdims=True)) * scale).astype(BF16)
            dq_ref[:, sl] = _dot(ds, kh).astype(dq_ref.dtype)
            dkv_ref[:, sl] += _dot(ds, qh, TN)
            dkv_ref[:, sv] += _dot(p.astype(BF16), doh, TN)

    row = pl.BlockSpec((tq, Wd), lambda i: (i, 0))
    full = pl.BlockSpec((Mm, 2 * Wd), lambda i: (0, 0))
    return pl.pallas_call(body, name=name, grid=(T // tq,), in_specs=[row, full, row], out_specs=[row, full],
                          out_shape=[_sds((T, Wd), BF16), _sds((Mm, 2 * Wd), F32)],
                          compiler_params=_params(("arbitrary",)))(q, kv, do)


def ffn_fwd(hf, w_in, cw, cb, w_out):
    T = hf.shape[0]
    F = w_out.K
    tn = _tile(w_in.C, 1536)
    nfp = F // tn
    P = mm_fwd("ffn_in", hf, w_in, F32, tn=tn, out_sds=_sds((2, T, F), F32),
               o_spec=lambda tm, tn_: pl.BlockSpec((None, tm, tn_), lambda i, j, k: (j // nfp, i, j % nfp)))
    a = ffn_gate_fwd("ffn_gate", P, cw, cb)
    f = mm_fwd("ffn_out", a, w_out, F32, tn=2048)
    return f, (P, a)


def ffn_bwd(hf, saved, df, w_in, cw, cb, w_out):
    P, a = saved
    T = hf.shape[0]
    F = w_out.K
    C = w_in.C
    da = mm_dx("ffn_out_dx", df, w_out, F32, tn=w_out.R)
    dw_out = mm_dw("ffn_out_dw", a, df, BF16, tn=1024)
    dP, dcw_g, dcw_u, dcb_g, dcb_u = ffn_gate_bwd("ffn_gate_bwd", P, da, cw, cb)
    dw_in = mm_dw("ffn_in_dw", hf, dP, BF16, n_shards=w_in.S, N=2 * F,
                  b_spec=lambda T_, tn: pl.BlockSpec((None, T_, tn), lambda i, j, k: (j // (F // tn), 0, j % (F // tn))))
    per = F // C
    dhf = mm_dx("ffn_in_dx", dP, w_in, F32,
                a_spec=lambda tm, tk: pl.BlockSpec((None, tm, tk), lambda i, j, k: (k // per, i, k % per)))
    grads = dict(ffn_w_in=dw_in, ffn_w_out=dw_out, ffn_conv_w=jnp.concatenate([dcw_g, dcw_u], axis=1),
                 ffn_conv_b=jnp.concatenate([dcb_g, dcb_u], axis=1))
    return dhf, grads


def xa_fwd(hq, mem_n, wq, wkv, wo, n_heads):
    q = mm_fwd("xa_q", hq, wq, BF16)
    kv = mm_fwd("xa_kv", mem_n, wkv, BF16)
    o = xattn_fwd("xa_core", q, kv, n_heads)
    c = mm_fwd("xa_o", o, wo, F32)
    return c, (q, kv, o)


def xa_bwd(hq, mem_n, saved, dc, wq, wkv, wo, n_heads):
    q, kv, o = saved
    do = mm_dx("xa_o_dx", dc, wo, F32)
    dwo = mm_dw("xa_o_dw", o, dc, BF16, n_shards=wo.S)
    dq, dkv = xattn_bwd("xa_core_bwd", q, kv, do, n_heads)
    dkv = dkv.astype(BF16)
    dwq = mm_dw("xa_q_dw", hq, dq, BF16)
    dhq = mm_dx("xa_q_dx", dq, wq, F32, tn=wq.R)
    dwkv = mm_dw("xa_kv_dw", mem_n, dkv, BF16)
    dmem_n = mm_dx("xa_kv_dx", dkv, wkv, F32, tn=wkv.R)
    return dhq, dmem_n, dict(xa_wq=dwq, xa_wkv=dwkv, xa_wo=dwo)


def _sb_logits(q, kblk, scale):
    z = _dot(q, kblk, NT) * scale
    l1 = -_softplus(z)
    return z, l1, z + l1


def _split2(a):
    a1 = a.astype(BF16)
    return a1, (a - a1.astype(F32)).astype(BF16)


def _dot2r(a, m):
    p1, p2 = _split2(a)
    return _dot(p1, m) + _dot(p2, m)


SB_TQ = 1024


def sb_fwd(name, qkv, n_heads, tq=SB_TQ):
    T = qkv.shape[0]
    hd = qkv.shape[1] // (3 * n_heads)
    scale = hd ** -0.5
    Q = CHUNK
    tq = min(tq, T)
    nb = tq // Q
    unroll = 2 if nb % 2 == 0 else 1

    def body(q_ref, k_ref, v_ref, o_ref, lt_ref):
        i = pl.program_id(1)
        q = q_ref[...]
        mcat = jnp.concatenate([_tri(Q, "lt"), jnp.ones((Q, Q), jnp.bool_)], axis=1).astype(BF16)

        def block(kb, carry, q_, masked):
            c, acc = carry
            off = pl.multiple_of(kb * Q, Q)
            kblk, vblk = k_ref[pl.ds(off, Q), :], v_ref[pl.ds(off, Q), :]
            _, l1, lb = _sb_logits(q_, kblk, scale)
            if masked:
                valid = _iota(l1.shape, 1) < _iota(l1.shape, 0)
                l1 = jnp.where(valid, l1, 0.0)
            r = _dot2r(l1, mcat)
            a = jnp.exp(lb + r[:, :Q] + c)
            if masked:
                a = jnp.where(valid, a, 0.0)
            return c + r[:, Q:], acc + _dot(a.astype(BF16), vblk)

        c, acc = jnp.zeros((tq, Q), F32), jnp.zeros((tq, hd), F32)
        for b in reversed(range(nb)):
            lo = b * Q
            cb, ab = block(i * nb + b, (c[lo:], acc[lo:]), q[lo:], True)
            c = cb if lo == 0 else jnp.concatenate([c[:lo], cb], axis=0)
            acc = ab if lo == 0 else jnp.concatenate([acc[:lo], ab], axis=0)

        def step(r, cr):
            for u in range(unroll):
                cr = block(i * nb - 1 - unroll * r - u, cr, q, False)
            return cr

        c, acc = lax.fori_loop(0, i * (nb // unroll), step, (c, acc))
        o_ref[...] = acc.astype(o_ref.dtype)
        lt_ref[...] = c

    H = n_heads
    return pl.pallas_call(
        body, name=name, grid=(H, T // tq),
        in_specs=[pl.BlockSpec((tq, hd), lambda h, i: (i, h)), pl.BlockSpec((T, hd), lambda h, i: (0, H + h)),
                  pl.BlockSpec((T, hd), lambda h, i: (0, 2 * H + h))],
        out_specs=[pl.BlockSpec((tq, hd), lambda h, i: (i, h)), pl.BlockSpec((None, tq, Q), lambda h, i: (h, i, 0))],
        out_shape=[_sds((T, H * hd), BF16), _sds((H, T, Q), F32)],
        compiler_params=_params(("parallel", "arbitrary")))(qkv, qkv, qkv)


def sb_bwd(name, qkv, do, lt, n_heads, tq=SB_TQ):
    T = qkv.shape[0]
    hd = qkv.shape[1] // (3 * n_heads)
    scale = hd ** -0.5
    Q = CHUNK
    tq = min(tq, T)
    nb = tq // Q
    unroll = 2 if nb % 2 == 0 else 1

    def body(q_ref, k_ref, v_ref, do_ref, lt_ref, dq_ref, dk_ref, dv_ref):
        i = pl.program_id(1)

        @pl.when(i == 0)
        def _():
            dk_ref[...] = jnp.zeros_like(dk_ref)
            dv_ref[...] = jnp.zeros_like(dv_ref)

        q, do_, ltot = q_ref[...], do_ref[...], lt_ref[...]
        ones = jnp.ones((Q, Q), jnp.bool_)
        mrev = jnp.concatenate([_tri(Q, "lt"), ones], axis=1).astype(BF16)
        mfwd = jnp.concatenate([_tri(Q, "gt"), ones], axis=1).astype(BF16)
        def block(kb, carry, q_, d_, lt_, masked):
            pin, pre, dq = carry
            off = pl.multiple_of(kb * Q, Q)
            kblk, vblk = k_ref[pl.ds(off, Q), :], v_ref[pl.ds(off, Q), :]
            _, l1, lb = _sb_logits(q_, kblk, scale)
            if masked:
                valid = _iota(l1.shape, 1) < _iota(l1.shape, 0)
                l1 = jnp.where(valid, l1, 0.0)
            r = _dot2r(l1, mrev)
            pin = pin + r[:, Q:]
            a = jnp.exp(lb + r[:, :Q] + (lt_ - pin))
            if masked:
                a = jnp.where(valid, a, 0.0)
            de = _dot(d_, vblk, NT) * a
            r2 = _dot2r(de, mfwd)
            dl1 = pre + r2[:, :Q]
            pre = pre + r2[:, Q:]
            sig = jnp.exp(lb)
            dz = (de * (1.0 - sig) - dl1 * sig) * scale
            if masked:
                dz = jnp.where(valid, dz, 0.0)
            dzb = dz.astype(BF16)
            dk_ref[pl.ds(off, Q), :] += _dot(dzb, q_, TN)
            dv_ref[pl.ds(off, Q), :] += _dot(a.astype(BF16), d_, TN)
            return pin, pre, dq + _dot(dzb, kblk)

        def step(r, cr):
            for u in range(unroll):
                cr = block(unroll * r + u, cr, q, do_, ltot, False)
            return cr

        init = (jnp.zeros((tq, Q), F32), jnp.zeros((tq, Q), F32), jnp.zeros((tq, hd), F32))
        carry = lax.fori_loop(0, i * (nb // unroll), step, init)
        for b in range(nb):
            lo = b * Q
            part = block(i * nb + b, tuple(a[lo:] for a in carry), q[lo:], do_[lo:], ltot[lo:], True)
            carry = part if lo == 0 else tuple(jnp.concatenate([a[:lo], pb], axis=0) for a, pb in zip(carry, part))
        dq_ref[...] = carry[2].astype(dq_ref.dtype)

    H = n_heads
    qs = pl.BlockSpec((tq, hd), lambda h, i: (i, h))
    full = pl.BlockSpec((T, hd), lambda h, i: (0, h))
    return pl.pallas_call(
        body, name=name, grid=(H, T // tq),
        in_specs=[qs, pl.BlockSpec((T, hd), lambda h, i: (0, H + h)), pl.BlockSpec((T, hd), lambda h, i: (0, 2 * H + h)),
                  qs, pl.BlockSpec((None, tq, Q), lambda h, i: (h, i, 0))],
        out_specs=[qs, full, full],
        out_shape=[_sds((T, H * hd), BF16), _sds((T, H * hd), F32), _sds((T, H * hd), F32)],
        compiler_params=_params(("parallel", "arbitrary")))(qkv, qkv, qkv, do, lt)


def sb_mixer_fwd(hn, w_qkv, w_out, n_heads):
    qkv = mm_fwd("sb_qkv", hn, w_qkv, BF16)
    o, lt = sb_fwd("sb_core", qkv, n_heads)
    m = mm_fwd("sb_out", o, w_out, F32, tn=1024)
    return m, (qkv, o, lt)


def sb_mixer_bwd(hn, saved, dm, w_qkv, w_out, n_heads):
    qkv, o, lt = saved
    do = mm_dx("sb_out_dx", dm, w_out, BF16, tn=w_out.R)
    dw_out = mm_dw("sb_out_dw", o, dm, BF16, tn=1024)
    dq, dk, dv = sb_bwd("sb_core_bwd", qkv, do, lt, n_heads)
    dqkv = jnp.concatenate([dq, dk.astype(BF16), dv.astype(BF16)], axis=1)
    dw_qkv = mm_dw("sb_qkv_dw", hn, dqkv, BF16, n_shards=w_qkv.S)
    dhn = mm_dx("sb_qkv_dx", dqkv, w_qkv, F32)
    return dhn, dict(sb_w_qkv=dw_qkv, sb_w_out=dw_out)


def _sgu_common(p_ref, vg_ref, vb_ref, Wd):
    pu, pv = p_ref[:, :Wd], p_ref[:, Wd:]
    u, v = _gelu(pu), _gelu(pv)
    xc = v - jnp.mean(v, axis=-1, keepdims=True)
    r = lax.rsqrt(jnp.mean(xc * xc, axis=-1, keepdims=True) + EPS)
    xh = xc * r
    return pu, pv, u, xh, r, xh * vg_ref[...] + vb_ref[...]


def sgu_fwd(name, P, vg, vb, ws, bexp):
    T = P.shape[0]
    Wd = P.shape[1] // 2
    G = ws.shape[0]
    gw = Wd // G
    Q = CHUNK

    def body(p_ref, vg_ref, vb_ref, ws_ref, be_ref, o_ref):
        _, _, u, _, _, vn = _sgu_common(p_ref, vg_ref, vb_ref, Wd)
        tril = _tri(Q, "le")
        for g in range(G):
            sl = slice(g * gw, (g + 1) * gw)
            wsg = jnp.where(tril, ws_ref[g], 0.0).astype(BF16)
            mixed = _dot(wsg, vn[:, sl].astype(BF16)) + be_ref[:, sl]
            o_ref[:, sl] = (u[:, sl] * mixed).astype(o_ref.dtype)

    vec = pl.BlockSpec((1, Wd), lambda c: (0, 0))
    return pl.pallas_call(
        body, name=name, grid=(T // Q,),
        in_specs=[pl.BlockSpec((Q, 2 * Wd), lambda c: (c, 0)), vec, vec, pl.BlockSpec((G, Q, Q), lambda c: (0, 0, 0)),
                  pl.BlockSpec((Q, Wd), lambda c: (0, 0))],
        out_specs=pl.BlockSpec((Q, Wd), lambda c: (c, 0)), out_shape=_sds((T, Wd), BF16),
        compiler_params=_params(("parallel",)))(P, vg, vb, ws, bexp)


def sgu_bwd(name, P, dgated, vg, vb, ws, bexp):
    T = P.shape[0]
    Wd = P.shape[1] // 2
    G = ws.shape[0]
    gw = Wd // G
    Q = CHUNK
    nc = T // Q

    def body(p_ref, dg_ref, vg_ref, vb_ref, ws_ref, be_ref, dp_ref, dws_ref, dvg_ref, dvb_ref, dbs_ref, dvn_scr, dbe_scr):
        c = pl.program_id(0)

        @pl.when(c == 0)
        def _():
            dws_ref[...] = jnp.zeros_like(dws_ref)
            dvg_ref[...] = jnp.zeros_like(dvg_ref)
            dvb_ref[...] = jnp.zeros_like(dvb_ref)
            dbe_scr[...] = jnp.zeros_like(dbe_scr)

        pu, pv, u, xh, r, vn = _sgu_common(p_ref, vg_ref, vb_ref, Wd)
        tril = _tri(Q, "le")
        for g in range(G):
            sl = slice(g * gw, (g + 1) * gw)
            wsg = jnp.where(tril, ws_ref[g], 0.0).astype(BF16)
            vng = vn[:, sl].astype(BF16)
            mixed = _dot(wsg, vng) + be_ref[:, sl]
            dgt = dg_ref[:, sl]
            dp_ref[:, sl] = (dgt * mixed * _dgelu(pu[:, sl])).astype(dp_ref.dtype)
            dmix = dgt * u[:, sl]
            dmb = dmix.astype(BF16)
            dws_ref[g] += jnp.where(tril, _dot(dmb, vng, NT), 0.0)
            dvn_scr[:, sl] = _dot(wsg, dmb, TN)
            dbe_scr[:, sl] += dmix
        dvn = dvn_scr[...]
        dvg_ref[...] += jnp.sum(dvn * xh, axis=0, keepdims=True)
        dvb_ref[...] += jnp.sum(dvn, axis=0, keepdims=True)
        dxh = dvn * vg_ref[...]
        dv = r * (dxh - jnp.mean(dxh, axis=-1, keepdims=True) - xh * jnp.mean(dxh * xh, axis=-1, keepdims=True))
        dp_ref[:, Wd:] = (dv * _dgelu(pv)).astype(dp_ref.dtype)

        @pl.when(c == nc - 1)
        def _():
            sel = (_iota((Wd, LANES), 0) // gw == _iota((Wd, LANES), 1)).astype(BF16)
            dbs_ref[...] = _dot3r(dbe_scr[...], sel)

    vec = pl.BlockSpec((1, Wd), lambda c: (0, 0))
    wsb = pl.BlockSpec((G, Q, Q), lambda c: (0, 0, 0))
    return pl.pallas_call(
        body, name=name, grid=(nc,),
        in_specs=[pl.BlockSpec((Q, 2 * Wd), lambda c: (c, 0)), pl.BlockSpec((Q, Wd), lambda c: (c, 0)), vec, vec, wsb,
                  pl.BlockSpec((Q, Wd), lambda c: (0, 0))],
        out_specs=[pl.BlockSpec((Q, 2 * Wd), lambda c: (c, 0)), wsb, vec, vec, pl.BlockSpec((Q, LANES), lambda c: (0, 0))],
        out_shape=[_sds((T, 2 * Wd), BF16), _sds((G, Q, Q), F32), _sds((1, Wd), F32), _sds((1, Wd), F32), _sds((Q, LANES), F32)],
        scratch_shapes=[pltpu.VMEM((Q, Wd), F32), pltpu.VMEM((Q, Wd), F32)],
        compiler_params=_params(("arbitrary",)))(P, dgated, vg, vb, ws, bexp)


def sg_mixer_fwd(hn, w_in, vg, vb, ws, bs, w_out):
    G = ws.shape[0]
    Wd = vg.shape[1]
    P = mm_fwd("sg_in", hn, w_in, F32)
    bexp = jnp.repeat(bs.T, Wd // G, axis=1)
    gated = sgu_fwd("sg_core", P, vg, vb, ws, bexp)
    m = mm_fwd("sg_out", gated, w_out, F32, tn=1024)
    return m, (P, bexp, gated)


def sg_mixer_bwd(hn, saved, dm, w_in, vg, vb, ws, w_out):
    P, bexp, gated = saved
    G = ws.shape[0]
    dgated = mm_dx("sg_out_dx", dm, w_out, F32, tn=w_out.R)
    dw_out = mm_dw("sg_out_dw", gated, dm, BF16, tn=1024)
    dP, dws, dvg, dvb, dbs = sgu_bwd("sg_core_bwd", P, dgated, vg, vb, ws, bexp)
    dw_in = mm_dw("sg_in_dw", hn, dP, BF16, n_shards=w_in.S)
    dhn = mm_dx("sg_in_dx", dP, w_in, F32)
    grads = dict(sg_w_in=dw_in, sg_w_out=dw_out, sg_w_spatial=dws, sg_v_norm_g=dvg, sg_v_norm_b=dvb,
                 sg_b_spatial=dbs[:, :G].T)
    return dhn, grads


def ssd_conv_fwd(name, xbc, cw, cb, tc=LANES):
    T, Cd = xbc.shape
    K = cw.shape[0]
    rb = min(CONV_ROWS, T)

    def body(p_ref, w_ref, b_ref, o_ref):
        def blk(bi, carry):
            r0 = pl.multiple_of(bi * rb, rb)
            _, pre = _conv_ext(_ext_rows(p_ref, r0, rb, T), w_ref, K)
            o_ref[pl.ds(r0, rb), :] = _silu(pre[HALO:HALO + rb] + b_ref[...])
            return carry

        lax.fori_loop(0, T // rb, blk, 0)

    col = pl.BlockSpec((T, tc), lambda j: (0, j))
    return pl.pallas_call(body, name=name, grid=(Cd // tc,),
                          in_specs=[col, pl.BlockSpec((K, tc), lambda j: (0, j)), pl.BlockSpec((1, tc), lambda j: (0, j))],
                          out_specs=col, out_shape=_sds((T, Cd), F32), compiler_params=_params(("parallel",)))(xbc, cw, cb)


def ssd_conv_bwd(name, xbc, dact, cw, cb, tc=LANES):
    T, Cd = xbc.shape
    K = cw.shape[0]

    rb = min(CONV_ROWS, T)

    def body(p_ref, da_ref, w_ref, b_ref, dp_ref, dw_ref, db_ref):
        def blk(bi, acc):
            r0 = pl.multiple_of(bi * rb, rb)
            taps, pre = _conv_ext(_ext_rows(p_ref, r0, rb, T), w_ref, K)
            dpre = _ext_rows(da_ref, r0, rb, T) * _dsilu(pre + b_ref[...])
            dp, dws, db = _conv_ext_bwd(dpre, taps, w_ref, K, rb)
            dp_ref[pl.ds(r0, rb), :] = dp.astype(dp_ref.dtype)
            return tuple(a + b for a, b in zip(acc, dws + [db]))

        acc = lax.fori_loop(0, T // rb, blk, tuple(jnp.zeros((1, tc), F32) for _ in range(K + 1)))
        _store_conv_grads(acc, dw_ref, db_ref, K)

    col = pl.BlockSpec((T, tc), lambda j: (0, j))
    wsp = pl.BlockSpec((K, tc), lambda j: (0, j))
    bsp = pl.BlockSpec((1, tc), lambda j: (0, j))
    return pl.pallas_call(body, name=name, grid=(Cd // tc,), in_specs=[col, col, wsp, bsp], out_specs=[col, wsp, bsp],
                          out_shape=[_sds((T, Cd), BF16), _sds((K, Cd), F32), _sds((1, Cd), F32)],
                          compiler_params=_params(("parallel",)))(xbc, dact, cw, cb)


def _expand_matrix(Hd):
    return (_iota((LANES, Hd), 1) // SSD_HEAD_DIM == _iota((LANES, Hd), 0)).astype(BF16)


def ssd_dt_fwd(name, dtr, bias, Hd, tr=512):
    T = dtr.shape[0]
    tr = min(tr, T)

    def body(d_ref, b_ref, o_ref):
        o_ref[...] = _dot3r(_softplus(d_ref[...] + b_ref[...]), _expand_matrix(Hd))

    return pl.pallas_call(body, name=name, grid=(T // tr,),
                          in_specs=[pl.BlockSpec((tr, LANES), lambda i: (i, 0)), pl.BlockSpec((1, LANES), lambda i: (0, 0))],
                          out_specs=pl.BlockSpec((tr, Hd), lambda i: (i, 0)), out_shape=_sds((T, Hd), F32),
                          compiler_params=_params(("parallel",)))(dtr, bias)


def ssd_dt_bwd(name, dtr, bias, ddtx, tr=512):
    T, Hd = ddtx.shape
    tr = min(tr, T)

    def body(d_ref, b_ref, g_ref, o_ref, db_ref):
        p1, p2, p3 = _split3(g_ref[...])
        em = _expand_matrix(Hd)
        ddt = _dot(p1, em, NT) + _dot(p2, em, NT) + _dot(p3, em, NT)
        draw = ddt * _sigmoid(d_ref[...] + b_ref[...])
        o_ref[...] = draw.astype(o_ref.dtype)
        part = jnp.sum(draw, axis=0, keepdims=True)

        @pl.when(pl.program_id(0) == 0)
        def _():
            db_ref[...] = part

        @pl.when(pl.program_id(0) > 0)
        def _():
            db_ref[...] += part

    row = pl.BlockSpec((tr, LANES), lambda i: (i, 0))
    vec = pl.BlockSpec((1, LANES), lambda i: (0, 0))
    return pl.pallas_call(body, name=name, grid=(T // tr,), in_specs=[row, vec, pl.BlockSpec((tr, Hd), lambda i: (i, 0))],
                          out_specs=[row, vec], out_shape=[_sds((T, LANES), BF16), _sds((1, LANES), F32)],
                          compiler_params=_params(("arbitrary",)))(dtr, bias, ddtx)


def _ssd_head_terms(a2, a2r, half, cb, causal, lane):
    hm = (lane < SSD_HEAD_DIM) if half == 0 else (lane >= SSD_HEAD_DIM)
    ccol = jnp.where(hm, a2, a2r)
    lm = jnp.exp(jnp.where(causal, ccol - ccol.T, -jnp.inf))
    return hm, lm, cb * lm


def ssd_core_fwd(name, act, dtx, alx, dx, G):
    T, Hd = dtx.shape
    Q, N = CHUNK, SSD_STATE
    gw = Hd // G
    nc = T // Q
    nx = Hd // N

    def body(xs_ref, b_ref, c_ref, dt_ref, al_ref, d_ref, y_ref, ss_ref, st_scr):
        @pl.when(pl.program_id(1) == 0)
        def _():
            st_scr[...] = jnp.zeros_like(st_scr)

        xs, dtv = xs_ref[...], dt_ref[...]
        Bb, Cb = b_ref[...].astype(BF16), c_ref[...].astype(BF16)
        dA = dtv * (-jnp.exp(al_ref[...]))
        a = _dot3l(_tri(Q, "le").astype(BF16), dA)
        a_last = jnp.sum(dA, axis=0, keepdims=True)
        xdt = xs * dtv
        cbm = _dot(Cb, Bb, NT)
        sprev = st_scr[...]
        ss_ref[...] = sprev
        causal, lane = _tri(Q, "le"), _iota((Q, LANES), 1)
        y_rest = _dot(Cb, sprev.astype(BF16)) * jnp.exp(a) + xs * d_ref[...]
        for q in range(gw // LANES):
            sl = slice(q * LANES, (q + 1) * LANES)
            a2, x2 = a[:, sl], xdt[:, sl]
            a2r = pltpu.roll(a2, SSD_HEAD_DIM, 1)
            acc = y_rest[:, sl]
            for half in (0, 1):
                hm, _, gm = _ssd_head_terms(a2, a2r, half, cbm, causal, lane)
                acc = acc + _dot(gm.astype(BF16), jnp.where(hm, x2, 0.0).astype(BF16))
            y_ref[:, sl] = acc
        w = jnp.exp(a_last - a)
        st_scr[...] = sprev * jnp.exp(a_last) + _dot(Bb, (w * xdt).astype(BF16), TN)

    xsp = pl.BlockSpec((Q, gw), lambda g, c: (c, g))
    vec = pl.BlockSpec((1, gw), lambda g, c: (0, g))
    return pl.pallas_call(
        body, name=name, grid=(G, nc),
        in_specs=[xsp, pl.BlockSpec((Q, N), lambda g, c: (c, nx + g)), pl.BlockSpec((Q, N), lambda g, c: (c, nx + G + g)),
                  xsp, vec, vec],
        out_specs=[xsp, pl.BlockSpec((None, N, gw), lambda g, c: (c, 0, g))],
        out_shape=[_sds((T, Hd), F32), _sds((nc, N, Hd), F32)],
        scratch_shapes=[pltpu.VMEM((N, gw), F32)],
        compiler_params=_params(("parallel", "arbitrary")))(act, act, act, dtx, alx, dx)


def ssd_core_bwd(name, act, dtx, alx, dx, ssave, dy, G):
    T, Hd = dtx.shape
    Q, N = CHUNK, SSD_STATE
    gw = Hd // G
    nc = T // Q
    nx = Hd // N

    def body(xs_ref, b_ref, c_ref, dt_ref, al_ref, d_ref, ss_ref, dy_ref,
             dxs_ref, db_ref, dc_ref, ddt_ref, dal_ref, dd_ref, ds_scr, dxdt_scr, da_scr):
        @pl.when(pl.program_id(1) == 0)
        def _():
            ds_scr[...] = jnp.zeros_like(ds_scr)
            dal_ref[...] = jnp.zeros_like(dal_ref)
            dd_ref[...] = jnp.zeros_like(dd_ref)

        xs, dtv, dy_ = xs_ref[...], dt_ref[...], dy_ref[...]
        Bb, Cb = b_ref[...].astype(BF16), c_ref[...].astype(BF16)
        Ax = -jnp.exp(al_ref[...])
        dA = dtv * Ax
        a = _dot3l(_tri(Q, "le").astype(BF16), dA)
        a_last = jnp.sum(dA, axis=0, keepdims=True)
        xdt = xs * dtv
        e, w, eal = jnp.exp(a), jnp.exp(a_last - a), jnp.exp(a_last)
        sprev, dsn = ss_ref[...], ds_scr[...]
        sprevb, dsnb = sprev.astype(BF16), dsn.astype(BF16)

        dd_ref[...] += jnp.sum(dy_ * xs, axis=0, keepdims=True)
        dmb = (dy_ * e).astype(BF16)
        dC = _dot(dmb, sprevb, NT)
        ds_scr[...] = _dot(Cb, dmb, TN) + dsn * eal
        dalast = jnp.sum(dsn * sprev, axis=0, keepdims=True) * eal
        dB = _dot((w * xdt).astype(BF16), dsnb, NT)
        dwx = _dot(Bb, dsnb)
        dww = dwx * xdt * w
        dalast = dalast + jnp.sum(dww, axis=0, keepdims=True)
        da_scr[...] = dy_ * _dot(Cb, sprevb) * e - dww
        dxdt_scr[...] = w * dwx
        cbm = _dot(Cb, Bb, NT)
        dcb = jnp.zeros((Q, Q), F32)
        causal, lane = _tri(Q, "le"), _iota((Q, LANES), 1)
        for q in range(gw // LANES):
            sl = slice(q * LANES, (q + 1) * LANES)
            a2, x2, dy2 = a[:, sl], xdt[:, sl], dy_[:, sl]
            a2r = pltpu.roll(a2, SSD_HEAD_DIM, 1)
            for half in (0, 1):
                hm, lm, gm = _ssd_head_terms(a2, a2r, half, cbm, causal, lane)
                dyh = jnp.where(hm, dy2, 0.0).astype(BF16)
                dg = _dot(dyh, jnp.where(hm, x2, 0.0).astype(BF16), NT)
                dxdt_scr[:, sl] += _dot(gm.astype(BF16), dyh, TN)
                dcb = dcb + dg * lm
                dseg = dg * gm
                v = jnp.sum(dseg, axis=1, keepdims=True) - jnp.sum(dseg.T, axis=1, keepdims=True)
                da_scr[:, sl] += jnp.where(hm, v, 0.0) * (1.0 / SSD_HEAD_DIM)
        dcbb = dcb.astype(BF16)
        dc_ref[...] = dC + _dot(dcbb, Bb)
        db_ref[...] = dB + _dot(dcbb, Cb, TN)
        dxdt = dxdt_scr[...]
        dxs_ref[...] = dy_ * d_ref[...] + dxdt * dtv
        da = da_scr[...] + jnp.where(_iota((Q, gw), 0) == Q - 1, dalast, 0.0)
        dda = _dot3l(_tri(Q, "ge").astype(BF16), da)
        ddt_ref[...] = dxdt * xs + dda * Ax
        dal_ref[...] += jnp.sum(dda * dtv, axis=0, keepdims=True) * Ax

    rc = lambda c: nc - 1 - c
    xsp = pl.BlockSpec((Q, gw), lambda g, c: (rc(c), g))
    bsp = pl.BlockSpec((Q, N), lambda g, c: (rc(c), nx + g))
    csp = pl.BlockSpec((Q, N), lambda g, c: (rc(c), nx + G + g))
    gsp = pl.BlockSpec((Q, N), lambda g, c: (rc(c), g))
    vec = pl.BlockSpec((1, gw), lambda g, c: (0, g))
    return pl.pallas_call(
        body, name=name, grid=(G, nc),
        in_specs=[xsp, bsp, csp, xsp, vec, vec, pl.BlockSpec((None, N, gw), lambda g, c: (rc(c), 0, g)), xsp],
        out_specs=[xsp, gsp, gsp, xsp, vec, vec],
        out_shape=[_sds((T, Hd), F32), _sds((T, G * N), F32), _sds((T, G * N), F32), _sds((T, Hd), F32),
                   _sds((1, Hd), F32), _sds((1, Hd), F32)],
        scratch_shapes=[pltpu.VMEM((N, gw), F32), pltpu.VMEM((Q, gw), F32), pltpu.VMEM((Q, gw), F32)],
        compiler_params=_params(("parallel", "arbitrary")))(act, act, act, dtx, alx, dx, ssave, dy)


def ssd_gate_fwd(name, y, z, ng, tr=256):
    T, Hd = y.shape
    tr = min(tr, T)

    def body(y_ref, z_ref, g_ref, o_ref):
        o_ref[...] = _rms(y_ref[...] * _silu(z_ref[...]), g_ref[...]).astype(o_ref.dtype)

    row = pl.BlockSpec((tr, Hd), lambda i: (i, 0))
    return pl.pallas_call(body, name=name, grid=(T // tr,), in_specs=[row, row, pl.BlockSpec((1, Hd), lambda i: (0, 0))],
                          out_specs=row, out_shape=_sds((T, Hd), BF16), compiler_params=_params(("parallel",)))(y, z, ng)


def ssd_gate_bwd(name, y, z, ng, dyn, tr=128):
    T, Hd = y.shape
    tr = min(tr, T)

    def body(y_ref, z_ref, g_ref, dn_ref, dy_ref, dz_ref, dg_ref):
        y_, z_, dn = y_ref[...], z_ref[...], dn_ref[...]
        y2 = y_ * _silu(z_)
        r = lax.rsqrt(jnp.mean(y2 * y2, axis=-1, keepdims=True) + EPS)
        xh = y2 * r
        dxh = dn * g_ref[...]
        dy2 = r * (dxh - xh * jnp.mean(dxh * xh, axis=-1, keepdims=True))
        dy_ref[...] = dy2 * _silu(z_)
        dz_ref[...] = (dy2 * y_ * _dsilu(z_)).astype(dz_ref.dtype)
        part = jnp.sum(dn * xh, axis=0, keepdims=True)

        @pl.when(pl.program_id(0) == 0)
        def _():
            dg_ref[...] = part

        @pl.when(pl.program_id(0) > 0)
        def _():
            dg_ref[...] += part

    row = pl.BlockSpec((tr, Hd), lambda i: (i, 0))
    vec = pl.BlockSpec((1, Hd), lambda i: (0, 0))
    return pl.pallas_call(body, name=name, grid=(T // tr,), in_specs=[row, row, vec, row], out_specs=[row, row, vec],
                          out_shape=[_sds((T, Hd), F32), _sds((T, Hd), BF16), _sds((1, Hd), F32)],
                          compiler_params=_params(("arbitrary",)))(y, z, ng, dyn)


def ssd_mixer_fwd(hn, wz, wxbc, wdt, cw, cb, dtb, alx, dx, ng, w_out, G):
    Hd = wz.N
    z = mm_fwd("ssd_z", hn, wz, F32)
    xbc = mm_fwd("ssd_xbc", hn, wxbc, F32)
    dtr = mm_fwd("ssd_dt", hn, wdt, F32)
    act = ssd_conv_fwd("ssd_conv", xbc, cw, cb)
    dtx = ssd_dt_fwd("ssd_dtx", dtr, dtb, Hd)
    y, ssave = ssd_core_fwd("ssd_core", act, dtx, alx, dx, G)
    yn = ssd_gate_fwd("ssd_gate", y, z, ng)
    m = mm_fwd("ssd_out", yn, w_out, F32, tn=1024)
    return m, (z, xbc, dtr, act, dtx, y, ssave, yn)


def ssd_mixer_bwd(hn, saved, dm, wz, wxbc, wdt, cw, cb, dtb, alx, dx, ng, w_out, G):
    z, xbc, dtr, act, dtx, y, ssave, yn = saved
    dyn = mm_dx("ssd_out_dx", dm, w_out, F32, tn=w_out.R)
    dw_out = mm_dw("ssd_out_dw", yn, dm, BF16, tn=1024)
    dy, dz, dng = ssd_gate_bwd("ssd_gate_bwd", y, z, ng, dyn)
    dxs, dB, dC, ddtx, dalx, ddx = ssd_core_bwd("ssd_core_bwd", act, dtx, alx, dx, ssave, dy, G)
    dxbc, dcw, dcb = ssd_conv_bwd("ssd_conv_bwd", xbc, jnp.concatenate([dxs, dB, dC], axis=1), cw, cb)
    ddtr, ddtb = ssd_dt_bwd("ssd_dtx_bwd", dtr, dtb, ddtx)
    dwz = mm_dw("ssd_z_dw", hn, dz, BF16)
    dwxbc = mm_dw("ssd_xbc_dw", hn, dxbc, BF16)
    dwdt = mm_dw("ssd_dt_dw", hn, ddtr, BF16)
    dhn = mm_dx("ssd_z_dx", dz, wz, F32)
    dhn = mm_dx("ssd_xbc_dx", dxbc, wxbc, F32, add=dhn)
    dhn = mm_dx("ssd_dt_dx", ddtr, wdt, F32, add=dhn)
    grads = dict(ssd_w_out=dw_out, ssd_wz=dwz, ssd_wxbc=dwxbc, ssd_wdt=dwdt, ssd_conv_w=dcw, ssd_conv_b=dcb,
                 ssd_dt_bias=ddtb, ssd_alx=dalx, ssd_dx=ddx, ssd_norm=dng)
    return dhn, grads


def adamw(name, w, m, v, ga, gb=None):
    Rr, C = w.shape
    tr = 8
    while Rr % (tr * 2) == 0 and tr * 2 * C * 4 <= (1 << 20):
        tr *= 2
    if Rr % tr:
        tr = Rr
    two = gb is not None
    c1 = 1.0 - ADAM_B1 ** ADAM_STEP
    c2 = 1.0 - ADAM_B2 ** ADAM_STEP

    def body(*refs):
        if two:
            w_ref, m_ref, v_ref, a_ref, b_ref, g_ref, d_ref, mo_ref, vo_ref = refs
            g = a_ref[...] + b_ref[...]
        else:
            w_ref, m_ref, v_ref, a_ref, g_ref, d_ref, mo_ref, vo_ref = refs
            g = a_ref[...]
        m2 = ADAM_B1 * m_ref[...] + (1.0 - ADAM_B1) * g
        v2 = ADAM_B2 * v_ref[...] + (1.0 - ADAM_B2) * (g * g)
        g_ref[...] = g
        mo_ref[...] = m2
        vo_ref[...] = v2
        d_ref[...] = -ADAM_LR * ((m2 / c1) / (jnp.sqrt(v2 / c2) + ADAM_EPS) + ADAM_WD * w_ref[...])

    blk = pl.BlockSpec((tr, C), lambda i: (i, 0))
    n_in = 5 if two else 4
    args = (w, m, v, ga) + ((gb,) if two else ())
    return pl.pallas_call(body, name=name, grid=(Rr // tr,), in_specs=[blk] * n_in, out_specs=[blk] * 4,
                          out_shape=[_sds((Rr, C), F32)] * 4, compiler_params=_params(("parallel",)))(*args)


def adamw_rows(name, w, m, v, ga, gb, row0, prev):
    Rr, C = ga.shape
    tr = 8
    while Rr % (tr * 2) == 0 and row0 % (tr * 2) == 0 and tr * 2 * C * 4 <= (1 << 20):
        tr *= 2
    assert Rr % tr == 0 and row0 % tr == 0, (Rr, row0, tr)
    off = row0 // tr
    c1 = 1.0 - ADAM_B1 ** ADAM_STEP
    c2 = 1.0 - ADAM_B2 ** ADAM_STEP

    def body(w_ref, m_ref, v_ref, a_ref, b_ref, *rest):
        g_ref, d_ref, mo_ref, vo_ref = rest[-4:]
        g = a_ref[...] + b_ref[...]
        m2 = ADAM_B1 * m_ref[...] + (1.0 - ADAM_B1) * g
        v2 = ADAM_B2 * v_ref[...] + (1.0 - ADAM_B2) * (g * g)
        g_ref[...] = g
        mo_ref[...] = m2
        vo_ref[...] = v2
        d_ref[...] = -ADAM_LR * ((m2 / c1) / (jnp.sqrt(v2 / c2) + ADAM_EPS) + ADAM_WD * w_ref[...])

    rows = pl.BlockSpec((tr, C), lambda i: (off + i, 0))
    part = pl.BlockSpec((tr, C), lambda i: (i, 0))
    carried = [] if prev is None else list(prev)
    return pl.pallas_call(body, name=name, grid=(Rr // tr,), in_specs=[rows] * 3 + [part] * 2 + [ANY] * len(carried),
                          out_specs=[rows] * 4, out_shape=[_sds(w.shape, F32)] * 4,
                          input_output_aliases={5 + k: k for k in range(len(carried))},
                          compiler_params=_params(("parallel",)))(w, m, v, ga, gb, *carried)


def _mesh_pos():
    return lax.axis_index("x"), lax.axis_index("y"), lax.axis_index("c")


def _peer_chips(x, y):
    return [(1 - x, y), (x, 1 - y), (1 - x, 1 - y)]


def sum_slots(name, own, r):
    _, Rr, C = r.shape
    tr = 8
    while Rr % (tr * 2) == 0 and tr * 2 * C * 4 <= (1 << 20):
        tr *= 2

    def body(o_in, r_ref, o_ref):
        o_ref[...] = ((o_in[...].astype(F32) + r_ref[0].astype(F32)) + r_ref[1].astype(F32)) + r_ref[2].astype(F32)

    blk = pl.BlockSpec((tr, C), lambda i: (i, 0))
    return pl.pallas_call(body, name=name, grid=(Rr // tr,), in_specs=[blk, pl.BlockSpec((3, tr, C), lambda i: (0, i, 0))],
                          out_specs=blk, out_shape=_sds((Rr, C), F32), compiler_params=_params(("parallel",)))(own, r)


HBM = pl.BlockSpec(memory_space=pltpu.HBM)
SEM = pl.BlockSpec(memory_space=pltpu.SEMAPHORE)
EFFECT = pltpu.SideEffectType.DATAFLOW_SIDE_EFFECTING


def _xchg_copy(mode, side, src, land, send, recv, t, j, peer, me, c):
    px, py = peer
    pidx = 2 * px + py
    if mode == "gather":
        s, dst = src, land.at[me if side == "out" else pidx]
    else:
        s, dst = src.at[pidx], land.at[j]
    k = 3 * t + j
    return pltpu.make_async_remote_copy(src_ref=s, dst_ref=dst, send_sem=send.at[k], recv_sem=recv.at[k],
                                        device_id=(px, py, c), device_id_type=MESH)


def xchg_start(name, mode, srcs, lands):
    counts = [len(g) for g in srcs]
    ng = len(counts)
    fs = [a for g in srcs for a in g]
    fl = [a for g in lands for a in g]
    n = len(fs)

    def body(*refs):
        src, land = refs[:n], refs[n:2 * n]
        send, recv = refs[2 * n:2 * n + ng], refs[2 * n + ng:2 * n + 2 * ng]
        token = refs[-1]
        x, y, c = _mesh_pos()
        me = 2 * x + y
        k = 0
        for gi in range(ng):
            for t in range(counts[gi]):
                for j, peer in enumerate(_peer_chips(x, y)):
                    _xchg_copy(mode, "out", src[k], land[k], send[gi], recv[gi], t, j, peer, me, c).start()
                k += 1
        token[...] = jnp.zeros_like(token)

    sems = tuple(pltpu.SemaphoreType.DMA((3 * cnt,)) for cnt in counts)
    thru = tuple(pltpu.HBM(a.shape, a.dtype) for a in fs + fl)
    out = pl.pallas_call(
        body, name=name, in_specs=[HBM] * (2 * n),
        out_specs=(SEM,) * (2 * ng) + (HBM,) * (2 * n) + (pl.BlockSpec(memory_space=pltpu.VMEM),),
        out_shape=sems + sems + thru + (_sds((8, LANES), F32),),
        input_output_aliases={i: 2 * ng + i for i in range(2 * n)},
        compiler_params=pltpu.CompilerParams(has_side_effects=EFFECT),
    )(*[pltpu.with_memory_space_constraint(a, pltpu.HBM) for a in fs + fl])
    send, recv = out[:ng], out[ng:2 * ng]
    thru_s, thru_l = out[2 * ng:2 * ng + n], out[2 * ng + n:2 * ng + 2 * n]
    groups, k = [], 0
    for gi, cnt in enumerate(counts):
        groups.append(dict(send=send[gi], recv=recv[gi], src=list(thru_s[k:k + cnt]), land=list(thru_l[k:k + cnt])))
        k += cnt
    return groups, out[-1]


def xchg_wait(name, mode, grp, after):
    src, land = grp["src"], grp["land"]
    n = len(src)

    def body(*refs):
        s_ref, l_ref = refs[:n], refs[n:2 * n]
        send, recv = refs[2 * n], refs[2 * n + 1]
        x, y, c = _mesh_pos()
        me = 2 * x + y
        for t in range(n):
            for j, peer in enumerate(_peer_chips(x, y)):
                _xchg_copy(mode, "out", s_ref[t], l_ref[t], send, recv, t, j, peer, me, c).wait_send()
                _xchg_copy(mode, "in", s_ref[t], l_ref[t], send, recv, t, j, peer, me, c).wait_recv()

    res = pl.pallas_call(
        body, name=name, in_specs=[HBM] * (2 * n) + [SEM, SEM, ANY],
        out_specs=(HBM,) * (2 * n), out_shape=tuple(pltpu.HBM(a.shape, a.dtype) for a in src + land),
        input_output_aliases={i: i for i in range(2 * n)},
        compiler_params=pltpu.CompilerParams(has_side_effects=EFFECT),
    )(*src, *land, grp["send"], grp["recv"], after)
    return list(res[:n]), list(res[n:])


def swap_with_sibling(name, tensors):
    n = len(tensors)

    def body(*refs):
        ins, outs = refs[:n], refs[n:2 * n]
        send_sems, recv_sems = refs[2 * n:]
        x, y, c = _mesh_pos()
        cps = []
        for t in range(n):
            cp = pltpu.make_async_remote_copy(src_ref=ins[t], dst_ref=outs[t], send_sem=send_sems.at[t], recv_sem=recv_sems.at[t],
                                              device_id=(x, y, 1 - c), device_id_type=MESH)
            cp.start()
            cps.append(cp)
        for cp in cps:
            cp.wait()

    return pl.pallas_call(
        body, name=name, in_specs=[ANY] * n, out_specs=[ANY] * n, out_shape=[_sds(t.shape, t.dtype) for t in tensors],
        scratch_shapes=[pltpu.SemaphoreType.DMA((n,)), pltpu.SemaphoreType.DMA((n,))],
    )(*tensors)


def all_reduce_small(name, v, after):
    Rr, C = v.shape
    nd = 8

    def body(v_ref, after_ref, o_ref, gath, send_sems, recv_sems):
        x, y, c = _mesh_pos()
        me = 4 * x + 2 * y + c
        cps = []
        for d in range(1, nd):
            bx, by, bc = (d >> 2) & 1, (d >> 1) & 1, d & 1
            tgt = (1 - x if bx else x, 1 - y if by else y, 1 - c if bc else c)
            cp = pltpu.make_async_remote_copy(src_ref=v_ref, dst_ref=gath.at[me], send_sem=send_sems.at[d - 1],
                                              recv_sem=recv_sems.at[d - 1], device_id=tgt, device_id_type=MESH)
            cp.start()
            cps.append((cp, tgt))
        gath[me] = v_ref[...]
        for d in range(1, nd):
            _, (tx, ty, tc) = cps[d - 1]
            pltpu.make_async_remote_copy(src_ref=v_ref, dst_ref=gath.at[4 * tx + 2 * ty + tc], send_sem=send_sems.at[d - 1],
                                         recv_sem=recv_sems.at[d - 1], device_id=(tx, ty, tc), device_id_type=MESH).wait_recv()
        acc = gath[0]
        for d in range(1, nd):
            acc = acc + gath[d]
        o_ref[...] = acc
        for cp, _ in cps:
            cp.wait_send()

    vm = pl.BlockSpec(memory_space=pltpu.VMEM)
    return pl.pallas_call(
        body, name=name, in_specs=[vm, ANY], out_specs=vm, out_shape=_sds((Rr, C), F32),
        scratch_shapes=[pltpu.VMEM((nd, Rr, C), F32), pltpu.SemaphoreType.DMA((nd - 1,)), pltpu.SemaphoreType.DMA((nd - 1,))],
        compiler_params=pltpu.CompilerParams(vmem_limit_bytes=VMEM_LIMIT),
    )(v, after)


def _pack(arrs):
    flat = jnp.concatenate([a.reshape(-1) for a in arrs])
    pad = (-flat.shape[0]) % (8 * LANES)
    return jnp.pad(flat, (0, pad)).reshape(-1, LANES)


def _unpack(buf, shapes):
    flat = buf.reshape(-1)
    out, off = [], 0
    for s in shapes:
        n = math.prod(s)
        out.append(flat[off:off + n].reshape(s))
        off += n
    return out


WEIGHTS = ["ln_mix_pre", "ln_mix_post", "ln_mem", "ln_xa_pre", "ln_xa_post", "ln_ffn_pre", "ln_ffn_post", "xa_wq", "xa_wkv",
           "xa_wo", "ffn_w_in", "ffn_conv_w", "ffn_conv_b", "ffn_w_out", "ssd_w_in", "ssd_conv_w", "ssd_conv_b", "ssd_dt_bias",
           "ssd_a_log", "ssd_d", "ssd_norm", "ssd_w_out", "sg_w_in", "sg_v_norm_g", "sg_v_norm_b", "sg_w_spatial",
           "sg_b_spatial", "sg_w_out", "sb_w_qkv", "sb_w_out"]
BIG = {"xa_wq": "rows", "xa_wkv": "rows", "xa_wo": "cols", "ffn_w_in": "cols", "ffn_w_out": "rows", "ssd_w_in": "cols",
       "ssd_w_out": "rows", "sg_w_in": "cols", "sg_w_out": "rows", "sb_w_qkv": "cols", "sb_w_out": "rows"}
SHARDED_SMALL = {"ffn_conv_w": 2, "ssd_conv_w": 2, "ssd_conv_b": 1, "ssd_norm": 1}
SMALL = [n for n in WEIGHTS if n not in BIG]
N_MIXERS = 3
HEAD = 128


def _unshard(a, axis):
    a = jnp.moveaxis(a, 0, axis)
    s = a.shape
    return a.reshape(s[:axis] + (s[axis] * s[axis + 1],) + s[axis + 2:])


def _step(p):
    x, mem, tgt = p["x"][0], p["mem"][0], p["loss_target"][0]
    T, D = x.shape
    depth = p["ln_mix_pre"].shape[0]
    S = N_CHIPS

    me = 2 * lax.axis_index("x") + lax.axis_index("y")
    Hd, Cd = S * p["ssd_norm"].shape[1], S * p["ssd_conv_b"].shape[1]
    nh = p["ssd_dt_bias"].shape[1]
    G = (Cd - Hd) // (2 * SSD_STATE)
    xa_heads = p["xa_wo"].shape[1] // HEAD
    sb_heads = D // HEAD

    def layer_parts(i):
        kind, j = i % N_MIXERS, i // N_MIXERS
        first = {0: [("ssd_w_in", j), ("ssd_conv_w", j), ("ssd_conv_b", j), ("ssd_norm", j)], 1: [("sg_w_in", j)],
                 2: [("sb_w_qkv", j)]}[kind]
        w_out = {0: "ssd_w_out", 1: "sg_w_out", 2: "sb_w_out"}[kind]
        return [first + [("ffn_conv_w", i)], [(w_out, j), ("xa_wq", i), ("xa_wkv", i), ("xa_wo", i)],
                [("ffn_w_in", i), ("ffn_w_out", i)]]

    srcs, lands = [], []
    for i in range(depth):
        for part in layer_parts(i):
            s_i, l_i = [], []
            for n, k in part:
                a = p[n][k].astype(BF16) if n in BIG else p[n][k]
                a = a.reshape((1,) * (2 - a.ndim) + a.shape)
                s_i.append(a)
                l_i.append(lax.dynamic_update_index_in_dim(lax.empty((S,) + a.shape, a.dtype), a, me, 0))
            srcs.append(s_i)
            lands.append(l_i)
    gather_groups, gather_token = xchg_start("gather_start", "gather", srcs, lands)

    class Gathered:
        def __init__(self, i):
            self.i, self.parts, self.got = i, layer_parts(i), {}

        def get(self, key, after):
            if key not in self.got:
                k = next(idx for idx, part in enumerate(self.parts) if key in part)
                _, zones = xchg_wait("gather_wait_%d_%d" % (self.i, k), "gather", gather_groups[3 * self.i + k], after)
                self.got.update(zip(self.parts[k], zones))
            return self.got[key]

    def layer_args(i, gz, x_in):
        kind, j = i % N_MIXERS, i // N_MIXERS
        w_of = lambda n, k: W(BIG[n], lambda operand: gz.get((n, k), operand)[:, None], 0, shape=(S, 1) + p[n].shape[1:])
        now = lambda n, k: gz.get((n, k), x_in)
        a = dict(xa=(w_of("xa_wq", i), w_of("xa_wkv", i), w_of("xa_wo", i), xa_heads),
                 ffn=(w_of("ffn_w_in", i), _unshard(now("ffn_conv_w", i), 1), p["ffn_conv_b"][i:i + 1], w_of("ffn_w_out", i)))
        if kind == 0:
            w_in = _unshard(now("ssd_w_in", j), 1)[None]
            a["mix"] = (W("full", w_in[:, :, :Hd], 0), W("full", w_in[:, :, Hd:Hd + Cd], 0),
                        W("full", jnp.pad(w_in[:, :, Hd + Cd:], ((0, 0), (0, 0), (0, LANES - nh))), 0),
                        _unshard(now("ssd_conv_w", j), 1), _unshard(now("ssd_conv_b", j), 1),
                        jnp.pad(p["ssd_dt_bias"][j], (0, LANES - nh))[None], jnp.repeat(p["ssd_a_log"][j], SSD_HEAD_DIM)[None],
                        jnp.repeat(p["ssd_d"][j], SSD_HEAD_DIM)[None], _unshard(now("ssd_norm", j), 1), w_of("ssd_w_out", j), G)
        elif kind == 1:
            a["mix"] = (w_of("sg_w_in", j), p["sg_v_norm_g"][j:j + 1], p["sg_v_norm_b"][j:j + 1], p["sg_w_spatial"][j],
                        w_of("sg_w_out", j))
        else:
            a["mix"] = (w_of("sb_w_qkv", j), w_of("sb_w_out", j), sb_heads)
        return a

    ln = lambda n, i: p[n][i:i + 1]

    h = rms_fwd("rms_first", x, ln("ln_mix_pre", 0), after=gather_token)
    saved, largs = [], []
    for i in range(depth):
        kind, j = i % N_MIXERS, i // N_MIXERS
        la = layer_args(i, Gathered(i), x)
        largs.append(la)
        if kind == 0:
            m, ms = ssd_mixer_fwd(h, *la["mix"])
        elif kind == 1:
            m, ms = sg_mixer_fwd(h, *la["mix"][:4], p["sg_b_spatial"][j], la["mix"][4])
        else:
            m, ms = sb_mixer_fwd(h, *la["mix"])
        x1, hq = resid_norm("resid_norm", x, m, ln("ln_mix_post", i), ln("ln_xa_pre", i))
        mem_n = rms_fwd("rms_mem", mem, ln("ln_mem", i))
        c, cs = xa_fwd(hq, mem_n, *la["xa"])
        x2, hf = resid_norm("resid_norm", x1, c, ln("ln_xa_post", i), ln("ln_ffn_pre", i))
        f, fs = ffn_fwd(hf, *la["ffn"])
        x3, hn = resid_norm("resid_norm", x2, f, ln("ln_ffn_post", i), ln("ln_mix_pre", i + 1) if i + 1 < depth else None)
        saved.append(dict(x=x, h=h, m=m, ms=ms, x1=x1, hq=hq, mem_n=mem_n, c=c, cs=cs, x2=x2, hf=hf, f=f, fs=fs))
        x, h = x3, hn
    loss_tile, dx = loss_fwd_bwd("loss", x, tgt)
    loss = lax.psum(loss_tile[0, 0], ("x", "y", "c"))

    gs = {n: [None] * p[n].shape[0] for n in WEIGHTS}
    scatter_groups = [[] for _ in range(depth)]

    def send_grads(i, part, keys):
        g_src = [gs[n][k] for n, k in keys]
        g_land = [lax.empty((3,) + a.shape[1:], a.dtype) for a in g_src]
        grp, tok = xchg_start("scatter_start_%d_%d" % (i, part), "scatter", [g_src], [g_land])
        scatter_groups[i].append((part, keys, grp[0]))
        return tok

    token = None
    for i in reversed(range(depth)):
        kind, j = i % N_MIXERS, i // N_MIXERS
        s, la = saved[i], largs[i]
        df, gs["ln_ffn_post"][i] = rms_bwd("rms_bwd_post", s["f"], ln("ln_ffn_post", i), dx, None, BF16, after=token)
        dhf, g = ffn_bwd(s["hf"], s["fs"], df, *la["ffn"])
        gs["ffn_w_in"][i], gs["ffn_conv_w"][i], gs["ffn_conv_b"][i] = g["ffn_w_in"], g["ffn_conv_w"], g["ffn_conv_b"]
        gs["ffn_w_out"][i] = g["ffn_w_out"].reshape(S, -1, D)
        dx, gs["ln_ffn_pre"][i] = rms_bwd("rms_bwd_pre", s["x2"], ln("ln_ffn_pre", i), dhf, dx, F32)
        token = send_grads(i, 2, [("ffn_w_in", i), ("ffn_w_out", i)])

        dc, gs["ln_xa_post"][i] = rms_bwd("rms_bwd_post", s["c"], ln("ln_xa_post", i), dx, None, BF16, after=token)
        dhq, dmem_n, g = xa_bwd(s["hq"], s["mem_n"], s["cs"], dc, *la["xa"])
        gs["xa_wq"][i] = g["xa_wq"].reshape(S, D // S, -1)
        gs["xa_wkv"][i] = g["xa_wkv"].reshape(S, D // S, -1)
        gs["xa_wo"][i] = g["xa_wo"]
        _, gs["ln_mem"][i] = rms_bwd("rms_bwd_mem", mem, ln("ln_mem", i), dmem_n, None, BF16)
        dx, gs["ln_xa_pre"][i] = rms_bwd("rms_bwd_pre", s["x1"], ln("ln_xa_pre", i), dhq, dx, F32)
        token = send_grads(i, 1, [("xa_wq", i), ("xa_wkv", i), ("xa_wo", i)])

        dm, gs["ln_mix_post"][i] = rms_bwd("rms_bwd_post", s["m"], ln("ln_mix_post", i), dx, None, BF16, after=token)
        if kind == 0:
            dhn, g = ssd_mixer_bwd(s["h"], s["ms"], dm, *la["mix"])
            full = jnp.concatenate([g["ssd_wz"], g["ssd_wxbc"], g["ssd_wdt"][:, :nh]], axis=1)
            gs["ssd_w_in"][j] = full.reshape(D, S, -1).transpose(1, 0, 2)
            gs["ssd_w_out"][j] = g["ssd_w_out"].reshape(S, Hd // S, D)
            gs["ssd_conv_w"][j], gs["ssd_conv_b"][j], gs["ssd_norm"][j] = g["ssd_conv_w"], g["ssd_conv_b"], g["ssd_norm"]
            gs["ssd_dt_bias"][j] = g["ssd_dt_bias"][:, :nh]
            gs["ssd_a_log"][j] = g["ssd_alx"].reshape(nh, SSD_HEAD_DIM).sum(-1)[None]
            gs["ssd_d"][j] = g["ssd_dx"].reshape(nh, SSD_HEAD_DIM).sum(-1)[None]
        elif kind == 1:
            dhn, g = sg_mixer_bwd(s["h"], s["ms"], dm, *la["mix"])
            gs["sg_w_in"][j] = g["sg_w_in"]
            gs["sg_w_out"][j] = g["sg_w_out"].reshape(S, -1, D)
            for n in ("sg_v_norm_g", "sg_v_norm_b", "sg_w_spatial", "sg_b_spatial"):
                gs[n][j] = g[n]
        else:
            dhn, g = sb_mixer_bwd(s["h"], s["ms"], dm, *la["mix"])
            gs["sb_w_qkv"][j] = g["sb_w_qkv"]
            gs["sb_w_out"][j] = g["sb_w_out"].reshape(S, -1, D)
        dx, gs["ln_mix_pre"][i] = rms_bwd("rms_bwd_pre", s["x"], ln("ln_mix_pre", i), dhn, dx, F32)

        mix_w = {0: ("ssd_w_in", "ssd_w_out"), 1: ("sg_w_in", "sg_w_out"), 2: ("sb_w_qkv", "sb_w_out")}[kind]
        token = send_grads(i, 0, [(n, j) for n in mix_w])

    out, running, behind = {}, {n: None for n in BIG}, token
    for i in reversed(range(depth)):
        keys, qs = [], []
        for part, part_keys, grp in scatter_groups[i]:
            sent, got = xchg_wait("scatter_wait_%d_%d" % (i, part), "scatter", grp, behind if i == 0 else token)
            for (n, k), own, r in zip(part_keys, sent, got):
                mine = lax.dynamic_index_in_dim(own, me, 0, keepdims=False)
                keys.append((n, k))
                qs.append(sum_slots("sum_grad_slots", mine.reshape(-1, mine.shape[-1]), r.reshape(3, -1, r.shape[-1])))
        sib = swap_with_sibling("swap_grads_%d" % i, qs)
        for (n, k), q, q2 in zip(keys, qs, sib):
            two_d = lambda a: a.reshape(-1, a.shape[-1])
            running[n] = adamw_rows("adamw_big", two_d(p[n]), two_d(p["m_" + n]), two_d(p["v_" + n]), q, q2,
                                    k * q.shape[0], running[n])
            behind = running[n][0]
    for n in BIG:
        out[n] = [r.reshape(p[n].shape) for r in running[n]]

    stack = lambda n: jnp.stack([a.reshape(p[n].shape[1:]) if n not in SHARDED_SMALL else a.reshape(a.shape[-len(p[n].shape) + 1:])
                                 for a in gs[n]])
    small_full = [stack(n) for n in SMALL]
    red = _unpack(all_reduce_small("reduce_small", _pack(small_full), after=behind), [a.shape for a in small_full])
    small_g = []
    for n, a in zip(SMALL, red):
        if n in SHARDED_SMALL:
            ax = SHARDED_SMALL[n]
            a = lax.dynamic_slice_in_dim(a, me * p[n].shape[ax], p[n].shape[ax], axis=ax)
        small_g.append(a)
    shapes = [p[n].shape for n in SMALL]
    res = adamw("adamw_small", _pack([p[n] for n in SMALL]), _pack([p["m_" + n] for n in SMALL]),
                _pack([p["v_" + n] for n in SMALL]), _pack(small_g))
    for k, r in enumerate(res):
        for n, a in zip(SMALL, _unpack(r, shapes)):
            out.setdefault(n, [None] * 4)[k] = a

    return (loss, dx[None]) + tuple(out[n][k] for k in range(4) for n in WEIGHTS)


def kernel(x, mem, ln_mix_pre, ln_mix_post, ln_mem, ln_xa_pre, ln_xa_post, ln_ffn_pre, ln_ffn_post, xa_wq, xa_wkv, xa_wo, ffn_w_in, ffn_conv_w, ffn_conv_b, ffn_w_out, ssd_w_in, ssd_conv_w, ssd_conv_b, ssd_dt_bias, ssd_a_log, ssd_d, ssd_norm, ssd_w_out, sg_w_in, sg_v_norm_g, sg_v_norm_b, sg_w_spatial, sg_b_spatial, sg_w_out, sb_w_qkv, sb_w_out, loss_target, m_ln_mix_pre, m_ln_mix_post, m_ln_mem, m_ln_xa_pre, m_ln_xa_post, m_ln_ffn_pre, m_ln_ffn_post, m_xa_wq, m_xa_wkv, m_xa_wo, m_ffn_w_in, m_ffn_conv_w, m_ffn_conv_b, m_ffn_w_out, m_ssd_w_in, m_ssd_conv_w, m_ssd_conv_b, m_ssd_dt_bias, m_ssd_a_log, m_ssd_d, m_ssd_norm, m_ssd_w_out, m_sg_w_in, m_sg_v_norm_g, m_sg_v_norm_b, m_sg_w_spatial, m_sg_b_spatial, m_sg_w_out, m_sb_w_qkv, m_sb_w_out, v_ln_mix_pre, v_ln_mix_post, v_ln_mem, v_ln_xa_pre, v_ln_xa_post, v_ln_ffn_pre, v_ln_ffn_post, v_xa_wq, v_xa_wkv, v_xa_wo, v_ffn_w_in, v_ffn_conv_w, v_ffn_conv_b, v_ffn_w_out, v_ssd_w_in, v_ssd_conv_w, v_ssd_conv_b, v_ssd_dt_bias, v_ssd_a_log, v_ssd_d, v_ssd_norm, v_ssd_w_out, v_sg_w_in, v_sg_v_norm_g, v_sg_v_norm_b, v_sg_w_spatial, v_sg_b_spatial, v_sg_w_out, v_sb_w_qkv, v_sb_w_out):
    return _step(dict(locals()))
```

```python
import functools
import math

import jax
import jax.numpy as jnp
from jax import lax
from jax.experimental import pallas as pl
from jax.experimental.pallas import tpu as pltpu

F32 = jnp.float32
BF16 = jnp.bfloat16
EPS = 1e-6
LANES = 128
VMEM_LIMIT = 56 * 1024 * 1024
CHUNK = 128
SSD_HEAD_DIM = 64
SSD_STATE = 128
N_CHIPS = 4
MESH = pl.DeviceIdType.MESH
ANY = pl.BlockSpec(memory_space=pl.ANY)

ADAM_LR, ADAM_B1, ADAM_B2, ADAM_EPS, ADAM_WD, ADAM_STEP = 0.001, 0.9, 0.999, 1e-08, 0.01, 10


def _params(sem):
    return pltpu.CompilerParams(dimension_semantics=sem, vmem_limit_bytes=VMEM_LIMIT)


def _sds(shape, dtype):
    return jax.ShapeDtypeStruct(tuple(shape), dtype)


def _tile(n, pref):
    if n <= pref:
        return n
    t = (pref // LANES) * LANES
    while t > LANES and n % t:
        t -= LANES
    assert n % t == 0, (n, pref)
    return t


def _split3(a):
    a1 = a.astype(BF16)
    r = a - a1.astype(F32)
    a2 = r.astype(BF16)
    a3 = (r - a2.astype(F32)).astype(BF16)
    return a1, a2, a3


def _dot(a, b, dims=(((1,), (0,)), ((), ()))):
    return lax.dot_general(a, b, dims, preferred_element_type=F32)


NN = (((1,), (0,)), ((), ()))
NT = (((1,), (1,)), ((), ()))
TN = (((0,), (0,)), ((), ()))


def _dot3r(a, m):
    p1, p2, p3 = _split3(a)
    return _dot(p1, m) + _dot(p2, m) + _dot(p3, m)


def _dot3l(m, a, dims=NN):
    p1, p2, p3 = _split3(a)
    return _dot(m, p1, dims) + _dot(m, p2, dims) + _dot(m, p3, dims)


def _iota(shape, dim):
    return lax.broadcasted_iota(jnp.int32, shape, dim)


def _tri(n, kind):
    r, c = _iota((n, n), 0), _iota((n, n), 1)
    return {"le": c <= r, "lt": c < r, "ge": c >= r, "gt": c > r}[kind]


def _sigmoid(x):
    return 1.0 / (1.0 + jnp.exp(-x))


def _silu(x):
    return x * _sigmoid(x)


def _dsilu(x):
    s = _sigmoid(x)
    return s * (1.0 + x * (1.0 - s))


_GC = math.sqrt(2.0 / math.pi)


def _gelu(x):
    return 0.5 * x * (1.0 + jnp.tanh(_GC * (x + 0.044715 * x * x * x)))


def _dgelu(x):
    th = jnp.tanh(_GC * (x + 0.044715 * x * x * x))
    return 0.5 * (1.0 + th) + 0.5 * x * (1.0 - th * th) * _GC * (1.0 + 3.0 * 0.044715 * x * x)


def _softplus(x):
    return jnp.maximum(x, 0.0) + jnp.log(1.0 + jnp.exp(-jnp.abs(x)))


def _mm(name, mode, a, b, out_sds, grid, a_spec, b_spec, o_spec, acc_shape, add=None, add_spec=None):
    dims = {"nn": NN, "nt": NT, "tn": TN}[mode]
    nk = grid[2]
    has_add = add is not None

    def body(*refs):
        if has_add:
            a_ref, b_ref, c_ref, o_ref = refs[:4]
        else:
            a_ref, b_ref, o_ref = refs[:3]
            c_ref = None
        part = lax.dot_general(a_ref[...], b_ref[...], dims, preferred_element_type=F32)

        def finish(r):
            if c_ref is not None:
                r = r + c_ref[...].astype(F32)
            o_ref[...] = r.astype(o_ref.dtype)

        if nk == 1:
            finish(part)
        else:
            acc = refs[-1]
            k = pl.program_id(2)

            @pl.when(k == 0)
            def _():
                acc[...] = part

            @pl.when(k > 0)
            def _():
                acc[...] += part

            @pl.when(k == nk - 1)
            def _():
                finish(acc[...])

    in_specs = [a_spec, b_spec] + ([add_spec] if has_add else [])
    args = (a, b) + ((add,) if has_add else ())
    return pl.pallas_call(
        body, name=name, grid=grid, in_specs=in_specs, out_specs=o_spec, out_shape=out_sds,
        scratch_shapes=[pltpu.VMEM(acc_shape, F32)] if nk > 1 else [],
        compiler_params=_params(("parallel", "parallel", "arbitrary")),
    )(*args)


class W:
    def __init__(self, kind, arr, layer, shape=None):
        self.kind, self._arr, self.layer = kind, arr, layer
        shape = arr.shape if shape is None else shape
        if kind == "cols":
            s, _, k, c = shape
            self.K, self.N, self.S, self.C = k, s * c, s, c
        elif kind == "rows":
            s, _, r, n = shape
            self.K, self.N, self.S, self.R = s * r, n, s, r
        else:
            _, k, n = shape
            self.K, self.N = k, n

    def get(self, operand):
        if callable(self._arr):
            self._arr = self._arr(operand)
        return self._arr


def mm_fwd(name, a, w, out_dtype, tm=1024, tn=1536, a_spec=None, out_sds=None, o_spec=None, add=None):
    M = a.shape[0]
    tm = min(tm, M)
    l = w.layer
    if w.kind == "cols":
        tn = _tile(w.C, tn)
        nps = w.C // tn
        tk, nk = w.K, 1
        b_spec = pl.BlockSpec((None, None, tk, tn), lambda i, j, k: (j // nps, l, 0, j % nps))
    elif w.kind == "rows":
        tn = _tile(w.N, tn)
        tk, nk = w.R, w.S
        b_spec = pl.BlockSpec((None, None, tk, tn), lambda i, j, k: (k, l, 0, j))
    else:
        tn = _tile(w.N, tn)
        tk, nk = w.K, 1
        b_spec = pl.BlockSpec((None, tk, tn), lambda i, j, k: (l, 0, j))
    grid = (M // tm, w.N // tn, nk)
    if a_spec is None:
        a_spec = pl.BlockSpec((tm, tk), lambda i, j, k: (i, k))
    if out_sds is None:
        out_sds = _sds((M, w.N), out_dtype)
        o_spec = pl.BlockSpec((tm, tn), lambda i, j, k: (i, j))
    else:
        o_spec = o_spec(tm, tn)
    add_spec = pl.BlockSpec((tm, tn), lambda i, j, k: (i, j)) if add is not None else None
    return _mm(name, "nn", a, w.get(a), out_sds, grid, a_spec, b_spec, o_spec, (tm, tn), add, add_spec)


def mm_dx(name, dy, w, out_dtype, tm=1024, tn=1024, a_spec=None, add=None):
    M = dy.shape[-2]
    tm = min(tm, M)
    l = w.layer
    if w.kind == "cols":
        tn = _tile(w.K, tn)
        tk, nk = w.C, w.S
        b_spec = pl.BlockSpec((None, None, tn, tk), lambda i, j, k: (k, l, j, 0))
    elif w.kind == "rows":
        tn = _tile(w.R, tn)
        npr = w.R // tn
        tk, nk = w.N, 1
        b_spec = pl.BlockSpec((None, None, tn, tk), lambda i, j, k: (j // npr, l, j % npr, 0))
    else:
        tn = _tile(w.K, tn)
        tk, nk = _tile(w.N, 2048), w.N // _tile(w.N, 2048)
        b_spec = pl.BlockSpec((None, tn, tk), lambda i, j, k: (l, j, k))
    grid = (M // tm, w.K // tn, nk)
    if a_spec is None:
        a_spec = pl.BlockSpec((tm, tk), lambda i, j, k: (i, k))
    else:
        a_spec = a_spec(tm, tk)
    out_sds = _sds((M, w.K), out_dtype)
    o_spec = pl.BlockSpec((tm, tn), lambda i, j, k: (i, j))
    add_spec = o_spec if add is not None else None
    return _mm(name, "nt", dy, w.get(dy), out_sds, grid, a_spec, b_spec, o_spec, (tm, tn), add, add_spec)


def mm_dw(name, a, dy, out_dtype, n_shards=None, tm=512, tn=1536, b_spec=None, N=None):
    T, K = a.shape
    N = dy.shape[-1] if N is None else N
    tm = _tile(K, tm)
    if n_shards:
        C = N // n_shards
        tn = _tile(C, tn)
        nps = C // tn
        out_sds = _sds((n_shards, K, C), out_dtype)
        o_spec = pl.BlockSpec((None, tm, tn), lambda i, j, k: (j // nps, i, j % nps))
    else:
        tn = _tile(N, tn)
        out_sds = _sds((K, N), out_dtype)
        o_spec = pl.BlockSpec((tm, tn), lambda i, j, k: (i, j))
    grid = (K // tm, N // tn, 1)
    a_spec = pl.BlockSpec((T, tm), lambda i, j, k: (0, i))
    if b_spec is None:
        b_spec = pl.BlockSpec((T, tn), lambda i, j, k: (0, j))
    else:
        b_spec = b_spec(T, tn)
    return _mm(name, "tn", a, dy, out_sds, grid, a_spec, b_spec, o_spec, (tm, tn))


def _rms(x, g):
    r = lax.rsqrt(jnp.mean(x * x, axis=-1, keepdims=True) + EPS)
    return x * r * g


def rms_fwd(name, x, g, tr=512, after=None):
    T, D = x.shape
    tr = min(tr, T)

    def body(x_ref, g_ref, *rest):
        o_ref = rest[-1]
        o_ref[...] = _rms(x_ref[...], g_ref[...]).astype(o_ref.dtype)

    row = pl.BlockSpec((tr, D), lambda i: (i, 0))
    vec = pl.BlockSpec((1, D), lambda i: (0, 0))
    extra = [] if after is None else [after]
    return pl.pallas_call(body, name=name, grid=(T // tr,), in_specs=[row, vec] + [ANY] * len(extra), out_specs=row,
                          out_shape=_sds((T, D), BF16), compiler_params=_params(("parallel",)))(x, g, *extra)


def resid_norm(name, x, m, g_post, g_next, tr=512):
    T, D = x.shape
    tr = min(tr, T)
    has_next = g_next is not None

    def body(*refs):
        if has_next:
            x_ref, m_ref, gp_ref, gn_ref, xo_ref, h_ref = refs
        else:
            x_ref, m_ref, gp_ref, xo_ref = refs
        xn = x_ref[...] + _rms(m_ref[...], gp_ref[...])
        xo_ref[...] = xn
        if has_next:
            h_ref[...] = _rms(xn, gn_ref[...]).astype(h_ref.dtype)

    row = pl.BlockSpec((tr, D), lambda i: (i, 0))
    vec = pl.BlockSpec((1, D), lambda i: (0, 0))
    ins = [row, row, vec] + ([vec] if has_next else [])
    args = (x, m, g_post) + ((g_next,) if has_next else ())
    outs = [row, row] if has_next else row
    shp = [_sds((T, D), F32), _sds((T, D), BF16)] if has_next else _sds((T, D), F32)
    res = pl.pallas_call(body, name=name, grid=(T // tr,), in_specs=ins, out_specs=outs, out_shape=shp,
                         compiler_params=_params(("parallel",)))(*args)
    return res if has_next else (res, None)


def rms_bwd(name, xin, g, dy, resid, out_dtype, tr=512, after=None):
    T, D = xin.shape
    tr = min(tr, T)
    has_res = resid is not None

    def body(*refs):
        dx_ref, dg_ref = refs[-2:]
        if has_res:
            x_ref, g_ref, dy_ref, r_ref = refs[:4]
        else:
            x_ref, g_ref, dy_ref = refs[:3]
        x = x_ref[...].astype(F32)
        dy_ = dy_ref[...].astype(F32)
        r = lax.rsqrt(jnp.mean(x * x, axis=-1, keepdims=True) + EPS)
        xh = x * r
        dxh = dy_ * g_ref[...]
        dx = r * (dxh - xh * jnp.mean(dxh * xh, axis=-1, keepdims=True))
        if has_res:
            dx = dx + r_ref[...]
        dx_ref[...] = dx.astype(dx_ref.dtype)
        part = jnp.sum(dy_ * xh, axis=0, keepdims=True)

        @pl.when(pl.program_id(0) == 0)
        def _():
            dg_ref[...] = part

        @pl.when(pl.program_id(0) > 0)
        def _():
            dg_ref[...] += part

    row = pl.BlockSpec((tr, D), lambda i: (i, 0))
    vec = pl.BlockSpec((1, D), lambda i: (0, 0))
    ins = [row, vec, row] + ([row] if has_res else []) + ([] if after is None else [ANY])
    args = (xin, g, dy) + ((resid,) if has_res else ()) + (() if after is None else (after,))
    return pl.pallas_call(body, name=name, grid=(T // tr,), in_specs=ins, out_specs=[row, vec],
                          out_shape=[_sds((T, D), out_dtype), _sds((1, D), F32)],
                          compiler_params=_params(("arbitrary",)))(*args)


def loss_fwd_bwd(name, y, tgt, tr=512):
    T, D = y.shape
    tr = min(tr, T)

    def body(y_ref, t_ref, l_ref, d_ref):
        e = y_ref[...] - t_ref[...]
        d_ref[...] = e * (1.0 / D)
        part = 0.5 * jnp.sum(jnp.mean(e * e, axis=-1, keepdims=True), axis=0, keepdims=True)
        part = jnp.broadcast_to(part, l_ref.shape)

        @pl.when(pl.program_id(0) == 0)
        def _():
            l_ref[...] = part

        @pl.when(pl.program_id(0) > 0)
        def _():
            l_ref[...] += part

    row = pl.BlockSpec((tr, D), lambda i: (i, 0))
    return pl.pallas_call(body, name=name, grid=(T // tr,), in_specs=[row, row],
                          out_specs=[pl.BlockSpec((8, LANES), lambda i: (0, 0)), row],
                          out_shape=[_sds((8, LANES), F32), _sds((T, D), F32)],
                          compiler_params=_params(("arbitrary",)))(y, tgt)


CONV_ROWS = 256
HALO = 8


def _ext_rows(ref, r0, rb, T):
    top = ref[pl.ds(pl.multiple_of(jnp.maximum(r0 - HALO, 0), HALO), HALO), :]
    bot = ref[pl.ds(pl.multiple_of(jnp.minimum(r0 + rb, T - HALO), HALO), HALO), :]
    return jnp.concatenate([jnp.where(r0 > 0, top, 0.0), ref[pl.ds(r0, rb), :], jnp.where(r0 + rb < T, bot, 0.0)], axis=0)


def _conv_ext(e, w_ref, K):
    taps = [e] + [pltpu.roll(e, s, 0) for s in range(1, K)]
    out = taps[0] * w_ref[pl.ds(K - 1, 1), :]
    for s in range(1, K):
        out = out + taps[s] * w_ref[pl.ds(K - 1 - s, 1), :]
    return taps, out


def _conv_ext_bwd(dpre, taps, w_ref, K, rb):
    n = dpre.shape[0]
    own = slice(HALO, HALO + rb)
    d_own = dpre[own]
    dws = [jnp.sum(d_own * taps[s][own], axis=0, keepdims=True) for s in range(K)]
    dp = d_own * w_ref[pl.ds(K - 1, 1), :]
    for s in range(1, K):
        dp = dp + pltpu.roll(dpre, n - s, 0)[own] * w_ref[pl.ds(K - 1 - s, 1), :]
    return dp, dws, jnp.sum(d_own, axis=0, keepdims=True)


def _store_conv_grads(acc, dw_ref, db_ref, K):
    for s in range(K):
        dw_ref[pl.ds(K - 1 - s, 1), :] = acc[s]
    db_ref[...] = acc[K]


def ffn_gate_fwd(name, P, cw, cb, tc=LANES):
    _, T, F = P.shape
    K = cw.shape[0]
    nf = F // tc
    rb = min(CONV_ROWS, T)

    def body(pg_ref, pu_ref, wg_ref, wu_ref, bg_ref, bu_ref, o_ref):
        def blk(bi, carry):
            r0 = pl.multiple_of(bi * rb, rb)
            _, g = _conv_ext(_ext_rows(pg_ref, r0, rb, T), wg_ref, K)
            _, u = _conv_ext(_ext_rows(pu_ref, r0, rb, T), wu_ref, K)
            own = slice(HALO, HALO + rb)
            o_ref[pl.ds(r0, rb), :] = (_gelu(g[own] + bg_ref[...]) * (u[own] + bu_ref[...])).astype(o_ref.dtype)
            return carry

        lax.fori_loop(0, T // rb, blk, 0)

    pg = pl.BlockSpec((None, T, tc), lambda j: (0, 0, j))
    pu = pl.BlockSpec((None, T, tc), lambda j: (1, 0, j))
    wg = pl.BlockSpec((K, tc), lambda j: (0, j))
    wu = pl.BlockSpec((K, tc), lambda j: (0, j + nf))
    bg = pl.BlockSpec((1, tc), lambda j: (0, j))
    bu = pl.BlockSpec((1, tc), lambda j: (0, j + nf))
    return pl.pallas_call(body, name=name, grid=(nf,), in_specs=[pg, pu, wg, wu, bg, bu],
                          out_specs=pl.BlockSpec((T, tc), lambda j: (0, j)), out_shape=_sds((T, F), BF16),
                          compiler_params=_params(("parallel",)))(P, P, cw, cw, cb, cb)


def ffn_gate_bwd(name, P, da, cw, cb, tc=LANES):
    _, T, F = P.shape
    K = cw.shape[0]
    nf = F // tc

    rb = min(CONV_ROWS, T)

    def body(pg_ref, pu_ref, da_ref, wg_ref, wu_ref, bg_ref, bu_ref, dp_ref, dwg_ref, dwu_ref, dbg_ref, dbu_ref):
        def blk(bi, acc):
            r0 = pl.multiple_of(bi * rb, rb)
            tg, g = _conv_ext(_ext_rows(pg_ref, r0, rb, T), wg_ref, K)
            tu, u = _conv_ext(_ext_rows(pu_ref, r0, rb, T), wu_ref, K)
            g = g + bg_ref[...]
            u = u + bu_ref[...]
            da_ = _ext_rows(da_ref, r0, rb, T)
            dpg, dwg, dbg = _conv_ext_bwd(da_ * u * _dgelu(g), tg, wg_ref, K, rb)
            dpu, dwu, dbu = _conv_ext_bwd(da_ * _gelu(g), tu, wu_ref, K, rb)
            dp_ref[0, pl.ds(r0, rb), :] = dpg.astype(dp_ref.dtype)
            dp_ref[1, pl.ds(r0, rb), :] = dpu.astype(dp_ref.dtype)
            return tuple(a + b for a, b in zip(acc, dwg + [dbg] + dwu + [dbu]))

        acc = lax.fori_loop(0, T // rb, blk, tuple(jnp.zeros((1, tc), F32) for _ in range(2 * K + 2)))
        _store_conv_grads(acc[:K + 1], dwg_ref, dbg_ref, K)
        _store_conv_grads(acc[K + 1:], dwu_ref, dbu_ref, K)

    pg = pl.BlockSpec((None, T, tc), lambda j: (0, 0, j))
    pu = pl.BlockSpec((None, T, tc), lambda j: (1, 0, j))
    col = pl.BlockSpec((T, tc), lambda j: (0, j))
    wg = pl.BlockSpec((K, tc), lambda j: (0, j))
    wu = pl.BlockSpec((K, tc), lambda j: (0, j + nf))
    bg = pl.BlockSpec((1, tc), lambda j: (0, j))
    bu = pl.BlockSpec((1, tc), lambda j: (0, j + nf))
    return pl.pallas_call(
        body, name=name, grid=(nf,), in_specs=[pg, pu, col, wg, wu, bg, bu],
        out_specs=[pl.BlockSpec((2, T, tc), lambda j: (0, 0, j)), wg, wg, bg, bg],
        out_shape=[_sds((2, T, F), BF16), _sds((K, F), F32), _sds((K, F), F32), _sds((1, F), F32), _sds((1, F), F32)],
        compiler_params=_params(("parallel",)))(P, P, da, cw, cw, cb, cb)


def xattn_fwd(name, q, kv, n_heads, tq=512):
    T, Wd = q.shape
    Mm = kv.shape[0]
    hd = Wd // n_heads
    scale = hd ** -0.5
    tq = min(tq, T)

    def body(q_ref, kv_ref, o_ref):
        for h in range(n_heads):
            qh = q_ref[:, h * hd:(h + 1) * hd]
            kh = kv_ref[:, h * hd:(h + 1) * hd]
            vh = kv_ref[:, Wd + h * hd:Wd + (h + 1) * hd]
            s = _dot(qh, kh, NT) * scale
            s = s - jnp.max(s, axis=-1, keepdims=True)
            e = jnp.exp(s)
            p = e / jnp.sum(e, axis=-1, keepdims=True)
            o_ref[:, h * hd:(h + 1) * hd] = _dot(p.astype(BF16), vh).astype(o_ref.dtype)

    return pl.pallas_call(body, name=name, grid=(T // tq,),
                          in_specs=[pl.BlockSpec((tq, Wd), lambda i: (i, 0)), pl.BlockSpec((Mm, 2 * Wd), lambda i: (0, 0))],
                          out_specs=pl.BlockSpec((tq, Wd), lambda i: (i, 0)), out_shape=_sds((T, Wd), BF16),
                          compiler_params=_params(("parallel",)))(q, kv)


def xattn_bwd(name, q, kv, do, n_heads, tq=512):
    T, Wd = q.shape
    Mm = kv.shape[0]
    hd = Wd // n_heads
    scale = hd ** -0.5
    tq = min(tq, T)

    def body(q_ref, kv_ref, do_ref, dq_ref, dkv_ref):
        @pl.when(pl.program_id(0) == 0)
        def _():
            dkv_ref[...] = jnp.zeros_like(dkv_ref)

        for h in range(n_heads):
            sl = slice(h * hd, (h + 1) * hd)
            sv = slice(Wd + h * hd, Wd + (h + 1) * hd)
            qh, kh, vh = q_ref[:, sl], kv_ref[:, sl], kv_ref[:, sv]
            doh = do_ref[:, sl].astype(BF16)
            s = _dot(qh, kh, NT) * scale
            s = s - jnp.max(s, axis=-1, keepdims=True)
            e = jnp.exp(s)
            p = e / jnp.sum(e, axis=-1, keepdims=True)
            dp = _dot(doh, vh, NT)
            ds = (p * (dp - jnp.sum(dp * p, axis=-1, keepdims=True)) * scale).astype(BF16)
            dq_ref[:, sl] = _dot(ds, kh).astype(dq_ref.dtype)
            dkv_ref[:, sl] += _dot(ds, qh, TN)
            dkv_ref[:, sv] += _dot(p.astype(BF16), doh, TN)

    row = pl.BlockSpec((tq, Wd), lambda i: (i, 0))
    full = pl.BlockSpec((Mm, 2 * Wd), lambda i: (0, 0))
    return pl.pallas_call(body, name=name, grid=(T // tq,), in_specs=[row, full, row], out_specs=[row, full],
                          out_shape=[_sds((T, Wd), BF16), _sds((Mm, 2 * Wd), F32)],
                          compiler_params=_params(("arbitrary",)))(q, kv, do)


def ffn_fwd(hf, w_in, cw, cb, w_out):
    T = hf.shape[0]
    F = w_out.K
    tn = _tile(w_in.C, 1536)
    nfp = F // tn
    P = mm_fwd("ffn_in", hf, w_in, F32, tn=tn, out_sds=_sds((2, T, F), F32),
               o_spec=lambda tm, tn_: pl.BlockSpec((None, tm, tn_), lambda i, j, k: (j // nfp, i, j % nfp)))
    a = ffn_gate_fwd("ffn_gate", P, cw, cb)
    f = mm_fwd("ffn_out", a, w_out, F32, tn=2048)
    return f, (P, a)


def ffn_bwd(hf, saved, df, w_in, cw, cb, w_out):
    P, a = saved
    T = hf.shape[0]
    F = w_out.K
    C = w_in.C
    da = mm_dx("ffn_out_dx", df, w_out, F32, tn=w_out.R)
    dw_out = mm_dw("ffn_out_dw", a, df, BF16, tn=1024)
    dP, dcw_g, dcw_u, dcb_g, dcb_u = ffn_gate_bwd("ffn_gate_bwd", P, da, cw, cb)
    dw_in = mm_dw("ffn_in_dw", hf, dP, BF16, n_shards=w_in.S, N=2 * F,
                  b_spec=lambda T_, tn: pl.BlockSpec((None, T_, tn), lambda i, j, k: (j // (F // tn), 0, j % (F // tn))))
    per = F // C
    dhf = mm_dx("ffn_in_dx", dP, w_in, F32,
                a_spec=lambda tm, tk: pl.BlockSpec((None, tm, tk), lambda i, j, k: (k // per, i, k % per)))
    grads = dict(ffn_w_in=dw_in, ffn_w_out=dw_out, ffn_conv_w=jnp.concatenate([dcw_g, dcw_u], axis=1),
                 ffn_conv_b=jnp.concatenate([dcb_g, dcb_u], axis=1))
    return dhf, grads


def xa_fwd(hq, mem_n, wq, wkv, wo, n_heads):
    q = mm_fwd("xa_q", hq, wq, BF16)
    kv = mm_fwd("xa_kv", mem_n, wkv, BF16)
    o = xattn_fwd("xa_core", q, kv, n_heads)
    c = mm_fwd("xa_o", o, wo, F32)
    return c, (q, kv, o)


def xa_bwd(hq, mem_n, saved, dc, wq, wkv, wo, n_heads):
    q, kv, o = saved
    do = mm_dx("xa_o_dx", dc, wo, F32)
    dwo = mm_dw("xa_o_dw", o, dc, BF16, n_shards=wo.S)
    dq, dkv = xattn_bwd("xa_core_bwd", q, kv, do, n_heads)
    dkv = dkv.astype(BF16)
    dwq = mm_dw("xa_q_dw", hq, dq, BF16)
    dhq = mm_dx("xa_q_dx", dq, wq, F32, tn=wq.R)
    dwkv = mm_dw("xa_kv_dw", mem_n, dkv, BF16)
    dmem_n = mm_dx("xa_kv_dx", dkv, wkv, F32, tn=wkv.R)
    return dhq, dmem_n, dict(xa_wq=dwq, xa_wkv=dwkv, xa_wo=dwo)


def _sb_logits(q, kblk, scale):
    z = _dot(q, kblk, NT) * scale
    l1 = -_softplus(z)
    return z, l1, z + l1


def _split2(a):
    a1 = a.astype(BF16)
    return a1, (a - a1.astype(F32)).astype(BF16)


def _dot2r(a, m):
    p1, p2 = _split2(a)
    return _dot(p1, m) + _dot(p2, m)


SB_TQ = 1024


def sb_fwd(name, qkv, n_heads, tq=SB_TQ):
    T = qkv.shape[0]
    hd = qkv.shape[1] // (3 * n_heads)
    scale = hd ** -0.5
    Q = CHUNK
    tq = min(tq, T)
    nb = tq // Q
    unroll = 4 if nb % 4 == 0 else (2 if nb % 2 == 0 else 1)

    def body(q_ref, k_ref, v_ref, o_ref, lt_ref):
        i = pl.program_id(1)
        q = q_ref[...]
        mrev = _tri(Q, "lt").astype(BF16)

        def block(kb, carry, q_, masked):
            c, acc = carry
            off = pl.multiple_of(kb * Q, Q)
            kblk, vblk = k_ref[pl.ds(off, Q), :], v_ref[pl.ds(off, Q), :]
            _, l1, lb = _sb_logits(q_, kblk, scale)
            if masked:
                valid = _iota(l1.shape, 1) < _iota(l1.shape, 0)
                l1 = jnp.where(valid, l1, 0.0)
            a = jnp.exp(lb + _dot2r(l1, mrev) + c)
            if masked:
                a = jnp.where(valid, a, 0.0)
            return c + jnp.sum(l1, axis=1, keepdims=True), acc + _dot(a.astype(BF16), vblk)

        c, acc = jnp.zeros((tq, Q), F32), jnp.zeros((tq, hd), F32)
        for b in reversed(range(nb)):
            lo = b * Q
            cb, ab = block(i * nb + b, (c[lo:], acc[lo:]), q[lo:], True)
            c = cb if lo == 0 else jnp.concatenate([c[:lo], cb], axis=0)
            acc = ab if lo == 0 else jnp.concatenate([acc[:lo], ab], axis=0)

        def step(r, cr):
            for u in range(unroll):
                cr = block(i * nb - 1 - unroll * r - u, cr, q, False)
            return cr

        c, acc = lax.fori_loop(0, i * (nb // unroll), step, (c, acc))
        o_ref[...] = acc.astype(o_ref.dtype)
        lt_ref[...] = c

    H = n_heads
    return pl.pallas_call(
        body, name=name, grid=(H, T // tq),
        in_specs=[pl.BlockSpec((tq, hd), lambda h, i: (i, h)), pl.BlockSpec((T, hd), lambda h, i: (0, H + h)),
                  pl.BlockSpec((T, hd), lambda h, i: (0, 2 * H + h))],
        out_specs=[pl.BlockSpec((tq, hd), lambda h, i: (i, h)), pl.BlockSpec((None, tq, Q), lambda h, i: (h, i, 0))],
        out_shape=[_sds((T, H * hd), BF16), _sds((H, T, Q), F32)],
        compiler_params=_params(("parallel", "arbitrary")))(qkv, qkv, qkv)


def sb_bwd(name, qkv, do, lt, n_heads, tq=SB_TQ):
    T = qkv.shape[0]
    hd = qkv.shape[1] // (3 * n_heads)
    scale = hd ** -0.5
    Q = CHUNK
    tq = min(tq, T)
    nb = tq // Q
    unroll = 4 if nb % 4 == 0 else (2 if nb % 2 == 0 else 1)

    def body(q_ref, k_ref, v_ref, do_ref, lt_ref, dq_ref, dk_ref, dv_ref):
        i = pl.program_id(1)

        @pl.when(i == 0)
        def _():
            dk_ref[...] = jnp.zeros_like(dk_ref)
            dv_ref[...] = jnp.zeros_like(dv_ref)

        q, do_, ltot = q_ref[...], do_ref[...], lt_ref[...]
        mrev = _tri(Q, "lt").astype(BF16)
        mfwd = _tri(Q, "gt").astype(BF16)

        def block(kb, carry, q_, d_, lt_, masked):
            pin, pre, dq = carry
            off = pl.multiple_of(kb * Q, Q)
            kblk, vblk = k_ref[pl.ds(off, Q), :], v_ref[pl.ds(off, Q), :]
            _, l1, lb = _sb_logits(q_, kblk, scale)
            if masked:
                valid = _iota(l1.shape, 1) < _iota(l1.shape, 0)
                l1 = jnp.where(valid, l1, 0.0)
            pin = pin + jnp.sum(l1, axis=1, keepdims=True)
            a = jnp.exp(lb + _dot2r(l1, mrev) + (lt_ - pin))
            if masked:
                a = jnp.where(valid, a, 0.0)
            de = _dot(d_, vblk, NT) * a
            dl1 = pre + _dot2r(de, mfwd)
            pre = pre + jnp.sum(de, axis=1, keepdims=True)
            sig = jnp.exp(lb)
            dz = (de * (1.0 - sig) - dl1 * sig) * scale
            if masked:
                dz = jnp.where(valid, dz, 0.0)
            dzb = dz.astype(BF16)
            dk_ref[pl.ds(off, Q), :] += _dot(dzb, q_, TN)
            dv_ref[pl.ds(off, Q), :] += _dot(a.astype(BF16), d_, TN)
            return pin, pre, dq + _dot(dzb, kblk)

        def step(r, cr):
            for u in range(unroll):
                cr = block(unroll * r + u, cr, q, do_, ltot, False)
            return cr

        init = (jnp.zeros((tq, Q), F32), jnp.zeros((tq, Q), F32), jnp.zeros((tq, hd), F32))
        carry = lax.fori_loop(0, i * (nb // unroll), step, init)
        for b in range(nb):
            lo = b * Q
            part = block(i * nb + b, tuple(a[lo:] for a in carry), q[lo:], do_[lo:], ltot[lo:], True)
            carry = part if lo == 0 else tuple(jnp.concatenate([a[:lo], pb], axis=0) for a, pb in zip(carry, part))
        dq_ref[...] = carry[2].astype(dq_ref.dtype)

    H = n_heads
    qs = pl.BlockSpec((tq, hd), lambda h, i: (i, h))
    full = pl.BlockSpec((T, hd), lambda h, i: (0, h))
    return pl.pallas_call(
        body, name=name, grid=(H, T // tq),
        in_specs=[qs, pl.BlockSpec((T, hd), lambda h, i: (0, H + h)), pl.BlockSpec((T, hd), lambda h, i: (0, 2 * H + h)),
                  qs, pl.BlockSpec((None, tq, Q), lambda h, i: (h, i, 0))],
        out_specs=[qs, full, full],
        out_shape=[_sds((T, H * hd), BF16), _sds((T, H * hd), F32), _sds((T, H * hd), F32)],
        compiler_params=_params(("parallel", "arbitrary")))(qkv, qkv, qkv, do, lt)


def sb_mixer_fwd(hn, w_qkv, w_out, n_heads):
    qkv = mm_fwd("sb_qkv", hn, w_qkv, BF16)
    o, lt = sb_fwd("sb_core", qkv, n_heads)
    m = mm_fwd("sb_out", o, w_out, F32, tn=1024)
    return m, (qkv, o, lt)


def sb_mixer_bwd(hn, saved, dm, w_qkv, w_out, n_heads):
    qkv, o, lt = saved
    do = mm_dx("sb_out_dx", dm, w_out, BF16, tn=w_out.R)
    dw_out = mm_dw("sb_out_dw", o, dm, BF16, tn=1024)
    dq, dk, dv = sb_bwd("sb_core_bwd", qkv, do, lt, n_heads)
    dqkv = jnp.concatenate([dq, dk.astype(BF16), dv.astype(BF16)], axis=1)
    dw_qkv = mm_dw("sb_qkv_dw", hn, dqkv, BF16, n_shards=w_qkv.S)
    dhn = mm_dx("sb_qkv_dx", dqkv, w_qkv, F32)
    return dhn, dict(sb_w_qkv=dw_qkv, sb_w_out=dw_out)


def _sgu_common(p_ref, vg_ref, vb_ref, Wd):
    pu, pv = p_ref[:, :Wd], p_ref[:, Wd:]
    u, v = _gelu(pu), _gelu(pv)
    xc = v - jnp.mean(v, axis=-1, keepdims=True)
    r = lax.rsqrt(jnp.mean(xc * xc, axis=-1, keepdims=True) + EPS)
    xh = xc * r
    return pu, pv, u, xh, r, xh * vg_ref[...] + vb_ref[...]


def sgu_fwd(name, P, vg, vb, ws, bexp):
    T = P.shape[0]
    Wd = P.shape[1] // 2
    G = ws.shape[0]
    gw = Wd // G
    Q = CHUNK

    def body(p_ref, vg_ref, vb_ref, ws_ref, be_ref, o_ref):
        _, _, u, _, _, vn = _sgu_common(p_ref, vg_ref, vb_ref, Wd)
        tril = _tri(Q, "le")
        for g in range(G):
            sl = slice(g * gw, (g + 1) * gw)
            wsg = jnp.where(tril, ws_ref[g], 0.0).astype(BF16)
            mixed = _dot(wsg, vn[:, sl].astype(BF16)) + be_ref[:, sl]
            o_ref[:, sl] = (u[:, sl] * mixed).astype(o_ref.dtype)

    vec = pl.BlockSpec((1, Wd), lambda c: (0, 0))
    return pl.pallas_call(
        body, name=name, grid=(T // Q,),
        in_specs=[pl.BlockSpec((Q, 2 * Wd), lambda c: (c, 0)), vec, vec, pl.BlockSpec((G, Q, Q), lambda c: (0, 0, 0)),
                  pl.BlockSpec((Q, Wd), lambda c: (0, 0))],
        out_specs=pl.BlockSpec((Q, Wd), lambda c: (c, 0)), out_shape=_sds((T, Wd), BF16),
        compiler_params=_params(("parallel",)))(P, vg, vb, ws, bexp)


def sgu_bwd(name, P, dgated, vg, vb, ws, bexp):
    T = P.shape[0]
    Wd = P.shape[1] // 2
    G = ws.shape[0]
    gw = Wd // G
    Q = CHUNK
    nc = T // Q

    def body(p_ref, dg_ref, vg_ref, vb_ref, ws_ref, be_ref, dp_ref, dws_ref, dvg_ref, dvb_ref, dbs_ref, dvn_scr, dbe_scr):
        c = pl.program_id(0)

        @pl.when(c == 0)
        def _():
            dws_ref[...] = jnp.zeros_like(dws_ref)
            dvg_ref[...] = jnp.zeros_like(dvg_ref)
            dvb_ref[...] = jnp.zeros_like(dvb_ref)
            dbe_scr[...] = jnp.zeros_like(dbe_scr)

        pu, pv, u, xh, r, vn = _sgu_common(p_ref, vg_ref, vb_ref, Wd)
        tril = _tri(Q, "le")
        for g in range(G):
            sl = slice(g * gw, (g + 1) * gw)
            wsg = jnp.where(tril, ws_ref[g], 0.0).astype(BF16)
            vng = vn[:, sl].astype(BF16)
            mixed = _dot(wsg, vng) + be_ref[:, sl]
            dgt = dg_ref[:, sl]
            dp_ref[:, sl] = (dgt * mixed * _dgelu(pu[:, sl])).astype(dp_ref.dtype)
            dmix = dgt * u[:, sl]
            dmb = dmix.astype(BF16)
            dws_ref[g] += jnp.where(tril, _dot(dmb, vng, NT), 0.0)
            dvn_scr[:, sl] = _dot(wsg, dmb, TN)
            dbe_scr[:, sl] += dmix
        dvn = dvn_scr[...]
        dvg_ref[...] += jnp.sum(dvn * xh, axis=0, keepdims=True)
        dvb_ref[...] += jnp.sum(dvn, axis=0, keepdims=True)
        dxh = dvn * vg_ref[...]
        dv = r * (dxh - jnp.mean(dxh, axis=-1, keepdims=True) - xh * jnp.mean(dxh * xh, axis=-1, keepdims=True))
        dp_ref[:, Wd:] = (dv * _dgelu(pv)).astype(dp_ref.dtype)

        @pl.when(c == nc - 1)
        def _():
            sel = (_iota((Wd, LANES), 0) // gw == _iota((Wd, LANES), 1)).astype(BF16)
            dbs_ref[...] = _dot3r(dbe_scr[...], sel)

    vec = pl.BlockSpec((1, Wd), lambda c: (0, 0))
    wsb = pl.BlockSpec((G, Q, Q), lambda c: (0, 0, 0))
    return pl.pallas_call(
        body, name=name, grid=(nc,),
        in_specs=[pl.BlockSpec((Q, 2 * Wd), lambda c: (c, 0)), pl.BlockSpec((Q, Wd), lambda c: (c, 0)), vec, vec, wsb,
                  pl.BlockSpec((Q, Wd), lambda c: (0, 0))],
        out_specs=[pl.BlockSpec((Q, 2 * Wd), lambda c: (c, 0)), wsb, vec, vec, pl.BlockSpec((Q, LANES), lambda c: (0, 0))],
        out_shape=[_sds((T, 2 * Wd), BF16), _sds((G, Q, Q), F32), _sds((1, Wd), F32), _sds((1, Wd), F32), _sds((Q, LANES), F32)],
        scratch_shapes=[pltpu.VMEM((Q, Wd), F32), pltpu.VMEM((Q, Wd), F32)],
        compiler_params=_params(("arbitrary",)))(P, dgated, vg, vb, ws, bexp)


def sg_mixer_fwd(hn, w_in, vg, vb, ws, bs, w_out):
    G = ws.shape[0]
    Wd = vg.shape[1]
    P = mm_fwd("sg_in", hn, w_in, F32)
    bexp = jnp.repeat(bs.T, Wd // G, axis=1)
    gated = sgu_fwd("sg_core", P, vg, vb, ws, bexp)
    m = mm_fwd("sg_out", gated, w_out, F32, tn=1024)
    return m, (P, bexp, gated)


def sg_mixer_bwd(hn, saved, dm, w_in, vg, vb, ws, w_out):
    P, bexp, gated = saved
    G = ws.shape[0]
    dgated = mm_dx("sg_out_dx", dm, w_out, F32, tn=w_out.R)
    dw_out = mm_dw("sg_out_dw", gated, dm, BF16, tn=1024)
    dP, dws, dvg, dvb, dbs = sgu_bwd("sg_core_bwd", P, dgated, vg, vb, ws, bexp)
    dw_in = mm_dw("sg_in_dw", hn, dP, BF16, n_shards=w_in.S)
    dhn = mm_dx("sg_in_dx", dP, w_in, F32)
    grads = dict(sg_w_in=dw_in, sg_w_out=dw_out, sg_w_spatial=dws, sg_v_norm_g=dvg, sg_v_norm_b=dvb,
                 sg_b_spatial=dbs[:, :G].T)
    return dhn, grads


def ssd_conv_fwd(name, xbc, cw, cb, tc=LANES):
    T, Cd = xbc.shape
    K = cw.shape[0]
    rb = min(CONV_ROWS, T)

    def body(p_ref, w_ref, b_ref, o_ref):
        def blk(bi, carry):
            r0 = pl.multiple_of(bi * rb, rb)
            _, pre = _conv_ext(_ext_rows(p_ref, r0, rb, T), w_ref, K)
            o_ref[pl.ds(r0, rb), :] = _silu(pre[HALO:HALO + rb] + b_ref[...])
            return carry

        lax.fori_loop(0, T // rb, blk, 0)

    col = pl.BlockSpec((T, tc), lambda j: (0, j))
    return pl.pallas_call(body, name=name, grid=(Cd // tc,),
                          in_specs=[col, pl.BlockSpec((K, tc), lambda j: (0, j)), pl.BlockSpec((1, tc), lambda j: (0, j))],
                          out_specs=col, out_shape=_sds((T, Cd), F32), compiler_params=_params(("parallel",)))(xbc, cw, cb)


def ssd_conv_bwd(name, xbc, dact, cw, cb, tc=LANES):
    T, Cd = xbc.shape
    K = cw.shape[0]

    rb = min(CONV_ROWS, T)

    def body(p_ref, da_ref, w_ref, b_ref, dp_ref, dw_ref, db_ref):
        def blk(bi, acc):
            r0 = pl.multiple_of(bi * rb, rb)
            taps, pre = _conv_ext(_ext_rows(p_ref, r0, rb, T), w_ref, K)
            dpre = _ext_rows(da_ref, r0, rb, T) * _dsilu(pre + b_ref[...])
            dp, dws, db = _conv_ext_bwd(dpre, taps, w_ref, K, rb)
            dp_ref[pl.ds(r0, rb), :] = dp.astype(dp_ref.dtype)
            return tuple(a + b for a, b in zip(acc, dws + [db]))

        acc = lax.fori_loop(0, T // rb, blk, tuple(jnp.zeros((1, tc), F32) for _ in range(K + 1)))
        _store_conv_grads(acc, dw_ref, db_ref, K)

    col = pl.BlockSpec((T, tc), lambda j: (0, j))
    wsp = pl.BlockSpec((K, tc), lambda j: (0, j))
    bsp = pl.BlockSpec((1, tc), lambda j: (0, j))
    return pl.pallas_call(body, name=name, grid=(Cd // tc,), in_specs=[col, col, wsp, bsp], out_specs=[col, wsp, bsp],
                          out_shape=[_sds((T, Cd), BF16), _sds((K, Cd), F32), _sds((1, Cd), F32)],
                          compiler_params=_params(("parallel",)))(xbc, dact, cw, cb)


def _expand_matrix(Hd):
    return (_iota((LANES, Hd), 1) // SSD_HEAD_DIM == _iota((LANES, Hd), 0)).astype(BF16)


def ssd_dt_fwd(name, dtr, bias, Hd, tr=512):
    T = dtr.shape[0]
    tr = min(tr, T)

    def body(d_ref, b_ref, o_ref):
        o_ref[...] = _dot3r(_softplus(d_ref[...] + b_ref[...]), _expand_matrix(Hd))

    return pl.pallas_call(body, name=name, grid=(T // tr,),
                          in_specs=[pl.BlockSpec((tr, LANES), lambda i: (i, 0)), pl.BlockSpec((1, LANES), lambda i: (0, 0))],
                          out_specs=pl.BlockSpec((tr, Hd), lambda i: (i, 0)), out_shape=_sds((T, Hd), F32),
                          compiler_params=_params(("parallel",)))(dtr, bias)


def ssd_dt_bwd(name, dtr, bias, ddtx, tr=512):
    T, Hd = ddtx.shape
    tr = min(tr, T)

    def body(d_ref, b_ref, g_ref, o_ref, db_ref):
        p1, p2, p3 = _split3(g_ref[...])
        em = _expand_matrix(Hd)
        ddt = _dot(p1, em, NT) + _dot(p2, em, NT) + _dot(p3, em, NT)
        draw = ddt * _sigmoid(d_ref[...] + b_ref[...])
        o_ref[...] = draw.astype(o_ref.dtype)
        part = jnp.sum(draw, axis=0, keepdims=True)

        @pl.when(pl.program_id(0) == 0)
        def _():
            db_ref[...] = part

        @pl.when(pl.program_id(0) > 0)
        def _():
            db_ref[...] += part

    row = pl.BlockSpec((tr, LANES), lambda i: (i, 0))
    vec = pl.BlockSpec((1, LANES), lambda i: (0, 0))
    return pl.pallas_call(body, name=name, grid=(T // tr,), in_specs=[row, vec, pl.BlockSpec((tr, Hd), lambda i: (i, 0))],
                          out_specs=[row, vec], out_shape=[_sds((T, LANES), BF16), _sds((1, LANES), F32)],
                          compiler_params=_params(("arbitrary",)))(dtr, bias, ddtx)


def _ssd_head_terms(a2, a2r, half, cb, causal, lane):
    hm = (lane < SSD_HEAD_DIM) if half == 0 else (lane >= SSD_HEAD_DIM)
    ccol = jnp.where(hm, a2, a2r)
    lm = jnp.exp(jnp.where(causal, ccol - ccol.T, -jnp.inf))
    return hm, lm, cb * lm


def ssd_core_fwd(name, act, dtx, alx, dx, G):
    T, Hd = dtx.shape
    Q, N = CHUNK, SSD_STATE
    gw = Hd // G
    nc = T // Q
    nx = Hd // N

    def body(xs_ref, b_ref, c_ref, dt_ref, al_ref, d_ref, y_ref, ss_ref, st_scr):
        @pl.when(pl.program_id(1) == 0)
        def _():
            st_scr[...] = jnp.zeros_like(st_scr)

        xs, dtv = xs_ref[...], dt_ref[...]
        Bb, Cb = b_ref[...].astype(BF16), c_ref[...].astype(BF16)
        dA = dtv * (-jnp.exp(al_ref[...]))
        a = _dot3l(_tri(Q, "le").astype(BF16), dA)
        a_last = jnp.sum(dA, axis=0, keepdims=True)
        xdt = xs * dtv
        cbm = _dot(Cb, Bb, NT)
        sprev = st_scr[...]
        ss_ref[...] = sprev
        causal, lane = _tri(Q, "le"), _iota((Q, LANES), 1)
        y_rest = _dot(Cb, sprev.astype(BF16)) * jnp.exp(a) + xs * d_ref[...]
        for q in range(gw // LANES):
            sl = slice(q * LANES, (q + 1) * LANES)
            a2, x2 = a[:, sl], xdt[:, sl]
            a2r = pltpu.roll(a2, SSD_HEAD_DIM, 1)
            acc = y_rest[:, sl]
            for half in (0, 1):
                hm, _, gm = _ssd_head_terms(a2, a2r, half, cbm, causal, lane)
                acc = acc + _dot(gm.astype(BF16), jnp.where(hm, x2, 0.0).astype(BF16))
            y_ref[:, sl] = acc
        w = jnp.exp(a_last - a)
        st_scr[...] = sprev * jnp.exp(a_last) + _dot(Bb, (w * xdt).astype(BF16), TN)

    xsp = pl.BlockSpec((Q, gw), lambda g, c: (c, g))
    vec = pl.BlockSpec((1, gw), lambda g, c: (0, g))
    return pl.pallas_call(
        body, name=name, grid=(G, nc),
        in_specs=[xsp, pl.BlockSpec((Q, N), lambda g, c: (c, nx + g)), pl.BlockSpec((Q, N), lambda g, c: (c, nx + G + g)),
                  xsp, vec, vec],
        out_specs=[xsp, pl.BlockSpec((None, N, gw), lambda g, c: (c, 0, g))],
        out_shape=[_sds((T, Hd), F32), _sds((nc, N, Hd), F32)],
        scratch_shapes=[pltpu.VMEM((N, gw), F32)],
        compiler_params=_params(("parallel", "arbitrary")))(act, act, act, dtx, alx, dx)


def ssd_core_bwd(name, act, dtx, alx, dx, ssave, dy, G):
    T, Hd = dtx.shape
    Q, N = CHUNK, SSD_STATE
    gw = Hd // G
    nc = T // Q
    nx = Hd // N

    def body(xs_ref, b_ref, c_ref, dt_ref, al_ref, d_ref, ss_ref, dy_ref,
             dxs_ref, db_ref, dc_ref, ddt_ref, dal_ref, dd_ref, ds_scr, dxdt_scr, da_scr):
        @pl.when(pl.program_id(1) == 0)
        def _():
            ds_scr[...] = jnp.zeros_like(ds_scr)
            dal_ref[...] = jnp.zeros_like(dal_ref)
            dd_ref[...] = jnp.zeros_like(dd_ref)

        xs, dtv, dy_ = xs_ref[...], dt_ref[...], dy_ref[...]
        Bb, Cb = b_ref[...].astype(BF16), c_ref[...].astype(BF16)
        Ax = -jnp.exp(al_ref[...])
        dA = dtv * Ax
        a = _dot3l(_tri(Q, "le").astype(BF16), dA)
        a_last = jnp.sum(dA, axis=0, keepdims=True)
        xdt = xs * dtv
        e, w, eal = jnp.exp(a), jnp.exp(a_last - a), jnp.exp(a_last)
        sprev, dsn = ss_ref[...], ds_scr[...]
        sprevb, dsnb = sprev.astype(BF16), dsn.astype(BF16)

        dd_ref[...] += jnp.sum(dy_ * xs, axis=0, keepdims=True)
        dmb = (dy_ * e).astype(BF16)
        dC = _dot(dmb, sprevb, NT)
        ds_scr[...] = _dot(Cb, dmb, TN) + dsn * eal
        dalast = jnp.sum(dsn * sprev, axis=0, keepdims=True) * eal
        dB = _dot((w * xdt).astype(BF16), dsnb, NT)
        dwx = _dot(Bb, dsnb)
        dww = dwx * xdt * w
        dalast = dalast + jnp.sum(dww, axis=0, keepdims=True)
        da_scr[...] = dy_ * _dot(Cb, sprevb) * e - dww
        dxdt_scr[...] = w * dwx
        cbm = _dot(Cb, Bb, NT)
        dcb = jnp.zeros((Q, Q), F32)
        causal, lane = _tri(Q, "le"), _iota((Q, LANES), 1)
        for q in range(gw // LANES):
            sl = slice(q * LANES, (q + 1) * LANES)
            a2, x2, dy2 = a[:, sl], xdt[:, sl], dy_[:, sl]
            a2r = pltpu.roll(a2, SSD_HEAD_DIM, 1)
            for half in (0, 1):
                hm, lm, gm = _ssd_head_terms(a2, a2r, half, cbm, causal, lane)
                dyh = jnp.where(hm, dy2, 0.0).astype(BF16)
                dg = _dot(dyh, jnp.where(hm, x2, 0.0).astype(BF16), NT)
                dxdt_scr[:, sl] += _dot(gm.astype(BF16), dyh, TN)
                dcb = dcb + dg * lm
                dseg = dg * gm
                v = jnp.sum(dseg, axis=1, keepdims=True) - jnp.sum(dseg.T, axis=1, keepdims=True)
                da_scr[:, sl] += jnp.where(hm, v, 0.0) * (1.0 / SSD_HEAD_DIM)
        dcbb = dcb.astype(BF16)
        dc_ref[...] = dC + _dot(dcbb, Bb)
        db_ref[...] = dB + _dot(dcbb, Cb, TN)
        dxdt = dxdt_scr[...]
        dxs_ref[...] = dy_ * d_ref[...] + dxdt * dtv
        da = da_scr[...] + jnp.where(_iota((Q, gw), 0) == Q - 1, dalast, 0.0)
        dda = _dot3l(_tri(Q, "ge").astype(BF16), da)
        ddt_ref[...] = dxdt * xs + dda * Ax
        dal_ref[...] += jnp.sum(dda * dtv, axis=0, keepdims=True) * Ax

    rc = lambda c: nc - 1 - c
    xsp = pl.BlockSpec((Q, gw), lambda g, c: (rc(c), g))
    bsp = pl.BlockSpec((Q, N), lambda g, c: (rc(c), nx + g))
    csp = pl.BlockSpec((Q, N), lambda g, c: (rc(c), nx + G + g))
    gsp = pl.BlockSpec((Q, N), lambda g, c: (rc(c), g))
    vec = pl.BlockSpec((1, gw), lambda g, c: (0, g))
    return pl.pallas_call(
        body, name=name, grid=(G, nc),
        in_specs=[xsp, bsp, csp, xsp, vec, vec, pl.BlockSpec((None, N, gw), lambda g, c: (rc(c), 0, g)), xsp],
        out_specs=[xsp, gsp, gsp, xsp, vec, vec],
        out_shape=[_sds((T, Hd), F32), _sds((T, G * N), F32), _sds((T, G * N), F32), _sds((T, Hd), F32),
                   _sds((1, Hd), F32), _sds((1, Hd), F32)],
        scratch_shapes=[pltpu.VMEM((N, gw), F32), pltpu.VMEM((Q, gw), F32), pltpu.VMEM((Q, gw), F32)],
        compiler_params=_params(("parallel", "arbitrary")))(act, act, act, dtx, alx, dx, ssave, dy)


def ssd_gate_fwd(name, y, z, ng, tr=256):
    T, Hd = y.shape
    tr = min(tr, T)

    def body(y_ref, z_ref, g_ref, o_ref):
        o_ref[...] = _rms(y_ref[...] * _silu(z_ref[...]), g_ref[...]).astype(o_ref.dtype)

    row = pl.BlockSpec((tr, Hd), lambda i: (i, 0))
    return pl.pallas_call(body, name=name, grid=(T // tr,), in_specs=[row, row, pl.BlockSpec((1, Hd), lambda i: (0, 0))],
                          out_specs=row, out_shape=_sds((T, Hd), BF16), compiler_params=_params(("parallel",)))(y, z, ng)


def ssd_gate_bwd(name, y, z, ng, dyn, tr=128):
    T, Hd = y.shape
    tr = min(tr, T)

    def body(y_ref, z_ref, g_ref, dn_ref, dy_ref, dz_ref, dg_ref):
        y_, z_, dn = y_ref[...], z_ref[...], dn_ref[...]
        y2 = y_ * _silu(z_)
        r = lax.rsqrt(jnp.mean(y2 * y2, axis=-1, keepdims=True) + EPS)
        xh = y2 * r
        dxh = dn * g_ref[...]
        dy2 = r * (dxh - xh * jnp.mean(dxh * xh, axis=-1, keepdims=True))
        dy_ref[...] = dy2 * _silu(z_)
        dz_ref[...] = (dy2 * y_ * _dsilu(z_)).astype(dz_ref.dtype)
        part = jnp.sum(dn * xh, axis=0, keepdims=True)

        @pl.when(pl.program_id(0) == 0)
        def _():
            dg_ref[...] = part

        @pl.when(pl.program_id(0) > 0)
        def _():
            dg_ref[...] += part

    row = pl.BlockSpec((tr, Hd), lambda i: (i, 0))
    vec = pl.BlockSpec((1, Hd), lambda i: (0, 0))
    return pl.pallas_call(body, name=name, grid=(T // tr,), in_specs=[row, row, vec, row], out_specs=[row, row, vec],
                          out_shape=[_sds((T, Hd), F32), _sds((T, Hd), BF16), _sds((1, Hd), F32)],
                          compiler_params=_params(("arbitrary",)))(y, z, ng, dyn)


def ssd_mixer_fwd(hn, wz, wxbc, wdt, cw, cb, dtb, alx, dx, ng, w_out, G):
    Hd = wz.N
    z = mm_fwd("ssd_z", hn, wz, F32)
    xbc = mm_fwd("ssd_xbc", hn, wxbc, F32)
    dtr = mm_fwd("ssd_dt", hn, wdt, F32)
    act = ssd_conv_fwd("ssd_conv", xbc, cw, cb)
    dtx = ssd_dt_fwd("ssd_dtx", dtr, dtb, Hd)
    y, ssave = ssd_core_fwd("ssd_core", act, dtx, alx, dx, G)
    yn = ssd_gate_fwd("ssd_gate", y, z, ng)
    m = mm_fwd("ssd_out", yn, w_out, F32, tn=1024)
    return m, (z, xbc, dtr, act, dtx, y, ssave, yn)


def ssd_mixer_bwd(hn, saved, dm, wz, wxbc, wdt, cw, cb, dtb, alx, dx, ng, w_out, G):
    z, xbc, dtr, act, dtx, y, ssave, yn = saved
    dyn = mm_dx("ssd_out_dx", dm, w_out, F32, tn=w_out.R)
    dw_out = mm_dw("ssd_out_dw", yn, dm, BF16, tn=1024)
    dy, dz, dng = ssd_gate_bwd("ssd_gate_bwd", y, z, ng, dyn)
    dxs, dB, dC, ddtx, dalx, ddx = ssd_core_bwd("ssd_core_bwd", act, dtx, alx, dx, ssave, dy, G)
    dxbc, dcw, dcb = ssd_conv_bwd("ssd_conv_bwd", xbc, jnp.concatenate([dxs, dB, dC], axis=1), cw, cb)
    ddtr, ddtb = ssd_dt_bwd("ssd_dtx_bwd", dtr, dtb, ddtx)
    dwz = mm_dw("ssd_z_dw", hn, dz, BF16)
    dwxbc = mm_dw("ssd_xbc_dw", hn, dxbc, BF16)
    dwdt = mm_dw("ssd_dt_dw", hn, ddtr, BF16)
    dhn = mm_dx("ssd_z_dx", dz, wz, F32)
    dhn = mm_dx("ssd_xbc_dx", dxbc, wxbc, F32, add=dhn)
    dhn = mm_dx("ssd_dt_dx", ddtr, wdt, F32, add=dhn)
    grads = dict(ssd_w_out=dw_out, ssd_wz=dwz, ssd_wxbc=dwxbc, ssd_wdt=dwdt, ssd_conv_w=dcw, ssd_conv_b=dcb,
                 ssd_dt_bias=ddtb, ssd_alx=dalx, ssd_dx=ddx, ssd_norm=dng)
    return dhn, grads


def adamw(name, w, m, v, ga, gb=None):
    Rr, C = w.shape
    tr = 8
    while Rr % (tr * 2) == 0 and tr * 2 * C * 4 <= (1 << 20):
        tr *= 2
    if Rr % tr:
        tr = Rr
    two = gb is not None
    c1 = 1.0 - ADAM_B1 ** ADAM_STEP
    c2 = 1.0 - ADAM_B2 ** ADAM_STEP

    def body(*refs):
        if two:
            w_ref, m_ref, v_ref, a_ref, b_ref, g_ref, d_ref, mo_ref, vo_ref = refs
            g = a_ref[...] + b_ref[...]
        else:
            w_ref, m_ref, v_ref, a_ref, g_ref, d_ref, mo_ref, vo_ref = refs
            g = a_ref[...]
        m2 = ADAM_B1 * m_ref[...] + (1.0 - ADAM_B1) * g
        v2 = ADAM_B2 * v_ref[...] + (1.0 - ADAM_B2) * (g * g)
        g_ref[...] = g
        mo_ref[...] = m2
        vo_ref[...] = v2
        d_ref[...] = -ADAM_LR * ((m2 / c1) / (jnp.sqrt(v2 / c2) + ADAM_EPS) + ADAM_WD * w_ref[...])

    blk = pl.BlockSpec((tr, C), lambda i: (i, 0))
    n_in = 5 if two else 4
    args = (w, m, v, ga) + ((gb,) if two else ())
    return pl.pallas_call(body, name=name, grid=(Rr // tr,), in_specs=[blk] * n_in, out_specs=[blk] * 4,
                          out_shape=[_sds((Rr, C), F32)] * 4, compiler_params=_params(("parallel",)))(*args)


def adamw_rows(name, w, m, v, ga, gb, row0, prev):
    Rr, C = ga.shape
    tr = 8
    while Rr % (tr * 2) == 0 and row0 % (tr * 2) == 0 and tr * 2 * C * 4 <= (1 << 20):
        tr *= 2
    assert Rr % tr == 0 and row0 % tr == 0, (Rr, row0, tr)
    off = row0 // tr
    c1 = 1.0 - ADAM_B1 ** ADAM_STEP
    c2 = 1.0 - ADAM_B2 ** ADAM_STEP

    def body(w_ref, m_ref, v_ref, a_ref, b_ref, *rest):
        g_ref, d_ref, mo_ref, vo_ref = rest[-4:]
        g = a_ref[...] + b_ref[...]
        m2 = ADAM_B1 * m_ref[...] + (1.0 - ADAM_B1) * g
        v2 = ADAM_B2 * v_ref[...] + (1.0 - ADAM_B2) * (g * g)
        g_ref[...] = g
        mo_ref[...] = m2
        vo_ref[...] = v2
        d_ref[...] = -ADAM_LR * ((m2 / c1) / (jnp.sqrt(v2 / c2) + ADAM_EPS) + ADAM_WD * w_ref[...])

    rows = pl.BlockSpec((tr, C), lambda i: (off + i, 0))
    part = pl.BlockSpec((tr, C), lambda i: (i, 0))
    carried = [] if prev is None else list(prev)
    return pl.pallas_call(body, name=name, grid=(Rr // tr,), in_specs=[rows] * 3 + [part] * 2 + [ANY] * len(carried),
                          out_specs=[rows] * 4, out_shape=[_sds(w.shape, F32)] * 4,
                          input_output_aliases={5 + k: k for k in range(len(carried))},
                          compiler_params=_params(("parallel",)))(w, m, v, ga, gb, *carried)


def _mesh_pos():
    return lax.axis_index("x"), lax.axis_index("y"), lax.axis_index("c")


def _peer_chips(x, y):
    return [(1 - x, y), (x, 1 - y), (1 - x, 1 - y)]


def sum_slots(name, own, r):
    _, Rr, C = r.shape
    tr = 8
    while Rr % (tr * 2) == 0 and tr * 2 * C * 4 <= (1 << 20):
        tr *= 2

    def body(o_in, r_ref, o_ref):
        o_ref[...] = ((o_in[...].astype(F32) + r_ref[0].astype(F32)) + r_ref[1].astype(F32)) + r_ref[2].astype(F32)

    blk = pl.BlockSpec((tr, C), lambda i: (i, 0))
    return pl.pallas_call(body, name=name, grid=(Rr // tr,), in_specs=[blk, pl.BlockSpec((3, tr, C), lambda i: (0, i, 0))],
                          out_specs=blk, out_shape=_sds((Rr, C), F32), compiler_params=_params(("parallel",)))(own, r)


HBM = pl.BlockSpec(memory_space=pltpu.HBM)
SEM = pl.BlockSpec(memory_space=pltpu.SEMAPHORE)
EFFECT = pltpu.SideEffectType.DATAFLOW_SIDE_EFFECTING


def _xchg_copy(mode, side, src, land, send, recv, t, j, peer, me, c):
    px, py = peer
    pidx = 2 * px + py
    if mode == "gather":
        s, dst = src, land.at[me if side == "out" else pidx]
    else:
        s, dst = src.at[pidx], land.at[j]
    k = 3 * t + j
    return pltpu.make_async_remote_copy(src_ref=s, dst_ref=dst, send_sem=send.at[k], recv_sem=recv.at[k],
                                        device_id=(px, py, c), device_id_type=MESH)


def xchg_start(name, mode, srcs, lands):
    counts = [len(g) for g in srcs]
    ng = len(counts)
    fs = [a for g in srcs for a in g]
    fl = [a for g in lands for a in g]
    n = len(fs)

    def body(*refs):
        src, land = refs[:n], refs[n:2 * n]
        send, recv = refs[2 * n:2 * n + ng], refs[2 * n + ng:2 * n + 2 * ng]
        token = refs[-1]
        x, y, c = _mesh_pos()
        me = 2 * x + y
        k = 0
        for gi in range(ng):
            for t in range(counts[gi]):
                for j, peer in enumerate(_peer_chips(x, y)):
                    _xchg_copy(mode, "out", src[k], land[k], send[gi], recv[gi], t, j, peer, me, c).start()
                k += 1
        token[...] = jnp.zeros_like(token)

    sems = tuple(pltpu.SemaphoreType.DMA((3 * cnt,)) for cnt in counts)
    thru = tuple(pltpu.HBM(a.shape, a.dtype) for a in fs + fl)
    out = pl.pallas_call(
        body, name=name, in_specs=[HBM] * (2 * n),
        out_specs=(SEM,) * (2 * ng) + (HBM,) * (2 * n) + (pl.BlockSpec(memory_space=pltpu.VMEM),),
        out_shape=sems + sems + thru + (_sds((8, LANES), F32),),
        input_output_aliases={i: 2 * ng + i for i in range(2 * n)},
        compiler_params=pltpu.CompilerParams(has_side_effects=EFFECT),
    )(*[pltpu.with_memory_space_constraint(a, pltpu.HBM) for a in fs + fl])
    send, recv = out[:ng], out[ng:2 * ng]
    thru_s, thru_l = out[2 * ng:2 * ng + n], out[2 * ng + n:2 * ng + 2 * n]
    groups, k = [], 0
    for gi, cnt in enumerate(counts):
        groups.append(dict(send=send[gi], recv=recv[gi], src=list(thru_s[k:k + cnt]), land=list(thru_l[k:k + cnt])))
        k += cnt
    return groups, out[-1]


def xchg_wait(name, mode, grp, after):
    src, land = grp["src"], grp["land"]
    n = len(src)

    def body(*refs):
        s_ref, l_ref = refs[:n], refs[n:2 * n]
        send, recv = refs[2 * n], refs[2 * n + 1]
        x, y, c = _mesh_pos()
        me = 2 * x + y
        for t in range(n):
            for j, peer in enumerate(_peer_chips(x, y)):
                _xchg_copy(mode, "out", s_ref[t], l_ref[t], send, recv, t, j, peer, me, c).wait_send()
                _xchg_copy(mode, "in", s_ref[t], l_ref[t], send, recv, t, j, peer, me, c).wait_recv()

    res = pl.pallas_call(
        body, name=name, in_specs=[HBM] * (2 * n) + [SEM, SEM, ANY],
        out_specs=(HBM,) * (2 * n), out_shape=tuple(pltpu.HBM(a.shape, a.dtype) for a in src + land),
        input_output_aliases={i: i for i in range(2 * n)},
        compiler_params=pltpu.CompilerParams(has_side_effects=EFFECT),
    )(*src, *land, grp["send"], grp["recv"], after)
    return list(res[:n]), list(res[n:])


def swap_with_sibling(name, tensors):
    n = len(tensors)

    def body(*refs):
        ins, outs = refs[:n], refs[n:2 * n]
        send_sems, recv_sems = refs[2 * n:]
        x, y, c = _mesh_pos()
        cps = []
        for t in range(n):
            cp = pltpu.make_async_remote_copy(src_ref=ins[t], dst_ref=outs[t], send_sem=send_sems.at[t], recv_sem=recv_sems.at[t],
                                              device_id=(x, y, 1 - c), device_id_type=MESH)
            cp.start()
            cps.append(cp)
        for cp in cps:
            cp.wait()

    return pl.pallas_call(
        body, name=name, in_specs=[ANY] * n, out_specs=[ANY] * n, out_shape=[_sds(t.shape, t.dtype) for t in tensors],
        scratch_shapes=[pltpu.SemaphoreType.DMA((n,)), pltpu.SemaphoreType.DMA((n,))],
    )(*tensors)


def all_reduce_small(name, v, after):
    Rr, C = v.shape
    nd = 8

    def body(v_ref, after_ref, o_ref, gath, send_sems, recv_sems):
        x, y, c = _mesh_pos()
        me = 4 * x + 2 * y + c
        cps = []
        for d in range(1, nd):
            bx, by, bc = (d >> 2) & 1, (d >> 1) & 1, d & 1
            tgt = (1 - x if bx else x, 1 - y if by else y, 1 - c if bc else c)
            cp = pltpu.make_async_remote_copy(src_ref=v_ref, dst_ref=gath.at[me], send_sem=send_sems.at[d - 1],
                                              recv_sem=recv_sems.at[d - 1], device_id=tgt, device_id_type=MESH)
            cp.start()
            cps.append((cp, tgt))
        gath[me] = v_ref[...]
        for d in range(1, nd):
            _, (tx, ty, tc) = cps[d - 1]
            pltpu.make_async_remote_copy(src_ref=v_ref, dst_ref=gath.at[4 * tx + 2 * ty + tc], send_sem=send_sems.at[d - 1],
                                         recv_sem=recv_sems.at[d - 1], device_id=(tx, ty, tc), device_id_type=MESH).wait_recv()
        acc = gath[0]
        for d in range(1, nd):
            acc = acc + gath[d]
        o_ref[...] = acc
        for cp, _ in cps:
            cp.wait_send()

    vm = pl.BlockSpec(memory_space=pltpu.VMEM)
    return pl.pallas_call(
        body, name=name, in_specs=[vm, ANY], out_specs=vm, out_shape=_sds((Rr, C), F32),
        scratch_shapes=[pltpu.VMEM((nd, Rr, C), F32), pltpu.SemaphoreType.DMA((nd - 1,)), pltpu.SemaphoreType.DMA((nd - 1,))],
        compiler_params=pltpu.CompilerParams(vmem_limit_bytes=VMEM_LIMIT),
    )(v, after)


def _pack(arrs):
    flat = jnp.concatenate([a.reshape(-1) for a in arrs])
    pad = (-flat.shape[0]) % (8 * LANES)
    return jnp.pad(flat, (0, pad)).reshape(-1, LANES)


def _unpack(buf, shapes):
    flat = buf.reshape(-1)
    out, off = [], 0
    for s in shapes:
        n = math.prod(s)
        out.append(flat[off:off + n].reshape(s))
        off += n
    return out


WEIGHTS = ["ln_mix_pre", "ln_mix_post", "ln_mem", "ln_xa_pre", "ln_xa_post", "ln_ffn_pre", "ln_ffn_post", "xa_wq", "xa_wkv",
           "xa_wo", "ffn_w_in", "ffn_conv_w", "ffn_conv_b", "ffn_w_out", "ssd_w_in", "ssd_conv_w", "ssd_conv_b", "ssd_dt_bias",
           "ssd_a_log", "ssd_d", "ssd_norm", "ssd_w_out", "sg_w_in", "sg_v_norm_g", "sg_v_norm_b", "sg_w_spatial",
           "sg_b_spatial", "sg_w_out", "sb_w_qkv", "sb_w_out"]
BIG = {"xa_wq": "rows", "xa_wkv": "rows", "xa_wo": "cols", "ffn_w_in": "cols", "ffn_w_out": "rows", "ssd_w_in": "cols",
       "ssd_w_out": "rows", "sg_w_in": "cols", "sg_w_out": "rows", "sb_w_qkv": "cols", "sb_w_out": "rows"}
SHARDED_SMALL = {"ffn_conv_w": 2, "ssd_conv_w": 2, "ssd_conv_b": 1, "ssd_norm": 1}
SMALL = [n for n in WEIGHTS if n not in BIG]
N_MIXERS = 3
HEAD = 128


def _unshard(a, axis):
    a = jnp.moveaxis(a, 0, axis)
    s = a.shape
    return a.reshape(s[:axis] + (s[axis] * s[axis + 1],) + s[axis + 2:])


def _step(p):
    x, mem, tgt = p["x"][0], p["mem"][0], p["loss_target"][0]
    T, D = x.shape
    depth = p["ln_mix_pre"].shape[0]
    S = N_CHIPS

    me = 2 * lax.axis_index("x") + lax.axis_index("y")
    Hd, Cd = S * p["ssd_norm"].shape[1], S * p["ssd_conv_b"].shape[1]
    nh = p["ssd_dt_bias"].shape[1]
    G = (Cd - Hd) // (2 * SSD_STATE)
    xa_heads = p["xa_wo"].shape[1] // HEAD
    sb_heads = D // HEAD

    def layer_parts(i):
        kind, j = i % N_MIXERS, i // N_MIXERS
        first = {0: [("ssd_w_in", j), ("ssd_conv_w", j), ("ssd_conv_b", j), ("ssd_norm", j)], 1: [("sg_w_in", j)],
                 2: [("sb_w_qkv", j)]}[kind]
        w_out = {0: "ssd_w_out", 1: "sg_w_out", 2: "sb_w_out"}[kind]
        return [first + [("ffn_conv_w", i)], [(w_out, j), ("xa_wq", i), ("xa_wkv", i), ("xa_wo", i)],
                [("ffn_w_in", i), ("ffn_w_out", i)]]

    srcs, lands = [], []
    for i in range(depth):
        for part in layer_parts(i):
            s_i, l_i = [], []
            for n, k in part:
                a = p[n][k].astype(BF16) if n in BIG else p[n][k]
                a = a.reshape((1,) * (2 - a.ndim) + a.shape)
                s_i.append(a)
                l_i.append(lax.dynamic_update_index_in_dim(lax.empty((S,) + a.shape, a.dtype), a, me, 0))
            srcs.append(s_i)
            lands.append(l_i)
    gather_groups, gather_token = xchg_start("gather_start", "gather", srcs, lands)

    class Gathered:
        def __init__(self, i):
            self.i, self.parts, self.got = i, layer_parts(i), {}

        def get(self, key, after):
            if key not in self.got:
                k = next(idx for idx, part in enumerate(self.parts) if key in part)
                _, zones = xchg_wait("gather_wait_%d_%d" % (self.i, k), "gather", gather_groups[3 * self.i + k], after)
                self.got.update(zip(self.parts[k], zones))
            return self.got[key]

    def layer_args(i, gz, x_in):
        kind, j = i % N_MIXERS, i // N_MIXERS
        w_of = lambda n, k: W(BIG[n], lambda operand: gz.get((n, k), operand)[:, None], 0, shape=(S, 1) + p[n].shape[1:])
        now = lambda n, k: gz.get((n, k), x_in)
        a = dict(xa=(w_of("xa_wq", i), w_of("xa_wkv", i), w_of("xa_wo", i), xa_heads),
                 ffn=(w_of("ffn_w_in", i), _unshard(now("ffn_conv_w", i), 1), p["ffn_conv_b"][i:i + 1], w_of("ffn_w_out", i)))
        if kind == 0:
            w_in = _unshard(now("ssd_w_in", j), 1)[None]
            a["mix"] = (W("full", w_in[:, :, :Hd], 0), W("full", w_in[:, :, Hd:Hd + Cd], 0),
                        W("full", jnp.pad(w_in[:, :, Hd + Cd:], ((0, 0), (0, 0), (0, LANES - nh))), 0),
                        _unshard(now("ssd_conv_w", j), 1), _unshard(now("ssd_conv_b", j), 1),
                        jnp.pad(p["ssd_dt_bias"][j], (0, LANES - nh))[None], jnp.repeat(p["ssd_a_log"][j], SSD_HEAD_DIM)[None],
                        jnp.repeat(p["ssd_d"][j], SSD_HEAD_DIM)[None], _unshard(now("ssd_norm", j), 1), w_of("ssd_w_out", j), G)
        elif kind == 1:
            a["mix"] = (w_of("sg_w_in", j), p["sg_v_norm_g"][j:j + 1], p["sg_v_norm_b"][j:j + 1], p["sg_w_spatial"][j],
                        w_of("sg_w_out", j))
        else:
            a["mix"] = (w_of("sb_w_qkv", j), w_of("sb_w_out", j), sb_heads)
        return a

    ln = lambda n, i: p[n][i:i + 1]

    h = rms_fwd("rms_first", x, ln("ln_mix_pre", 0), after=gather_token)
    saved, largs = [], []
    for i in range(depth):
        kind, j = i % N_MIXERS, i // N_MIXERS
        la = layer_args(i, Gathered(i), x)
        largs.append(la)
        if kind == 0:
            m, ms = ssd_mixer_fwd(h, *la["mix"])
        elif kind == 1:
            m, ms = sg_mixer_fwd(h, *la["mix"][:4], p["sg_b_spatial"][j], la["mix"][4])
        else:
            m, ms = sb_mixer_fwd(h, *la["mix"])
        x1, hq = resid_norm("resid_norm", x, m, ln("ln_mix_post", i), ln("ln_xa_pre", i))
        mem_n = rms_fwd("rms_mem", mem, ln("ln_mem", i))
        c, cs = xa_fwd(hq, mem_n, *la["xa"])
        x2, hf = resid_norm("resid_norm", x1, c, ln("ln_xa_post", i), ln("ln_ffn_pre", i))
        f, fs = ffn_fwd(hf, *la["ffn"])
        x3, hn = resid_norm("resid_norm", x2, f, ln("ln_ffn_post", i), ln("ln_mix_pre", i + 1) if i + 1 < depth else None)
        saved.append(dict(x=x, h=h, m=m, ms=ms, x1=x1, hq=hq, mem_n=mem_n, c=c, cs=cs, x2=x2, hf=hf, f=f, fs=fs))
        x, h = x3, hn
    loss_tile, dx = loss_fwd_bwd("loss", x, tgt)
    loss = lax.psum(loss_tile[0, 0], ("x", "y", "c"))

    gs = {n: [None] * p[n].shape[0] for n in WEIGHTS}
    scatter_groups = [[] for _ in range(depth)]

    def send_grads(i, part, keys):
        g_src = [gs[n][k] for n, k in keys]
        g_land = [lax.empty((3,) + a.shape[1:], a.dtype) for a in g_src]
        grp, tok = xchg_start("scatter_start_%d_%d" % (i, part), "scatter", [g_src], [g_land])
        scatter_groups[i].append((part, keys, grp[0]))
        return tok

    token = None
    for i in reversed(range(depth)):
        kind, j = i % N_MIXERS, i // N_MIXERS
        s, la = saved[i], largs[i]
        df, gs["ln_ffn_post"][i] = rms_bwd("rms_bwd_post", s["f"], ln("ln_ffn_post", i), dx, None, BF16, after=token)
        dhf, g = ffn_bwd(s["hf"], s["fs"], df, *la["ffn"])
        gs["ffn_w_in"][i], gs["ffn_conv_w"][i], gs["ffn_conv_b"][i] = g["ffn_w_in"], g["ffn_conv_w"], g["ffn_conv_b"]
        gs["ffn_w_out"][i] = g["ffn_w_out"].reshape(S, -1, D)
        dx, gs["ln_ffn_pre"][i] = rms_bwd("rms_bwd_pre", s["x2"], ln("ln_ffn_pre", i), dhf, dx, F32)
        token = send_grads(i, 2, [("ffn_w_in", i), ("ffn_w_out", i)])

        dc, gs["ln_xa_post"][i] = rms_bwd("rms_bwd_post", s["c"], ln("ln_xa_post", i), dx, None, BF16, after=token)
        dhq, dmem_n, g = xa_bwd(s["hq"], s["mem_n"], s["cs"], dc, *la["xa"])
        gs["xa_wq"][i] = g["xa_wq"].reshape(S, D // S, -1)
        gs["xa_wkv"][i] = g["xa_wkv"].reshape(S, D // S, -1)
        gs["xa_wo"][i] = g["xa_wo"]
        _, gs["ln_mem"][i] = rms_bwd("rms_bwd_mem", mem, ln("ln_mem", i), dmem_n, None, BF16)
        dx, gs["ln_xa_pre"][i] = rms_bwd("rms_bwd_pre", s["x1"], ln("ln_xa_pre", i), dhq, dx, F32)
        token = send_grads(i, 1, [("xa_wq", i), ("xa_wkv", i), ("xa_wo", i)])

        dm, gs["ln_mix_post"][i] = rms_bwd("rms_bwd_post", s["m"], ln("ln_mix_post", i), dx, None, BF16, after=token)
        if kind == 0:
            dhn, g = ssd_mixer_bwd(s["h"], s["ms"], dm, *la["mix"])
            full = jnp.concatenate([g["ssd_wz"], g["ssd_wxbc"], g["ssd_wdt"][:, :nh]], axis=1)
            gs["ssd_w_in"][j] = full.reshape(D, S, -1).transpose(1, 0, 2)
            gs["ssd_w_out"][j] = g["ssd_w_out"].reshape(S, Hd // S, D)
            gs["ssd_conv_w"][j], gs["ssd_conv_b"][j], gs["ssd_norm"][j] = g["ssd_conv_w"], g["ssd_conv_b"], g["ssd_norm"]
            gs["ssd_dt_bias"][j] = g["ssd_dt_bias"][:, :nh]
            gs["ssd_a_log"][j] = g["ssd_alx"].reshape(nh, SSD_HEAD_DIM).sum(-1)[None]
            gs["ssd_d"][j] = g["ssd_dx"].reshape(nh, SSD_HEAD_DIM).sum(-1)[None]
        elif kind == 1:
            dhn, g = sg_mixer_bwd(s["h"], s["ms"], dm, *la["mix"])
            gs["sg_w_in"][j] = g["sg_w_in"]
            gs["sg_w_out"][j] = g["sg_w_out"].reshape(S, -1, D)
            for n in ("sg_v_norm_g", "sg_v_norm_b", "sg_w_spatial", "sg_b_spatial"):
                gs[n][j] = g[n]
        else:
            dhn, g = sb_mixer_bwd(s["h"], s["ms"], dm, *la["mix"])
            gs["sb_w_qkv"][j] = g["sb_w_qkv"]
            gs["sb_w_out"][j] = g["sb_w_out"].reshape(S, -1, D)
        dx, gs["ln_mix_pre"][i] = rms_bwd("rms_bwd_pre", s["x"], ln("ln_mix_pre", i), dhn, dx, F32)

        mix_w = {0: ("ssd_w_in", "ssd_w_out"), 1: ("sg_w_in", "sg_w_out"), 2: ("sb_w_qkv", "sb_w_out")}[kind]
        token = send_grads(i, 0, [(n, j) for n in mix_w])

    out, running, behind = {}, {n: None for n in BIG}, token
    for i in reversed(range(depth)):
        keys, qs = [], []
        for part, part_keys, grp in scatter_groups[i]:
            sent, got = xchg_wait("scatter_wait_%d_%d" % (i, part), "scatter", grp, behind if i == 0 else token)
            for (n, k), own, r in zip(part_keys, sent, got):
                mine = lax.dynamic_index_in_dim(own, me, 0, keepdims=False)
                keys.append((n, k))
                qs.append(sum_slots("sum_grad_slots", mine.reshape(-1, mine.shape[-1]), r.reshape(3, -1, r.shape[-1])))
        sib = swap_with_sibling("swap_grads_%d" % i, qs)
        for (n, k), q, q2 in zip(keys, qs, sib):
            two_d = lambda a: a.reshape(-1, a.shape[-1])
            running[n] = adamw_rows("adamw_big", two_d(p[n]), two_d(p["m_" + n]), two_d(p["v_" + n]), q, q2,
                                    k * q.shape[0], running[n])
            behind = running[n][0]
    for n in BIG:
        out[n] = [r.reshape(p[n].shape) for r in running[n]]

    stack = lambda n: jnp.stack([a.reshape(p[n].shape[1:]) if n not in SHARDED_SMALL else a.reshape(a.shape[-len(p[n].shape) + 1:])
                                 for a in gs[n]])
    small_full = [stack(n) for n in SMALL]
    red = _unpack(all_reduce_small("reduce_small", _pack(small_full), after=behind), [a.shape for a in small_full])
    small_g = []
    for n, a in zip(SMALL, red):
        if n in SHARDED_SMALL:
            ax = SHARDED_SMALL[n]
            a = lax.dynamic_slice_in_dim(a, me * p[n].shape[ax], p[n].shape[ax], axis=ax)
        small_g.append(a)
    shapes = [p[n].shape for n in SMALL]
    res = adamw("adamw_small", _pack([p[n] for n in SMALL]), _pack([p["m_" + n] for n in SMALL]),
                _pack([p["v_" + n] for n in SMALL]), _pack(small_g))
    for k, r in enumerate(res):
        for n, a in zip(SMALL, _unpack(r, shapes)):
            out.setdefault(n, [None] * 4)[k] = a

    return (loss, dx[None]) + tuple(out[n][k] for k in range(4) for n in WEIGHTS)


def kernel(x, mem, ln_mix_pre, ln_mix_post, ln_mem, ln_xa_pre, ln_xa_post, ln_ffn_pre, ln_ffn_post, xa_wq, xa_wkv, xa_wo, ffn_w_in, ffn_conv_w, ffn_conv_b, ffn_w_out, ssd_w_in, ssd_conv_w, ssd_conv_b, ssd_dt_bias, ssd_a_log, ssd_d, ssd_norm, ssd_w_out, sg_w_in, sg_v_norm_g, sg_v_norm_b, sg_w_spatial, sg_b_spatial, sg_w_out, sb_w_qkv, sb_w_out, loss_target, m_ln_mix_pre, m_ln_mix_post, m_ln_mem, m_ln_xa_pre, m_ln_xa_post, m_ln_ffn_pre, m_ln_ffn_post, m_xa_wq, m_xa_wkv, m_xa_wo, m_ffn_w_in, m_ffn_conv_w, m_ffn_conv_b, m_ffn_w_out, m_ssd_w_in, m_ssd_conv_w, m_ssd_conv_b, m_ssd_dt_bias, m_ssd_a_log, m_ssd_d, m_ssd_norm, m_ssd_w_out, m_sg_w_in, m_sg_v_norm_g, m_sg_v_norm_b, m_sg_w_spatial, m_sg_b_spatial, m_sg_w_out, m_sb_w_qkv, m_sb_w_out, v_ln_mix_pre, v_ln_mix_post, v_ln_mem, v_ln_xa_pre, v_ln_xa_post, v_ln_ffn_pre, v_ln_ffn_post, v_xa_wq, v_xa_wkv, v_xa_wo, v_ffn_w_in, v_ffn_conv_w, v_ffn_conv_b, v_ffn_w_out, v_ssd_w_in, v_ssd_conv_w, v_ssd_conv_b, v_ssd_dt_bias, v_ssd_a_log, v_ssd_d, v_ssd_norm, v_ssd_w_out, v_sg_w_in, v_sg_v_norm_g, v_sg_v_norm_b, v_sg_w_spatial, v_sg_b_spatial, v_sg_w_out, v_sb_w_qkv, v_sb_w_out):
    return _step(dict(locals()))
```

```python
import functools
import math

import jax
import jax.numpy as jnp
from jax import lax
from jax.experimental import pallas as pl
from jax.experimental.pallas import tpu as pltpu

F32 = jnp.float32
BF16 = jnp.bfloat16
EPS = 1e-6
LANES = 128
VMEM_LIMIT = 56 * 1024 * 1024
CHUNK = 128
SSD_HEAD_DIM = 64
SSD_STATE = 128
N_CHIPS = 4
MESH = pl.DeviceIdType.MESH
ANY = pl.BlockSpec(memory_space=pl.ANY)

ADAM_LR, ADAM_B1, ADAM_B2, ADAM_EPS, ADAM_WD, ADAM_STEP = 0.001, 0.9, 0.999, 1e-08, 0.01, 10


def _params(sem):
    return pltpu.CompilerParams(dimension_semantics=sem, vmem_limit_bytes=VMEM_LIMIT)


def _sds(shape, dtype):
    return jax.ShapeDtypeStruct(tuple(shape), dtype)


def _tile(n, pref):
    if n <= pref:
        return n
    t = (pref // LANES) * LANES
    while t > LANES and n % t:
        t -= LANES
    assert n % t == 0, (n, pref)
    return t


def _split3(a):
    a1 = a.astype(BF16)
    r = a - a1.astype(F32)
    a2 = r.astype(BF16)
    a3 = (r - a2.astype(F32)).astype(BF16)
    return a1, a2, a3


def _dot(a, b, dims=(((1,), (0,)), ((), ()))):
    return lax.dot_general(a, b, dims, preferred_element_type=F32)


NN = (((1,), (0,)), ((), ()))
NT = (((1,), (1,)), ((), ()))
TN = (((0,), (0,)), ((), ()))


def _dot3r(a, m):
    p1, p2, p3 = _split3(a)
    return _dot(p1, m) + _dot(p2, m) + _dot(p3, m)


def _dot3l(m, a, dims=NN):
    p1, p2, p3 = _split3(a)
    return _dot(m, p1, dims) + _dot(m, p2, dims) + _dot(m, p3, dims)


def _iota(shape, dim):
    return lax.broadcasted_iota(jnp.int32, shape, dim)


def _tri(n, kind):
    r, c = _iota((n, n), 0), _iota((n, n), 1)
    return {"le": c <= r, "lt": c < r, "ge": c >= r, "gt": c > r}[kind]


def _sigmoid(x):
    return 1.0 / (1.0 + jnp.exp(-x))


def _silu(x):
    return x * _sigmoid(x)


def _dsilu(x):
    s = _sigmoid(x)
    return s * (1.0 + x * (1.0 - s))


_GC = math.sqrt(2.0 / math.pi)


def _gelu(x):
    return 0.5 * x * (1.0 + jnp.tanh(_GC * (x + 0.044715 * x * x * x)))


def _dgelu(x):
    th = jnp.tanh(_GC * (x + 0.044715 * x * x * x))
    return 0.5 * (1.0 + th) + 0.5 * x * (1.0 - th * th) * _GC * (1.0 + 3.0 * 0.044715 * x * x)


def _softplus(x):
    return jnp.maximum(x, 0.0) + jnp.log(1.0 + jnp.exp(-jnp.abs(x)))


def _mm(name, mode, a, b, out_sds, grid, a_spec, b_spec, o_spec, acc_shape, add=None, add_spec=None):
    dims = {"nn": NN, "nt": NT, "tn": TN}[mode]
    nk = grid[2]
    has_add = add is not None

    def body(*refs):
        if has_add:
            a_ref, b_ref, c_ref, o_ref = refs[:4]
        else:
            a_ref, b_ref, o_ref = refs[:3]
            c_ref = None
        part = lax.dot_general(a_ref[...], b_ref[...], dims, preferred_element_type=F32)

        def finish(r):
            if c_ref is not None:
                r = r + c_ref[...].astype(F32)
            o_ref[...] = r.astype(o_ref.dtype)

        if nk == 1:
            finish(part)
        else:
            acc = refs[-1]
            k = pl.program_id(2)

            @pl.when(k == 0)
            def _():
                acc[...] = part

            @pl.when(k > 0)
            def _():
                acc[...] += part

            @pl.when(k == nk - 1)
            def _():
                finish(acc[...])

    in_specs = [a_spec, b_spec] + ([add_spec] if has_add else [])
    args = (a, b) + ((add,) if has_add else ())
    return pl.pallas_call(
        body, name=name, grid=grid, in_specs=in_specs, out_specs=o_spec, out_shape=out_sds,
        scratch_shapes=[pltpu.VMEM(acc_shape, F32)] if nk > 1 else [],
        compiler_params=_params(("parallel", "parallel", "arbitrary")),
    )(*args)


class W:
    def __init__(self, kind, arr, layer, shape=None):
        self.kind, self._arr, self.layer = kind, arr, layer
        shape = arr.shape if shape is None else shape
        if kind == "cols":
            s, _, k, c = shape
            self.K, self.N, self.S, self.C = k, s * c, s, c
        elif kind == "rows":
            s, _, r, n = shape
            self.K, self.N, self.S, self.R = s * r, n, s, r
        else:
            _, k, n = shape
            self.K, self.N = k, n

    def get(self, operand):
        if callable(self._arr):
            self._arr = self._arr(operand)
        return self._arr


def mm_fwd(name, a, w, out_dtype, tm=1024, tn=1536, a_spec=None, out_sds=None, o_spec=None, add=None):
    M = a.shape[0]
    tm = min(tm, M)
    l = w.layer
    if w.kind == "cols":
        tn = _tile(w.C, tn)
        nps = w.C // tn
        tk, nk = w.K, 1
        b_spec = pl.BlockSpec((None, None, tk, tn), lambda i, j, k: (j // nps, l, 0, j % nps))
    elif w.kind == "rows":
        tn = _tile(w.N, tn)
        tk, nk = w.R, w.S
        b_spec = pl.BlockSpec((None, None, tk, tn), lambda i, j, k: (k, l, 0, j))
    else:
        tn = _tile(w.N, tn)
        tk, nk = w.K, 1
        b_spec = pl.BlockSpec((None, tk, tn), lambda i, j, k: (l, 0, j))
    grid = (M // tm, w.N // tn, nk)
    if a_spec is None:
        a_spec = pl.BlockSpec((tm, tk), lambda i, j, k: (i, k))
    if out_sds is None:
        out_sds = _sds((M, w.N), out_dtype)
        o_spec = pl.BlockSpec((tm, tn), lambda i, j, k: (i, j))
    else:
        o_spec = o_spec(tm, tn)
    add_spec = pl.BlockSpec((tm, tn), lambda i, j, k: (i, j)) if add is not None else None
    return _mm(name, "nn", a, w.get(a), out_sds, grid, a_spec, b_spec, o_spec, (tm, tn), add, add_spec)


def mm_dx(name, dy, w, out_dtype, tm=1024, tn=1024, a_spec=None, add=None):
    M = dy.shape[-2]
    tm = min(tm, M)
    l = w.layer
    if w.kind == "cols":
        tn = _tile(w.K, tn)
        tk, nk = w.C, w.S
        b_spec = pl.BlockSpec((None, None, tn, tk), lambda i, j, k: (k, l, j, 0))
    elif w.kind == "rows":
        tn = _tile(w.R, tn)
        npr = w.R // tn
        tk, nk = w.N, 1
        b_spec = pl.BlockSpec((None, None, tn, tk), lambda i, j, k: (j // npr, l, j % npr, 0))
    else:
        tn = _tile(w.K, tn)
        tk, nk = _tile(w.N, 2048), w.N // _tile(w.N, 2048)
        b_spec = pl.BlockSpec((None, tn, tk), lambda i, j, k: (l, j, k))
    grid = (M // tm, w.K // tn, nk)
    if a_spec is None:
        a_spec = pl.BlockSpec((tm, tk), lambda i, j, k: (i, k))
    else:
        a_spec = a_spec(tm, tk)
    out_sds = _sds((M, w.K), out_dtype)
    o_spec = pl.BlockSpec((tm, tn), lambda i, j, k: (i, j))
    add_spec = o_spec if add is not None else None
    return _mm(name, "nt", dy, w.get(dy), out_sds, grid, a_spec, b_spec, o_spec, (tm, tn), add, add_spec)


def mm_dw(name, a, dy, out_dtype, n_shards=None, tm=512, tn=1536, b_spec=None, N=None):
    T, K = a.shape
    N = dy.shape[-1] if N is None else N
    tm = _tile(K, tm)
    if n_shards:
        C = N // n_shards
        tn = _tile(C, tn)
        nps = C // tn
        out_sds = _sds((n_shards, K, C), out_dtype)
        o_spec = pl.BlockSpec((None, tm, tn), lambda i, j, k: (j // nps, i, j % nps))
    else:
        tn = _tile(N, tn)
        out_sds = _sds((K, N), out_dtype)
        o_spec = pl.BlockSpec((tm, tn), lambda i, j, k: (i, j))
    grid = (K // tm, N // tn, 1)
    a_spec = pl.BlockSpec((T, tm), lambda i, j, k: (0, i))
    if b_spec is None:
        b_spec = pl.BlockSpec((T, tn), lambda i, j, k: (0, j))
    else:
        b_spec = b_spec(T, tn)
    return _mm(name, "tn", a, dy, out_sds, grid, a_spec, b_spec, o_spec, (tm, tn))


def _rms(x, g):
    r = lax.rsqrt(jnp.mean(x * x, axis=-1, keepdims=True) + EPS)
    return x * r * g


def rms_fwd(name, x, g, tr=512, after=None):
    T, D = x.shape
    tr = min(tr, T)

    def body(x_ref, g_ref, *rest):
        o_ref = rest[-1]
        o_ref[...] = _rms(x_ref[...], g_ref[...]).astype(o_ref.dtype)

    row = pl.BlockSpec((tr, D), lambda i: (i, 0))
    vec = pl.BlockSpec((1, D), lambda i: (0, 0))
    extra = [] if after is None else [after]
    return pl.pallas_call(body, name=name, grid=(T // tr,), in_specs=[row, vec] + [ANY] * len(extra), out_specs=row,
                          out_shape=_sds((T, D), BF16), compiler_params=_params(("parallel",)))(x, g, *extra)


def resid_norm(name, x, m, g_post, g_next, tr=512):
    T, D = x.shape
    tr = min(tr, T)
    has_next = g_next is not None

    def body(*refs):
        if has_next:
            x_ref, m_ref, gp_ref, gn_ref, xo_ref, h_ref = refs
        else:
            x_ref, m_ref, gp_ref, xo_ref = refs
        xn = x_ref[...] + _rms(m_ref[...], gp_ref[...])
        xo_ref[...] = xn
        if has_next:
            h_ref[...] = _rms(xn, gn_ref[...]).astype(h_ref.dtype)

    row = pl.BlockSpec((tr, D), lambda i: (i, 0))
    vec = pl.BlockSpec((1, D), lambda i: (0, 0))
    ins = [row, row, vec] + ([vec] if has_next else [])
    args = (x, m, g_post) + ((g_next,) if has_next else ())
    outs = [row, row] if has_next else row
    shp = [_sds((T, D), F32), _sds((T, D), BF16)] if has_next else _sds((T, D), F32)
    res = pl.pallas_call(body, name=name, grid=(T // tr,), in_specs=ins, out_specs=outs, out_shape=shp,
                         compiler_params=_params(("parallel",)))(*args)
    return res if has_next else (res, None)


def rms_bwd(name, xin, g, dy, resid, out_dtype, tr=512, after=None):
    T, D = xin.shape
    tr = min(tr, T)
    has_res = resid is not None

    def body(*refs):
        dx_ref, dg_ref = refs[-2:]
        if has_res:
            x_ref, g_ref, dy_ref, r_ref = refs[:4]
        else:
            x_ref, g_ref, dy_ref = refs[:3]
        x = x_ref[...].astype(F32)
        dy_ = dy_ref[...].astype(F32)
        r = lax.rsqrt(jnp.mean(x * x, axis=-1, keepdims=True) + EPS)
        xh = x * r
        dxh = dy_ * g_ref[...]
        dx = r * (dxh - xh * jnp.mean(dxh * xh, axis=-1, keepdims=True))
        if has_res:
            dx = dx + r_ref[...]
        dx_ref[...] = dx.astype(dx_ref.dtype)
        part = jnp.sum(dy_ * xh, axis=0, keepdims=True)

        @pl.when(pl.program_id(0) == 0)
        def _():
            dg_ref[...] = part

        @pl.when(pl.program_id(0) > 0)
        def _():
            dg_ref[...] += part

    row = pl.BlockSpec((tr, D), lambda i: (i, 0))
    vec = pl.BlockSpec((1, D), lambda i: (0, 0))
    ins = [row, vec, row] + ([row] if has_res else []) + ([] if after is None else [ANY])
    args = (xin, g, dy) + ((resid,) if has_res else ()) + (() if after is None else (after,))
    return pl.pallas_call(body, name=name, grid=(T // tr,), in_specs=ins, out_specs=[row, vec],
                          out_shape=[_sds((T, D), out_dtype), _sds((1, D), F32)],
                          compiler_params=_params(("arbitrary",)))(*args)


def loss_fwd_bwd(name, y, tgt, tr=512):
    T, D = y.shape
    tr = min(tr, T)

    def body(y_ref, t_ref, l_ref, d_ref):
        e = y_ref[...] - t_ref[...]
        d_ref[...] = e * (1.0 / D)
        part = 0.5 * jnp.sum(jnp.mean(e * e, axis=-1, keepdims=True), axis=0, keepdims=True)
        part = jnp.broadcast_to(part, l_ref.shape)

        @pl.when(pl.program_id(0) == 0)
        def _():
            l_ref[...] = part

        @pl.when(pl.program_id(0) > 0)
        def _():
            l_ref[...] += part

    row = pl.BlockSpec((tr, D), lambda i: (i, 0))
    return pl.pallas_call(body, name=name, grid=(T // tr,), in_specs=[row, row],
                          out_specs=[pl.BlockSpec((8, LANES), lambda i: (0, 0)), row],
                          out_shape=[_sds((8, LANES), F32), _sds((T, D), F32)],
                          compiler_params=_params(("arbitrary",)))(y, tgt)


CONV_ROWS = 256
HALO = 8


def _ext_rows(ref, r0, rb, T):
    top = ref[pl.ds(pl.multiple_of(jnp.maximum(r0 - HALO, 0), HALO), HALO), :]
    bot = ref[pl.ds(pl.multiple_of(jnp.minimum(r0 + rb, T - HALO), HALO), HALO), :]
    return jnp.concatenate([jnp.where(r0 > 0, top, 0.0), ref[pl.ds(r0, rb), :], jnp.where(r0 + rb < T, bot, 0.0)], axis=0)


def _conv_ext(e, w_ref, K):
    taps = [e] + [pltpu.roll(e, s, 0) for s in range(1, K)]
    out = taps[0] * w_ref[pl.ds(K - 1, 1), :]
    for s in range(1, K):
        out = out + taps[s] * w_ref[pl.ds(K - 1 - s, 1), :]
    return taps, out


def _conv_ext_bwd(dpre, taps, w_ref, K, rb):
    n = dpre.shape[0]
    own = slice(HALO, HALO + rb)
    d_own = dpre[own]
    dws = [jnp.sum(d_own * taps[s][own], axis=0, keepdims=True) for s in range(K)]
    dp = d_own * w_ref[pl.ds(K - 1, 1), :]
    for s in range(1, K):
        dp = dp + pltpu.roll(dpre, n - s, 0)[own] * w_ref[pl.ds(K - 1 - s, 1), :]
    return dp, dws, jnp.sum(d_own, axis=0, keepdims=True)


def _store_conv_grads(acc, dw_ref, db_ref, K):
    for s in range(K):
        dw_ref[pl.ds(K - 1 - s, 1), :] = acc[s]
    db_ref[...] = acc[K]


def ffn_gate_fwd(name, P, cw, cb, tc=LANES):
    _, T, F = P.shape
    K = cw.shape[0]
    nf = F // tc
    rb = min(CONV_ROWS, T)

    def body(pg_ref, pu_ref, wg_ref, wu_ref, bg_ref, bu_ref, o_ref):
        def blk(bi, carry):
            r0 = pl.multiple_of(bi * rb, rb)
            _, g = _conv_ext(_ext_rows(pg_ref, r0, rb, T), wg_ref, K)
            _, u = _conv_ext(_ext_rows(pu_ref, r0, rb, T), wu_ref, K)
            own = slice(HALO, HALO + rb)
            o_ref[pl.ds(r0, rb), :] = (_gelu(g[own] + bg_ref[...]) * (u[own] + bu_ref[...])).astype(o_ref.dtype)
            return carry

        lax.fori_loop(0, T // rb, blk, 0)

    pg = pl.BlockSpec((None, T, tc), lambda j: (0, 0, j))
    pu = pl.BlockSpec((None, T, tc), lambda j: (1, 0, j))
    wg = pl.BlockSpec((K, tc), lambda j: (0, j))
    wu = pl.BlockSpec((K, tc), lambda j: (0, j + nf))
    bg = pl.BlockSpec((1, tc), lambda j: (0, j))
    bu = pl.BlockSpec((1, tc), lambda j: (0, j + nf))
    return pl.pallas_call(body, name=name, grid=(nf,), in_specs=[pg, pu, wg, wu, bg, bu],
                          out_specs=pl.BlockSpec((T, tc), lambda j: (0, j)), out_shape=_sds((T, F), BF16),
                          compiler_params=_params(("parallel",)))(P, P, cw, cw, cb, cb)


def ffn_gate_bwd(name, P, da, cw, cb, tc=LANES):
    _, T, F = P.shape
    K = cw.shape[0]
    nf = F // tc

    rb = min(CONV_ROWS, T)

    def body(pg_ref, pu_ref, da_ref, wg_ref, wu_ref, bg_ref, bu_ref, dp_ref, dwg_ref, dwu_ref, dbg_ref, dbu_ref):
        def blk(bi, acc):
            r0 = pl.multiple_of(bi * rb, rb)
            tg, g = _conv_ext(_ext_rows(pg_ref, r0, rb, T), wg_ref, K)
            tu, u = _conv_ext(_ext_rows(pu_ref, r0, rb, T), wu_ref, K)
            g = g + bg_ref[...]
            u = u + bu_ref[...]
            da_ = _ext_rows(da_ref, r0, rb, T)
            dpg, dwg, dbg = _conv_ext_bwd(da_ * u * _dgelu(g), tg, wg_ref, K, rb)
            dpu, dwu, dbu = _conv_ext_bwd(da_ * _gelu(g), tu, wu_ref, K, rb)
            dp_ref[0, pl.ds(r0, rb), :] = dpg.astype(dp_ref.dtype)
            dp_ref[1, pl.ds(r0, rb), :] = dpu.astype(dp_ref.dtype)
            return tuple(a + b for a, b in zip(acc, dwg + [dbg] + dwu + [dbu]))

        acc = lax.fori_loop(0, T // rb, blk, tuple(jnp.zeros((1, tc), F32) for _ in range(2 * K + 2)))
        _store_conv_grads(acc[:K + 1], dwg_ref, dbg_ref, K)
        _store_conv_grads(acc[K + 1:], dwu_ref, dbu_ref, K)

    pg = pl.BlockSpec((None, T, tc), lambda j: (0, 0, j))
    pu = pl.BlockSpec((None, T, tc), lambda j: (1, 0, j))
    col = pl.BlockSpec((T, tc), lambda j: (0, j))
    wg = pl.BlockSpec((K, tc), lambda j: (0, j))
    wu = pl.BlockSpec((K, tc), lambda j: (0, j + nf))
    bg = pl.BlockSpec((1, tc), lambda j: (0, j))
    bu = pl.BlockSpec((1, tc), lambda j: (0, j + nf))
    return pl.pallas_call(
        body, name=name, grid=(nf,), in_specs=[pg, pu, col, wg, wu, bg, bu],
        out_specs=[pl.BlockSpec((2, T, tc), lambda j: (0, 0, j)), wg, wg, bg, bg],
        out_shape=[_sds((2, T, F), BF16), _sds((K, F), F32), _sds((K, F), F32), _sds((1, F), F32), _sds((1, F), F32)],
        compiler_params=_params(("parallel",)))(P, P, da, cw, cw, cb, cb)


def xattn_fwd(name, q, kv, n_heads, tq=512):
    T, Wd = q.shape
    Mm = kv.shape[0]
    hd = Wd // n_heads
    scale = hd ** -0.5
    tq = min(tq, T)

    def body(q_ref, kv_ref, o_ref):
        for h in range(n_heads):
            qh = q_ref[:, h * hd:(h + 1) * hd]
            kh = kv_ref[:, h * hd:(h + 1) * hd]
            vh = kv_ref[:, Wd + h * hd:Wd + (h + 1) * hd]
            s = _dot(qh, kh, NT) * scale
            s = s - jnp.max(s, axis=-1, keepdims=True)
            e = jnp.exp(s)
            p = e / jnp.sum(e, axis=-1, keepdims=True)
            o_ref[:, h * hd:(h + 1) * hd] = _dot(p.astype(BF16), vh).astype(o_ref.dtype)

    return pl.pallas_call(body, name=name, grid=(T // tq,),
                          in_specs=[pl.BlockSpec((tq, Wd), lambda i: (i, 0)), pl.BlockSpec((Mm, 2 * Wd), lambda i: (0, 0))],
                          out_specs=pl.BlockSpec((tq, Wd), lambda i: (i, 0)), out_shape=_sds((T, Wd), BF16),
                          compiler_params=_params(("parallel",)))(q, kv)


def xattn_bwd(name, q, kv, do, n_heads, tq=512):
    T, Wd = q.shape
    Mm = kv.shape[0]
    hd = Wd // n_heads
    scale = hd ** -0.5
    tq = min(tq, T)

    def body(q_ref, kv_ref, do_ref, dq_ref, dkv_ref):
        @pl.when(pl.program_id(0) == 0)
        def _():
            dkv_ref[...] = jnp.zeros_like(dkv_ref)

        for h in range(n_heads):
            sl = slice(h * hd, (h + 1) * hd)
            sv = slice(Wd + h * hd, Wd + (h + 1) * hd)
            qh, kh, vh = q_ref[:, sl], kv_ref[:, sl], kv_ref[:, sv]
            doh = do_ref[:, sl].astype(BF16)
            s = _dot(qh, kh, NT) * scale
            s = s - jnp.max(s, axis=-1, keepdims=True)
            e = jnp.exp(s)
            p = e / jnp.sum(e, axis=-1, keepdims=True)
            dp = _dot(doh, vh, NT)
            ds = (p * (dp - jnp.sum(dp * p, axis=-1, keepdims=True)) * scale).astype(BF16)
            dq_ref[:, sl] = _dot(ds, kh).astype(dq_ref.dtype)
            dkv_ref[:, sl] += _dot(ds, qh, TN)
            dkv_ref[:, sv] += _dot(p.astype(BF16), doh, TN)

    row = pl.BlockSpec((tq, Wd), lambda i: (i, 0))
    full = pl.BlockSpec((Mm, 2 * Wd), lambda i: (0, 0))
    return pl.pallas_call(body, name=name, grid=(T // tq,), in_specs=[row, full, row], out_specs=[row, full],
                          out_shape=[_sds((T, Wd), BF16), _sds((Mm, 2 * Wd), F32)],
                          compiler_params=_params(("arbitrary",)))(q, kv, do)


def ffn_fwd(hf, w_in, cw, cb, w_out):
    T = hf.shape[0]
    F = w_out.K
    tn = _tile(w_in.C, 1536)
    nfp = F // tn
    P = mm_fwd("ffn_in", hf, w_in, F32, tn=tn, out_sds=_sds((2, T, F), F32),
               o_spec=lambda tm, tn_: pl.BlockSpec((None, tm, tn_), lambda i, j, k: (j // nfp, i, j % nfp)))
    a = ffn_gate_fwd("ffn_gate", P, cw, cb)
    f = mm_fwd("ffn_out", a, w_out, F32, tn=2048)
    return f, (P, a)


def ffn_bwd(hf, saved, df, w_in, cw, cb, w_out):
    P, a = saved
    T = hf.shape[0]
    F = w_out.K
    C = w_in.C
    da = mm_dx("ffn_out_dx", df, w_out, F32, tn=w_out.R)
    dw_out = mm_dw("ffn_out_dw", a, df, BF16, tn=1024)
    dP, dcw_g, dcw_u, dcb_g, dcb_u = ffn_gate_bwd("ffn_gate_bwd", P, da, cw, cb)
    dw_in = mm_dw("ffn_in_dw", hf, dP, BF16, n_shards=w_in.S, N=2 * F,
                  b_spec=lambda T_, tn: pl.BlockSpec((None, T_, tn), lambda i, j, k: (j // (F // tn), 0, j % (F // tn))))
    per = F // C
    dhf = mm_dx("ffn_in_dx", dP, w_in, F32,
                a_spec=lambda tm, tk: pl.BlockSpec((None, tm, tk), lambda i, j, k: (k // per, i, k % per)))
    grads = dict(ffn_w_in=dw_in, ffn_w_out=dw_out, ffn_conv_w=jnp.concatenate([dcw_g, dcw_u], axis=1),
                 ffn_conv_b=jnp.concatenate([dcb_g, dcb_u], axis=1))
    return dhf, grads


def xa_fwd(hq, mem_n, wq, wkv, wo, n_heads):
    q = mm_fwd("xa_q", hq, wq, BF16)
    kv = mm_fwd("xa_kv", mem_n, wkv, BF16)
    o = xattn_fwd("xa_core", q, kv, n_heads)
    c = mm_fwd("xa_o", o, wo, F32)
    return c, (q, kv, o)


def xa_bwd(hq, mem_n, saved, dc, wq, wkv, wo, n_heads):
    q, kv, o = saved
    do = mm_dx("xa_o_dx", dc, wo, F32)
    dwo = mm_dw("xa_o_dw", o, dc, BF16, n_shards=wo.S)
    dq, dkv = xattn_bwd("xa_core_bwd", q, kv, do, n_heads)
    dkv = dkv.astype(BF16)
    dwq = mm_dw("xa_q_dw", hq, dq, BF16)
    dhq = mm_dx("xa_q_dx", dq, wq, F32, tn=wq.R)
    dwkv = mm_dw("xa_kv_dw", mem_n, dkv, BF16)
    dmem_n = mm_dx("xa_kv_dx", dkv, wkv, F32, tn=wkv.R)
    return dhq, dmem_n, dict(xa_wq=dwq, xa_wkv=dwkv, xa_wo=dwo)


def _sb_logits(q, kblk, scale):
    z = _dot(q, kblk, NT) * scale
    l1 = -_softplus(z)
    return z, l1, z + l1


def _split2(a):
    a1 = a.astype(BF16)
    return a1, (a - a1.astype(F32)).astype(BF16)


def _dot2r(a, m):
    p1, p2 = _split2(a)
    return _dot(p1, m) + _dot(p2, m)


SB_TQ = 1024


def sb_fwd(name, qkv, n_heads, tq=SB_TQ):
    T = qkv.shape[0]
    hd = qkv.shape[1] // (3 * n_heads)
    scale = hd ** -0.5
    Q = CHUNK
    tq = min(tq, T)
    nb = tq // Q
    unroll = 4 if nb % 4 == 0 else (2 if nb % 2 == 0 else 1)

    def body(q_ref, k_ref, v_ref, o_ref, lt_ref):
        i = pl.program_id(1)
        q = q_ref[...]
        mrev = _tri(Q, "lt").astype(BF16)

        def block(kb, carry, q_, masked):
            c, acc = carry
            off = pl.multiple_of(kb * Q, Q)
            kblk, vblk = k_ref[pl.ds(off, Q), :], v_ref[pl.ds(off, Q), :]
            _, l1, lb = _sb_logits(q_, kblk, scale)
            if masked:
                valid = _iota(l1.shape, 1) < _iota(l1.shape, 0)
                l1 = jnp.where(valid, l1, 0.0)
            a = jnp.exp(lb + _dot2r(l1, mrev) + c)
            if masked:
                a = jnp.where(valid, a, 0.0)
            return c + jnp.sum(l1, axis=1, keepdims=True), acc + _dot(a.astype(BF16), vblk)

        c, acc = jnp.zeros((tq, Q), F32), jnp.zeros((tq, hd), F32)
        for b in reversed(range(nb)):
            lo = b * Q
            cb, ab = block(i * nb + b, (c[lo:], acc[lo:]), q[lo:], True)
            c = cb if lo == 0 else jnp.concatenate([c[:lo], cb], axis=0)
            acc = ab if lo == 0 else jnp.concatenate([acc[:lo], ab], axis=0)

        def step(r, cr):
            for u in range(unroll):
                cr = block(i * nb - 1 - unroll * r - u, cr, q, False)
            return cr

        c, acc = lax.fori_loop(0, i * (nb // unroll), step, (c, acc))
        o_ref[...] = acc.astype(o_ref.dtype)
        lt_ref[...] = c

    H = n_heads
    return pl.pallas_call(
        body, name=name, grid=(H, T // tq),
        in_specs=[pl.BlockSpec((tq, hd), lambda h, i: (i, h)), pl.BlockSpec((T, hd), lambda h, i: (0, H + h)),
                  pl.BlockSpec((T, hd), lambda h, i: (0, 2 * H + h))],
        out_specs=[pl.BlockSpec((tq, hd), lambda h, i: (i, h)), pl.BlockSpec((None, tq, Q), lambda h, i: (h, i, 0))],
        out_shape=[_sds((T, H * hd), BF16), _sds((H, T, Q), F32)],
        compiler_params=_params(("parallel", "arbitrary")))(qkv, qkv, qkv)


def sb_bwd(name, qkv, do, lt, n_heads, tq=SB_TQ):
    T = qkv.shape[0]
    hd = qkv.shape[1] // (3 * n_heads)
    scale = hd ** -0.5
    Q = CHUNK
    tq = min(tq, T)
    nb = tq // Q
    unroll = 4 if nb % 4 == 0 else (2 if nb % 2 == 0 else 1)

    def body(q_ref, k_ref, v_ref, do_ref, lt_ref, dq_ref, dk_ref, dv_ref):
        i = pl.program_id(1)

        @pl.when(i == 0)
        def _():
            dk_ref[...] = jnp.zeros_like(dk_ref)
            dv_ref[...] = jnp.zeros_like(dv_ref)

        q, do_, ltot = q_ref[...], do_ref[...], lt_ref[...]
        mrev = _tri(Q, "lt").astype(BF16)
        mfwd = _tri(Q, "gt").astype(BF16)

        def block(kb, carry, q_, d_, lt_, masked):
            pin, pre, dq = carry
            off = pl.multiple_of(kb * Q, Q)
            kblk, vblk = k_ref[pl.ds(off, Q), :], v_ref[pl.ds(off, Q), :]
            _, l1, lb = _sb_logits(q_, kblk, scale)
            if masked:
                valid = _iota(l1.shape, 1) < _iota(l1.shape, 0)
                l1 = jnp.where(valid, l1, 0.0)
            pin = pin + jnp.sum(l1, axis=1, keepdims=True)
            a = jnp.exp(lb + _dot2r(l1, mrev) + (lt_ - pin))
            if masked:
                a = jnp.where(valid, a, 0.0)
            de = _dot(d_, vblk, NT) * a
            dl1 = pre + _dot2r(de, mfwd)
            pre = pre + jnp.sum(de, axis=1, keepdims=True)
            sig = jnp.exp(lb)
            dz = (de * (1.0 - sig) - dl1 * sig) * scale
            if masked:
                dz = jnp.where(valid, dz, 0.0)
            dzb = dz.astype(BF16)
            dk_ref[pl.ds(off, Q), :] += _dot(dzb, q_, TN)
            dv_ref[pl.ds(off, Q), :] += _dot(a.astype(BF16), d_, TN)
            return pin, pre, dq + _dot(dzb, kblk)

        def step(r, cr):
            for u in range(unroll):
                cr = block(unroll * r + u, cr, q, do_, ltot, False)
            return cr

        init = (jnp.zeros((tq, Q), F32), jnp.zeros((tq, Q), F32), jnp.zeros((tq, hd), F32))
        carry = lax.fori_loop(0, i * (nb // unroll), step, init)
        for b in range(nb):
            lo = b * Q
            part = block(i * nb + b, tuple(a[lo:] for a in carry), q[lo:], do_[lo:], ltot[lo:], True)
            carry = part if lo == 0 else tuple(jnp.concatenate([a[:lo], pb], axis=0) for a, pb in zip(carry, part))
        dq_ref[...] = carry[2].astype(dq_ref.dtype)

    H = n_heads
    qs = pl.BlockSpec((tq, hd), lambda h, i: (i, h))
    full = pl.BlockSpec((T, hd), lambda h, i: (0, h))
    return pl.pallas_call(
        body, name=name, grid=(H, T // tq),
        in_specs=[qs, pl.BlockSpec((T, hd), lambda h, i: (0, H + h)), pl.BlockSpec((T, hd), lambda h, i: (0, 2 * H + h)),
                  qs, pl.BlockSpec((None, tq, Q), lambda h, i: (h, i, 0))],
        out_specs=[qs, full, full],
        out_shape=[_sds((T, H * hd), BF16), _sds((T, H * hd), F32), _sds((T, H * hd), F32)],
        compiler_params=_params(("parallel", "arbitrary")))(qkv, qkv, qkv, do, lt)


def sb_mixer_fwd(hn, w_qkv, w_out, n_heads):
    qkv = mm_fwd("sb_qkv", hn, w_qkv, BF16)
    o, lt = sb_fwd("sb_core", qkv, n_heads)
    m = mm_fwd("sb_out", o, w_out, F32, tn=1024)
    return m, (qkv, o, lt)


def sb_mixer_bwd(hn, saved, dm, w_qkv, w_out, n_heads):
    qkv, o, lt = saved
    do = mm_dx("sb_out_dx", dm, w_out, BF16, tn=w_out.R)
    dw_out = mm_dw("sb_out_dw", o, dm, BF16, tn=1024)
    dq, dk, dv = sb_bwd("sb_core_bwd", qkv, do, lt, n_heads)
    dqkv = jnp.concatenate([dq, dk.astype(BF16), dv.astype(BF16)], axis=1)
    dw_qkv = mm_dw("sb_qkv_dw", hn, dqkv, BF16, n_shards=w_qkv.S)
    dhn = mm_dx("sb_qkv_dx", dqkv, w_qkv, F32)
    return dhn, dict(sb_w_qkv=dw_qkv, sb_w_out=dw_out)


def _sgu_common(p_ref, vg_ref, vb_ref, Wd):
    pu, pv = p_ref[:, :Wd], p_ref[:, Wd:]
    u, v = _gelu(pu), _gelu(pv)
    xc = v - jnp.mean(v, axis=-1, keepdims=True)
    r = lax.rsqrt(jnp.mean(xc * xc, axis=-1, keepdims=True) + EPS)
    xh = xc * r
    return pu, pv, u, xh, r, xh * vg_ref[...] + vb_ref[...]


def sgu_fwd(name, P, vg, vb, ws, bexp):
    T = P.shape[0]
    Wd = P.shape[1] // 2
    G = ws.shape[0]
    gw = Wd // G
    Q = CHUNK

    def body(p_ref, vg_ref, vb_ref, ws_ref, be_ref, o_ref):
        _, _, u, _, _, vn = _sgu_common(p_ref, vg_ref, vb_ref, Wd)
        tril = _tri(Q, "le")
        for g in range(G):
            sl = slice(g * gw, (g + 1) * gw)
            wsg = jnp.where(tril, ws_ref[g], 0.0).astype(BF16)
            mixed = _dot(wsg, vn[:, sl].astype(BF16)) + be_ref[:, sl]
            o_ref[:, sl] = (u[:, sl] * mixed).astype(o_ref.dtype)

    vec = pl.BlockSpec((1, Wd), lambda c: (0, 0))
    return pl.pallas_call(
        body, name=name, grid=(T // Q,),
        in_specs=[pl.BlockSpec((Q, 2 * Wd), lambda c: (c, 0)), vec, vec, pl.BlockSpec((G, Q, Q), lambda c: (0, 0, 0)),
                  pl.BlockSpec((Q, Wd), lambda c: (0, 0))],
        out_specs=pl.BlockSpec((Q, Wd), lambda c: (c, 0)), out_shape=_sds((T, Wd), BF16),
        compiler_params=_params(("parallel",)))(P, vg, vb, ws, bexp)


def sgu_bwd(name, P, dgated, vg, vb, ws, bexp):
    T = P.shape[0]
    Wd = P.shape[1] // 2
    G = ws.shape[0]
    gw = Wd // G
    Q = CHUNK
    nc = T // Q

    def body(p_ref, dg_ref, vg_ref, vb_ref, ws_ref, be_ref, dp_ref, dws_ref, dvg_ref, dvb_ref, dbs_ref, dvn_scr, dbe_scr):
        c = pl.program_id(0)

        @pl.when(c == 0)
        def _():
            dws_ref[...] = jnp.zeros_like(dws_ref)
            dvg_ref[...] = jnp.zeros_like(dvg_ref)
            dvb_ref[...] = jnp.zeros_like(dvb_ref)
            dbe_scr[...] = jnp.zeros_like(dbe_scr)

        pu, pv, u, xh, r, vn = _sgu_common(p_ref, vg_ref, vb_ref, Wd)
        tril = _tri(Q, "le")
        for g in range(G):
            sl = slice(g * gw, (g + 1) * gw)
            wsg = jnp.where(tril, ws_ref[g], 0.0).astype(BF16)
            vng = vn[:, sl].astype(BF16)
            mixed = _dot(wsg, vng) + be_ref[:, sl]
            dgt = dg_ref[:, sl]
            dp_ref[:, sl] = (dgt * mixed * _dgelu(pu[:, sl])).astype(dp_ref.dtype)
            dmix = dgt * u[:, sl]
            dmb = dmix.astype(BF16)
            dws_ref[g] += jnp.where(tril, _dot(dmb, vng, NT), 0.0)
            dvn_scr[:, sl] = _dot(wsg, dmb, TN)
            dbe_scr[:, sl] += dmix
        dvn = dvn_scr[...]
        dvg_ref[...] += jnp.sum(dvn * xh, axis=0, keepdims=True)
        dvb_ref[...] += jnp.sum(dvn, axis=0, keepdims=True)
        dxh = dvn * vg_ref[...]
        dv = r * (dxh - jnp.mean(dxh, axis=-1, keepdims=True) - xh * jnp.mean(dxh * xh, axis=-1, keepdims=True))
        dp_ref[:, Wd:] = (dv * _dgelu(pv)).astype(dp_ref.dtype)

        @pl.when(c == nc - 1)
        def _():
            sel = (_iota((Wd, LANES), 0) // gw == _iota((Wd, LANES), 1)).astype(BF16)
            dbs_ref[...] = _dot3r(dbe_scr[...], sel)

    vec = pl.BlockSpec((1, Wd), lambda c: (0, 0))
    wsb = pl.BlockSpec((G, Q, Q), lambda c: (0, 0, 0))
    return pl.pallas_call(
        body, name=name, grid=(nc,),
        in_specs=[pl.BlockSpec((Q, 2 * Wd), lambda c: (c, 0)), pl.BlockSpec((Q, Wd), lambda c: (c, 0)), vec, vec, wsb,
                  pl.BlockSpec((Q, Wd), lambda c: (0, 0))],
        out_specs=[pl.BlockSpec((Q, 2 * Wd), lambda c: (c, 0)), wsb, vec, vec, pl.BlockSpec((Q, LANES), lambda c: (0, 0))],
        out_shape=[_sds((T, 2 * Wd), BF16), _sds((G, Q, Q), F32), _sds((1, Wd), F32), _sds((1, Wd), F32), _sds((Q, LANES), F32)],
        scratch_shapes=[pltpu.VMEM((Q, Wd), F32), pltpu.VMEM((Q, Wd), F32)],
        compiler_params=_params(("arbitrary",)))(P, dgated, vg, vb, ws, bexp)


def sg_mixer_fwd(hn, w_in, vg, vb, ws, bs, w_out):
    G = ws.shape[0]
    Wd = vg.shape[1]
    P = mm_fwd("sg_in", hn, w_in, F32)
    bexp = jnp.repeat(bs.T, Wd // G, axis=1)
    gated = sgu_fwd("sg_core", P, vg, vb, ws, bexp)
    m = mm_fwd("sg_out", gated, w_out, F32, tn=1024)
    return m, (P, bexp, gated)


def sg_mixer_bwd(hn, saved, dm, w_in, vg, vb, ws, w_out):
    P, bexp, gated = saved
    G = ws.shape[0]
    dgated = mm_dx("sg_out_dx", dm, w_out, F32, tn=w_out.R)
    dw_out = mm_dw("sg_out_dw", gated, dm, BF16, tn=1024)
    dP, dws, dvg, dvb, dbs = sgu_bwd("sg_core_bwd", P, dgated, vg, vb, ws, bexp)
    dw_in = mm_dw("sg_in_dw", hn, dP, BF16, n_shards=w_in.S)
    dhn = mm_dx("sg_in_dx", dP, w_in, F32)
    grads = dict(sg_w_in=dw_in, sg_w_out=dw_out, sg_w_spatial=dws, sg_v_norm_g=dvg, sg_v_norm_b=dvb,
                 sg_b_spatial=dbs[:, :G].T)
    return dhn, grads


def ssd_conv_fwd(name, xbc, cw, cb, tc=LANES):
    T, Cd = xbc.shape
    K = cw.shape[0]
    rb = min(CONV_ROWS, T)

    def body(p_ref, w_ref, b_ref, o_ref):
        def blk(bi, carry):
            r0 = pl.multiple_of(bi * rb, rb)
            _, pre = _conv_ext(_ext_rows(p_ref, r0, rb, T), w_ref, K)
            o_ref[pl.ds(r0, rb), :] = _silu(pre[HALO:HALO + rb] + b_ref[...])
            return carry

        lax.fori_loop(0, T // rb, blk, 0)

    col = pl.BlockSpec((T, tc), lambda j: (0, j))
    return pl.pallas_call(body, name=name, grid=(Cd // tc,),
                          in_specs=[col, pl.BlockSpec((K, tc), lambda j: (0, j)), pl.BlockSpec((1, tc), lambda j: (0, j))],
                          out_specs=col, out_shape=_sds((T, Cd), F32), compiler_params=_params(("parallel",)))(xbc, cw, cb)


def ssd_conv_bwd(name, xbc, dact, cw, cb, tc=LANES):
    T, Cd = xbc.shape
    K = cw.shape[0]

    rb = min(CONV_ROWS, T)

    def body(p_ref, da_ref, w_ref, b_ref, dp_ref, dw_ref, db_ref):
        def blk(bi, acc):
            r0 = pl.multiple_of(bi * rb, rb)
            taps, pre = _conv_ext(_ext_rows(p_ref, r0, rb, T), w_ref, K)
            dpre = _ext_rows(da_ref, r0, rb, T) * _dsilu(pre + b_ref[...])
            dp, dws, db = _conv_ext_bwd(dpre, taps, w_ref, K, rb)
            dp_ref[pl.ds(r0, rb), :] = dp.astype(dp_ref.dtype)
            return tuple(a + b for a, b in zip(acc, dws + [db]))

        acc = lax.fori_loop(0, T // rb, blk, tuple(jnp.zeros((1, tc), F32) for _ in range(K + 1)))
        _store_conv_grads(acc, dw_ref, db_ref, K)

    col = pl.BlockSpec((T, tc), lambda j: (0, j))
    wsp = pl.BlockSpec((K, tc), lambda j: (0, j))
    bsp = pl.BlockSpec((1, tc), lambda j: (0, j))
    return pl.pallas_call(body, name=name, grid=(Cd // tc,), in_specs=[col, col, wsp, bsp], out_specs=[col, wsp, bsp],
                          out_shape=[_sds((T, Cd), BF16), _sds((K, Cd), F32), _sds((1, Cd), F32)],
                          compiler_params=_params(("parallel",)))(xbc, dact, cw, cb)


def _expand_matrix(Hd):
    return (_iota((LANES, Hd), 1) // SSD_HEAD_DIM == _iota((LANES, Hd), 0)).astype(BF16)


def ssd_dt_fwd(name, dtr, bias, Hd, tr=512):
    T = dtr.shape[0]
    tr = min(tr, T)

    def body(d_ref, b_ref, o_ref):
        o_ref[...] = _dot3r(_softplus(d_ref[...] + b_ref[...]), _expand_matrix(Hd))

    return pl.pallas_call(body, name=name, grid=(T // tr,),
                          in_specs=[pl.BlockSpec((tr, LANES), lambda i: (i, 0)), pl.BlockSpec((1, LANES), lambda i: (0, 0))],
                          out_specs=pl.BlockSpec((tr, Hd), lambda i: (i, 0)), out_shape=_sds((T, Hd), F32),
                          compiler_params=_params(("parallel",)))(dtr, bias)


def ssd_dt_bwd(name, dtr, bias, ddtx, tr=512):
    T, Hd = ddtx.shape
    tr = min(tr, T)

    def body(d_ref, b_ref, g_ref, o_ref, db_ref):
        p1, p2, p3 = _split3(g_ref[...])
        em = _expand_matrix(Hd)
        ddt = _dot(p1, em, NT) + _dot(p2, em, NT) + _dot(p3, em, NT)
        draw = ddt * _sigmoid(d_ref[...] + b_ref[...])
        o_ref[...] = draw.astype(o_ref.dtype)
        part = jnp.sum(draw, axis=0, keepdims=True)

        @pl.when(pl.program_id(0) == 0)
        def _():
            db_ref[...] = part

        @pl.when(pl.program_id(0) > 0)
        def _():
            db_ref[...] += part

    row = pl.BlockSpec((tr, LANES), lambda i: (i, 0))
    vec = pl.BlockSpec((1, LANES), lambda i: (0, 0))
    return pl.pallas_call(body, name=name, grid=(T // tr,), in_specs=[row, vec, pl.BlockSpec((tr, Hd), lambda i: (i, 0))],
                          out_specs=[row, vec], out_shape=[_sds((T, LANES), BF16), _sds((1, LANES), F32)],
                          compiler_params=_params(("arbitrary",)))(dtr, bias, ddtx)


def _ssd_head_terms(a2, a2r, half, cb, causal, lane):
    hm = (lane < SSD_HEAD_DIM) if half == 0 else (lane >= SSD_HEAD_DIM)
    ccol = jnp.where(hm, a2, a2r)
    lm = jnp.exp(jnp.where(causal, ccol - ccol.T, -jnp.inf))
    return hm, lm, cb * lm


def ssd_core_fwd(name, act, dtx, alx, dx, G):
    T, Hd = dtx.shape
    Q, N = CHUNK, SSD_STATE
    gw = Hd // G
    nc = T // Q
    nx = Hd // N

    def body(xs_ref, b_ref, c_ref, dt_ref, al_ref, d_ref, y_ref, ss_ref, st_scr):
        @pl.when(pl.program_id(1) == 0)
        def _():
            st_scr[...] = jnp.zeros_like(st_scr)

        xs, dtv = xs_ref[...], dt_ref[...]
        Bb, Cb = b_ref[...].astype(BF16), c_ref[...].astype(BF16)
        dA = dtv * (-jnp.exp(al_ref[...]))
        a = _dot3l(_tri(Q, "le").astype(BF16), dA)
        a_last = jnp.sum(dA, axis=0, keepdims=True)
        xdt = xs * dtv
        cbm = _dot(Cb, Bb, NT)
        sprev = st_scr[...]
        ss_ref[...] = sprev
        causal, lane = _tri(Q, "le"), _iota((Q, LANES), 1)
        y_rest = _dot(Cb, sprev.astype(BF16)) * jnp.exp(a) + xs * d_ref[...]
        for q in range(gw // LANES):
            sl = slice(q * LANES, (q + 1) * LANES)
            a2, x2 = a[:, sl], xdt[:, sl]
            a2r = pltpu.roll(a2, SSD_HEAD_DIM, 1)
            acc = y_rest[:, sl]
            for half in (0, 1):
                hm, _, gm = _ssd_head_terms(a2, a2r, half, cbm, causal, lane)
                acc = acc + _dot(gm.astype(BF16), jnp.where(hm, x2, 0.0).astype(BF16))
            y_ref[:, sl] = acc
        w = jnp.exp(a_last - a)
        st_scr[...] = sprev * jnp.exp(a_last) + _dot(Bb, (w * xdt).astype(BF16), TN)

    xsp = pl.BlockSpec((Q, gw), lambda g, c: (c, g))
    vec = pl.BlockSpec((1, gw), lambda g, c: (0, g))
    return pl.pallas_call(
        body, name=name, grid=(G, nc),
        in_specs=[xsp, pl.BlockSpec((Q, N), lambda g, c: (c, nx + g)), pl.BlockSpec((Q, N), lambda g, c: (c, nx + G + g)),
                  xsp, vec, vec],
        out_specs=[xsp, pl.BlockSpec((None, N, gw), lambda g, c: (c, 0, g))],
        out_shape=[_sds((T, Hd), F32), _sds((nc, N, Hd), F32)],
        scratch_shapes=[pltpu.VMEM((N, gw), F32)],
        compiler_params=_params(("parallel", "arbitrary")))(act, act, act, dtx, alx, dx)


def ssd_core_bwd(name, act, dtx, alx, dx, ssave, dy, G):
    T, Hd = dtx.shape
    Q, N = CHUNK, SSD_STATE
    gw = Hd // G
    nc = T // Q
    nx = Hd // N

    def body(xs_ref, b_ref, c_ref, dt_ref, al_ref, d_ref, ss_ref, dy_ref,
             dxs_ref, db_ref, dc_ref, ddt_ref, dal_ref, dd_ref, ds_scr, dxdt_scr, da_scr):
        @pl.when(pl.program_id(1) == 0)
        def _():
            ds_scr[...] = jnp.zeros_like(ds_scr)
            dal_ref[...] = jnp.zeros_like(dal_ref)
            dd_ref[...] = jnp.zeros_like(dd_ref)

        xs, dtv, dy_ = xs_ref[...], dt_ref[...], dy_ref[...]
        Bb, Cb = b_ref[...].astype(BF16), c_ref[...].astype(BF16)
        Ax = -jnp.exp(al_ref[...])
        dA = dtv * Ax
        a = _dot3l(_tri(Q, "le").astype(BF16), dA)
        a_last = jnp.sum(dA, axis=0, keepdims=True)
        xdt = xs * dtv
        e, w, eal = jnp.exp(a), jnp.exp(a_last - a), jnp.exp(a_last)
        sprev, dsn = ss_ref[...], ds_scr[...]
        sprevb, dsnb = sprev.astype(BF16), dsn.astype(BF16)

        dd_ref[...] += jnp.sum(dy_ * xs, axis=0, keepdims=True)
        dmb = (dy_ * e).astype(BF16)
        dC = _dot(dmb, sprevb, NT)
        ds_scr[...] = _dot(Cb, dmb, TN) + dsn * eal
        dalast = jnp.sum(dsn * sprev, axis=0, keepdims=True) * eal
        dB = _dot((w * xdt).astype(BF16), dsnb, NT)
        dwx = _dot(Bb, dsnb)
        dww = dwx * xdt * w
        dalast = dalast + jnp.sum(dww, axis=0, keepdims=True)
        da_scr[...] = dy_ * _dot(Cb, sprevb) * e - dww
        dxdt_scr[...] = w * dwx
        cbm = _dot(Cb, Bb, NT)
        dcb = jnp.zeros((Q, Q), F32)
        causal, lane = _tri(Q, "le"), _iota((Q, LANES), 1)
        for q in range(gw // LANES):
            sl = slice(q * LANES, (q + 1) * LANES)
            a2, x2, dy2 = a[:, sl], xdt[:, sl], dy_[:, sl]
            a2r = pltpu.roll(a2, SSD_HEAD_DIM, 1)
            for half in (0, 1):
                hm, lm, gm = _ssd_head_terms(a2, a2r, half, cbm, causal, lane)
                dyh = jnp.where(hm, dy2, 0.0).astype(BF16)
                dg = _dot(dyh, jnp.where(hm, x2, 0.0).astype(BF16), NT)
                dxdt_scr[:, sl] += _dot(gm.astype(BF16), dyh, TN)
                dcb = dcb + dg * lm
                dseg = dg * gm
                v = jnp.sum(dseg, axis=1, keepdims=True) - jnp.sum(dseg.T, axis=1, keepdims=True)
                da_scr[:, sl] += jnp.where(hm, v, 0.0) * (1.0 / SSD_HEAD_DIM)
        dcbb = dcb.astype(BF16)
        dc_ref[...] = dC + _dot(dcbb, Bb)
        db_ref[...] = dB + _dot(dcbb, Cb, TN)
        dxdt = dxdt_scr[...]
        dxs_ref[...] = dy_ * d_ref[...] + dxdt * dtv
        da = da_scr[...] + jnp.where(_iota((Q, gw), 0) == Q - 1, dalast, 0.0)
        dda = _dot3l(_tri(Q, "ge").astype(BF16), da)
        ddt_ref[...] = dxdt * xs + dda * Ax
        dal_ref[...] += jnp.sum(dda * dtv, axis=0, keepdims=True) * Ax

    rc = lambda c: nc - 1 - c
    xsp = pl.BlockSpec((Q, gw), lambda g, c: (rc(c), g))
    bsp = pl.BlockSpec((Q, N), lambda g, c: (rc(c), nx + g))
    csp = pl.BlockSpec((Q, N), lambda g, c: (rc(c), nx + G + g))
    gsp = pl.BlockSpec((Q, N), lambda g, c: (rc(c), g))
    vec = pl.BlockSpec((1, gw), lambda g, c: (0, g))
    return pl.pallas_call(
        body, name=name, grid=(G, nc),
        in_specs=[xsp, bsp, csp, xsp, vec, vec, pl.BlockSpec((None, N, gw), lambda g, c: (rc(c), 0, g)), xsp],
        out_specs=[xsp, gsp, gsp, xsp, vec, vec],
        out_shape=[_sds((T, Hd), F32), _sds((T, G * N), F32), _sds((T, G * N), F32), _sds((T, Hd), F32),
                   _sds((1, Hd), F32), _sds((1, Hd), F32)],
        scratch_shapes=[pltpu.VMEM((N, gw), F32), pltpu.VMEM((Q, gw), F32), pltpu.VMEM((Q, gw), F32)],
        compiler_params=_params(("parallel", "arbitrary")))(act, act, act, dtx, alx, dx, ssave, dy)


def ssd_gate_fwd(name, y, z, ng, tr=256):
    T, Hd = y.shape
    tr = min(tr, T)

    def body(y_ref, z_ref, g_ref, o_ref):
        o_ref[...] = _rms(y_ref[...] * _silu(z_ref[...]), g_ref[...]).astype(o_ref.dtype)

    row = pl.BlockSpec((tr, Hd), lambda i: (i, 0))
    return pl.pallas_call(body, name=name, grid=(T // tr,), in_specs=[row, row, pl.BlockSpec((1, Hd), lambda i: (0, 0))],
                          out_specs=row, out_shape=_sds((T, Hd), BF16), compiler_params=_params(("parallel",)))(y, z, ng)


def ssd_gate_bwd(name, y, z, ng, dyn, tr=128):
    T, Hd = y.shape
    tr = min(tr, T)

    def body(y_ref, z_ref, g_ref, dn_ref, dy_ref, dz_ref, dg_ref):
        y_, z_, dn = y_ref[...], z_ref[...], dn_ref[...]
        y2 = y_ * _silu(z_)
        r = lax.rsqrt(jnp.mean(y2 * y2, axis=-1, keepdims=True) + EPS)
        xh = y2 * r
        dxh = dn * g_ref[...]
        dy2 = r * (dxh - xh * jnp.mean(dxh * xh, axis=-1, keepdims=True))
        dy_ref[...] = dy2 * _silu(z_)
        dz_ref[...] = (dy2 * y_ * _dsilu(z_)).astype(dz_ref.dtype)
        part = jnp.sum(dn * xh, axis=0, keepdims=True)

        @pl.when(pl.program_id(0) == 0)
        def _():
            dg_ref[...] = part

        @pl.when(pl.program_id(0) > 0)
        def _():
            dg_ref[...] += part

    row = pl.BlockSpec((tr, Hd), lambda i: (i, 0))
    vec = pl.BlockSpec((1, Hd), lambda i: (0, 0))
    return pl.pallas_call(body, name=name, grid=(T // tr,), in_specs=[row, row, vec, row], out_specs=[row, row, vec],
                          out_shape=[_sds((T, Hd), F32), _sds((T, Hd), BF16), _sds((1, Hd), F32)],
                          compiler_params=_params(("arbitrary",)))(y, z, ng, dyn)


def ssd_mixer_fwd(hn, wz, wxbc, wdt, cw, cb, dtb, alx, dx, ng, w_out, G):
    Hd = wz.N
    z = mm_fwd("ssd_z", hn, wz, F32)
    xbc = mm_fwd("ssd_xbc", hn, wxbc, F32)
    dtr = mm_fwd("ssd_dt", hn, wdt, F32)
    act = ssd_conv_fwd("ssd_conv", xbc, cw, cb)
    dtx = ssd_dt_fwd("ssd_dtx", dtr, dtb, Hd)
    y, ssave = ssd_core_fwd("ssd_core", act, dtx, alx, dx, G)
    yn = ssd_gate_fwd("ssd_gate", y, z, ng)
    m = mm_fwd("ssd_out", yn, w_out, F32, tn=1024)
    return m, (z, xbc, dtr, act, dtx, y, ssave, yn)


def ssd_mixer_bwd(hn, saved, dm, wz, wxbc, wdt, cw, cb, dtb, alx, dx, ng, w_out, G):
    z, xbc, dtr, act, dtx, y, ssave, yn = saved
    dyn = mm_dx("ssd_out_dx", dm, w_out, F32, tn=w_out.R)
    dw_out = mm_dw("ssd_out_dw", yn, dm, BF16, tn=1024)
    dy, dz, dng = ssd_gate_bwd("ssd_gate_bwd", y, z, ng, dyn)
    dxs, dB, dC, ddtx, dalx, ddx = ssd_core_bwd("ssd_core_bwd", act, dtx, alx, dx, ssave, dy, G)
    dxbc, dcw, dcb = ssd_conv_bwd("ssd_conv_bwd", xbc, jnp.concatenate([dxs, dB, dC], axis=1), cw, cb)
    ddtr, ddtb = ssd_dt_bwd("ssd_dtx_bwd", dtr, dtb, ddtx)
    dwz = mm_dw("ssd_z_dw", hn, dz, BF16)
    dwxbc = mm_dw("ssd_xbc_dw", hn, dxbc, BF16)
    dwdt = mm_dw("ssd_dt_dw", hn, ddtr, BF16)
    dhn = mm_dx("ssd_z_dx", dz, wz, F32)
    dhn = mm_dx("ssd_xbc_dx", dxbc, wxbc, F32, add=dhn)
    dhn = mm_dx("ssd_dt_dx", ddtr, wdt, F32, add=dhn)
    grads = dict(ssd_w_out=dw_out, ssd_wz=dwz, ssd_wxbc=dwxbc, ssd_wdt=dwdt, ssd_conv_w=dcw, ssd_conv_b=dcb,
                 ssd_dt_bias=ddtb, ssd_alx=dalx, ssd_dx=ddx, ssd_norm=dng)
    return dhn, grads


def adamw(name, w, m, v, ga, gb=None):
    Rr, C = w.shape
    tr = 8
    while Rr % (tr * 2) == 0 and tr * 2 * C * 4 <= (1 << 20):
        tr *= 2
    if Rr % tr:
        tr = Rr
    two = gb is not None
    c1 = 1.0 - ADAM_B1 ** ADAM_STEP
    c2 = 1.0 - ADAM_B2 ** ADAM_STEP

    def body(*refs):
        if two:
            w_ref, m_ref, v_ref, a_ref, b_ref, g_ref, d_ref, mo_ref, vo_ref = refs
            g = a_ref[...] + b_ref[...]
        else:
            w_ref, m_ref, v_ref, a_ref, g_ref, d_ref, mo_ref, vo_ref = refs
            g = a_ref[...]
        m2 = ADAM_B1 * m_ref[...] + (1.0 - ADAM_B1) * g
        v2 = ADAM_B2 * v_ref[...] + (1.0 - ADAM_B2) * (g * g)
        g_ref[...] = g
        mo_ref[...] = m2
        vo_ref[...] = v2
        d_ref[...] = -ADAM_LR * ((m2 / c1) / (jnp.sqrt(v2 / c2) + ADAM_EPS) + ADAM_WD * w_ref[...])

    blk = pl.BlockSpec((tr, C), lambda i: (i, 0))
    n_in = 5 if two else 4
    args = (w, m, v, ga) + ((gb,) if two else ())
    return pl.pallas_call(body, name=name, grid=(Rr // tr,), in_specs=[blk] * n_in, out_specs=[blk] * 4,
                          out_shape=[_sds((Rr, C), F32)] * 4, compiler_params=_params(("parallel",)))(*args)


def adamw_rows(name, w, m, v, ga, gb, row0, prev):
    Rr, C = ga.shape
    tr = 8
    while Rr % (tr * 2) == 0 and row0 % (tr * 2) == 0 and tr * 2 * C * 4 <= (1 << 20):
        tr *= 2
    assert Rr % tr == 0 and row0 % tr == 0, (Rr, row0, tr)
    off = row0 // tr
    c1 = 1.0 - ADAM_B1 ** ADAM_STEP
    c2 = 1.0 - ADAM_B2 ** ADAM_STEP

    def body(w_ref, m_ref, v_ref, a_ref, b_ref, *rest):
        g_ref, d_ref, mo_ref, vo_ref = rest[-4:]
        g = a_ref[...].astype(F32) + b_ref[...].astype(F32)
        m2 = ADAM_B1 * m_ref[...] + (1.0 - ADAM_B1) * g
        v2 = ADAM_B2 * v_ref[...] + (1.0 - ADAM_B2) * (g * g)
        g_ref[...] = g
        mo_ref[...] = m2
        vo_ref[...] = v2
        d_ref[...] = -ADAM_LR * ((m2 / c1) / (jnp.sqrt(v2 / c2) + ADAM_EPS) + ADAM_WD * w_ref[...])

    rows = pl.BlockSpec((tr, C), lambda i: (off + i, 0))
    part = pl.BlockSpec((tr, C), lambda i: (i, 0))
    carried = [] if prev is None else list(prev)
    return pl.pallas_call(body, name=name, grid=(Rr // tr,), in_specs=[rows] * 3 + [part] * 2 + [ANY] * len(carried),
                          out_specs=[rows] * 4, out_shape=[_sds(w.shape, F32)] * 4,
                          input_output_aliases={5 + k: k for k in range(len(carried))},
                          compiler_params=_params(("parallel",)))(w, m, v, ga, gb, *carried)


def _mesh_pos():
    return lax.axis_index("x"), lax.axis_index("y"), lax.axis_index("c")


def _peer_chips(x, y):
    return [(1 - x, y), (x, 1 - y), (1 - x, 1 - y)]


def sum_slots(name, own, r):
    _, Rr, C = r.shape
    tr = 8
    while Rr % (tr * 2) == 0 and tr * 2 * C * 4 <= (1 << 20):
        tr *= 2

    def body(o_in, r_ref, o_ref):
        s = ((o_in[...].astype(F32) + r_ref[0].astype(F32)) + r_ref[1].astype(F32)) + r_ref[2].astype(F32)
        o_ref[...] = s.astype(o_ref.dtype)

    blk = pl.BlockSpec((tr, C), lambda i: (i, 0))
    return pl.pallas_call(body, name=name, grid=(Rr // tr,), in_specs=[blk, pl.BlockSpec((3, tr, C), lambda i: (0, i, 0))],
                          out_specs=blk, out_shape=_sds((Rr, C), BF16), compiler_params=_params(("parallel",)))(own, r)


HBM = pl.BlockSpec(memory_space=pltpu.HBM)
SEM = pl.BlockSpec(memory_space=pltpu.SEMAPHORE)
EFFECT = pltpu.SideEffectType.DATAFLOW_SIDE_EFFECTING


def _xchg_copy(mode, side, src, land, send, recv, t, j, peer, me, c):
    px, py = peer
    pidx = 2 * px + py
    if mode == "gather":
        s, dst = src, land.at[me if side == "out" else pidx]
    else:
        s, dst = src.at[pidx], land.at[j]
    k = 3 * t + j
    return pltpu.make_async_remote_copy(src_ref=s, dst_ref=dst, send_sem=send.at[k], recv_sem=recv.at[k],
                                        device_id=(px, py, c), device_id_type=MESH)


def xchg_start(name, mode, srcs, lands):
    counts = [len(g) for g in srcs]
    ng = len(counts)
    fs = [a for g in srcs for a in g]
    fl = [a for g in lands for a in g]
    n = len(fs)

    def body(*refs):
        src, land = refs[:n], refs[n:2 * n]
        send, recv = refs[2 * n:2 * n + ng], refs[2 * n + ng:2 * n + 2 * ng]
        token = refs[-1]
        x, y, c = _mesh_pos()
        me = 2 * x + y
        k = 0
        for gi in range(ng):
            for t in range(counts[gi]):
                for j, peer in enumerate(_peer_chips(x, y)):
                    _xchg_copy(mode, "out", src[k], land[k], send[gi], recv[gi], t, j, peer, me, c).start()
                k += 1
        token[...] = jnp.zeros_like(token)

    sems = tuple(pltpu.SemaphoreType.DMA((3 * cnt,)) for cnt in counts)
    thru = tuple(pltpu.HBM(a.shape, a.dtype) for a in fs + fl)
    out = pl.pallas_call(
        body, name=name, in_specs=[HBM] * (2 * n),
        out_specs=(SEM,) * (2 * ng) + (HBM,) * (2 * n) + (pl.BlockSpec(memory_space=pltpu.VMEM),),
        out_shape=sems + sems + thru + (_sds((8, LANES), F32),),
        input_output_aliases={i: 2 * ng + i for i in range(2 * n)},
        compiler_params=pltpu.CompilerParams(has_side_effects=EFFECT),
    )(*[pltpu.with_memory_space_constraint(a, pltpu.HBM) for a in fs + fl])
    send, recv = out[:ng], out[ng:2 * ng]
    thru_s, thru_l = out[2 * ng:2 * ng + n], out[2 * ng + n:2 * ng + 2 * n]
    groups, k = [], 0
    for gi, cnt in enumerate(counts):
        groups.append(dict(send=send[gi], recv=recv[gi], src=list(thru_s[k:k + cnt]), land=list(thru_l[k:k + cnt])))
        k += cnt
    return groups, out[-1]


def xchg_wait(name, mode, grp, after):
    src, land = grp["src"], grp["land"]
    n = len(src)

    def body(*refs):
        s_ref, l_ref = refs[:n], refs[n:2 * n]
        send, recv = refs[2 * n], refs[2 * n + 1]
        x, y, c = _mesh_pos()
        me = 2 * x + y
        for t in range(n):
            for j, peer in enumerate(_peer_chips(x, y)):
                _xchg_copy(mode, "out", s_ref[t], l_ref[t], send, recv, t, j, peer, me, c).wait_send()
                _xchg_copy(mode, "in", s_ref[t], l_ref[t], send, recv, t, j, peer, me, c).wait_recv()

    res = pl.pallas_call(
        body, name=name, in_specs=[HBM] * (2 * n) + [SEM, SEM, ANY],
        out_specs=(HBM,) * (2 * n), out_shape=tuple(pltpu.HBM(a.shape, a.dtype) for a in src + land),
        input_output_aliases={i: i for i in range(2 * n)},
        compiler_params=pltpu.CompilerParams(has_side_effects=EFFECT),
    )(*src, *land, grp["send"], grp["recv"], after)
    return list(res[:n]), list(res[n:])


def swap_with_sibling(name, tensors):
    n = len(tensors)

    def body(*refs):
        ins, outs = refs[:n], refs[n:2 * n]
        send_sems, recv_sems = refs[2 * n:]
        x, y, c = _mesh_pos()
        cps = []
        for t in range(n):
            cp = pltpu.make_async_remote_copy(src_ref=ins[t], dst_ref=outs[t], send_sem=send_sems.at[t], recv_sem=recv_sems.at[t],
                                              device_id=(x, y, 1 - c), device_id_type=MESH)
            cp.start()
            cps.append(cp)
        for cp in cps:
            cp.wait()

    return pl.pallas_call(
        body, name=name, in_specs=[ANY] * n, out_specs=[ANY] * n, out_shape=[_sds(t.shape, t.dtype) for t in tensors],
        scratch_shapes=[pltpu.SemaphoreType.DMA((n,)), pltpu.SemaphoreType.DMA((n,))],
    )(*tensors)


def all_reduce_small(name, v, after):
    Rr, C = v.shape
    nd = 8

    def body(v_ref, after_ref, o_ref, gath, send_sems, recv_sems):
        x, y, c = _mesh_pos()
        me = 4 * x + 2 * y + c
        cps = []
        for d in range(1, nd):
            bx, by, bc = (d >> 2) & 1, (d >> 1) & 1, d & 1
            tgt = (1 - x if bx else x, 1 - y if by else y, 1 - c if bc else c)
            cp = pltpu.make_async_remote_copy(src_ref=v_ref, dst_ref=gath.at[me], send_sem=send_sems.at[d - 1],
                                              recv_sem=recv_sems.at[d - 1], device_id=tgt, device_id_type=MESH)
            cp.start()
            cps.append((cp, tgt))
        gath[me] = v_ref[...]
        for d in range(1, nd):
            _, (tx, ty, tc) = cps[d - 1]
            pltpu.make_async_remote_copy(src_ref=v_ref, dst_ref=gath.at[4 * tx + 2 * ty + tc], send_sem=send_sems.at[d - 1],
                                         recv_sem=recv_sems.at[d - 1], device_id=(tx, ty, tc), device_id_type=MESH).wait_recv()
        acc = gath[0]
        for d in range(1, nd):
            acc = acc + gath[d]
        o_ref[...] = acc
        for cp, _ in cps:
            cp.wait_send()

    vm = pl.BlockSpec(memory_space=pltpu.VMEM)
    return pl.pallas_call(
        body, name=name, in_specs=[vm, ANY], out_specs=vm, out_shape=_sds((Rr, C), F32),
        scratch_shapes=[pltpu.VMEM((nd, Rr, C), F32), pltpu.SemaphoreType.DMA((nd - 1,)), pltpu.SemaphoreType.DMA((nd - 1,))],
        compiler_params=pltpu.CompilerParams(vmem_limit_bytes=VMEM_LIMIT),
    )(v, after)


def _pack(arrs):
    flat = jnp.concatenate([a.reshape(-1) for a in arrs])
    pad = (-flat.shape[0]) % (8 * LANES)
    return jnp.pad(flat, (0, pad)).reshape(-1, LANES)


def _unpack(buf, shapes):
    flat = buf.reshape(-1)
    out, off = [], 0
    for s in shapes:
        n = math.prod(s)
        out.append(flat[off:off + n].reshape(s))
        off += n
    return out


WEIGHTS = ["ln_mix_pre", "ln_mix_post", "ln_mem", "ln_xa_pre", "ln_xa_post", "ln_ffn_pre", "ln_ffn_post", "xa_wq", "xa_wkv",
           "xa_wo", "ffn_w_in", "ffn_conv_w", "ffn_conv_b", "ffn_w_out", "ssd_w_in", "ssd_conv_w", "ssd_conv_b", "ssd_dt_bias",
           "ssd_a_log", "ssd_d", "ssd_norm", "ssd_w_out", "sg_w_in", "sg_v_norm_g", "sg_v_norm_b", "sg_w_spatial",
           "sg_b_spatial", "sg_w_out", "sb_w_qkv", "sb_w_out"]
BIG = {"xa_wq": "rows", "xa_wkv": "rows", "xa_wo": "cols", "ffn_w_in": "cols", "ffn_w_out": "rows", "ssd_w_in": "cols",
       "ssd_w_out": "rows", "sg_w_in": "cols", "sg_w_out": "rows", "sb_w_qkv": "cols", "sb_w_out": "rows"}
SHARDED_SMALL = {"ffn_conv_w": 2, "ssd_conv_w": 2, "ssd_conv_b": 1, "ssd_norm": 1}
SMALL = [n for n in WEIGHTS if n not in BIG]
N_MIXERS = 3
HEAD = 128


def _unshard(a, axis):
    a = jnp.moveaxis(a, 0, axis)
    s = a.shape
    return a.reshape(s[:axis] + (s[axis] * s[axis + 1],) + s[axis + 2:])


def _step(p):
    x, mem, tgt = p["x"][0], p["mem"][0], p["loss_target"][0]
    T, D = x.shape
    depth = p["ln_mix_pre"].shape[0]
    S = N_CHIPS

    me = 2 * lax.axis_index("x") + lax.axis_index("y")
    Hd, Cd = S * p["ssd_norm"].shape[1], S * p["ssd_conv_b"].shape[1]
    nh = p["ssd_dt_bias"].shape[1]
    G = (Cd - Hd) // (2 * SSD_STATE)
    xa_heads = p["xa_wo"].shape[1] // HEAD
    sb_heads = D // HEAD

    def layer_parts(i):
        kind, j = i % N_MIXERS, i // N_MIXERS
        first = {0: [("ssd_w_in", j), ("ssd_conv_w", j), ("ssd_conv_b", j), ("ssd_norm", j)], 1: [("sg_w_in", j)],
                 2: [("sb_w_qkv", j)]}[kind]
        w_out = {0: "ssd_w_out", 1: "sg_w_out", 2: "sb_w_out"}[kind]
        return [first + [("ffn_conv_w", i)], [(w_out, j), ("xa_wq", i), ("xa_wkv", i), ("xa_wo", i)],
                [("ffn_w_in", i), ("ffn_w_out", i)]]

    srcs, lands = [], []
    for i in range(depth):
        for part in layer_parts(i):
            s_i, l_i = [], []
            for n, k in part:
                a = p[n][k].astype(BF16) if n in BIG else p[n][k]
                a = a.reshape((1,) * (2 - a.ndim) + a.shape)
                s_i.append(a)
                l_i.append(lax.dynamic_update_index_in_dim(lax.empty((S,) + a.shape, a.dtype), a, me, 0))
            srcs.append(s_i)
            lands.append(l_i)
    gather_groups, gather_token = xchg_start("gather_start", "gather", srcs, lands)

    class Gathered:
        def __init__(self, i):
            self.i, self.parts, self.got = i, layer_parts(i), {}

        def get(self, key, after):
            if key not in self.got:
                k = next(idx for idx, part in enumerate(self.parts) if key in part)
                _, zones = xchg_wait("gather_wait_%d_%d" % (self.i, k), "gather", gather_groups[3 * self.i + k], after)
                self.got.update(zip(self.parts[k], zones))
            return self.got[key]

    def layer_args(i, gz, x_in):
        kind, j = i % N_MIXERS, i // N_MIXERS
        w_of = lambda n, k: W(BIG[n], lambda operand: gz.get((n, k), operand)[:, None], 0, shape=(S, 1) + p[n].shape[1:])
        now = lambda n, k: gz.get((n, k), x_in)
        a = dict(xa=(w_of("xa_wq", i), w_of("xa_wkv", i), w_of("xa_wo", i), xa_heads),
                 ffn=(w_of("ffn_w_in", i), _unshard(now("ffn_conv_w", i), 1), p["ffn_conv_b"][i:i + 1], w_of("ffn_w_out", i)))
        if kind == 0:
            w_in = _unshard(now("ssd_w_in", j), 1)[None]
            a["mix"] = (W("full", w_in[:, :, :Hd], 0), W("full", w_in[:, :, Hd:Hd + Cd], 0),
                        W("full", jnp.pad(w_in[:, :, Hd + Cd:], ((0, 0), (0, 0), (0, LANES - nh))), 0),
                        _unshard(now("ssd_conv_w", j), 1), _unshard(now("ssd_conv_b", j), 1),
                        jnp.pad(p["ssd_dt_bias"][j], (0, LANES - nh))[None], jnp.repeat(p["ssd_a_log"][j], SSD_HEAD_DIM)[None],
                        jnp.repeat(p["ssd_d"][j], SSD_HEAD_DIM)[None], _unshard(now("ssd_norm", j), 1), w_of("ssd_w_out", j), G)
        elif kind == 1:
            a["mix"] = (w_of("sg_w_in", j), p["sg_v_norm_g"][j:j + 1], p["sg_v_norm_b"][j:j + 1], p["sg_w_spatial"][j],
                        w_of("sg_w_out", j))
        else:
            a["mix"] = (w_of("sb_w_qkv", j), w_of("sb_w_out", j), sb_heads)
        return a

    ln = lambda n, i: p[n][i:i + 1]

    h = rms_fwd("rms_first", x, ln("ln_mix_pre", 0), after=gather_token)
    saved, largs = [], []
    for i in range(depth):
        kind, j = i % N_MIXERS, i // N_MIXERS
        la = layer_args(i, Gathered(i), x)
        largs.append(la)
        if kind == 0:
            m, ms = ssd_mixer_fwd(h, *la["mix"])
        elif kind == 1:
            m, ms = sg_mixer_fwd(h, *la["mix"][:4], p["sg_b_spatial"][j], la["mix"][4])
        else:
            m, ms = sb_mixer_fwd(h, *la["mix"])
        x1, hq = resid_norm("resid_norm", x, m, ln("ln_mix_post", i), ln("ln_xa_pre", i))
        mem_n = rms_fwd("rms_mem", mem, ln("ln_mem", i))
        c, cs = xa_fwd(hq, mem_n, *la["xa"])
        x2, hf = resid_norm("resid_norm", x1, c, ln("ln_xa_post", i), ln("ln_ffn_pre", i))
        f, fs = ffn_fwd(hf, *la["ffn"])
        x3, hn = resid_norm("resid_norm", x2, f, ln("ln_ffn_post", i), ln("ln_mix_pre", i + 1) if i + 1 < depth else None)
        saved.append(dict(x=x, h=h, m=m, ms=ms, x1=x1, hq=hq, mem_n=mem_n, c=c, cs=cs, x2=x2, hf=hf, f=f, fs=fs))
        x, h = x3, hn
    loss_tile, dx = loss_fwd_bwd("loss", x, tgt)
    loss = lax.psum(loss_tile[0, 0], ("x", "y", "c"))

    gs = {n: [None] * p[n].shape[0] for n in WEIGHTS}
    scatter_groups = [[] for _ in range(depth)]

    def send_grads(i, part, keys):
        g_src = [gs[n][k] for n, k in keys]
        g_land = [lax.empty((3,) + a.shape[1:], a.dtype) for a in g_src]
        grp, tok = xchg_start("scatter_start_%d_%d" % (i, part), "scatter", [g_src], [g_land])
        scatter_groups[i].append((part, keys, grp[0]))
        return tok

    token = None
    for i in reversed(range(depth)):
        kind, j = i % N_MIXERS, i // N_MIXERS
        s, la = saved[i], largs[i]
        df, gs["ln_ffn_post"][i] = rms_bwd("rms_bwd_post", s["f"], ln("ln_ffn_post", i), dx, None, BF16, after=token)
        dhf, g = ffn_bwd(s["hf"], s["fs"], df, *la["ffn"])
        gs["ffn_w_in"][i], gs["ffn_conv_w"][i], gs["ffn_conv_b"][i] = g["ffn_w_in"], g["ffn_conv_w"], g["ffn_conv_b"]
        gs["ffn_w_out"][i] = g["ffn_w_out"].reshape(S, -1, D)
        dx, gs["ln_ffn_pre"][i] = rms_bwd("rms_bwd_pre", s["x2"], ln("ln_ffn_pre", i), dhf, dx, F32)
        token = send_grads(i, 2, [("ffn_w_in", i), ("ffn_w_out", i)])

        dc, gs["ln_xa_post"][i] = rms_bwd("rms_bwd_post", s["c"], ln("ln_xa_post", i), dx, None, BF16, after=token)
        dhq, dmem_n, g = xa_bwd(s["hq"], s["mem_n"], s["cs"], dc, *la["xa"])
        gs["xa_wq"][i] = g["xa_wq"].reshape(S, D // S, -1)
        gs["xa_wkv"][i] = g["xa_wkv"].reshape(S, D // S, -1)
        gs["xa_wo"][i] = g["xa_wo"]
        _, gs["ln_mem"][i] = rms_bwd("rms_bwd_mem", mem, ln("ln_mem", i), dmem_n, None, BF16)
        dx, gs["ln_xa_pre"][i] = rms_bwd("rms_bwd_pre", s["x1"], ln("ln_xa_pre", i), dhq, dx, F32)
        token = send_grads(i, 1, [("xa_wq", i), ("xa_wkv", i), ("xa_wo", i)])

        dm, gs["ln_mix_post"][i] = rms_bwd("rms_bwd_post", s["m"], ln("ln_mix_post", i), dx, None, BF16, after=token)
        if kind == 0:
            dhn, g = ssd_mixer_bwd(s["h"], s["ms"], dm, *la["mix"])
            full = jnp.concatenate([g["ssd_wz"], g["ssd_wxbc"], g["ssd_wdt"][:, :nh]], axis=1)
            gs["ssd_w_in"][j] = full.reshape(D, S, -1).transpose(1, 0, 2)
            gs["ssd_w_out"][j] = g["ssd_w_out"].reshape(S, Hd // S, D)
            gs["ssd_conv_w"][j], gs["ssd_conv_b"][j], gs["ssd_norm"][j] = g["ssd_conv_w"], g["ssd_conv_b"], g["ssd_norm"]
            gs["ssd_dt_bias"][j] = g["ssd_dt_bias"][:, :nh]
            gs["ssd_a_log"][j] = g["ssd_alx"].reshape(nh, SSD_HEAD_DIM).sum(-1)[None]
            gs["ssd_d"][j] = g["ssd_dx"].reshape(nh, SSD_HEAD_DIM).sum(-1)[None]
        elif kind == 1:
            dhn, g = sg_mixer_bwd(s["h"], s["ms"], dm, *la["mix"])
            gs["sg_w_in"][j] = g["sg_w_in"]
            gs["sg_w_out"][j] = g["sg_w_out"].reshape(S, -1, D)
            for n in ("sg_v_norm_g", "sg_v_norm_b", "sg_w_spatial", "sg_b_spatial"):
                gs[n][j] = g[n]
        else:
            dhn, g = sb_mixer_bwd(s["h"], s["ms"], dm, *la["mix"])
            gs["sb_w_qkv"][j] = g["sb_w_qkv"]
            gs["sb_w_out"][j] = g["sb_w_out"].reshape(S, -1, D)
        dx, gs["ln_mix_pre"][i] = rms_bwd("rms_bwd_pre", s["x"], ln("ln_mix_pre", i), dhn, dx, F32)

        mix_w = {0: ("ssd_w_in", "ssd_w_out"), 1: ("sg_w_in", "sg_w_out"), 2: ("sb_w_qkv", "sb_w_out")}[kind]
        token = send_grads(i, 0, [(n, j) for n in mix_w])

    out, running, behind = {}, {n: None for n in BIG}, token
    for i in reversed(range(depth)):
        keys, qs = [], []
        for part, part_keys, grp in scatter_groups[i]:
            sent, got = xchg_wait("scatter_wait_%d_%d" % (i, part), "scatter", grp, behind if i == 0 else token)
            for (n, k), own, r in zip(part_keys, sent, got):
                mine = lax.dynamic_index_in_dim(own, me, 0, keepdims=False)
                keys.append((n, k))
                qs.append(sum_slots("sum_grad_slots", mine.reshape(-1, mine.shape[-1]), r.reshape(3, -1, r.shape[-1])))
        sib = swap_with_sibling("swap_grads_%d" % i, qs)
        for (n, k), q, q2 in zip(keys, qs, sib):
            two_d = lambda a: a.reshape(-1, a.shape[-1])
            running[n] = adamw_rows("adamw_big", two_d(p[n]), two_d(p["m_" + n]), two_d(p["v_" + n]), q, q2,
                                    k * q.shape[0], running[n])
            behind = running[n][0]
    for n in BIG:
        out[n] = [r.reshape(p[n].shape) for r in running[n]]

    stack = lambda n: jnp.stack([a.reshape(p[n].shape[1:]) if n not in SHARDED_SMALL else a.reshape(a.shape[-len(p[n].shape) + 1:])
                                 for a in gs[n]])
    small_full = [stack(n) for n in SMALL]
    red = _unpack(all_reduce_small("reduce_small", _pack(small_full), after=behind), [a.shape for a in small_full])
    small_g = []
    for n, a in zip(SMALL, red):
        if n in SHARDED_SMALL:
            ax = SHARDED_SMALL[n]
            a = lax.dynamic_slice_in_dim(a, me * p[n].shape[ax], p[n].shape[ax], axis=ax)
        small_g.append(a)
    shapes = [p[n].shape for n in SMALL]
    res = adamw("adamw_small", _pack([p[n] for n in SMALL]), _pack([p["m_" + n] for n in SMALL]),
                _pack([p["v_" + n] for n in SMALL]), _pack(small_g))
    for k, r in enumerate(res):
        for n, a in zip(SMALL, _unpack(r, shapes)):
            out.setdefault(n, [None] * 4)[k] = a

    return (loss, dx[None]) + tuple(out[n][k] for k in range(4) for n in WEIGHTS)


def kernel(x, mem, ln_mix_pre, ln_mix_post, ln_mem, ln_xa_pre, ln_xa_post, ln_ffn_pre, ln_ffn_post, xa_wq, xa_wkv, xa_wo, ffn_w_in, ffn_conv_w, ffn_conv_b, ffn_w_out, ssd_w_in, ssd_conv_w, ssd_conv_b, ssd_dt_bias, ssd_a_log, ssd_d, ssd_norm, ssd_w_out, sg_w_in, sg_v_norm_g, sg_v_norm_b, sg_w_spatial, sg_b_spatial, sg_w_out, sb_w_qkv, sb_w_out, loss_target, m_ln_mix_pre, m_ln_mix_post, m_ln_mem, m_ln_xa_pre, m_ln_xa_post, m_ln_ffn_pre, m_ln_ffn_post, m_xa_wq, m_xa_wkv, m_xa_wo, m_ffn_w_in, m_ffn_conv_w, m_ffn_conv_b, m_ffn_w_out, m_ssd_w_in, m_ssd_conv_w, m_ssd_conv_b, m_ssd_dt_bias, m_ssd_a_log, m_ssd_d, m_ssd_norm, m_ssd_w_out, m_sg_w_in, m_sg_v_norm_g, m_sg_v_norm_b, m_sg_w_spatial, m_sg_b_spatial, m_sg_w_out, m_sb_w_qkv, m_sb_w_out, v_ln_mix_pre, v_ln_mix_post, v_ln_mem, v_ln_xa_pre, v_ln_xa_post, v_ln_ffn_pre, v_ln_ffn_post, v_xa_wq, v_xa_wkv, v_xa_wo, v_ffn_w_in, v_ffn_conv_w, v_ffn_conv_b, v_ffn_w_out, v_ssd_w_in, v_ssd_conv_w, v_ssd_conv_b, v_ssd_dt_bias, v_ssd_a_log, v_ssd_d, v_ssd_norm, v_ssd_w_out, v_sg_w_in, v_sg_v_norm_g, v_sg_v_norm_b, v_sg_w_spatial, v_sg_b_spatial, v_sg_w_out, v_sb_w_qkv, v_sb_w_out):
    return _step(dict(locals()))
```
